```python
import jax
import jax.numpy as jnp
from jax import lax
import numpy as np

D_MODEL = 1024
BATCH = 8
SEQ = 4096
DEPTH = 4

GRID_W = 64
CTX_LEN = 256

RET_HEADS = 4
RET_DK = 64
RET_DV = 128
RET_CHUNK = 128
RET_W = RET_HEADS * RET_DV
MLA_HEADS = 8
MLA_Q_RANK = 256
MLA_KV_RANK = 128
MLA_NOPE = 64
MLA_ROPE = 32
MLA_DV = 64
MLA_QK = MLA_NOPE + MLA_ROPE
MLA_W = MLA_HEADS * MLA_DV
ATTN_BLOCK = 128
RWKV_HEADS = 8
RWKV_HD = 64
RWKV_W = RWKV_HEADS * RWKV_HD
RWKV_DECAY_LORA = 64
RWKV_A_LORA = 64
RWKV_GATE_LORA = 128
RWKV_GN_EPS = 64e-5
N_EXPERTS = 32
TOP_K = 4
D_EXPERT = D_MODEL
SWIGLU_LIMIT = 7.0
SWIGLU_ALPHA = 1.702
MOE_BLOCK = 128

ROPE_BASE = 10000.0
NORM_EPS = 1e-6
HEAD_NORM_EPS = 1e-5

RET_SEGS = (RET_HEADS * RET_DK, RET_HEADS * RET_DK, RET_W, RET_W, RET_W)
MLA_SEGS = (MLA_Q_RANK, MLA_KV_RANK, MLA_ROPE)
RWKV_SEGS = (RWKV_W, RWKV_W, RWKV_W, RWKV_DECAY_LORA, RWKV_DECAY_LORA,
             RWKV_A_LORA, RWKV_A_LORA, RWKV_GATE_LORA)
RET_IN = 2 * RET_HEADS * RET_DK + 3 * RET_W
MLA_IN = MLA_Q_RANK + MLA_KV_RANK + MLA_ROPE
RWKV_IN = 3 * RWKV_W + 2 * RWKV_DECAY_LORA + 2 * RWKV_A_LORA + RWKV_GATE_LORA
D_IN = RET_IN + MLA_IN + RWKV_IN + 3 * D_MODEL

kernel_name = 'hybrid_ret_mla_rwkv7_moe_dit'


def _split_cols(t, sizes):
    cuts, acc = [], 0
    for sz in sizes[:-1]:
        acc += sz
        cuts.append(acc)
    return jnp.split(t, cuts, axis=-1)


def _rmsnorm(x, g):
    x32 = x.astype(jnp.float32)
    y = x32 * lax.rsqrt(jnp.mean(x32 * x32, axis=-1, keepdims=True) + NORM_EPS)
    return (y * g.astype(jnp.float32)).astype(x.dtype)


def _layernorm_last(x, eps):
    x32 = x.astype(jnp.float32)
    xc = x32 - jnp.mean(x32, axis=-1, keepdims=True)
    return xc * lax.rsqrt(jnp.mean(xc * xc, axis=-1, keepdims=True) + eps)


def _rope_angles(pos, dim):
    inv = ROPE_BASE ** (-jnp.arange(0, dim, 2, dtype=jnp.float32) / dim)
    return pos.astype(jnp.float32)[:, None] * inv[None, :]


def _rotate(x, ang):
    cos = jnp.cos(ang)[:, None, :].astype(x.dtype)
    sin = jnp.sin(ang)[:, None, :].astype(x.dtype)
    x1, x2 = jnp.split(x, 2, axis=-1)
    return jnp.concatenate([x1 * cos - x2 * sin, x1 * sin + x2 * cos], axis=-1)


def _axial_rope(x, row_ang, col_ang):
    xr, xc = jnp.split(x, 2, axis=-1)
    return jnp.concatenate([_rotate(xr, row_ang), _rotate(xc, col_ang)], axis=-1)


def _bidir(scan_fn, s0, ctx_f, ctx_b, lat_f, lat_b):
    flip = lambda ts: [jnp.flip(t, axis=1) for t in ts]
    sc_f, yc_f = scan_fn(s0, *ctx_f)
    sc_b, yc_b = scan_fn(s0, *flip(ctx_b))
    _, y_f = scan_fn(sc_f, *lat_f)
    _, y_b = scan_fn(sc_b, *flip(lat_b))
    return y_f, jnp.flip(y_b, axis=1), yc_f, jnp.flip(yc_b, axis=1)


def _retention_scan(s0, q, k, v, gamma):
    b, l, h, _ = q.shape
    dv = v.shape[-1]
    n = l // RET_CHUNK
    idx = jnp.arange(RET_CHUNK, dtype=jnp.float32)
    lg = jnp.log(gamma)
    diff = idx[:, None] - idx[None, :]
    dmat = jnp.where(diff >= 0, jnp.exp(lg[:, None, None] * jnp.maximum(diff, 0.0)), 0.0)
    q_dec = jnp.exp((idx + 1.0)[:, None] * lg[None, :])
    k_dec = jnp.exp((RET_CHUNK - 1.0 - idx)[:, None] * lg[None, :])
    c_dec = jnp.exp(RET_CHUNK * lg)

    def chunks(t):
        return t.reshape(b, n, RET_CHUNK, h, t.shape[-1]).swapaxes(0, 1)

    def step(s, inp):
        qb, kb, vb = inp
        att = jnp.einsum('bchd,bshd->bhcs', qb, kb) * dmat
        o = jnp.einsum('bhcs,bshv->bchv', att, vb)
        o = o + jnp.einsum('bchd,bhdv->bchv', qb, s) * q_dec[None, :, :, None]
        s = s * c_dec[None, :, None, None] + jnp.einsum('bshd,bshv->bhdv', kb * k_dec[None, :, :, None], vb)
        return s, o

    s, o = lax.scan(step, s0, (chunks(q), chunks(k), chunks(v)))
    return s, o.swapaxes(0, 1).reshape(b, l, h, dv)


def _retention_branch(p, pc, ang, gamma, need_ctx):
    def prep(t, rotate):
        b, l, _ = t.shape
        q, k, v, gf, gb = _split_cols(t, RET_SEGS)
        q = q.reshape(b, l, RET_HEADS, RET_DK)
        k = k.reshape(b, l, RET_HEADS, RET_DK)
        if rotate:
            q, k = _rotate(q, ang), _rotate(k, ang)
        qkv = (q.astype(jnp.float32), k.astype(jnp.float32) * RET_DK ** -0.5,
               v.reshape(b, l, RET_HEADS, RET_DV).astype(jnp.float32))
        return qkv, gf, gb

    lat, gf, gb = prep(p, True)
    ctx, gfc, gbc = prep(pc, False)
    s0 = jnp.zeros((p.shape[0], RET_HEADS, RET_DK, RET_DV), jnp.float32)
    scan = lambda s, q, k, v: _retention_scan(s, q, k, v, gamma)
    o_f, o_b, oc_f, oc_b = _bidir(scan, s0, ctx, ctx, lat, lat)

    def finish(o_f, o_b, gf, gb):
        b, l = gf.shape[:2]
        swish = lambda g: jax.nn.silu(g.astype(jnp.float32)).reshape(b, l, RET_HEADS, RET_DV)
        y = (_layernorm_last(o_f, HEAD_NORM_EPS) * swish(gf)
             + _layernorm_last(o_b, HEAD_NORM_EPS) * swish(gb))
        return y.reshape(b, l, RET_W).astype(p.dtype)

    out = finish(o_f, o_b, gf, gb)
    out_c = finish(oc_f, oc_b, gfc, gbc) if need_ctx else None
    return out, out_c


def _softmax_attend(q, k, v):
    s = jnp.einsum('bqhd,bkhd->bhqk', q, k, preferred_element_type=jnp.float32) * MLA_QK ** -0.5
    p = jax.nn.softmax(s, axis=-1).astype(v.dtype)
    return jnp.einsum('bhqk,bkhd->bqhd', p, v)


def _mla_branch(p, pc, row_ang, col_ang, q_norm_g, w_uq, kv_norm_g, w_ukv, qk_q_g, qk_k_g, need_ctx):
    def prep(t, rotate):
        b, l, _ = t.shape
        cq, ckv, kpe = _split_cols(t, MLA_SEGS)
        q = (_rmsnorm(cq, q_norm_g) @ w_uq).reshape(b, l, MLA_HEADS, MLA_QK)
        kv = (_rmsnorm(ckv, kv_norm_g) @ w_ukv).reshape(b, l, MLA_HEADS, MLA_NOPE + MLA_DV)
        k_nope, v = jnp.split(kv, [MLA_NOPE], axis=-1)
        k_pe = jnp.broadcast_to(kpe[:, :, None, :], (b, l, MLA_HEADS, MLA_ROPE))
        q = _rmsnorm(q, qk_q_g)
        k = _rmsnorm(jnp.concatenate([k_nope, k_pe], axis=-1), qk_k_g)
        if rotate:
            q = jnp.concatenate([q[..., :MLA_NOPE], _axial_rope(q[..., MLA_NOPE:], row_ang, col_ang)], axis=-1)
            k = jnp.concatenate([k[..., :MLA_NOPE], _axial_rope(k[..., MLA_NOPE:], row_ang, col_ang)], axis=-1)
        return q, k, v

    q, k, v = prep(p, True)
    qc, kc, vc = prep(pc, False)
    b, s = p.shape[:2]
    k_all = jnp.concatenate([kc, k], axis=1)
    v_all = jnp.concatenate([vc, v], axis=1)
    qb = q.reshape(b, s // ATTN_BLOCK, ATTN_BLOCK, MLA_HEADS, MLA_QK).swapaxes(0, 1)
    o = lax.map(lambda qq: _softmax_attend(qq, k_all, v_all), qb)
    out = o.swapaxes(0, 1).reshape(b, s, MLA_W)
    out_c = _softmax_attend(qc, kc, vc).reshape(b, pc.shape[1], MLA_W) if need_ctx else None
    return out, out_c


def _token_shift(t, mu_prev, mu_next):
    prev = jnp.pad(t, ((0, 0), (1, 0), (0, 0)))[:, :-1]
    nxt = jnp.pad(t, ((0, 0), (0, 1), (0, 0)))[:, 1:]
    return t + mu_prev * (prev - t) + mu_next * (nxt - t)


def _rwkv_scan(s0, r, decay, k, v, kk, a):
    def step(s, inp):
        rt, wt, kt, vt, kkt, at = inp
        sa = jnp.einsum('bhij,bhj->bhi', s, -kkt)
        s = (s * wt[:, :, None, :] + sa[..., None] * (kkt * at)[:, :, None, :]
             + vt[..., None] * kt[:, :, None, :])
        return s, jnp.einsum('bhij,bhj->bhi', s, rt)

    s, y = lax.scan(step, s0, tuple(t.swapaxes(0, 1) for t in (r, decay, k, v, kk, a)))
    return s, y.swapaxes(0, 1)


def _rwkv_branch(p, pc, mu_prev, mu_next, w0, w2, a0, a2, g2, k_k, k_a, r_k, lnx_g, lnx_b, need_ctx):
    def prep(t):
        b, l, _ = t.shape
        t = _token_shift(t, mu_prev, mu_next).astype(jnp.float32)
        r, k, v, whf, whb, ahf, ahb, gh = _split_cols(t, RWKV_SEGS)
        heads = lambda z: z.reshape(b, l, RWKV_HEADS, RWKV_HD)
        kk = heads(k * k_k)
        kk = kk * lax.rsqrt(jnp.maximum(jnp.sum(kk * kk, axis=-1, keepdims=True), 1e-24))
        dirs = []
        for d, (wh, ah) in enumerate(((whf, ahf), (whb, ahb))):
            w = -jax.nn.softplus(-(w0[d] + jnp.tanh(wh) @ w2[d])) - 0.5
            a = jax.nn.sigmoid(a0[d] + ah @ a2[d])
            kd = k * (1.0 + (a - 1.0) * k_a)
            dirs.append((heads(jnp.exp(-jnp.exp(w))), heads(kd), heads(a)))
        g = jax.nn.sigmoid(gh) @ g2
        return heads(r), heads(v), kk, dirs, g

    r, v, kk, dirs, g = prep(p)
    rc, vc, kkc, dirsc, gc = prep(pc)
    lat = [(r, dw, dk, v, kk, da) for dw, dk, da in dirs]
    ctx = [(rc, dw, dk, vc, kkc, da) for dw, dk, da in dirsc]
    s0 = jnp.zeros((p.shape[0], RWKV_HEADS, RWKV_HD, RWKV_HD), jnp.float32)
    y_f, y_b, yc_f, yc_b = _bidir(_rwkv_scan, s0, ctx[0], ctx[1], lat[0], lat[1])
    gn_g = lnx_g.reshape(RWKV_HEADS, RWKV_HD)
    gn_b = lnx_b.reshape(RWKV_HEADS, RWKV_HD)

    def finish(y_f, y_b, r, v, dirs, g):
        b, l = g.shape[:2]
        y = _layernorm_last(y_f + y_b, RWKV_GN_EPS) * gn_g + gn_b
        for _, kd, _ in dirs:
            y = y + jnp.sum(r * kd * r_k, axis=-1, keepdims=True) * v
        return (y.reshape(b, l, RWKV_W) * g).astype(p.dtype)

    out = finish(y_f, y_b, r, v, dirs, g)
    out_c = finish(yc_f, yc_b, rc, vc, dirsc, gc) if need_ctx else None
    return out, out_c


def _mixer(h, hc, row_ang, col_ang, ret_ang, gamma, w_in,
           mla_q_norm_g, mla_w_uq, mla_kv_norm_g, mla_w_ukv, mla_qk_q_g, mla_qk_k_g,
           rwkv_mu_prev, rwkv_mu_next, rwkv_w0, rwkv_w2, rwkv_a0, rwkv_a2, rwkv_g2,
           rwkv_k_k, rwkv_k_a, rwkv_r_k, rwkv_lnx_g, rwkv_lnx_b,
           w_br_ret, w_br_mla, w_br_rwkv, w_out, need_ctx):
    segs = (RET_IN, MLA_IN, RWKV_IN, 3 * h.shape[-1])
    p_ret, p_mla, p_rwkv, p_gate = _split_cols(h @ w_in, segs)
    c_ret, c_mla, c_rwkv, c_gate = _split_cols(hc @ w_in, segs)
    ret, ret_c = _retention_branch(p_ret, c_ret, ret_ang, gamma, need_ctx)
    mla, mla_c = _mla_branch(p_mla, c_mla, row_ang, col_ang, mla_q_norm_g, mla_w_uq, mla_kv_norm_g,
                             mla_w_ukv, mla_qk_q_g, mla_qk_k_g, need_ctx)
    rwk, rwk_c = _rwkv_branch(p_rwkv, c_rwkv, rwkv_mu_prev, rwkv_mu_next, rwkv_w0, rwkv_w2, rwkv_a0,
                              rwkv_a2, rwkv_g2, rwkv_k_k, rwkv_k_a, rwkv_r_k, rwkv_lnx_g, rwkv_lnx_b,
                              need_ctx)

    def merge(a, m, r, gate_logits):
        g_a, g_m, g_r = jnp.split(jax.nn.sigmoid(gate_logits), 3, axis=-1)
        return (g_a * (a @ w_br_ret) + g_m * (m @ w_br_mla) + g_r * (r @ w_br_rwkv)) @ w_out

    out = merge(ret, mla, rwk, p_gate)
    out_c = merge(ret_c, mla_c, rwk_c, c_gate) if need_ctx else None
    return out, out_c


def _moe(h, w_router, b_router, w_gu, b_gu, w_down, b_down):
    n, d = h.shape
    logits = jnp.einsum('nd,de->ne', h, w_router, preferred_element_type=jnp.float32) + b_router.astype(jnp.float32)
    top_val, top_idx = lax.top_k(logits, TOP_K)
    gate = jax.nn.softmax(top_val, axis=-1)
    nk = n * TOP_K
    flat_e = top_idx.reshape(-1)
    flat_tok = jnp.arange(nk, dtype=jnp.int32) // TOP_K
    order = jnp.argsort(flat_e)
    e_sorted = flat_e[order]
    counts = jnp.bincount(flat_e, length=N_EXPERTS)
    padded = (counts + MOE_BLOCK - 1) // MOE_BLOCK * MOE_BLOCK
    pad_end = jnp.cumsum(padded)
    start = jnp.cumsum(counts) - counts
    dest = (pad_end - padded)[e_sorted] + jnp.arange(nk, dtype=jnp.int32) - start[e_sorted]
    n_blocks = -(-nk // MOE_BLOCK) + N_EXPERTS
    tok_buf = jnp.full((n_blocks * MOE_BLOCK,), n, jnp.int32).at[dest].set(flat_tok[order])
    gate_buf = jnp.zeros((n_blocks * MOE_BLOCK,), jnp.float32).at[dest].set(gate.reshape(-1)[order])
    block_expert = jnp.minimum(
        jnp.searchsorted(pad_end // MOE_BLOCK, jnp.arange(n_blocks), side='right'), N_EXPERTS - 1)
    h_pad = jnp.concatenate([h, jnp.zeros((1, d), h.dtype)], axis=0)

    def expert_block(args):
        tok, e = args
        gu = h_pad[tok] @ w_gu[e] + b_gu[e]
        g_lin, up = jnp.split(gu, 2, axis=-1)
        g_lin = jnp.minimum(g_lin, SWIGLU_LIMIT)
        up = jnp.clip(up, -SWIGLU_LIMIT, SWIGLU_LIMIT)
        act = g_lin * jax.nn.sigmoid(SWIGLU_ALPHA * g_lin) * (up + 1.0)
        return act @ w_down[e] + b_down[e]

    y = lax.map(expert_block, (tok_buf.reshape(n_blocks, MOE_BLOCK), block_expert))
    y = y.reshape(-1, d) * gate_buf[:, None].astype(y.dtype)
    return jax.ops.segment_sum(y, tok_buf, num_segments=n + 1)[:n]


def setup_inputs(seed: int = 0) -> dict:
    key = jax.random.key(seed)
    keys = list(jax.random.split(key, 64))

    def nrm(shape, scale):
        return jax.random.normal(keys.pop(), shape, jnp.float32) * scale

    def uni(shape, lo, hi):
        return jax.random.uniform(keys.pop(), shape, jnp.float32, lo, hi)

    L, D, E, F = DEPTH, D_MODEL, N_EXPERTS, D_EXPERT
    return {
        'x': nrm((BATCH, SEQ, D), 1.0),
        'c': nrm((BATCH, D), 1.0),
        'ctx': nrm((BATCH, CTX_LEN, D), 1.0),
        'c_ctx': nrm((D,), 1.0),
        'ada_w': nrm((L, D, 6 * D), 0.5 * D ** -0.5),
        'ada_b': nrm((L, 6 * D), 0.01),
        'norm1_g': 1.0 + nrm((L, D), 0.02),
        'norm2_g': 1.0 + nrm((L, D), 0.02),
        'w_in': nrm((L, D, D_IN), D ** -0.5),
        'mla_q_norm_g': 1.0 + nrm((L, MLA_Q_RANK), 0.02),
        'mla_w_uq': nrm((L, MLA_Q_RANK, MLA_HEADS * MLA_QK), MLA_Q_RANK ** -0.5),
        'mla_kv_norm_g': 1.0 + nrm((L, MLA_KV_RANK), 0.02),
        'mla_w_ukv': nrm((L, MLA_KV_RANK, MLA_HEADS * (MLA_NOPE + MLA_DV)), MLA_KV_RANK ** -0.5),
        'mla_qk_q_g': 1.0 + nrm((L, MLA_QK), 0.02),
        'mla_qk_k_g': 1.0 + nrm((L, MLA_QK), 0.02),
        'rwkv_mu_prev': uni((L, RWKV_IN), 0.0, 0.5),
        'rwkv_mu_next': uni((L, RWKV_IN), 0.0, 0.5),
        'rwkv_w0': uni((L, 2, RWKV_W), -6.0, -1.0),
        'rwkv_w2': nrm((L, 2, RWKV_DECAY_LORA, RWKV_W), 0.5 * RWKV_DECAY_LORA ** -0.5),
        'rwkv_a0': nrm((L, 2, RWKV_W), 0.1),
        'rwkv_a2': nrm((L, 2, RWKV_A_LORA, RWKV_W), 0.5 * RWKV_A_LORA ** -0.5),
        'rwkv_g2': nrm((L, RWKV_GATE_LORA, RWKV_W), RWKV_GATE_LORA ** -0.5),
        'rwkv_k_k': 1.0 + nrm((L, RWKV_W), 0.1),
        'rwkv_k_a': 1.0 + nrm((L, RWKV_W), 0.1),
        'rwkv_r_k': nrm((L, RWKV_HEADS, RWKV_HD), 0.1),
        'rwkv_lnx_g': 1.0 + nrm((L, RWKV_W), 0.02),
        'rwkv_lnx_b': nrm((L, RWKV_W), 0.01),
        'w_br_ret': nrm((L, RET_W, D), RET_W ** -0.5),
        'w_br_mla': nrm((L, MLA_W, D), MLA_W ** -0.5),
        'w_br_rwkv': nrm((L, RWKV_W, D), RWKV_W ** -0.5),
        'w_out': nrm((L, D, D), D ** -0.5),
        'w_router': nrm((L, D, E), D ** -0.5),
        'b_router': nrm((L, E), 0.01),
        'w_gu': nrm((L, E, D, 2 * F), D ** -0.5),
        'b_gu': nrm((L, E, 2 * F), 0.01),
        'w_down': nrm((L, E, F, D), F ** -0.5),
        'b_down': nrm((L, E, D), 0.01),
    }


def reference(x, c, ctx, c_ctx, ada_w, ada_b, norm1_g, norm2_g, w_in,
              mla_q_norm_g, mla_w_uq, mla_kv_norm_g, mla_w_ukv, mla_qk_q_g, mla_qk_k_g,
              rwkv_mu_prev, rwkv_mu_next, rwkv_w0, rwkv_w2, rwkv_a0, rwkv_a2, rwkv_g2,
              rwkv_k_k, rwkv_k_a, rwkv_r_k, rwkv_lnx_g, rwkv_lnx_b,
              w_br_ret, w_br_mla, w_br_rwkv, w_out,
              w_router, b_router, w_gu, b_gu, w_down, b_down):
    b, s, d = x.shape
    rows = s // GRID_W
    row = jnp.repeat(jnp.arange(rows, dtype=jnp.int32), GRID_W)
    col = jnp.tile(jnp.arange(GRID_W, dtype=jnp.int32), rows)
    row_ang = _rope_angles(row, MLA_ROPE // 2)
    col_ang = _rope_angles(col, MLA_ROPE // 2)
    ret_ang = _rope_angles(jnp.arange(s, dtype=jnp.int32), RET_DK)
    gamma = 1.0 - jnp.exp2(-5.0 - jnp.arange(RET_HEADS, dtype=jnp.float32))

    for l in range(DEPTH):
        last = l == DEPTH - 1
        mod = jax.nn.silu(c) @ ada_w[l] + ada_b[l]
        sh1, sc1, g1, sh2, sc2, g2 = jnp.split(mod[:, None, :], 6, axis=-1)
        csh1, csc1, cg1, csh2, csc2, cg2 = jnp.split(jax.nn.silu(c_ctx) @ ada_w[l] + ada_b[l], 6)
        h = _rmsnorm(x, norm1_g[l]) * (1.0 + sc1) + sh1
        hc = _rmsnorm(ctx, norm1_g[l]) * (1.0 + csc1) + csh1
        mix, mix_c = _mixer(h, hc, row_ang, col_ang, ret_ang, gamma, w_in[l],
                            mla_q_norm_g[l], mla_w_uq[l], mla_kv_norm_g[l], mla_w_ukv[l],
                            mla_qk_q_g[l], mla_qk_k_g[l],
                            rwkv_mu_prev[l], rwkv_mu_next[l], rwkv_w0[l], rwkv_w2[l], rwkv_a0[l],
                            rwkv_a2[l], rwkv_g2[l], rwkv_k_k[l], rwkv_k_a[l], rwkv_r_k[l],
                            rwkv_lnx_g[l], rwkv_lnx_b[l],
                            w_br_ret[l], w_br_mla[l], w_br_rwkv[l], w_out[l], not last)
        x = x + g1 * mix
        h2 = _rmsnorm(x, norm2_g[l]) * (1.0 + sc2) + sh2
        moe_args = (w_router[l], b_router[l], w_gu[l], b_gu[l], w_down[l], b_down[l])
        if last:
            x = x + g2 * _moe(h2.reshape(b * s, d), *moe_args).reshape(b, s, d)
        else:
            ctx = ctx + cg1 * mix_c
            hc2 = _rmsnorm(ctx, norm2_g[l]) * (1.0 + csc2) + csh2
            f = _moe(jnp.concatenate([h2.reshape(b * s, d), hc2.reshape(-1, d)], axis=0), *moe_args)
            x = x + g2 * f[: b * s].reshape(b, s, d)
            ctx = ctx + cg2 * f[b * s:].reshape(ctx.shape)
    return x
```

```python
import functools

import jax
import jax.numpy as jnp
import numpy as np
from jax import lax
from jax.experimental import pallas as pl
from jax.experimental.pallas import tpu as pltpu

F32 = jnp.float32
BF16 = jnp.bfloat16

D_MODEL = 1024
GRID_W = 64
RET_HEADS, RET_DK, RET_DV, RET_CHUNK = 4, 64, 128, 128
RET_W = RET_HEADS * RET_DV
MLA_HEADS, MLA_Q_RANK, MLA_KV_RANK, MLA_NOPE, MLA_ROPE, MLA_DV = 8, 256, 128, 64, 32, 64
MLA_QK = MLA_NOPE + MLA_ROPE
MLA_W = MLA_HEADS * MLA_DV
RWKV_HEADS, RWKV_HD = 8, 64
RWKV_W = RWKV_HEADS * RWKV_HD
RWKV_DECAY_LORA, RWKV_A_LORA, RWKV_GATE_LORA = 64, 64, 128
RWKV_GN_EPS = 64e-5
RWKV_IN = 3 * RWKV_W + 2 * RWKV_DECAY_LORA + 2 * RWKV_A_LORA + RWKV_GATE_LORA
N_EXPERTS, TOP_K = 32, 4
SWIGLU_LIMIT, SWIGLU_ALPHA = 7.0, 1.702
ROPE_BASE = 10000.0
NORM_EPS = 1e-6
HEAD_NORM_EPS = 1e-5

GATE_OFF = 0
RET_OFF = 3 * D_MODEL
MLA_OFF = RET_OFF + 2048
RWKV_OFF = MLA_OFF + 512 + 128
P_COLS = RWKV_OFF + RWKV_IN
IN_COL_TILE = 1920

ROW_TILE = 256
RWKV_CHUNK = 64
HALF = 256
MOE_SUB = 128
MOE_TILE = 256
VMEM_LIMIT = 48 * 1024 * 1024

NT_DIMS = (((1,), (1,)), ((), ()))
TN_DIMS = (((0,), (0,)), ((), ()))


def _params(*sem):
    return pltpu.CompilerParams(dimension_semantics=sem, vmem_limit_bytes=VMEM_LIMIT)


def _dot(a, b):
    return jnp.dot(a, b, preferred_element_type=F32)


def _dot_nt(a, b):
    return lax.dot_general(a, b, NT_DIMS, preferred_element_type=F32)


def _dot_tn(a, b):
    return lax.dot_general(a, b, TN_DIMS, preferred_element_type=F32)


def _split3(x):
    hi = x.astype(BF16)
    r1 = x - hi.astype(F32)
    mid = r1.astype(BF16)
    lo = (r1 - mid.astype(F32)).astype(BF16)
    return hi, mid, lo


def _dot_exact_rhs01(x, m01):
    hi, mid, lo = _split3(x)
    return _dot(hi, m01) + _dot(mid, m01) + _dot(lo, m01)


def _dot_exact_lhs01(m01, x):
    hi, mid, lo = _split3(x)
    return _dot(m01, hi) + _dot(m01, mid) + _dot(m01, lo)


def _sigmoid(x):
    return 1.0 / (1.0 + jnp.exp(-x))


def _silu(x):
    return x * _sigmoid(x)


def _row_to_col(row, n):
    eye = lax.broadcasted_iota(jnp.int32, (n, n), 0) == lax.broadcasted_iota(jnp.int32, (n, n), 1)
    return jnp.sum(jnp.where(eye, jnp.broadcast_to(row, (n, n)), 0.0), axis=1, keepdims=True)


def _ada_kernel(c_ref, w_ref, b_ref, o_ref):
    s = _silu(c_ref[...])
    o_ref[...] = _dot(s.astype(BF16), w_ref[...]) + b_ref[...]


def _ada_all(cc, ada_w_bf, ada_b):
    depth = ada_w_bf.shape[0]
    tn = 1536
    return pl.pallas_call(
        _ada_kernel,
        grid=(depth, 6 * D_MODEL // tn),
        in_specs=[
            pl.BlockSpec((16, D_MODEL), lambda l, j: (0, 0)),
            pl.BlockSpec((None, D_MODEL, tn), lambda l, j: (l, 0, j)),
            pl.BlockSpec((None, 1, tn), lambda l, j: (l, 0, j)),
        ],
        out_specs=pl.BlockSpec((None, 16, tn), lambda l, j: (l, 0, j)),
        out_shape=jax.ShapeDtypeStruct((depth, 16, 6 * D_MODEL), F32),
        compiler_params=_params("arbitrary", "arbitrary"),
        name="ada_mod",
    )(cc, ada_w_bf, ada_b.reshape(depth, 1, 6 * D_MODEL))


def _mod_row(i, tiles_per_b, ctx_tiles, batch):
    return jnp.where(i % tiles_per_b < ctx_tiles, batch, i // tiles_per_b)


def _inproj_kernel(x_ref, mod_ref, g_ref, w_ref, o_ref):
    x = x_ref[...]
    y = x * lax.rsqrt(jnp.mean(x * x, axis=-1, keepdims=True) + NORM_EPS) * g_ref[...]
    h = y * (1.0 + mod_ref[1:2, :]) + mod_ref[0:1, :]
    o_ref[...] = _dot(h.astype(BF16), w_ref[...])


def _inproj(x2, mod_l, g, w_pad, geom):
    t = x2.shape[0]
    tpb, ctx_tiles, batch = geom
    row = functools.partial(_mod_row, tiles_per_b=tpb, ctx_tiles=ctx_tiles, batch=batch)
    return pl.pallas_call(
        _inproj_kernel,
        grid=(P_COLS // IN_COL_TILE, t // ROW_TILE),
        in_specs=[
            pl.BlockSpec((ROW_TILE, D_MODEL), lambda j, i: (i, 0)),
            pl.BlockSpec((None, 6, D_MODEL), lambda j, i: (row(i), 0, 0)),
            pl.BlockSpec((1, D_MODEL), lambda j, i: (0, 0)),
            pl.BlockSpec((D_MODEL, IN_COL_TILE), lambda j, i: (0, j)),
        ],
        out_specs=pl.BlockSpec((ROW_TILE, IN_COL_TILE), lambda j, i: (i, j)),
        out_shape=jax.ShapeDtypeStruct((t, P_COLS), F32),
        compiler_params=_params("arbitrary", "arbitrary"),
        name="in_proj",
    )(x2, mod_l, g.reshape(1, D_MODEL), w_pad)


def _pad_w_in(w_in):
    d = w_in.shape[0]
    ret = w_in[:, 0:2048]
    cq = w_in[:, 2048:2304]
    ckv = w_in[:, 2304:2432]
    kpe = w_in[:, 2432:2464]
    rwkv = w_in[:, 2464:2464 + RWKV_IN]
    gate = w_in[:, 2464 + RWKV_IN:]
    z = lambda n: jnp.zeros((d, n), w_in.dtype)
    return jnp.concatenate([gate, ret, cq, ckv, z(64), kpe, z(32), z(128), rwkv], axis=1).astype(BF16)


def _scan_chunk(d, j, n_ctx, n_all):
    bwd = jnp.where(j < n_ctx, n_ctx - 1 - j, n_all - 1 - j + n_ctx)
    return jnp.where(d == 0, j, bwd)


def _ret_kernel(q_ref, k_ref, v_ref, g_ref, cos_ref, sin_ref, dmat_ref, qdec_ref, kdec_ref,
                o_ref, s_ref, *, cdec):
    @pl.when(pl.program_id(2) == 0)
    def _():
        s_ref[...] = jnp.zeros_like(s_ref)

    c = RET_CHUNK
    w = RET_HEADS * RET_DK
    cos = cos_ref[...]
    sin = sin_ref[...]
    first = (lax.broadcasted_iota(jnp.int32, (c, w), 1) & (RET_DK - 1)) < RET_DK // 2

    def rot(x):
        partner = jnp.where(first, pltpu.roll(x, w - RET_DK // 2, 1), pltpu.roll(x, RET_DK // 2, 1))
        return x * cos + partner * sin

    q = rot(q_ref[...])
    k = rot(k_ref[...]) * (RET_DK ** -0.5)
    kd_t = (k * kdec_ref[...]).T.astype(BF16)
    qb = q.astype(BF16)
    kb = k.astype(BF16)
    vb = v_ref[...].astype(BF16)
    g = g_ref[...]
    for h in range(RET_HEADS):
        qh = qb[:, h * RET_DK:(h + 1) * RET_DK]
        kh = kb[:, h * RET_DK:(h + 1) * RET_DK]
        vh = vb[:, h * RET_DV:(h + 1) * RET_DV]
        s_h = s_ref[h]
        att = _dot_nt(qh, kh) * dmat_ref[h]
        o = _dot(att.astype(BF16), vh) + _dot(qh, s_h.astype(BF16)) * qdec_ref[h]
        s_ref[h] = s_h * cdec[h] + _dot(kd_t[h * RET_DK:(h + 1) * RET_DK, :], vh)
        oc = o - jnp.mean(o, axis=-1, keepdims=True)
        y = oc * lax.rsqrt(jnp.mean(oc * oc, axis=-1, keepdims=True) + HEAD_NORM_EPS)
        o_ref[:, h * RET_DV:(h + 1) * RET_DV] = y * _silu(g[:, h * RET_DV:(h + 1) * RET_DV])


def _retention(p3, cos_t, sin_t, consts, ctx_len):
    b, l, _ = p3.shape
    n_all = l // RET_CHUNK
    n_ctx = ctx_len // RET_CHUNK
    dmat, qdec, kdec, cdec = consts
    ch = functools.partial(_scan_chunk, n_ctx=n_ctx, n_all=n_all)
    qb, vb = RET_OFF // 256, RET_OFF // 512
    return pl.pallas_call(
        functools.partial(_ret_kernel, cdec=cdec),
        grid=(b, 2, n_all),
        in_specs=[
            pl.BlockSpec((None, RET_CHUNK, 256), lambda bi, d, j: (bi, ch(d, j), qb)),
            pl.BlockSpec((None, RET_CHUNK, 256), lambda bi, d, j: (bi, ch(d, j), qb + 1)),
            pl.BlockSpec((None, RET_CHUNK, 512), lambda bi, d, j: (bi, ch(d, j), vb + 1)),
            pl.BlockSpec((None, RET_CHUNK, 512), lambda bi, d, j: (bi, ch(d, j), vb + 2 + d)),
            pl.BlockSpec((RET_CHUNK, 256), lambda bi, d, j: (ch(d, j), 0)),
            pl.BlockSpec((RET_CHUNK, 256), lambda bi, d, j: (ch(d, j), 0)),
            pl.BlockSpec((None, RET_HEADS, RET_CHUNK, RET_CHUNK), lambda bi, d, j: (d, 0, 0, 0)),
            pl.BlockSpec((None, RET_HEADS, RET_CHUNK, RET_DV), lambda bi, d, j: (d, 0, 0, 0)),
            pl.BlockSpec((None, RET_CHUNK, 256), lambda bi, d, j: (d, 0, 0)),
        ],
        out_specs=pl.BlockSpec((None, None, RET_CHUNK, RET_W), lambda bi, d, j: (d, bi, ch(d, j), 0)),
        out_shape=jax.ShapeDtypeStruct((2, b, l, RET_W), F32),
        scratch_shapes=[pltpu.VMEM((RET_HEADS, RET_DK, RET_DV), F32)],
        compiler_params=_params("arbitrary", "arbitrary", "arbitrary"),
        name="retention",
    )(p3, p3, p3, p3, cos_t, sin_t, dmat, qdec, kdec)


def _retention_consts():
    gamma = 1.0 - jnp.exp2(-5.0 - jnp.arange(RET_HEADS, dtype=F32))
    lg = jnp.log(gamma)
    idx = jnp.arange(RET_CHUNK, dtype=F32)
    diff = idx[:, None] - idx[None, :]
    lower = jnp.where(diff >= 0, jnp.exp(lg[:, None, None] * jnp.maximum(diff, 0.0)), 0.0)
    dmat = jnp.stack([lower, jnp.swapaxes(lower, 1, 2)])
    qd_f = jnp.exp((idx + 1.0)[None, :] * lg[:, None])
    qd_b = jnp.exp((RET_CHUNK - idx)[None, :] * lg[:, None])
    qdec = jnp.broadcast_to(jnp.stack([qd_f, qd_b])[..., None], (2, RET_HEADS, RET_CHUNK, RET_DV))
    kd_f = jnp.exp((RET_CHUNK - 1.0 - idx)[None, :] * lg[:, None])
    kd_b = jnp.exp(idx[None, :] * lg[:, None])
    kdec = jnp.repeat(jnp.swapaxes(jnp.stack([kd_f, kd_b]), 1, 2), RET_DK, axis=2)
    gam64 = 1.0 - np.exp2(-5.0 - np.arange(RET_HEADS))
    cdec = tuple(float(np.float32(np.exp(RET_CHUNK * np.log(np.float32(gv))))) for gv in gam64)
    return dmat, qdec, kdec, cdec


def _mla_prep_kernel(p_ref, qng_ref, wuq_ref, kvg_ref, wuk_ref, wuv_ref, qg_ref, kg_ref,
                     cos_ref, sin_ref, q_out, k_out, v_out):
    blk = p_ref[...]
    cq = blk[:, 0:MLA_Q_RANK]
    ckv = blk[:, MLA_Q_RANK:MLA_Q_RANK + MLA_KV_RANK]
    kpe = blk[:, MLA_Q_RANK + MLA_KV_RANK:]

    def rms(x, g, n):
        return x * lax.rsqrt(jnp.sum(x * x, axis=-1, keepdims=True) * (1.0 / n) + NORM_EPS) * g

    qn = rms(cq, qng_ref[...], MLA_Q_RANK).astype(BF16)
    kvn = rms(ckv, kvg_ref[...], MLA_KV_RANK).astype(BF16)
    q_raw = _dot(qn, wuq_ref[...])
    k_raw = _dot(kvn, wuk_ref[...])
    v_out[...] = _dot(kvn, wuv_ref[...]).astype(BF16)
    cos = cos_ref[...]
    sin = sin_ref[...]
    tm = blk.shape[0]
    first = (lax.broadcasted_iota(jnp.int32, (tm, 128), 1) & 15) < 8

    def rope(x):
        partner = jnp.where(first, pltpu.roll(x, 120, 1), pltpu.roll(x, 8, 1))
        return x * cos + partner * sin

    for h in range(MLA_HEADS):
        sl = slice(h * 128, (h + 1) * 128)
        q_out[:, sl] = rope(rms(q_raw[:, sl], qg_ref[...], MLA_QK)).astype(BF16)
        k_out[:, sl] = rope(rms(k_raw[:, sl] + kpe, kg_ref[...], MLA_QK)).astype(BF16)


def _mla_prep(p2, w, cos_t, sin_t, rows_per_b):
    t = p2.shape[0]
    tiles_per_b = rows_per_b // ROW_TILE
    const = lambda shape: pl.BlockSpec(shape, lambda i: (0,) * len(shape))
    return pl.pallas_call(
        _mla_prep_kernel,
        grid=(t // ROW_TILE,),
        in_specs=[
            pl.BlockSpec((ROW_TILE, 512), lambda i: (i, MLA_OFF // 512)),
            const((1, MLA_Q_RANK)), const((MLA_Q_RANK, 1024)),
            const((1, MLA_KV_RANK)), const((MLA_KV_RANK, 1024)), const((MLA_KV_RANK, MLA_W)),
            const((1, 128)), const((1, 128)),
            pl.BlockSpec((ROW_TILE, 128), lambda i: (i % tiles_per_b, 0)),
            pl.BlockSpec((ROW_TILE, 128), lambda i: (i % tiles_per_b, 0)),
        ],
        out_specs=[
            pl.BlockSpec((ROW_TILE, 1024), lambda i: (i, 0)),
            pl.BlockSpec((ROW_TILE, 1024), lambda i: (i, 0)),
            pl.BlockSpec((ROW_TILE, MLA_W), lambda i: (i, 0)),
        ],
        out_shape=[
            jax.ShapeDtypeStruct((t, 1024), BF16),
            jax.ShapeDtypeStruct((t, 1024), BF16),
            jax.ShapeDtypeStruct((t, MLA_W), BF16),
        ],
        compiler_params=_params("arbitrary"),
        name="mla_prep",
    )(p2, w["qng"], w["wuq"], w["kvg"], w["wuk"], w["wuv"], w["qg"], w["kg"], cos_t, sin_t)


def _mla_weights(q_norm_g, w_uq, kv_norm_g, w_ukv, qk_q_g, qk_k_g):
    wq = w_uq.reshape(MLA_Q_RANK, MLA_HEADS, MLA_QK)
    wq = jnp.pad(wq, ((0, 0), (0, 0), (0, 128 - MLA_QK))).reshape(MLA_Q_RANK, 1024)
    wkv = w_ukv.reshape(MLA_KV_RANK, MLA_HEADS, MLA_NOPE + MLA_DV)
    wk = jnp.pad(wkv[:, :, :MLA_NOPE], ((0, 0), (0, 0), (0, 128 - MLA_NOPE))).reshape(MLA_KV_RANK, 1024)
    wv = wkv[:, :, MLA_NOPE:].reshape(MLA_KV_RANK, MLA_W)
    padg = lambda g: jnp.pad(g, (0, 128 - MLA_QK)).reshape(1, 128)
    return dict(qng=q_norm_g.reshape(1, -1), wuq=wq.astype(BF16), kvg=kv_norm_g.reshape(1, -1),
                wuk=wk.astype(BF16), wuv=wv.astype(BF16), qg=padg(qk_q_g), kg=padg(qk_k_g))


def _attn_kernel(q_ref, k_ref, v_ref, o_ref, *, ctx_len, ctx_tiles):
    scale = MLA_QK ** -0.5

    def attend(kv_len):
        outs = []
        for hh in range(2):
            q = q_ref[:, hh * 128:(hh + 1) * 128]
            k = k_ref[0:kv_len, hh * 128:(hh + 1) * 128]
            v = v_ref[0:kv_len, hh * MLA_DV:(hh + 1) * MLA_DV]
            s = _dot_nt(q, k) * scale
            p = jnp.exp(s - jnp.max(s, axis=-1, keepdims=True))
            den = jnp.sum(p, axis=-1, keepdims=True)
            outs.append(_dot(p.astype(BF16), v) / den)
        o_ref[...] = jnp.concatenate(outs, axis=-1).astype(BF16)

    is_ctx = pl.program_id(2) < ctx_tiles

    @pl.when(is_ctx)
    def _():
        attend(ctx_len)

    @pl.when(jnp.logical_not(is_ctx))
    def _():
        attend(k_ref.shape[0])


def _attention(q3, k3, v3, ctx_len):
    b, l, _ = q3.shape
    tq = ROW_TILE
    return pl.pallas_call(
        functools.partial(_attn_kernel, ctx_len=ctx_len, ctx_tiles=ctx_len // tq),
        grid=(b, MLA_HEADS // 2, l // tq),
        in_specs=[
            pl.BlockSpec((None, tq, 256), lambda bi, h, i: (bi, i, h)),
            pl.BlockSpec((None, l, 256), lambda bi, h, i: (bi, 0, h)),
            pl.BlockSpec((None, l, 128), lambda bi, h, i: (bi, 0, h)),
        ],
        out_specs=pl.BlockSpec((None, tq, 128), lambda bi, h, i: (bi, i, h)),
        out_shape=jax.ShapeDtypeStruct((b, l, MLA_W), BF16),
        compiler_params=_params("arbitrary", "arbitrary", "arbitrary"),
        name="mla_attention",
    )(q3, k3, v3)


def _rwkv_elem_kernel(t_ref, prev_ref, next_ref, mup_ref, mun_ref, w0_ref, w2_ref, a0_ref, a2_ref,
                      g2_ref, kk_ref, ka_ref, rk_ref, blk_ref,
                      r_out, v_out, na_out, kd_out, bb_out, lw_out, g_out, bonus_out,
                      *, tiles_per_b, ctx_tiles):
    i = pl.program_id(0) % tiles_per_b
    starts = jnp.logical_or(i == 0, i == ctx_tiles)
    ends = jnp.logical_or(i == ctx_tiles - 1, i == tiles_per_b - 1)
    t = t_ref[...]
    tm = t.shape[0]
    prev_row = jnp.where(starts, 0.0, prev_ref[7:8, :])
    next_row = jnp.where(ends, 0.0, next_ref[0:1, :])
    rid = lax.broadcasted_iota(jnp.int32, t.shape, 0)
    prev = jnp.where(rid == 0, prev_row, pltpu.roll(t, 1, 0))
    nxt = jnp.where(rid == tm - 1, next_row, pltpu.roll(t, tm - 1, 0))
    t = t + mup_ref[...] * (prev - t) + mun_ref[...] * (nxt - t)

    w = RWKV_W
    r = t[:, 0:w]
    k = t[:, w:2 * w]
    v = t[:, 2 * w:3 * w]
    o = 3 * w
    wh = (t[:, o:o + 64], t[:, o + 64:o + 128])
    ah = (t[:, o + 128:o + 192], t[:, o + 192:o + 256])
    gh = t[:, o + 256:o + 384]
    blk = blk_ref[...]

    kk = k * kk_ref[...]
    kk = kk * lax.rsqrt(jnp.maximum(_dot_exact_rhs01(kk * kk, blk), 1e-24))
    r_out[...] = r
    v_out[...] = v
    na_out[...] = -kk
    g_out[...] = _dot(_sigmoid(gh).astype(BF16), g2_ref[...])
    rk = r * rk_ref[...]
    bonus = jnp.zeros_like(v)
    for d in range(2):
        z = w0_ref[d] + _dot(jnp.tanh(wh[d]).astype(BF16), w2_ref[d])
        nz = -z
        softplus = jnp.maximum(nz, 0.0) + jnp.log(1.0 + jnp.exp(-jnp.abs(nz)))
        lw_out[d] = -jnp.exp(-softplus - 0.5)
        a = _sigmoid(a0_ref[d] + _dot(ah[d].astype(BF16), a2_ref[d]))
        kd = k * (1.0 + (a - 1.0) * ka_ref[...])
        kd_out[d] = kd
        bb_out[d] = kk * a
        bonus = bonus + _dot_exact_rhs01(rk * kd, blk) * v
    bonus_out[...] = bonus


def _rwkv_elem(p2, w, rows_per_b, ctx_len):
    t = p2.shape[0]
    tm = ROW_TILE
    tiles_per_b = rows_per_b // tm
    n8 = t // 8
    cb = RWKV_OFF // RWKV_IN
    const = lambda shape: pl.BlockSpec(shape, lambda i: (0,) * len(shape))
    one = pl.BlockSpec((tm, RWKV_W), lambda i: (i, 0))
    two = pl.BlockSpec((2, tm, RWKV_W), lambda i: (0, i, 0))
    s1 = jax.ShapeDtypeStruct((t, RWKV_W), F32)
    s2 = jax.ShapeDtypeStruct((2, t, RWKV_W), F32)
    return pl.pallas_call(
        functools.partial(_rwkv_elem_kernel, tiles_per_b=tiles_per_b, ctx_tiles=ctx_len // tm),
        grid=(t // tm,),
        in_specs=[
            pl.BlockSpec((tm, RWKV_IN), lambda i: (i, cb)),
            pl.BlockSpec((8, RWKV_IN), lambda i: (jnp.maximum(i * (tm // 8) - 1, 0), cb)),
            pl.BlockSpec((8, RWKV_IN), lambda i: (jnp.minimum((i + 1) * (tm // 8), n8 - 1), cb)),
            const((1, RWKV_IN)), const((1, RWKV_IN)),
            const((2, 1, RWKV_W)), const((2, RWKV_DECAY_LORA, RWKV_W)),
            const((2, 1, RWKV_W)), const((2, RWKV_A_LORA, RWKV_W)),
            const((RWKV_GATE_LORA, RWKV_W)),
            const((1, RWKV_W)), const((1, RWKV_W)), const((1, RWKV_W)),
            const((RWKV_W, RWKV_W)),
        ],
        out_specs=[one, one, one, two, two, two, one, one],
        out_shape=[s1, s1, s1, s2, s2, s2, s1, s1],
        compiler_params=_params("arbitrary"),
        name="rwkv_elem",
    )(p2, p2, p2, w["mup"], w["mun"], w["w0"], w["w2"], w["a0"], w["a2"], w["g2"],
      w["kk"], w["ka"], w["rk"], w["blk"])


def _block_rows(x, bm):
    return jnp.where(bm, jnp.concatenate([x, x, x, x], axis=0), 0.0)


def _rwkv_chunk_kernel(r_ref, v_ref, na_ref, kd_ref, bb_ref, lw_ref, tri_ref, ms_ref, mi_ref,
                       o_ref, h_ref):
    @pl.when(pl.program_id(2) == 0)
    def _():
        h_ref[...] = jnp.zeros_like(h_ref)

    c = RWKV_CHUNK
    is_fwd = pl.program_id(1) == 0
    lw = lw_ref[...]
    cum = _dot_exact_lhs01(tri_ref[...], lw)
    tot = jnp.where(is_fwd, cum[c - 1:c, :], cum[0:1, :])
    e_pos = jnp.exp(cum)
    e_prev = jnp.exp(cum - lw)
    e_neg = jnp.exp(-cum)
    e_rel = jnp.exp(tot - cum)
    e_tot = jnp.exp(tot)
    kd = kd_ref[...]
    bb = bb_ref[...]
    a_t = na_ref[...] * e_prev
    r_t = r_ref[...] * e_pos
    b_t = bb * e_neg
    k_t = kd * e_neg
    b_p = bb * e_rel
    k_p = kd * e_rel
    v = v_ref[...]

    assert c == 64 and RWKV_HD == 64
    bm = (lax.broadcasted_iota(jnp.int32, (HALF, HALF), 0) >> 6
          == lax.broadcasted_iota(jnp.int32, (HALF, HALF), 1) >> 6)
    eye_p = ((lax.broadcasted_iota(jnp.int32, (c, HALF), 1) & (c - 1))
             == lax.broadcasted_iota(jnp.int32, (c, HALF), 0))
    eye_s = (lax.broadcasted_iota(jnp.int32, (HALF, HALF), 0)
             == lax.broadcasted_iota(jnp.int32, (HALF, HALF), 1))
    strict = ms_ref[...] > 0.5
    incl = mi_ref[...] > 0.5
    bf = lambda x: x.astype(BF16)

    for g in range(2):
        sl = slice(g * HALF, (g + 1) * HALF)
        a_h, r_h, v_h = a_t[:, sl], r_t[:, sl], v[:, sl]
        lhs = bf(jnp.concatenate([a_h, r_h], axis=0))
        rhs = bf(jnp.concatenate([_block_rows(b_t[:, sl], bm), _block_rows(k_t[:, sl], bm)], axis=0))
        big = _dot_nt(lhs, rhs)
        a_ab = jnp.where(strict, big[0:c, 0:HALF], 0.0)
        a_ak = jnp.where(strict, big[0:c, HALF:], 0.0)
        a_rb = jnp.where(incl, big[c:, 0:HALF], 0.0)
        a_rk = jnp.where(incl, big[c:, HALF:], 0.0)
        v_bd = bf(_block_rows(v_h, bm))
        akv = _dot(bf(a_ak), v_bd)
        t_inv = jnp.where(eye_p, 1.0, 0.0) + a_ab
        pw = _dot(bf(a_ab), bf(_block_rows(a_ab, bm)))
        for it in range(5):
            prod = _dot(bf(jnp.concatenate([t_inv, pw], axis=0)), bf(_block_rows(pw, bm)))
            t_inv = t_inv + prod[0:c]
            pw = prod[c:]
        wu = _dot(bf(t_inv), bf(jnp.concatenate([_block_rows(a_h, bm), _block_rows(akv, bm)], axis=1)))
        w_m, u0 = wu[:, 0:HALF], wu[:, HALF:]
        gy_y0 = _dot(bf(a_rb), bf(jnp.concatenate([_block_rows(w_m, bm), _block_rows(u0, bm)], axis=1)))
        gy = r_h + gy_y0[:, 0:HALF]
        y0 = gy_y0[:, HALF:] + _dot(bf(a_rk), v_bd)
        bp_h, kp_h = bf(b_p[:, sl]), bf(k_p[:, sl])
        f_bd = jnp.where(bm, _dot_tn(bp_h, bf(w_m)), 0.0) + jnp.where(eye_s, e_tot[:, sl], 0.0)
        g_bd = jnp.where(bm, _dot_tn(bp_h, bf(u0)) + _dot_tn(kp_h, bf(v_h)), 0.0)
        h_old = bf(h_ref[g])
        o_ref[:, sl] = _dot(bf(gy), h_old) + y0
        h_ref[g] = _dot(bf(f_bd), h_old) + g_bd


def _rwkv_chunk(e, b, l, ctx_len, consts):
    c = RWKV_CHUNK
    n_all, n_ctx = l // c, ctx_len // c
    ch = functools.partial(_scan_chunk, n_ctx=n_ctx, n_all=n_all)
    tri, ms, mi = consts
    r3, v3, na3 = (x.reshape(b, l, RWKV_W) for x in (e["r"], e["v"], e["na"]))
    kd4, bb4, lw4 = (x.reshape(2, b, l, RWKV_W) for x in (e["kd"], e["bb"], e["lw"]))
    one = pl.BlockSpec((None, c, RWKV_W), lambda bi, d, j: (bi, ch(d, j), 0))
    two = pl.BlockSpec((None, None, c, RWKV_W), lambda bi, d, j: (d, bi, ch(d, j), 0))
    return pl.pallas_call(
        _rwkv_chunk_kernel,
        grid=(b, 2, n_all),
        in_specs=[one, one, one, two, two, two,
                  pl.BlockSpec((None, c, c), lambda bi, d, j: (d, 0, 0)),
                  pl.BlockSpec((None, c, HALF), lambda bi, d, j: (d, 0, 0)),
                  pl.BlockSpec((None, c, HALF), lambda bi, d, j: (d, 0, 0))],
        out_specs=two,
        out_shape=jax.ShapeDtypeStruct((2, b, l, RWKV_W), F32),
        scratch_shapes=[pltpu.VMEM((2, HALF, HALF), F32)],
        compiler_params=_params("arbitrary", "arbitrary", "arbitrary"),
        name="rwkv_chunk",
    )(r3, v3, na3, kd4, bb4, lw4, tri, ms, mi)


def _rwkv_consts():
    c = RWKV_CHUNK
    idx = np.arange(c)
    lower = idx[None, :] <= idx[:, None]
    tri = np.stack([lower, lower.T]).astype(np.float32)
    s_in_head = np.tile(idx, HALF // c)[None, :]
    t_row = idx[:, None]
    ms = np.stack([s_in_head < t_row, s_in_head > t_row]).astype(np.float32)
    mi = np.stack([s_in_head <= t_row, s_in_head >= t_row]).astype(np.float32)
    return jnp.asarray(tri, BF16), jnp.asarray(ms), jnp.asarray(mi)


def _merge_kernel(x_ref, mod_ref, gate_ref, ret_ref, mla_ref, y_ref, bonus_ref, g_ref,
                  lng_ref, lnb_ref, blk_ref, wr_ref, wm_ref, ww_ref, wo_ref, o_ref):
    blk = blk_ref[...]
    ret = ret_ref[0] + ret_ref[1]
    y = y_ref[0] + y_ref[1]
    mean = _dot_exact_rhs01(y, blk) * (1.0 / RWKV_HD)
    yc = y - mean
    var = _dot_exact_rhs01(yc * yc, blk) * (1.0 / RWKV_HD)
    y = yc * lax.rsqrt(var + RWKV_GN_EPS) * lng_ref[...] + lnb_ref[...] + bonus_ref[...]
    rwk = y * g_ref[...]
    gate = gate_ref[...]
    d = D_MODEL
    mix = (_sigmoid(gate[:, 0:d]) * _dot(ret.astype(BF16), wr_ref[...])
           + _sigmoid(gate[:, d:2 * d]) * _dot(mla_ref[...], wm_ref[...])
           + _sigmoid(gate[:, 2 * d:]) * _dot(rwk.astype(BF16), ww_ref[...]))
    o_ref[...] = x_ref[...] + mod_ref[2:3, :] * _dot(mix.astype(BF16), wo_ref[...])


def _merge(x2, mod_l, p2, ret, mla, y, bonus, g, w, geom):
    t = x2.shape[0]
    tm = ROW_TILE
    tpb, ctx_tiles, batch = geom
    row = functools.partial(_mod_row, tiles_per_b=tpb, ctx_tiles=ctx_tiles, batch=batch)
    const = lambda shape: pl.BlockSpec(shape, lambda i: (0,) * len(shape))
    return pl.pallas_call(
        _merge_kernel,
        grid=(t // tm,),
        in_specs=[
            pl.BlockSpec((tm, D_MODEL), lambda i: (i, 0)),
            pl.BlockSpec((None, 6, D_MODEL), lambda i: (row(i), 0, 0)),
            pl.BlockSpec((tm, 3 * D_MODEL), lambda i: (i, 0)),
            pl.BlockSpec((2, tm, RET_W), lambda i: (0, i, 0)),
            pl.BlockSpec((tm, MLA_W), lambda i: (i, 0)),
            pl.BlockSpec((2, tm, RWKV_W), lambda i: (0, i, 0)),
            pl.BlockSpec((tm, RWKV_W), lambda i: (i, 0)),
            pl.BlockSpec((tm, RWKV_W), lambda i: (i, 0)),
            const((1, RWKV_W)), const((1, RWKV_W)), const((RWKV_W, RWKV_W)),
            const((RET_W, D_MODEL)), const((MLA_W, D_MODEL)), const((RWKV_W, D_MODEL)),
            const((D_MODEL, D_MODEL)),
        ],
        out_specs=pl.BlockSpec((tm, D_MODEL), lambda i: (i, 0)),
        out_shape=jax.ShapeDtypeStruct((t, D_MODEL), F32),
        compiler_params=_params("arbitrary"),
        name="merge",
    )(x2, mod_l, p2, ret, mla, y, bonus, g, w["lng"], w["lnb"], w["blk"],
      w["wr"], w["wm"], w["ww"], w["wo"])


def _router_kernel(x_ref, mod_ref, g_ref, wr_ref, br_ref, h_out, idx_out, gate_out):
    x = x_ref[...]
    y = x * lax.rsqrt(jnp.mean(x * x, axis=-1, keepdims=True) + NORM_EPS) * g_ref[...]
    h = (y * (1.0 + mod_ref[4:5, :]) + mod_ref[3:4, :]).astype(BF16)
    h_out[...] = h
    logits = _dot(h, wr_ref[...]) + br_ref[...]
    lane = lax.broadcasted_iota(jnp.int32, logits.shape, 1).astype(F32)
    vals, idxs = [], []
    for _ in range(TOP_K):
        m = jnp.max(logits, axis=-1, keepdims=True)
        sel = jnp.min(jnp.where(logits == m, lane, 128.0), axis=-1, keepdims=True)
        vals.append(m)
        idxs.append(sel)
        logits = jnp.where(lane == sel, -jnp.inf, logits)
    es = [jnp.exp(vv - vals[0]) for vv in vals]
    den = es[0] + es[1] + es[2] + es[3]
    idx_row = jnp.zeros(lane.shape, F32)
    gate_row = jnp.zeros(lane.shape, F32)
    for kq in range(TOP_K):
        idx_row = jnp.where(lane == float(kq), idxs[kq], idx_row)
        gate_row = jnp.where(lane == float(kq), es[kq] / den, gate_row)
    idx_out[...] = idx_row.astype(jnp.int32)
    gate_out[...] = gate_row


def _router(x2, mod_l, g, wr_pad, br_pad, geom):
    t = x2.shape[0]
    tm = ROW_TILE
    tpb, ctx_tiles, batch = geom
    row = functools.partial(_mod_row, tiles_per_b=tpb, ctx_tiles=ctx_tiles, batch=batch)
    return pl.pallas_call(
        _router_kernel,
        grid=(t // tm,),
        in_specs=[
            pl.BlockSpec((tm, D_MODEL), lambda i: (i, 0)),
            pl.BlockSpec((None, 6, D_MODEL), lambda i: (row(i), 0, 0)),
            pl.BlockSpec((1, D_MODEL), lambda i: (0, 0)),
            pl.BlockSpec((D_MODEL, 128), lambda i: (0, 0)),
            pl.BlockSpec((1, 128), lambda i: (0, 0)),
        ],
        out_specs=[
            pl.BlockSpec((tm, D_MODEL), lambda i: (i, 0)),
            pl.BlockSpec((tm, 128), lambda i: (i, 0)),
            pl.BlockSpec((tm, 128), lambda i: (i, 0)),
        ],
        out_shape=[
            jax.ShapeDtypeStruct((t, D_MODEL), BF16),
            jax.ShapeDtypeStruct((t, 128), jnp.int32),
            jax.ShapeDtypeStruct((t, 128), F32),
        ],
        compiler_params=_params("arbitrary"),
        name="moe_router",
    )(x2, mod_l, g.reshape(1, D_MODEL), wr_pad, br_pad)


def _moe_plan(top_idx, top_gate, t, tb_rows):
    nk = t * TOP_K
    n_rows = -(-(nk + N_EXPERTS * (MOE_TILE - 1)) // MOE_TILE) * MOE_TILE
    n_sub = n_rows // MOE_SUB
    n_tb = t // tb_rows
    n_items = n_sub + N_EXPERTS * n_tb

    flat_e = top_idx.reshape(-1)
    flat_tok = jnp.arange(nk, dtype=jnp.int32) // TOP_K
    order = jnp.argsort(flat_e, stable=True)
    e_sorted = flat_e[order]
    counts = jnp.bincount(flat_e, length=N_EXPERTS)
    padded = (counts + MOE_TILE - 1) // MOE_TILE * MOE_TILE
    pad_end = jnp.cumsum(padded)
    start = jnp.cumsum(counts) - counts
    dest = (pad_end - padded)[e_sorted] + jnp.arange(nk, dtype=jnp.int32) - start[e_sorted]
    tok_buf = jnp.full((n_rows,), t, jnp.int32).at[dest].set(flat_tok[order])
    gate_buf = jnp.zeros((n_rows,), F32).at[dest].set(top_gate.reshape(-1)[order])
    tile_expert = jnp.minimum(
        jnp.searchsorted(pad_end // MOE_TILE, jnp.arange(n_rows // MOE_TILE), side="right"),
        N_EXPERTS - 1).astype(jnp.int32)

    tok_sub = tok_buf.reshape(n_sub, MOE_SUB)
    valid = tok_sub < t
    lo = jnp.min(jnp.where(valid, tok_sub, t), axis=1) // tb_rows
    hi = jnp.max(jnp.where(valid, tok_sub, -1), axis=1) // tb_rows
    has = jnp.any(valid, axis=1)
    lo = jnp.where(has, lo, 0).astype(jnp.int32)
    n_it = jnp.where(has, hi - lo + 1, 1).astype(jnp.int32)
    first_item = jnp.cumsum(n_it) - n_it
    total = jnp.sum(n_it)
    w = jnp.arange(n_items, dtype=jnp.int32)
    wv = jnp.minimum(w, total - 1)
    st = (jnp.searchsorted(first_item, wv, side="right") - 1).astype(jnp.int32)
    tb = lo[st] + wv - first_item[st]
    live = w < total
    is_first = jnp.logical_and(live, wv == first_item[st])
    is_last = jnp.logical_and(live, wv == first_item[st] + n_it[st] - 1)
    run_ffn = jnp.logical_and(is_last, st % 2 == 1)
    flags1 = (live.astype(jnp.int32) + 2 * is_first.astype(jnp.int32) + 4 * run_ffn.astype(jnp.int32))

    key = jnp.where(live, tb * n_sub + st, n_tb * n_sub)
    perm = jnp.argsort(key)
    live2 = live[perm]
    last_live = total - 1
    pos = jnp.minimum(w, last_live)
    st2 = st[perm][pos]
    tb2 = tb[perm][pos]
    prev_tb = jnp.concatenate([jnp.full((1,), -1, jnp.int32), tb2[:-1]])
    next_tb = jnp.concatenate([tb2[1:], jnp.full((1,), -1, jnp.int32)])
    first2 = jnp.logical_and(live2, tb2 != prev_tb)
    last2 = jnp.logical_and(live2, jnp.logical_or(tb2 != next_tb, w == last_live))
    flags2 = live2.astype(jnp.int32) + 2 * first2.astype(jnp.int32) + 4 * last2.astype(jnp.int32)

    return dict(n_rows=n_rows, n_sub=n_sub, n_items=n_items,
                tok_rows=tok_buf.reshape(n_rows // MOE_TILE, 2, MOE_SUB),
                gate_rows=gate_buf.reshape(n_rows // MOE_TILE, 2, MOE_SUB),
                tile_expert=tile_expert, st1=st, tb1=tb.astype(jnp.int32), flags1=flags1,
                st2=st2, tb2=tb2.astype(jnp.int32), flags2=flags2)


def _moe_ffn_kernel(st_ref, tb_ref, fl_ref, te_ref, h_ref, tok_ref, gate_ref, wgu_ref, bgu_ref,
                    wd_ref, bd_ref, o_ref, x_ref, *, tb_rows):
    w = pl.program_id(0)
    flags = fl_ref[w]
    half = st_ref[w] % 2
    row0 = pl.multiple_of(half * MOE_SUB, MOE_SUB)

    @pl.when(flags & 2 != 0)
    def _():
        x_ref[pl.ds(row0, MOE_SUB), :] = jnp.zeros((MOE_SUB, D_MODEL), F32)

    @pl.when(flags & 1 != 0)
    def _():
        tok_row = jnp.where(half == 0, tok_ref[0:1, :], tok_ref[1:2, :]).astype(F32)
        tok_col = _row_to_col(tok_row, MOE_SUB)
        base = (tb_ref[w] * tb_rows).astype(F32)
        col = lax.broadcasted_iota(jnp.int32, (MOE_SUB, tb_rows), 1).astype(F32) + base
        onehot = jnp.where(tok_col == col, 1.0, 0.0).astype(BF16)
        x_ref[pl.ds(row0, MOE_SUB), :] += _dot(onehot, h_ref[...])

    @pl.when(flags & 4 != 0)
    def _():
        gu = _dot(x_ref[...].astype(BF16), wgu_ref[...]) + bgu_ref[...]
        f = gu.shape[1] // 2
        g_lin = jnp.minimum(gu[:, 0:f], SWIGLU_LIMIT)
        up = jnp.clip(gu[:, f:], -SWIGLU_LIMIT, SWIGLU_LIMIT)
        act = g_lin * _sigmoid(SWIGLU_ALPHA * g_lin) * (up + 1.0)
        y = _dot(act.astype(BF16), wd_ref[...]) + bd_ref[...]
        gate_col = jnp.concatenate([_row_to_col(gate_ref[0:1, :], MOE_SUB),
                                    _row_to_col(gate_ref[1:2, :], MOE_SUB)], axis=0)
        o_ref[...] = y * gate_col


def _moe_ffn(plan, h2, wgu, bgu, wd, bd, tb_rows):
    n_rows = plan["n_rows"]
    f2 = wgu.shape[2]
    grid_spec = pltpu.PrefetchScalarGridSpec(
        num_scalar_prefetch=4,
        grid=(plan["n_items"],),
        in_specs=[
            pl.BlockSpec((tb_rows, D_MODEL), lambda w, st, tb, fl, te: (tb[w], 0)),
            pl.BlockSpec((None, 2, MOE_SUB), lambda w, st, tb, fl, te: (st[w] // 2, 0, 0)),
            pl.BlockSpec((None, 2, MOE_SUB), lambda w, st, tb, fl, te: (st[w] // 2, 0, 0)),
            pl.BlockSpec((None, D_MODEL, f2), lambda w, st, tb, fl, te: (te[st[w] // 2], 0, 0)),
            pl.BlockSpec((None, 1, f2), lambda w, st, tb, fl, te: (te[st[w] // 2], 0, 0)),
            pl.BlockSpec((None, f2 // 2, D_MODEL), lambda w, st, tb, fl, te: (te[st[w] // 2], 0, 0)),
            pl.BlockSpec((None, 1, D_MODEL), lambda w, st, tb, fl, te: (te[st[w] // 2], 0, 0)),
        ],
        out_specs=pl.BlockSpec((MOE_TILE, D_MODEL), lambda w, st, tb, fl, te: (st[w] // 2, 0)),
        scratch_shapes=[pltpu.VMEM((MOE_TILE, D_MODEL), F32)],
    )
    return pl.pallas_call(
        functools.partial(_moe_ffn_kernel, tb_rows=tb_rows),
        grid_spec=grid_spec,
        out_shape=jax.ShapeDtypeStruct((n_rows, D_MODEL), F32),
        compiler_params=_params("arbitrary"),
        name="moe_experts",
    )(plan["st1"], plan["tb1"], plan["flags1"], plan["tile_expert"],
      h2, plan["tok_rows"], plan["gate_rows"], wgu, bgu, wd, bd)


def _moe_combine_kernel(st_ref, tb_ref, fl_ref, y_ref, tok_ref, x_ref, modb_ref, modc_ref,
                        o_ref, acc_ref, *, tb_rows, blocks_per_b, ctx_len):
    w = pl.program_id(0)
    flags = fl_ref[w]

    @pl.when(flags & 2 != 0)
    def _():
        acc_ref[...] = jnp.zeros_like(acc_ref)

    @pl.when(flags & 1 != 0)
    def _():
        half = st_ref[w] % 2
        tok_row = jnp.where(half == 0, tok_ref[0:1, :], tok_ref[1:2, :])
        rows = lax.broadcasted_iota(jnp.int32, (tb_rows, MOE_SUB), 0) + tb_ref[w] * tb_rows
        onehot_t = jnp.where(rows == tok_row, 1.0, 0.0).astype(BF16)
        acc_ref[...] += _dot(onehot_t, y_ref[...].astype(BF16))

    @pl.when(flags & 4 != 0)
    def _():
        row_in_b = lax.broadcasted_iota(jnp.int32, (tb_rows, 1), 0) + (tb_ref[w] % blocks_per_b) * tb_rows
        in_ctx = row_in_b < ctx_len
        g2 = jnp.where(in_ctx, modc_ref[5:6, :], modb_ref[5:6, :])
        o_ref[...] = x_ref[...] + g2 * acc_ref[...]


def _moe_combine(plan, y_sorted, x2, mod_l, tb_rows, blocks_per_b, ctx_len, batch):
    t = x2.shape[0]
    grid_spec = pltpu.PrefetchScalarGridSpec(
        num_scalar_prefetch=3,
        grid=(plan["n_items"],),
        in_specs=[
            pl.BlockSpec((MOE_SUB, D_MODEL), lambda w, st, tb, fl: (st[w], 0)),
            pl.BlockSpec((None, 2, MOE_SUB), lambda w, st, tb, fl: (st[w] // 2, 0, 0)),
            pl.BlockSpec((tb_rows, D_MODEL), lambda w, st, tb, fl: (tb[w], 0)),
            pl.BlockSpec((None, 6, D_MODEL), lambda w, st, tb, fl: (tb[w] // blocks_per_b, 0, 0)),
            pl.BlockSpec((None, 6, D_MODEL), lambda w, st, tb, fl: (batch, 0, 0)),
        ],
        out_specs=pl.BlockSpec((tb_rows, D_MODEL), lambda w, st, tb, fl: (tb[w], 0)),
        scratch_shapes=[pltpu.VMEM((tb_rows, D_MODEL), F32)],
    )
    return pl.pallas_call(
        functools.partial(_moe_combine_kernel, tb_rows=tb_rows, blocks_per_b=blocks_per_b,
                          ctx_len=ctx_len),
        grid_spec=grid_spec,
        out_shape=jax.ShapeDtypeStruct((t, D_MODEL), F32),
        compiler_params=_params("arbitrary"),
        name="moe_combine",
    )(plan["st2"], plan["tb2"], plan["flags2"], y_sorted, plan["tok_rows"], x2, mod_l, mod_l)


def _rope_angles(pos, dim):
    inv = ROPE_BASE ** (-jnp.arange(0, dim, 2, dtype=F32) / dim)
    return pos.astype(F32)[:, None] * inv[None, :]


def _rope_tables(seq, ctx_len):
    rows = seq // GRID_W
    pos = jnp.arange(seq, dtype=jnp.int32)
    ang = _rope_angles(pos, RET_DK)
    r_cos = jnp.tile(jnp.concatenate([jnp.cos(ang), jnp.cos(ang)], -1), (1, RET_HEADS))
    r_sin = jnp.tile(jnp.concatenate([-jnp.sin(ang), jnp.sin(ang)], -1), (1, RET_HEADS))
    ra = _rope_angles(pos // GRID_W, MLA_ROPE // 2)
    ca = _rope_angles(pos % GRID_W, MLA_ROPE // 2)
    one = jnp.ones((seq, MLA_NOPE), F32)
    zero = jnp.zeros((seq, MLA_NOPE), F32)
    m_cos = jnp.concatenate([one, jnp.cos(ra), jnp.cos(ra), jnp.cos(ca), jnp.cos(ca), one[:, :32]], -1)
    m_sin = jnp.concatenate([zero, -jnp.sin(ra), jnp.sin(ra), -jnp.sin(ca), jnp.sin(ca), zero[:, :32]], -1)
    del rows
    ctx1 = lambda n: jnp.ones((ctx_len, n), F32)
    ctx0 = lambda n: jnp.zeros((ctx_len, n), F32)
    return (jnp.concatenate([ctx1(256), r_cos], 0), jnp.concatenate([ctx0(256), r_sin], 0),
            jnp.concatenate([ctx1(128), m_cos], 0), jnp.concatenate([ctx0(128), m_sin], 0))


def kernel(x, c, ctx, c_ctx, ada_w, ada_b, norm1_g, norm2_g, w_in, mla_q_norm_g, mla_w_uq, mla_kv_norm_g, mla_w_ukv, mla_qk_q_g, mla_qk_k_g, rwkv_mu_prev, rwkv_mu_next, rwkv_w0, rwkv_w2, rwkv_a0, rwkv_a2, rwkv_g2, rwkv_k_k, rwkv_k_a, rwkv_r_k, rwkv_lnx_g, rwkv_lnx_b, w_br_ret, w_br_mla, w_br_rwkv, w_out, w_router, b_router, w_gu, b_gu, w_down, b_down):
    b, s, d = x.shape
    ctx_len = ctx.shape[1]
    depth = ada_w.shape[0]
    l = ctx_len + s
    t = b * l
    assert d == D_MODEL and b < 16 and ctx_len % ROW_TILE == 0 and s % ROW_TILE == 0
    geom = (l // ROW_TILE, ctx_len // ROW_TILE, b)
    blocks_per_b = 4
    tb_rows = l // blocks_per_b
    assert l % blocks_per_b == 0 and tb_rows % 8 == 0

    xa = jnp.concatenate([ctx, x], axis=1).reshape(t, d)
    cc = jnp.zeros((16, d), F32).at[:b].set(c).at[b].set(c_ctx)
    mod = _ada_all(cc, ada_w.astype(BF16), ada_b).reshape(depth, 16, 6, d)

    ret_cos, ret_sin, mla_cos, mla_sin = _rope_tables(s, ctx_len)
    ret_consts = _retention_consts()
    rwkv_consts = _rwkv_consts()
    head_blk = jnp.asarray(np.kron(np.eye(RWKV_HEADS), np.ones((RWKV_HD, RWKV_HD))), BF16)

    for li in range(depth):
        p2 = _inproj(xa, mod[li], norm1_g[li], _pad_w_in(w_in[li]), geom)
        p3 = p2.reshape(b, l, P_COLS)

        ret = _retention(p3, ret_cos, ret_sin, ret_consts, ctx_len).reshape(2, t, RET_W)

        mw = _mla_weights(mla_q_norm_g[li], mla_w_uq[li], mla_kv_norm_g[li], mla_w_ukv[li],
                          mla_qk_q_g[li], mla_qk_k_g[li])
        q2, k2, v2 = _mla_prep(p2, mw, mla_cos, mla_sin, l)
        mla = _attention(q2.reshape(b, l, 1024), k2.reshape(b, l, 1024), v2.reshape(b, l, MLA_W),
                         ctx_len).reshape(t, MLA_W)

        rw = dict(mup=rwkv_mu_prev[li].reshape(1, -1), mun=rwkv_mu_next[li].reshape(1, -1),
                  w0=rwkv_w0[li].reshape(2, 1, RWKV_W), w2=rwkv_w2[li].astype(BF16),
                  a0=rwkv_a0[li].reshape(2, 1, RWKV_W), a2=rwkv_a2[li].astype(BF16),
                  g2=rwkv_g2[li].astype(BF16), kk=rwkv_k_k[li].reshape(1, -1),
                  ka=rwkv_k_a[li].reshape(1, -1), rk=rwkv_r_k[li].reshape(1, -1), blk=head_blk)
        r_, v_, na_, kd_, bb_, lw_, g_, bonus_ = _rwkv_elem(p2, rw, l, ctx_len)
        y = _rwkv_chunk(dict(r=r_, v=v_, na=na_, kd=kd_, bb=bb_, lw=lw_), b, l, ctx_len,
                        rwkv_consts).reshape(2, t, RWKV_W)

        mg = dict(lng=rwkv_lnx_g[li].reshape(1, -1), lnb=rwkv_lnx_b[li].reshape(1, -1), blk=head_blk,
                  wr=w_br_ret[li].astype(BF16), wm=w_br_mla[li].astype(BF16),
                  ww=w_br_rwkv[li].astype(BF16), wo=w_out[li].astype(BF16))
        xa = _merge(xa, mod[li], p2, ret, mla, y, bonus_, g_, mg, geom)

        wr_pad = jnp.pad(w_router[li], ((0, 0), (0, 128 - N_EXPERTS))).astype(BF16)
        br_pad = jnp.concatenate([b_router[li], jnp.full((128 - N_EXPERTS,), -1e30, F32)]).reshape(1, 128)
        h2, idx_rows, gate_rows = _router(xa, mod[li], norm2_g[li], wr_pad, br_pad, geom)
        plan = _moe_plan(idx_rows[:, :TOP_K], gate_rows[:, :TOP_K], t, tb_rows)
        y_sorted = _moe_ffn(plan, h2, w_gu[li].astype(BF16), b_gu[li].reshape(N_EXPERTS, 1, -1),
                            w_down[li].astype(BF16), b_down[li].reshape(N_EXPERTS, 1, -1), tb_rows)
        xa = _moe_combine(plan, y_sorted, xa, mod[li], tb_rows, blocks_per_b, ctx_len, b)

    return xa.reshape(b, l, d)[:, ctx_len:, :]
```

```python
import functools

import jax
import jax.numpy as jnp
import numpy as np
from jax import lax
from jax.experimental import pallas as pl
from jax.experimental.pallas import tpu as pltpu

F32 = jnp.float32
BF16 = jnp.bfloat16

D_MODEL = 1024
GRID_W = 64
RET_HEADS, RET_DK, RET_DV, RET_CHUNK = 4, 64, 128, 128
RET_W = RET_HEADS * RET_DV
MLA_HEADS, MLA_Q_RANK, MLA_KV_RANK, MLA_NOPE, MLA_ROPE, MLA_DV = 8, 256, 128, 64, 32, 64
MLA_QK = MLA_NOPE + MLA_ROPE
MLA_W = MLA_HEADS * MLA_DV
RWKV_HEADS, RWKV_HD = 8, 64
RWKV_W = RWKV_HEADS * RWKV_HD
RWKV_DECAY_LORA, RWKV_A_LORA, RWKV_GATE_LORA = 64, 64, 128
RWKV_GN_EPS = 64e-5
RWKV_IN = 3 * RWKV_W + 2 * RWKV_DECAY_LORA + 2 * RWKV_A_LORA + RWKV_GATE_LORA
N_EXPERTS, TOP_K = 32, 4
SWIGLU_LIMIT, SWIGLU_ALPHA = 7.0, 1.702
ROPE_BASE = 10000.0
NORM_EPS = 1e-6
HEAD_NORM_EPS = 1e-5

GATE_OFF = 0
RET_OFF = 3 * D_MODEL
MLA_OFF = RET_OFF + 2048
RWKV_OFF = MLA_OFF + 512 + 128
P_COLS = RWKV_OFF + RWKV_IN
IN_COL_TILE = 1920

ROW_TILE = 256
RWKV_CHUNK = 64
HALF = 256
MOE_SUB = 128
MOE_TILE = 256
VMEM_LIMIT = 48 * 1024 * 1024

NT_DIMS = (((1,), (1,)), ((), ()))
TN_DIMS = (((0,), (0,)), ((), ()))


def _params(*sem):
    return pltpu.CompilerParams(dimension_semantics=sem, vmem_limit_bytes=VMEM_LIMIT)


def _dot(a, b):
    return jnp.dot(a, b, preferred_element_type=F32)


def _dot_nt(a, b):
    return lax.dot_general(a, b, NT_DIMS, preferred_element_type=F32)


def _dot_tn(a, b):
    return lax.dot_general(a, b, TN_DIMS, preferred_element_type=F32)


def _split3(x):
    hi = x.astype(BF16)
    r1 = x - hi.astype(F32)
    mid = r1.astype(BF16)
    lo = (r1 - mid.astype(F32)).astype(BF16)
    return hi, mid, lo


def _dot_exact_rhs01(x, m01):
    hi, mid, lo = _split3(x)
    return _dot(hi, m01) + _dot(mid, m01) + _dot(lo, m01)


def _dot_exact_lhs01(m01, x):
    hi, mid, lo = _split3(x)
    return _dot(m01, hi) + _dot(m01, mid) + _dot(m01, lo)


def _sigmoid(x):
    return 1.0 / (1.0 + jnp.exp(-x))


def _silu(x):
    return x * _sigmoid(x)


def _row_to_col(row, n):
    eye = lax.broadcasted_iota(jnp.int32, (n, n), 0) == lax.broadcasted_iota(jnp.int32, (n, n), 1)
    return jnp.sum(jnp.where(eye, jnp.broadcast_to(row, (n, n)), 0.0), axis=1, keepdims=True)


def _ada_kernel(c_ref, w_ref, b_ref, o_ref):
    s = _silu(c_ref[...])
    o_ref[...] = _dot(s.astype(BF16), w_ref[...]) + b_ref[...]


def _ada_all(cc, ada_w_bf, ada_b):
    depth = ada_w_bf.shape[0]
    tn = 1536
    return pl.pallas_call(
        _ada_kernel,
        grid=(depth, 6 * D_MODEL // tn),
        in_specs=[
            pl.BlockSpec((16, D_MODEL), lambda l, j: (0, 0)),
            pl.BlockSpec((None, D_MODEL, tn), lambda l, j: (l, 0, j)),
            pl.BlockSpec((None, 1, tn), lambda l, j: (l, 0, j)),
        ],
        out_specs=pl.BlockSpec((None, 16, tn), lambda l, j: (l, 0, j)),
        out_shape=jax.ShapeDtypeStruct((depth, 16, 6 * D_MODEL), F32),
        compiler_params=_params("arbitrary", "arbitrary"),
        name="ada_mod",
    )(cc, ada_w_bf, ada_b.reshape(depth, 1, 6 * D_MODEL))


def _mod_row(i, tiles_per_b, ctx_tiles, batch):
    return jnp.where(i % tiles_per_b < ctx_tiles, batch, i // tiles_per_b)


def _inproj_kernel(x_ref, mod_ref, g_ref, w_ref, o_ref):
    x = x_ref[...]
    y = x * lax.rsqrt(jnp.mean(x * x, axis=-1, keepdims=True) + NORM_EPS) * g_ref[...]
    h = y * (1.0 + mod_ref[1:2, :]) + mod_ref[0:1, :]
    o_ref[...] = _dot(h.astype(BF16), w_ref[...])


def _inproj(x2, mod_l, g, w_pad, geom):
    t = x2.shape[0]
    tpb, ctx_tiles, batch = geom
    row = functools.partial(_mod_row, tiles_per_b=tpb, ctx_tiles=ctx_tiles, batch=batch)
    return pl.pallas_call(
        _inproj_kernel,
        grid=(P_COLS // IN_COL_TILE, t // ROW_TILE),
        in_specs=[
            pl.BlockSpec((ROW_TILE, D_MODEL), lambda j, i: (i, 0)),
            pl.BlockSpec((None, 6, D_MODEL), lambda j, i: (row(i), 0, 0)),
            pl.BlockSpec((1, D_MODEL), lambda j, i: (0, 0)),
            pl.BlockSpec((D_MODEL, IN_COL_TILE), lambda j, i: (0, j)),
        ],
        out_specs=pl.BlockSpec((ROW_TILE, IN_COL_TILE), lambda j, i: (i, j)),
        out_shape=jax.ShapeDtypeStruct((t, P_COLS), F32),
        compiler_params=_params("arbitrary", "arbitrary"),
        name="in_proj",
    )(x2, mod_l, g.reshape(1, D_MODEL), w_pad)


def _pad_w_in(w_in):
    d = w_in.shape[0]
    ret = w_in[:, 0:2048]
    cq = w_in[:, 2048:2304]
    ckv = w_in[:, 2304:2432]
    kpe = w_in[:, 2432:2464]
    rwkv = w_in[:, 2464:2464 + RWKV_IN]
    gate = w_in[:, 2464 + RWKV_IN:]
    z = lambda n: jnp.zeros((d, n), w_in.dtype)
    return jnp.concatenate([gate, ret, cq, ckv, z(64), kpe, z(32), z(128), rwkv], axis=1).astype(BF16)


def _scan_chunk(d, j, n_ctx, n_all):
    bwd = jnp.where(j < n_ctx, n_ctx - 1 - j, n_all - 1 - j + n_ctx)
    return jnp.where(d == 0, j, bwd)


def _ret_kernel(q_ref, k_ref, v_ref, g_ref, cos_ref, sin_ref, dmat_ref, qdec_ref, kdec_ref,
                o_ref, s_ref, *, cdec):
    @pl.when(pl.program_id(2) == 0)
    def _():
        s_ref[...] = jnp.zeros_like(s_ref)

    c = RET_CHUNK
    w = RET_HEADS * RET_DK
    cos = cos_ref[...]
    sin = sin_ref[...]
    first = (lax.broadcasted_iota(jnp.int32, (c, w), 1) & (RET_DK - 1)) < RET_DK // 2

    def rot(x):
        partner = jnp.where(first, pltpu.roll(x, w - RET_DK // 2, 1), pltpu.roll(x, RET_DK // 2, 1))
        return x * cos + partner * sin

    q = rot(q_ref[...])
    k = rot(k_ref[...]) * (RET_DK ** -0.5)
    kd_t = (k * kdec_ref[...]).T.astype(BF16)
    qb = q.astype(BF16)
    kb = k.astype(BF16)
    vb = v_ref[...].astype(BF16)
    g = g_ref[...]
    for h in range(RET_HEADS):
        qh = qb[:, h * RET_DK:(h + 1) * RET_DK]
        kh = kb[:, h * RET_DK:(h + 1) * RET_DK]
        vh = vb[:, h * RET_DV:(h + 1) * RET_DV]
        s_h = s_ref[h]
        att = _dot_nt(qh, kh) * dmat_ref[h]
        o = _dot(att.astype(BF16), vh) + _dot(qh, s_h.astype(BF16)) * qdec_ref[h]
        s_ref[h] = s_h * cdec[h] + _dot(kd_t[h * RET_DK:(h + 1) * RET_DK, :], vh)
        oc = o - jnp.mean(o, axis=-1, keepdims=True)
        y = oc * lax.rsqrt(jnp.mean(oc * oc, axis=-1, keepdims=True) + HEAD_NORM_EPS)
        o_ref[:, h * RET_DV:(h + 1) * RET_DV] = y * _silu(g[:, h * RET_DV:(h + 1) * RET_DV])


def _retention(p3, cos_t, sin_t, consts, ctx_len):
    b, l, _ = p3.shape
    n_all = l // RET_CHUNK
    n_ctx = ctx_len // RET_CHUNK
    dmat, qdec, kdec, cdec = consts
    ch = functools.partial(_scan_chunk, n_ctx=n_ctx, n_all=n_all)
    qb, vb = RET_OFF // 256, RET_OFF // 512
    return pl.pallas_call(
        functools.partial(_ret_kernel, cdec=cdec),
        grid=(b, 2, n_all),
        in_specs=[
            pl.BlockSpec((None, RET_CHUNK, 256), lambda bi, d, j: (bi, ch(d, j), qb)),
            pl.BlockSpec((None, RET_CHUNK, 256), lambda bi, d, j: (bi, ch(d, j), qb + 1)),
            pl.BlockSpec((None, RET_CHUNK, 512), lambda bi, d, j: (bi, ch(d, j), vb + 1)),
            pl.BlockSpec((None, RET_CHUNK, 512), lambda bi, d, j: (bi, ch(d, j), vb + 2 + d)),
            pl.BlockSpec((RET_CHUNK, 256), lambda bi, d, j: (ch(d, j), 0)),
            pl.BlockSpec((RET_CHUNK, 256), lambda bi, d, j: (ch(d, j), 0)),
            pl.BlockSpec((None, RET_HEADS, RET_CHUNK, RET_CHUNK), lambda bi, d, j: (d, 0, 0, 0)),
            pl.BlockSpec((None, RET_HEADS, RET_CHUNK, RET_DV), lambda bi, d, j: (d, 0, 0, 0)),
            pl.BlockSpec((None, RET_CHUNK, 256), lambda bi, d, j: (d, 0, 0)),
        ],
        out_specs=pl.BlockSpec((None, None, RET_CHUNK, RET_W), lambda bi, d, j: (d, bi, ch(d, j), 0)),
        out_shape=jax.ShapeDtypeStruct((2, b, l, RET_W), F32),
        scratch_shapes=[pltpu.VMEM((RET_HEADS, RET_DK, RET_DV), F32)],
        compiler_params=_params("arbitrary", "arbitrary", "arbitrary"),
        name="retention",
    )(p3, p3, p3, p3, cos_t, sin_t, dmat, qdec, kdec)


def _retention_consts():
    gamma = 1.0 - jnp.exp2(-5.0 - jnp.arange(RET_HEADS, dtype=F32))
    lg = jnp.log(gamma)
    idx = jnp.arange(RET_CHUNK, dtype=F32)
    diff = idx[:, None] - idx[None, :]
    lower = jnp.where(diff >= 0, jnp.exp(lg[:, None, None] * jnp.maximum(diff, 0.0)), 0.0)
    dmat = jnp.stack([lower, jnp.swapaxes(lower, 1, 2)])
    qd_f = jnp.exp((idx + 1.0)[None, :] * lg[:, None])
    qd_b = jnp.exp((RET_CHUNK - idx)[None, :] * lg[:, None])
    qdec = jnp.broadcast_to(jnp.stack([qd_f, qd_b])[..., None], (2, RET_HEADS, RET_CHUNK, RET_DV))
    kd_f = jnp.exp((RET_CHUNK - 1.0 - idx)[None, :] * lg[:, None])
    kd_b = jnp.exp(idx[None, :] * lg[:, None])
    kdec = jnp.repeat(jnp.swapaxes(jnp.stack([kd_f, kd_b]), 1, 2), RET_DK, axis=2)
    gam64 = 1.0 - np.exp2(-5.0 - np.arange(RET_HEADS))
    cdec = tuple(float(np.float32(np.exp(RET_CHUNK * np.log(np.float32(gv))))) for gv in gam64)
    return dmat, qdec, kdec, cdec


def _mla_prep_kernel(p_ref, qng_ref, wuq_ref, kvg_ref, wuk_ref, wuv_ref, qg_ref, kg_ref,
                     cos_ref, sin_ref, q_out, k_out, v_out):
    blk = p_ref[...]
    cq = blk[:, 0:MLA_Q_RANK]
    ckv = blk[:, MLA_Q_RANK:MLA_Q_RANK + MLA_KV_RANK]
    kpe = blk[:, MLA_Q_RANK + MLA_KV_RANK:]

    def rms(x, g, n):
        return x * lax.rsqrt(jnp.sum(x * x, axis=-1, keepdims=True) * (1.0 / n) + NORM_EPS) * g

    qn = rms(cq, qng_ref[...], MLA_Q_RANK).astype(BF16)
    kvn = rms(ckv, kvg_ref[...], MLA_KV_RANK).astype(BF16)
    q_raw = _dot(qn, wuq_ref[...])
    k_raw = _dot(kvn, wuk_ref[...])
    v_out[...] = _dot(kvn, wuv_ref[...]).astype(BF16)
    cos = cos_ref[...]
    sin = sin_ref[...]
    tm = blk.shape[0]
    first = (lax.broadcasted_iota(jnp.int32, (tm, 128), 1) & 15) < 8

    def rope(x):
        partner = jnp.where(first, pltpu.roll(x, 120, 1), pltpu.roll(x, 8, 1))
        return x * cos + partner * sin

    for h in range(MLA_HEADS):
        sl = slice(h * 128, (h + 1) * 128)
        q_out[:, sl] = rope(rms(q_raw[:, sl], qg_ref[...], MLA_QK)).astype(BF16)
        k_out[:, sl] = rope(rms(k_raw[:, sl] + kpe, kg_ref[...], MLA_QK)).astype(BF16)


def _mla_prep(p2, w, cos_t, sin_t, rows_per_b):
    t = p2.shape[0]
    tiles_per_b = rows_per_b // ROW_TILE
    const = lambda shape: pl.BlockSpec(shape, lambda i: (0,) * len(shape))
    return pl.pallas_call(
        _mla_prep_kernel,
        grid=(t // ROW_TILE,),
        in_specs=[
            pl.BlockSpec((ROW_TILE, 512), lambda i: (i, MLA_OFF // 512)),
            const((1, MLA_Q_RANK)), const((MLA_Q_RANK, 1024)),
            const((1, MLA_KV_RANK)), const((MLA_KV_RANK, 1024)), const((MLA_KV_RANK, MLA_W)),
            const((1, 128)), const((1, 128)),
            pl.BlockSpec((ROW_TILE, 128), lambda i: (i % tiles_per_b, 0)),
            pl.BlockSpec((ROW_TILE, 128), lambda i: (i % tiles_per_b, 0)),
        ],
        out_specs=[
            pl.BlockSpec((ROW_TILE, 1024), lambda i: (i, 0)),
            pl.BlockSpec((ROW_TILE, 1024), lambda i: (i, 0)),
            pl.BlockSpec((ROW_TILE, MLA_W), lambda i: (i, 0)),
        ],
        out_shape=[
            jax.ShapeDtypeStruct((t, 1024), BF16),
            jax.ShapeDtypeStruct((t, 1024), BF16),
            jax.ShapeDtypeStruct((t, MLA_W), BF16),
        ],
        compiler_params=_params("arbitrary"),
        name="mla_prep",
    )(p2, w["qng"], w["wuq"], w["kvg"], w["wuk"], w["wuv"], w["qg"], w["kg"], cos_t, sin_t)


def _mla_weights(q_norm_g, w_uq, kv_norm_g, w_ukv, qk_q_g, qk_k_g):
    wq = w_uq.reshape(MLA_Q_RANK, MLA_HEADS, MLA_QK)
    wq = jnp.pad(wq, ((0, 0), (0, 0), (0, 128 - MLA_QK))).reshape(MLA_Q_RANK, 1024)
    wkv = w_ukv.reshape(MLA_KV_RANK, MLA_HEADS, MLA_NOPE + MLA_DV)
    wk = jnp.pad(wkv[:, :, :MLA_NOPE], ((0, 0), (0, 0), (0, 128 - MLA_NOPE))).reshape(MLA_KV_RANK, 1024)
    wv = wkv[:, :, MLA_NOPE:].reshape(MLA_KV_RANK, MLA_W)
    padg = lambda g: jnp.pad(g, (0, 128 - MLA_QK)).reshape(1, 128)
    return dict(qng=q_norm_g.reshape(1, -1), wuq=wq.astype(BF16), kvg=kv_norm_g.reshape(1, -1),
                wuk=wk.astype(BF16), wuv=wv.astype(BF16), qg=padg(qk_q_g), kg=padg(qk_k_g))


def _attn_kernel(q_ref, k_ref, v_ref, o_ref, *, ctx_len, ctx_tiles):
    scale = MLA_QK ** -0.5

    def attend(kv_len):
        outs = []
        for hh in range(2):
            q = q_ref[:, hh * 128:(hh + 1) * 128]
            k = k_ref[0:kv_len, hh * 128:(hh + 1) * 128]
            v = v_ref[0:kv_len, hh * MLA_DV:(hh + 1) * MLA_DV]
            s = _dot_nt(q, k) * scale
            p = jnp.exp(s - jnp.max(s, axis=-1, keepdims=True))
            den = jnp.sum(p, axis=-1, keepdims=True)
            outs.append(_dot(p.astype(BF16), v) / den)
        o_ref[...] = jnp.concatenate(outs, axis=-1).astype(BF16)

    is_ctx = pl.program_id(2) < ctx_tiles

    @pl.when(is_ctx)
    def _():
        attend(ctx_len)

    @pl.when(jnp.logical_not(is_ctx))
    def _():
        attend(k_ref.shape[0])


def _attention(q3, k3, v3, ctx_len):
    b, l, _ = q3.shape
    tq = ROW_TILE
    return pl.pallas_call(
        functools.partial(_attn_kernel, ctx_len=ctx_len, ctx_tiles=ctx_len // tq),
        grid=(b, MLA_HEADS // 2, l // tq),
        in_specs=[
            pl.BlockSpec((None, tq, 256), lambda bi, h, i: (bi, i, h)),
            pl.BlockSpec((None, l, 256), lambda bi, h, i: (bi, 0, h)),
            pl.BlockSpec((None, l, 128), lambda bi, h, i: (bi, 0, h)),
        ],
        out_specs=pl.BlockSpec((None, tq, 128), lambda bi, h, i: (bi, i, h)),
        out_shape=jax.ShapeDtypeStruct((b, l, MLA_W), BF16),
        compiler_params=_params("arbitrary", "arbitrary", "arbitrary"),
        name="mla_attention",
    )(q3, k3, v3)


def _rwkv_elem_kernel(t_ref, prev_ref, next_ref, mup_ref, mun_ref, w0_ref, w2_ref, a0_ref, a2_ref,
                      g2_ref, kk_ref, ka_ref, rk_ref, blk_ref,
                      r_out, v_out, na_out, kd_out, bb_out, lw_out, g_out, bonus_out,
                      *, tiles_per_b, ctx_tiles):
    i = pl.program_id(0) % tiles_per_b
    starts = jnp.logical_or(i == 0, i == ctx_tiles)
    ends = jnp.logical_or(i == ctx_tiles - 1, i == tiles_per_b - 1)
    t = t_ref[...]
    tm = t.shape[0]
    prev_row = jnp.where(starts, 0.0, prev_ref[7:8, :])
    next_row = jnp.where(ends, 0.0, next_ref[0:1, :])
    rid = lax.broadcasted_iota(jnp.int32, t.shape, 0)
    prev = jnp.where(rid == 0, prev_row, pltpu.roll(t, 1, 0))
    nxt = jnp.where(rid == tm - 1, next_row, pltpu.roll(t, tm - 1, 0))
    t = t + mup_ref[...] * (prev - t) + mun_ref[...] * (nxt - t)

    w = RWKV_W
    r = t[:, 0:w]
    k = t[:, w:2 * w]
    v = t[:, 2 * w:3 * w]
    o = 3 * w
    wh = (t[:, o:o + 64], t[:, o + 64:o + 128])
    ah = (t[:, o + 128:o + 192], t[:, o + 192:o + 256])
    gh = t[:, o + 256:o + 384]
    blk = blk_ref[...]

    kk = k * kk_ref[...]
    kk = kk * lax.rsqrt(jnp.maximum(_dot_exact_rhs01(kk * kk, blk), 1e-24))
    r_out[...] = r
    v_out[...] = v
    na_out[...] = -kk
    g_out[...] = _dot(_sigmoid(gh).astype(BF16), g2_ref[...])
    rk = r * rk_ref[...]
    bonus = jnp.zeros_like(v)
    for d in range(2):
        z = w0_ref[d] + _dot(jnp.tanh(wh[d]).astype(BF16), w2_ref[d])
        nz = -z
        softplus = jnp.maximum(nz, 0.0) + jnp.log(1.0 + jnp.exp(-jnp.abs(nz)))
        lw_out[d] = -jnp.exp(-softplus - 0.5)
        a = _sigmoid(a0_ref[d] + _dot(ah[d].astype(BF16), a2_ref[d]))
        kd = k * (1.0 + (a - 1.0) * ka_ref[...])
        kd_out[d] = kd
        bb_out[d] = kk * a
        bonus = bonus + _dot_exact_rhs01(rk * kd, blk) * v
    bonus_out[...] = bonus


def _rwkv_elem(p2, w, rows_per_b, ctx_len):
    t = p2.shape[0]
    tm = ROW_TILE
    tiles_per_b = rows_per_b // tm
    n8 = t // 8
    cb = RWKV_OFF // RWKV_IN
    const = lambda shape: pl.BlockSpec(shape, lambda i: (0,) * len(shape))
    one = pl.BlockSpec((tm, RWKV_W), lambda i: (i, 0))
    two = pl.BlockSpec((2, tm, RWKV_W), lambda i: (0, i, 0))
    s1 = jax.ShapeDtypeStruct((t, RWKV_W), F32)
    s2 = jax.ShapeDtypeStruct((2, t, RWKV_W), F32)
    return pl.pallas_call(
        functools.partial(_rwkv_elem_kernel, tiles_per_b=tiles_per_b, ctx_tiles=ctx_len // tm),
        grid=(t // tm,),
        in_specs=[
            pl.BlockSpec((tm, RWKV_IN), lambda i: (i, cb)),
            pl.BlockSpec((8, RWKV_IN), lambda i: (jnp.maximum(i * (tm // 8) - 1, 0), cb)),
            pl.BlockSpec((8, RWKV_IN), lambda i: (jnp.minimum((i + 1) * (tm // 8), n8 - 1), cb)),
            const((1, RWKV_IN)), const((1, RWKV_IN)),
            const((2, 1, RWKV_W)), const((2, RWKV_DECAY_LORA, RWKV_W)),
            const((2, 1, RWKV_W)), const((2, RWKV_A_LORA, RWKV_W)),
            const((RWKV_GATE_LORA, RWKV_W)),
            const((1, RWKV_W)), const((1, RWKV_W)), const((1, RWKV_W)),
            const((RWKV_W, RWKV_W)),
        ],
        out_specs=[one, one, one, two, two, two, one, one],
        out_shape=[s1, s1, s1, s2, s2, s2, s1, s1],
        compiler_params=_params("arbitrary"),
        name="rwkv_elem",
    )(p2, p2, p2, w["mup"], w["mun"], w["w0"], w["w2"], w["a0"], w["a2"], w["g2"],
      w["kk"], w["ka"], w["rk"], w["blk"])


def _block_rows(x, bm):
    return jnp.where(bm, jnp.concatenate([x, x, x, x], axis=0), 0.0)


def _rwkv_chain(r, v, na, kd, bb, lw, tri, strict, incl, h_ref, o_ref, bi, d):
    c = RWKV_CHUNK
    cum = _dot_exact_lhs01(tri, lw)
    tot = cum[c - 1:c, :] if d == 0 else cum[0:1, :]
    e_pos = jnp.exp(cum)
    e_prev = jnp.exp(cum - lw)
    e_neg = jnp.exp(-cum)
    e_rel = jnp.exp(tot - cum)
    e_tot = jnp.exp(tot)
    a_t = na * e_prev
    r_t = r * e_pos
    b_t = bb * e_neg
    k_t = kd * e_neg
    b_p = bb * e_rel
    k_p = kd * e_rel

    assert c == 64 and RWKV_HD == 64
    bm = (lax.broadcasted_iota(jnp.int32, (HALF, HALF), 0) >> 6
          == lax.broadcasted_iota(jnp.int32, (HALF, HALF), 1) >> 6)
    eye_p = ((lax.broadcasted_iota(jnp.int32, (c, HALF), 1) & (c - 1))
             == lax.broadcasted_iota(jnp.int32, (c, HALF), 0))
    eye_s = (lax.broadcasted_iota(jnp.int32, (HALF, HALF), 0)
             == lax.broadcasted_iota(jnp.int32, (HALF, HALF), 1))
    bf = lambda x: x.astype(BF16)

    for g in range(2):
        sl = slice(g * HALF, (g + 1) * HALF)
        a_h, r_h, v_h = a_t[:, sl], r_t[:, sl], v[:, sl]
        lhs = bf(jnp.concatenate([a_h, r_h], axis=0))
        rhs = bf(jnp.concatenate([_block_rows(b_t[:, sl], bm), _block_rows(k_t[:, sl], bm)], axis=0))
        big = _dot_nt(lhs, rhs)
        a_ab = jnp.where(strict, big[0:c, 0:HALF], 0.0)
        a_ak = jnp.where(strict, big[0:c, HALF:], 0.0)
        a_rb = jnp.where(incl, big[c:, 0:HALF], 0.0)
        a_rk = jnp.where(incl, big[c:, HALF:], 0.0)
        v_bd = bf(_block_rows(v_h, bm))
        akv = _dot(bf(a_ak), v_bd)
        t_inv = jnp.where(eye_p, 1.0, 0.0) + a_ab
        pw = _dot(bf(a_ab), bf(_block_rows(a_ab, bm)))
        for it in range(5):
            prod = _dot(bf(jnp.concatenate([t_inv, pw], axis=0)), bf(_block_rows(pw, bm)))
            t_inv = t_inv + prod[0:c]
            pw = prod[c:]
        wu = _dot(bf(t_inv), bf(jnp.concatenate([_block_rows(a_h, bm), _block_rows(akv, bm)], axis=1)))
        w_m, u0 = wu[:, 0:HALF], wu[:, HALF:]
        gy_y0 = _dot(bf(a_rb), bf(jnp.concatenate([_block_rows(w_m, bm), _block_rows(u0, bm)], axis=1)))
        gy = r_h + gy_y0[:, 0:HALF]
        y0 = gy_y0[:, HALF:] + _dot(bf(a_rk), v_bd)
        bp_h, kp_h = bf(b_p[:, sl]), bf(k_p[:, sl])
        f_bd = jnp.where(bm, _dot_tn(bp_h, bf(w_m)), 0.0) + jnp.where(eye_s, e_tot[:, sl], 0.0)
        g_bd = jnp.where(bm, _dot_tn(bp_h, bf(u0)) + _dot_tn(kp_h, bf(v_h)), 0.0)
        h_old = bf(h_ref[bi, d, g])
        o_ref[bi, :, sl] = _dot(bf(gy), h_old) + y0
        h_ref[bi, d, g] = _dot(bf(f_bd), h_old) + g_bd


RWKV_ROWS = 2


def _rwkv_chunk_kernel(rf_ref, vf_ref, naf_ref, kdf_ref, bbf_ref, lwf_ref,
                       rb_ref, vb_ref, nab_ref, kdb_ref, bbb_ref, lwb_ref,
                       tri_ref, ms_ref, mi_ref, of_ref, ob_ref, h_ref):
    @pl.when(pl.program_id(1) == 0)
    def _():
        h_ref[...] = jnp.zeros_like(h_ref)

    dirs = ((rf_ref, vf_ref, naf_ref, kdf_ref, bbf_ref, lwf_ref, of_ref),
            (rb_ref, vb_ref, nab_ref, kdb_ref, bbb_ref, lwb_ref, ob_ref))
    c = RWKV_CHUNK
    assert c == 64 and RWKV_HD == 64
    bm = (lax.broadcasted_iota(jnp.int32, (HALF, HALF), 0) >> 6
          == lax.broadcasted_iota(jnp.int32, (HALF, HALF), 1) >> 6)
    eye_p = ((lax.broadcasted_iota(jnp.int32, (c, HALF), 1) & (c - 1))
             == lax.broadcasted_iota(jnp.int32, (c, HALF), 0))
    eye_s = (lax.broadcasted_iota(jnp.int32, (HALF, HALF), 0)
             == lax.broadcasted_iota(jnp.int32, (HALF, HALF), 1))
    bf = lambda x: x.astype(BF16)
    blk = lambda x: _block_rows(x, bm)
    cat0 = lambda xs: jnp.concatenate(xs, axis=0)
    cat1 = lambda xs: jnp.concatenate(xs, axis=1)

    chains = [(d, bi) + dirs[d] for d in range(2) for bi in range(RWKV_ROWS)]
    cums = [_dot_exact_lhs01(tri_ref[d], lw_ref[bi]) for d, bi, _, _, _, _, _, lw_ref, _ in chains]
    inst = []
    for (d, bi, r_ref, v_ref, na_ref, kd_ref, bb_ref, lw_ref, o_ref), cum in zip(chains, cums):
        lw = lw_ref[bi]
        tot = cum[c - 1:c, :] if d == 0 else cum[0:1, :]
        e_neg = jnp.exp(-cum)
        e_rel = jnp.exp(tot - cum)
        e_tot = jnp.exp(tot)
        a_t = na_ref[bi] * jnp.exp(cum - lw)
        r_t = r_ref[bi] * jnp.exp(cum)
        kd, bb, v = kd_ref[bi], bb_ref[bi], v_ref[bi]
        for g in range(2):
            sl = slice(g * HALF, (g + 1) * HALF)
            inst.append(dict(
                d=d, bi=bi, g=g, sl=sl, o_ref=o_ref, a=a_t[:, sl], r=r_t[:, sl], v=v[:, sl],
                bt=bb[:, sl] * e_neg[:, sl], kt=kd[:, sl] * e_neg[:, sl],
                bp=bf(bb[:, sl] * e_rel[:, sl]), kp=bf(kd[:, sl] * e_rel[:, sl]), etot=e_tot[:, sl]))
    strict = [ms_ref[0] > 0.5, ms_ref[1] > 0.5]
    incl = [mi_ref[0] > 0.5, mi_ref[1] > 0.5]

    bigs = [_dot_nt(bf(cat0([s["a"], s["r"]])), bf(cat0([blk(s["bt"]), blk(s["kt"])]))) for s in inst]
    for s, big in zip(inst, bigs):
        s["a_ab"] = jnp.where(strict[s["d"]], big[0:c, 0:HALF], 0.0)
        s["a_ak"] = bf(jnp.where(strict[s["d"]], big[0:c, HALF:], 0.0))
        s["a_rb"] = bf(jnp.where(incl[s["d"]], big[c:, 0:HALF], 0.0))
        s["a_rk"] = bf(jnp.where(incl[s["d"]], big[c:, HALF:], 0.0))
        s["v_bd"] = bf(blk(s["v"]))
        s["t"] = jnp.where(eye_p, 1.0, 0.0) + s["a_ab"]
    pws = [_dot(bf(s["a_ab"]), bf(blk(s["a_ab"]))) for s in inst]
    akvs = [_dot(s["a_ak"], s["v_bd"]) for s in inst]
    arkv = [_dot(s["a_rk"], s["v_bd"]) for s in inst]
    for it in range(5):
        prods = [_dot(bf(cat0([s["t"], pw])), bf(blk(pw))) for s, pw in zip(inst, pws)]
        for s, prod in zip(inst, prods):
            s["t"] = s["t"] + prod[0:c]
        pws = [prod[c:] for prod in prods]
    wus = [_dot(bf(s["t"]), bf(cat1([blk(s["a"]), blk(akv)]))) for s, akv in zip(inst, akvs)]
    gys = [_dot(s["a_rb"], bf(cat1([blk(wu[:, 0:HALF]), blk(wu[:, HALF:])]))) for s, wu in zip(inst, wus)]
    fs = [_dot_tn(s["bp"], bf(wu[:, 0:HALF])) for s, wu in zip(inst, wus)]
    gs = [_dot_tn(s["bp"], bf(wu[:, HALF:])) + _dot_tn(s["kp"], bf(s["v"])) for s, wu in zip(inst, wus)]
    h_olds = [bf(h_ref[s["bi"], s["d"], s["g"]]) for s in inst]
    outs = [_dot(bf(s["r"] + gy[:, 0:HALF]), h_old) + gy[:, HALF:] + ak
            for s, gy, h_old, ak in zip(inst, gys, h_olds, arkv)]
    h_news = [_dot(bf(jnp.where(bm, f, 0.0) + jnp.where(eye_s, s["etot"], 0.0)), h_old)
              + jnp.where(bm, g_, 0.0) for s, f, g_, h_old in zip(inst, fs, gs, h_olds)]
    for s, out, h_new in zip(inst, outs, h_news):
        s["o_ref"][s["bi"], :, s["sl"]] = out
        h_ref[s["bi"], s["d"], s["g"]] = h_new


def _rwkv_chunk(e, b, l, ctx_len, consts):
    c = RWKV_CHUNK
    rows = RWKV_ROWS
    assert b % rows == 0
    n_all, n_ctx = l // c, ctx_len // c
    ch = functools.partial(_scan_chunk, n_ctx=n_ctx, n_all=n_all)
    tri, ms, mi = consts
    r3, v3, na3 = (x.reshape(b, l, RWKV_W) for x in (e["r"], e["v"], e["na"]))
    kd4, bb4, lw4 = (x.reshape(2, b, l, RWKV_W) for x in (e["kd"], e["bb"], e["lw"]))

    def specs(d):
        one = pl.BlockSpec((rows, c, RWKV_W), lambda bi, j: (bi, ch(d, j), 0))
        two = pl.BlockSpec((None, rows, c, RWKV_W), lambda bi, j: (d, bi, ch(d, j), 0))
        return one, two

    one_f, two_f = specs(0)
    one_b, two_b = specs(1)
    const = lambda shape: pl.BlockSpec(shape, lambda bi, j: (0,) * len(shape))
    out = jax.ShapeDtypeStruct((b, l, RWKV_W), F32)
    return pl.pallas_call(
        _rwkv_chunk_kernel,
        grid=(b // rows, n_all),
        in_specs=[one_f, one_f, one_f, two_f, two_f, two_f,
                  one_b, one_b, one_b, two_b, two_b, two_b,
                  const((2, c, c)), const((2, c, HALF)), const((2, c, HALF))],
        out_specs=[one_f, one_b],
        out_shape=[out, out],
        scratch_shapes=[pltpu.VMEM((rows, 2, 2, HALF, HALF), F32)],
        compiler_params=_params("arbitrary", "arbitrary"),
        name="rwkv_chunk",
    )(r3, v3, na3, kd4, bb4, lw4, r3, v3, na3, kd4, bb4, lw4, tri, ms, mi)


def _rwkv_consts():
    c = RWKV_CHUNK
    idx = np.arange(c)
    lower = idx[None, :] <= idx[:, None]
    tri = np.stack([lower, lower.T]).astype(np.float32)
    s_in_head = np.tile(idx, HALF // c)[None, :]
    t_row = idx[:, None]
    ms = np.stack([s_in_head < t_row, s_in_head > t_row]).astype(np.float32)
    mi = np.stack([s_in_head <= t_row, s_in_head >= t_row]).astype(np.float32)
    return jnp.asarray(tri, BF16), jnp.asarray(ms), jnp.asarray(mi)


def _merge_kernel(x_ref, mod_ref, gate_ref, ret_ref, mla_ref, yf_ref, yb_ref, bonus_ref, g_ref,
                  lng_ref, lnb_ref, blk_ref, wr_ref, wm_ref, ww_ref, wo_ref, o_ref):
    blk = blk_ref[...]
    ret = ret_ref[0] + ret_ref[1]
    y = yf_ref[...] + yb_ref[...]
    mean = _dot_exact_rhs01(y, blk) * (1.0 / RWKV_HD)
    yc = y - mean
    var = _dot_exact_rhs01(yc * yc, blk) * (1.0 / RWKV_HD)
    y = yc * lax.rsqrt(var + RWKV_GN_EPS) * lng_ref[...] + lnb_ref[...] + bonus_ref[...]
    rwk = y * g_ref[...]
    gate = gate_ref[...]
    d = D_MODEL
    mix = (_sigmoid(gate[:, 0:d]) * _dot(ret.astype(BF16), wr_ref[...])
           + _sigmoid(gate[:, d:2 * d]) * _dot(mla_ref[...], wm_ref[...])
           + _sigmoid(gate[:, 2 * d:]) * _dot(rwk.astype(BF16), ww_ref[...]))
    o_ref[...] = x_ref[...] + mod_ref[2:3, :] * _dot(mix.astype(BF16), wo_ref[...])


def _merge(x2, mod_l, p2, ret, mla, yf, yb, bonus, g, w, geom):
    t = x2.shape[0]
    tm = ROW_TILE
    tpb, ctx_tiles, batch = geom
    row = functools.partial(_mod_row, tiles_per_b=tpb, ctx_tiles=ctx_tiles, batch=batch)
    const = lambda shape: pl.BlockSpec(shape, lambda i: (0,) * len(shape))
    return pl.pallas_call(
        _merge_kernel,
        grid=(t // tm,),
        in_specs=[
            pl.BlockSpec((tm, D_MODEL), lambda i: (i, 0)),
            pl.BlockSpec((None, 6, D_MODEL), lambda i: (row(i), 0, 0)),
            pl.BlockSpec((tm, 3 * D_MODEL), lambda i: (i, 0)),
            pl.BlockSpec((2, tm, RET_W), lambda i: (0, i, 0)),
            pl.BlockSpec((tm, MLA_W), lambda i: (i, 0)),
            pl.BlockSpec((tm, RWKV_W), lambda i: (i, 0)),
            pl.BlockSpec((tm, RWKV_W), lambda i: (i, 0)),
            pl.BlockSpec((tm, RWKV_W), lambda i: (i, 0)),
            pl.BlockSpec((tm, RWKV_W), lambda i: (i, 0)),
            const((1, RWKV_W)), const((1, RWKV_W)), const((RWKV_W, RWKV_W)),
            const((RET_W, D_MODEL)), const((MLA_W, D_MODEL)), const((RWKV_W, D_MODEL)),
            const((D_MODEL, D_MODEL)),
        ],
        out_specs=pl.BlockSpec((tm, D_MODEL), lambda i: (i, 0)),
        out_shape=jax.ShapeDtypeStruct((t, D_MODEL), F32),
        compiler_params=_params("arbitrary"),
        name="merge",
    )(x2, mod_l, p2, ret, mla, yf, yb, bonus, g, w["lng"], w["lnb"], w["blk"],
      w["wr"], w["wm"], w["ww"], w["wo"])


def _router_kernel(x_ref, mod_ref, g_ref, wr_ref, br_ref, h_out, idx_out, gate_out):
    x = x_ref[...]
    y = x * lax.rsqrt(jnp.mean(x * x, axis=-1, keepdims=True) + NORM_EPS) * g_ref[...]
    h = (y * (1.0 + mod_ref[4:5, :]) + mod_ref[3:4, :]).astype(BF16)
    h_out[...] = h
    logits = _dot(h, wr_ref[...]) + br_ref[...]
    lane = lax.broadcasted_iota(jnp.int32, logits.shape, 1).astype(F32)
    vals, idxs = [], []
    for _ in range(TOP_K):
        m = jnp.max(logits, axis=-1, keepdims=True)
        sel = jnp.min(jnp.where(logits == m, lane, 128.0), axis=-1, keepdims=True)
        vals.append(m)
        idxs.append(sel)
        logits = jnp.where(lane == sel, -jnp.inf, logits)
    es = [jnp.exp(vv - vals[0]) for vv in vals]
    den = es[0] + es[1] + es[2] + es[3]
    idx_row = jnp.zeros(lane.shape, F32)
    gate_row = jnp.zeros(lane.shape, F32)
    for kq in range(TOP_K):
        idx_row = jnp.where(lane == float(kq), idxs[kq], idx_row)
        gate_row = jnp.where(lane == float(kq), es[kq] / den, gate_row)
    idx_out[...] = idx_row.astype(jnp.int32)
    gate_out[...] = gate_row


def _router(x2, mod_l, g, wr_pad, br_pad, geom):
    t = x2.shape[0]
    tm = ROW_TILE
    tpb, ctx_tiles, batch = geom
    row = functools.partial(_mod_row, tiles_per_b=tpb, ctx_tiles=ctx_tiles, batch=batch)
    return pl.pallas_call(
        _router_kernel,
        grid=(t // tm,),
        in_specs=[
            pl.BlockSpec((tm, D_MODEL), lambda i: (i, 0)),
            pl.BlockSpec((None, 6, D_MODEL), lambda i: (row(i), 0, 0)),
            pl.BlockSpec((1, D_MODEL), lambda i: (0, 0)),
            pl.BlockSpec((D_MODEL, 128), lambda i: (0, 0)),
            pl.BlockSpec((1, 128), lambda i: (0, 0)),
        ],
        out_specs=[
            pl.BlockSpec((tm, D_MODEL), lambda i: (i, 0)),
            pl.BlockSpec((tm, 128), lambda i: (i, 0)),
            pl.BlockSpec((tm, 128), lambda i: (i, 0)),
        ],
        out_shape=[
            jax.ShapeDtypeStruct((t, D_MODEL), BF16),
            jax.ShapeDtypeStruct((t, 128), jnp.int32),
            jax.ShapeDtypeStruct((t, 128), F32),
        ],
        compiler_params=_params("arbitrary"),
        name="moe_router",
    )(x2, mod_l, g.reshape(1, D_MODEL), wr_pad, br_pad)


def _moe_plan(top_idx, top_gate, t, tb_rows):
    nk = t * TOP_K
    n_rows = -(-(nk + N_EXPERTS * (MOE_TILE - 1)) // MOE_TILE) * MOE_TILE
    n_sub = n_rows // MOE_SUB
    n_tb = t // tb_rows
    n_items = n_sub + N_EXPERTS * n_tb

    flat_e = top_idx.reshape(-1)
    flat_tok = jnp.arange(nk, dtype=jnp.int32) // TOP_K
    order = jnp.argsort(flat_e, stable=True)
    e_sorted = flat_e[order]
    counts = jnp.bincount(flat_e, length=N_EXPERTS)
    padded = (counts + MOE_TILE - 1) // MOE_TILE * MOE_TILE
    pad_end = jnp.cumsum(padded)
    start = jnp.cumsum(counts) - counts
    dest = (pad_end - padded)[e_sorted] + jnp.arange(nk, dtype=jnp.int32) - start[e_sorted]
    tok_buf = jnp.full((n_rows,), t, jnp.int32).at[dest].set(flat_tok[order])
    gate_buf = jnp.zeros((n_rows,), F32).at[dest].set(top_gate.reshape(-1)[order])
    tile_expert = jnp.minimum(
        jnp.searchsorted(pad_end // MOE_TILE, jnp.arange(n_rows // MOE_TILE), side="right"),
        N_EXPERTS - 1).astype(jnp.int32)

    tok_sub = tok_buf.reshape(n_sub, MOE_SUB)
    valid = tok_sub < t
    lo = jnp.min(jnp.where(valid, tok_sub, t), axis=1) // tb_rows
    hi = jnp.max(jnp.where(valid, tok_sub, -1), axis=1) // tb_rows
    has = jnp.any(valid, axis=1)
    lo = jnp.where(has, lo, 0).astype(jnp.int32)
    n_it = jnp.where(has, hi - lo + 1, 1).astype(jnp.int32)
    first_item = jnp.cumsum(n_it) - n_it
    total = jnp.sum(n_it)
    w = jnp.arange(n_items, dtype=jnp.int32)
    wv = jnp.minimum(w, total - 1)
    st = (jnp.searchsorted(first_item, wv, side="right") - 1).astype(jnp.int32)
    tb = lo[st] + wv - first_item[st]
    live = w < total
    is_first = jnp.logical_and(live, wv == first_item[st])
    is_last = jnp.logical_and(live, wv == first_item[st] + n_it[st] - 1)
    run_ffn = jnp.logical_and(is_last, st % 2 == 1)
    flags1 = (live.astype(jnp.int32) + 2 * is_first.astype(jnp.int32) + 4 * run_ffn.astype(jnp.int32))

    key = jnp.where(live, tb * n_sub + st, n_tb * n_sub)
    perm = jnp.argsort(key)
    live2 = live[perm]
    last_live = total - 1
    pos = jnp.minimum(w, last_live)
    st2 = st[perm][pos]
    tb2 = tb[perm][pos]
    prev_tb = jnp.concatenate([jnp.full((1,), -1, jnp.int32), tb2[:-1]])
    next_tb = jnp.concatenate([tb2[1:], jnp.full((1,), -1, jnp.int32)])
    first2 = jnp.logical_and(live2, tb2 != prev_tb)
    last2 = jnp.logical_and(live2, jnp.logical_or(tb2 != next_tb, w == last_live))
    flags2 = live2.astype(jnp.int32) + 2 * first2.astype(jnp.int32) + 4 * last2.astype(jnp.int32)

    return dict(n_rows=n_rows, n_sub=n_sub, n_items=n_items,
                tok_rows=tok_buf.reshape(n_rows // MOE_TILE, 2, MOE_SUB),
                gate_rows=gate_buf.reshape(n_rows // MOE_TILE, 2, MOE_SUB),
                tile_expert=tile_expert, st1=st, tb1=tb.astype(jnp.int32), flags1=flags1,
                st2=st2, tb2=tb2.astype(jnp.int32), flags2=flags2)


def _moe_ffn_kernel(st_ref, tb_ref, fl_ref, te_ref, h_ref, tok_ref, gate_ref, wgu_ref, bgu_ref,
                    wd_ref, bd_ref, o_ref, x_ref, *, tb_rows):
    w = pl.program_id(0)
    flags = fl_ref[w]
    half = st_ref[w] % 2
    row0 = pl.multiple_of(half * MOE_SUB, MOE_SUB)

    @pl.when(flags & 2 != 0)
    def _():
        x_ref[pl.ds(row0, MOE_SUB), :] = jnp.zeros((MOE_SUB, D_MODEL), F32)

    @pl.when(flags & 1 != 0)
    def _():
        tok_row = jnp.where(half == 0, tok_ref[0:1, :], tok_ref[1:2, :]).astype(F32)
        tok_col = _row_to_col(tok_row, MOE_SUB)
        base = (tb_ref[w] * tb_rows).astype(F32)
        col = lax.broadcasted_iota(jnp.int32, (MOE_SUB, tb_rows), 1).astype(F32) + base
        onehot = jnp.where(tok_col == col, 1.0, 0.0).astype(BF16)
        x_ref[pl.ds(row0, MOE_SUB), :] += _dot(onehot, h_ref[...])

    @pl.when(flags & 4 != 0)
    def _():
        gu = _dot(x_ref[...].astype(BF16), wgu_ref[...]) + bgu_ref[...]
        f = gu.shape[1] // 2
        g_lin = jnp.minimum(gu[:, 0:f], SWIGLU_LIMIT)
        up = jnp.clip(gu[:, f:], -SWIGLU_LIMIT, SWIGLU_LIMIT)
        act = g_lin * _sigmoid(SWIGLU_ALPHA * g_lin) * (up + 1.0)
        y = _dot(act.astype(BF16), wd_ref[...]) + bd_ref[...]
        gate_col = jnp.concatenate([_row_to_col(gate_ref[0:1, :], MOE_SUB),
                                    _row_to_col(gate_ref[1:2, :], MOE_SUB)], axis=0)
        o_ref[...] = y * gate_col


def _moe_ffn(plan, h2, wgu, bgu, wd, bd, tb_rows):
    n_rows = plan["n_rows"]
    f2 = wgu.shape[2]
    grid_spec = pltpu.PrefetchScalarGridSpec(
        num_scalar_prefetch=4,
        grid=(plan["n_items"],),
        in_specs=[
            pl.BlockSpec((tb_rows, D_MODEL), lambda w, st, tb, fl, te: (tb[w], 0)),
            pl.BlockSpec((None, 2, MOE_SUB), lambda w, st, tb, fl, te: (st[w] // 2, 0, 0)),
            pl.BlockSpec((None, 2, MOE_SUB), lambda w, st, tb, fl, te: (st[w] // 2, 0, 0)),
            pl.BlockSpec((None, D_MODEL, f2), lambda w, st, tb, fl, te: (te[st[w] // 2], 0, 0)),
            pl.BlockSpec((None, 1, f2), lambda w, st, tb, fl, te: (te[st[w] // 2], 0, 0)),
            pl.BlockSpec((None, f2 // 2, D_MODEL), lambda w, st, tb, fl, te: (te[st[w] // 2], 0, 0)),
            pl.BlockSpec((None, 1, D_MODEL), lambda w, st, tb, fl, te: (te[st[w] // 2], 0, 0)),
        ],
        out_specs=pl.BlockSpec((MOE_TILE, D_MODEL), lambda w, st, tb, fl, te: (st[w] // 2, 0)),
        scratch_shapes=[pltpu.VMEM((MOE_TILE, D_MODEL), F32)],
    )
    return pl.pallas_call(
        functools.partial(_moe_ffn_kernel, tb_rows=tb_rows),
        grid_spec=grid_spec,
        out_shape=jax.ShapeDtypeStruct((n_rows, D_MODEL), F32),
        compiler_params=_params("arbitrary"),
        name="moe_experts",
    )(plan["st1"], plan["tb1"], plan["flags1"], plan["tile_expert"],
      h2, plan["tok_rows"], plan["gate_rows"], wgu, bgu, wd, bd)


def _moe_combine_kernel(st_ref, tb_ref, fl_ref, y_ref, tok_ref, x_ref, modb_ref, modc_ref,
                        o_ref, acc_ref, *, tb_rows, blocks_per_b, ctx_len):
    w = pl.program_id(0)
    flags = fl_ref[w]

    @pl.when(flags & 2 != 0)
    def _():
        acc_ref[...] = jnp.zeros_like(acc_ref)

    @pl.when(flags & 1 != 0)
    def _():
        half = st_ref[w] % 2
        tok_row = jnp.where(half == 0, tok_ref[0:1, :], tok_ref[1:2, :])
        rows = lax.broadcasted_iota(jnp.int32, (tb_rows, MOE_SUB), 0) + tb_ref[w] * tb_rows
        onehot_t = jnp.where(rows == tok_row, 1.0, 0.0).astype(BF16)
        acc_ref[...] += _dot(onehot_t, y_ref[...].astype(BF16))

    @pl.when(flags & 4 != 0)
    def _():
        row_in_b = lax.broadcasted_iota(jnp.int32, (tb_rows, 1), 0) + (tb_ref[w] % blocks_per_b) * tb_rows
        in_ctx = row_in_b < ctx_len
        g2 = jnp.where(in_ctx, modc_ref[5:6, :], modb_ref[5:6, :])
        o_ref[...] = x_ref[...] + g2 * acc_ref[...]


def _moe_combine(plan, y_sorted, x2, mod_l, tb_rows, blocks_per_b, ctx_len, batch):
    t = x2.shape[0]
    grid_spec = pltpu.PrefetchScalarGridSpec(
        num_scalar_prefetch=3,
        grid=(plan["n_items"],),
        in_specs=[
            pl.BlockSpec((MOE_SUB, D_MODEL), lambda w, st, tb, fl: (st[w], 0)),
            pl.BlockSpec((None, 2, MOE_SUB), lambda w, st, tb, fl: (st[w] // 2, 0, 0)),
            pl.BlockSpec((tb_rows, D_MODEL), lambda w, st, tb, fl: (tb[w], 0)),
            pl.BlockSpec((None, 6, D_MODEL), lambda w, st, tb, fl: (tb[w] // blocks_per_b, 0, 0)),
            pl.BlockSpec((None, 6, D_MODEL), lambda w, st, tb, fl: (batch, 0, 0)),
        ],
        out_specs=pl.BlockSpec((tb_rows, D_MODEL), lambda w, st, tb, fl: (tb[w], 0)),
        scratch_shapes=[pltpu.VMEM((tb_rows, D_MODEL), F32)],
    )
    return pl.pallas_call(
        functools.partial(_moe_combine_kernel, tb_rows=tb_rows, blocks_per_b=blocks_per_b,
                          ctx_len=ctx_len),
        grid_spec=grid_spec,
        out_shape=jax.ShapeDtypeStruct((t, D_MODEL), F32),
        compiler_params=_params("arbitrary"),
        name="moe_combine",
    )(plan["st2"], plan["tb2"], plan["flags2"], y_sorted, plan["tok_rows"], x2, mod_l, mod_l)


def _route_kernel(x_ref, mod_ref, g_ref, wr_ref, br_ref, tri_ref,
                  h_out, idx_out, rank_out, gate_out, cnt_out, run_ref):
    @pl.when(pl.program_id(0) == 0)
    def _():
        run_ref[...] = jnp.zeros_like(run_ref)

    x = x_ref[...]
    y = x * lax.rsqrt(jnp.mean(x * x, axis=-1, keepdims=True) + NORM_EPS) * g_ref[...]
    h = y * (1.0 + mod_ref[4:5, :]) + mod_ref[3:4, :]
    h_out[...] = h
    logits = _dot(h.astype(BF16), wr_ref[...]) + br_ref[...]
    lane = lax.broadcasted_iota(jnp.int32, logits.shape, 1).astype(F32)
    vals, idxs = [], []
    for _ in range(TOP_K):
        m = jnp.max(logits, axis=-1, keepdims=True)
        sel = jnp.min(jnp.where(logits == m, lane, 128.0), axis=-1, keepdims=True)
        vals.append(m)
        idxs.append(sel)
        logits = jnp.where(lane == sel, -jnp.inf, logits)
    es = [jnp.exp(vv - vals[0]) for vv in vals]
    den = es[0] + es[1] + es[2] + es[3]
    chosen = jnp.zeros(lane.shape, F32)
    for kq in range(TOP_K):
        chosen = jnp.where(lane == idxs[kq], 1.0, chosen)
    before = _dot(tri_ref[...], chosen.astype(BF16)) + run_ref[...]
    idx_row = jnp.zeros(lane.shape, F32)
    rank_row = jnp.zeros(lane.shape, F32)
    gate_row = jnp.zeros(lane.shape, F32)
    for kq in range(TOP_K):
        rk = jnp.sum(jnp.where(lane == idxs[kq], before, 0.0), axis=-1, keepdims=True)
        idx_row = jnp.where(lane == float(kq), idxs[kq], idx_row)
        rank_row = jnp.where(lane == float(kq), rk, rank_row)
        gate_row = jnp.where(lane == float(kq), es[kq] / den, gate_row)
    idx_out[...] = idx_row.astype(jnp.int32)
    rank_out[...] = rank_row.astype(jnp.int32)
    gate_out[...] = gate_row
    run_ref[...] += jnp.sum(chosen, axis=0, keepdims=True)
    cnt_out[...] = run_ref[...]


def _route(x2, mod_l, g, wr_pad, br_pad, tri, geom):
    t = x2.shape[0]
    tm = ROW_TILE
    tpb, ctx_tiles, batch = geom
    row = functools.partial(_mod_row, tiles_per_b=tpb, ctx_tiles=ctx_tiles, batch=batch)
    lanes = lambda dt: jax.ShapeDtypeStruct((t, 128), dt)
    tile = pl.BlockSpec((tm, 128), lambda i: (i, 0))
    return pl.pallas_call(
        _route_kernel,
        grid=(t // tm,),
        in_specs=[
            pl.BlockSpec((tm, D_MODEL), lambda i: (i, 0)),
            pl.BlockSpec((None, 6, D_MODEL), lambda i: (row(i), 0, 0)),
            pl.BlockSpec((1, D_MODEL), lambda i: (0, 0)),
            pl.BlockSpec((D_MODEL, 128), lambda i: (0, 0)),
            pl.BlockSpec((1, 128), lambda i: (0, 0)),
            pl.BlockSpec((tm, tm), lambda i: (0, 0)),
        ],
        out_specs=[pl.BlockSpec((tm, D_MODEL), lambda i: (i, 0)), tile, tile, tile,
                   pl.BlockSpec((1, 128), lambda i: (0, 0))],
        out_shape=[jax.ShapeDtypeStruct((t, D_MODEL), F32), lanes(jnp.int32), lanes(jnp.int32),
                   lanes(F32), jax.ShapeDtypeStruct((1, 128), F32)],
        scratch_shapes=[pltpu.VMEM((1, 128), F32)],
        compiler_params=_params("arbitrary"),
        name="moe_router",
    )(x2, mod_l, g.reshape(1, D_MODEL), wr_pad, br_pad, tri)


def _moe_layout(idx, rank, counts, t):
    n_rows = -(-(t * TOP_K + N_EXPERTS * (MOE_TILE - 1)) // MOE_TILE) * MOE_TILE
    n_tiles = n_rows // MOE_TILE
    padded = (counts + MOE_TILE - 1) // MOE_TILE * MOE_TILE
    pad_end = jnp.cumsum(padded)
    pad_start = pad_end - padded
    onehot = idx[..., None] == jnp.arange(N_EXPERTS, dtype=jnp.int32)
    dest = jnp.sum(jnp.where(onehot, pad_start, 0), axis=-1) + rank
    tile_end = pad_end // MOE_TILE
    tile_expert = jnp.sum(jnp.arange(n_tiles, dtype=jnp.int32)[:, None] >= tile_end[None, :], axis=1)
    tile_expert = jnp.minimum(tile_expert, N_EXPERTS - 1).astype(jnp.int32)
    n_used = tile_end[-1:].astype(jnp.int32)
    return dest.astype(jnp.int32).reshape(t // ROW_TILE, 1, ROW_TILE * TOP_K), tile_expert, n_used, n_rows


def _dispatch_kernel(dest_ref, h_ref, xs_in_ref, xs_ref, sem):
    del xs_in_ref
    tm = h_ref.shape[0]

    def issue(a, carry):
        pltpu.make_async_copy(h_ref.at[pl.ds(a // TOP_K, 1), :],
                              xs_ref.at[pl.ds(dest_ref[0, a], 1), :], sem.at[0]).start()
        return carry

    lax.fori_loop(0, tm * TOP_K, issue, 0, unroll=8)
    for _ in range(TOP_K):
        pltpu.make_async_copy(h_ref, xs_ref.at[pl.ds(0, tm), :], sem.at[0]).wait()


def _dispatch(dest, h2, n_rows):
    t = h2.shape[0]
    tm = ROW_TILE
    zeros = jnp.zeros((n_rows, D_MODEL), F32)
    return pl.pallas_call(
        _dispatch_kernel,
        grid=(t // tm,),
        in_specs=[
            pl.BlockSpec((None, 1, tm * TOP_K), lambda i: (i, 0, 0), memory_space=pltpu.SMEM),
            pl.BlockSpec((tm, D_MODEL), lambda i: (i, 0)),
            pl.BlockSpec(memory_space=pl.ANY),
        ],
        out_specs=pl.BlockSpec(memory_space=pl.ANY),
        out_shape=jax.ShapeDtypeStruct((n_rows, D_MODEL), F32),
        scratch_shapes=[pltpu.SemaphoreType.DMA((1,))],
        input_output_aliases={2: 0},
        compiler_params=_params("arbitrary"),
        name="moe_dispatch",
    )(dest, h2, zeros)


def _expert_kernel(te_ref, nu_ref, x_ref, wgu_ref, bgu_ref, wd_ref, bd_ref, o_ref):
    del te_ref
    used = pl.program_id(0) < nu_ref[0]

    @pl.when(jnp.logical_not(used))
    def _():
        o_ref[...] = jnp.zeros_like(o_ref)

    @pl.when(used)
    def _():
        gu = _dot(x_ref[...].astype(BF16), wgu_ref[...]) + bgu_ref[...]
        f = gu.shape[1] // 2
        g_lin = jnp.minimum(gu[:, 0:f], SWIGLU_LIMIT)
        up = jnp.clip(gu[:, f:], -SWIGLU_LIMIT, SWIGLU_LIMIT)
        act = g_lin * _sigmoid(SWIGLU_ALPHA * g_lin) * (up + 1.0)
        o_ref[...] = _dot(act.astype(BF16), wd_ref[...]) + bd_ref[...]


def _experts(xs, tile_expert, n_used, wgu, bgu, wd, bd):
    n_rows = xs.shape[0]
    f2 = wgu.shape[2]
    used = lambda g, nu: jnp.minimum(g, nu[0] - 1)
    grid_spec = pltpu.PrefetchScalarGridSpec(
        num_scalar_prefetch=2,
        grid=(n_rows // MOE_TILE,),
        in_specs=[
            pl.BlockSpec((MOE_TILE, D_MODEL), lambda g, te, nu: (used(g, nu), 0)),
            pl.BlockSpec((None, D_MODEL, f2), lambda g, te, nu: (te[used(g, nu)], 0, 0)),
            pl.BlockSpec((None, 1, f2), lambda g, te, nu: (te[used(g, nu)], 0, 0)),
            pl.BlockSpec((None, f2 // 2, D_MODEL), lambda g, te, nu: (te[used(g, nu)], 0, 0)),
            pl.BlockSpec((None, 1, D_MODEL), lambda g, te, nu: (te[used(g, nu)], 0, 0)),
        ],
        out_specs=pl.BlockSpec((MOE_TILE, D_MODEL), lambda g, te, nu: (g, 0)),
    )
    return pl.pallas_call(
        _expert_kernel,
        grid_spec=grid_spec,
        out_shape=jax.ShapeDtypeStruct((n_rows, D_MODEL), F32),
        compiler_params=_params("arbitrary"),
        name="moe_experts",
    )(tile_expert, n_used, xs, wgu, bgu, wd, bd)


def _collect_kernel(dest_ref, gate_ref, x_ref, mod_ref, y_ref, o_ref, buf_ref, sem):
    tm = x_ref.shape[0]

    def issue(a, carry):
        pltpu.make_async_copy(y_ref.at[pl.ds(dest_ref[0, a], 1), :],
                              buf_ref.at[a % TOP_K, pl.ds(a // TOP_K, 1), :], sem.at[0]).start()
        return carry

    lax.fori_loop(0, tm * TOP_K, issue, 0, unroll=8)
    for kq in range(TOP_K):
        pltpu.make_async_copy(y_ref.at[pl.ds(0, tm), :], buf_ref.at[kq], sem.at[0]).wait()
    gate = gate_ref[...]
    acc = gate[:, 0:1] * buf_ref[0]
    for kq in range(1, TOP_K):
        acc = acc + gate[:, kq:kq + 1] * buf_ref[kq]
    o_ref[...] = x_ref[...] + mod_ref[5:6, :] * acc


def _collect(dest, gate_rows, x2, mod_l, ys, geom):
    t = x2.shape[0]
    tm = ROW_TILE
    tpb, ctx_tiles, batch = geom
    row = functools.partial(_mod_row, tiles_per_b=tpb, ctx_tiles=ctx_tiles, batch=batch)
    return pl.pallas_call(
        _collect_kernel,
        grid=(t // tm,),
        in_specs=[
            pl.BlockSpec((None, 1, tm * TOP_K), lambda i: (i, 0, 0), memory_space=pltpu.SMEM),
            pl.BlockSpec((tm, 128), lambda i: (i, 0)),
            pl.BlockSpec((tm, D_MODEL), lambda i: (i, 0)),
            pl.BlockSpec((None, 6, D_MODEL), lambda i: (row(i), 0, 0)),
            pl.BlockSpec(memory_space=pl.ANY),
        ],
        out_specs=pl.BlockSpec((tm, D_MODEL), lambda i: (i, 0)),
        out_shape=jax.ShapeDtypeStruct((t, D_MODEL), F32),
        scratch_shapes=[pltpu.VMEM((TOP_K, tm, D_MODEL), F32), pltpu.SemaphoreType.DMA((1,))],
        compiler_params=_params("arbitrary"),
        name="moe_collect",
    )(dest, gate_rows, x2, mod_l, ys)


def _rope_angles(pos, dim):
    inv = ROPE_BASE ** (-jnp.arange(0, dim, 2, dtype=F32) / dim)
    return pos.astype(F32)[:, None] * inv[None, :]


def _rope_tables(seq, ctx_len):
    rows = seq // GRID_W
    pos = jnp.arange(seq, dtype=jnp.int32)
    ang = _rope_angles(pos, RET_DK)
    r_cos = jnp.tile(jnp.concatenate([jnp.cos(ang), jnp.cos(ang)], -1), (1, RET_HEADS))
    r_sin = jnp.tile(jnp.concatenate([-jnp.sin(ang), jnp.sin(ang)], -1), (1, RET_HEADS))
    ra = _rope_angles(pos // GRID_W, MLA_ROPE // 2)
    ca = _rope_angles(pos % GRID_W, MLA_ROPE // 2)
    one = jnp.ones((seq, MLA_NOPE), F32)
    zero = jnp.zeros((seq, MLA_NOPE), F32)
    m_cos = jnp.concatenate([one, jnp.cos(ra), jnp.cos(ra), jnp.cos(ca), jnp.cos(ca), one[:, :32]], -1)
    m_sin = jnp.concatenate([zero, -jnp.sin(ra), jnp.sin(ra), -jnp.sin(ca), jnp.sin(ca), zero[:, :32]], -1)
    del rows
    ctx1 = lambda n: jnp.ones((ctx_len, n), F32)
    ctx0 = lambda n: jnp.zeros((ctx_len, n), F32)
    return (jnp.concatenate([ctx1(256), r_cos], 0), jnp.concatenate([ctx0(256), r_sin], 0),
            jnp.concatenate([ctx1(128), m_cos], 0), jnp.concatenate([ctx0(128), m_sin], 0))


def kernel(x, c, ctx, c_ctx, ada_w, ada_b, norm1_g, norm2_g, w_in, mla_q_norm_g, mla_w_uq, mla_kv_norm_g, mla_w_ukv, mla_qk_q_g, mla_qk_k_g, rwkv_mu_prev, rwkv_mu_next, rwkv_w0, rwkv_w2, rwkv_a0, rwkv_a2, rwkv_g2, rwkv_k_k, rwkv_k_a, rwkv_r_k, rwkv_lnx_g, rwkv_lnx_b, w_br_ret, w_br_mla, w_br_rwkv, w_out, w_router, b_router, w_gu, b_gu, w_down, b_down):
    b, s, d = x.shape
    ctx_len = ctx.shape[1]
    depth = ada_w.shape[0]
    l = ctx_len + s
    t = b * l
    assert d == D_MODEL and b < 16 and ctx_len % ROW_TILE == 0 and s % ROW_TILE == 0
    geom = (l // ROW_TILE, ctx_len // ROW_TILE, b)
    xa =jnp.concatenate([ctx, x], axis=1).reshape(t, d)
    cc = jnp.zeros((16, d), F32).at[:b].set(c).at[b].set(c_ctx)
    mod = _ada_all(cc, ada_w.astype(BF16), ada_b).reshape(depth, 16, 6, d)

    ret_cos, ret_sin, mla_cos, mla_sin = _rope_tables(s, ctx_len)
    ret_consts = _retention_consts()
    rwkv_consts = _rwkv_consts()
    head_blk = jnp.asarray(np.kron(np.eye(RWKV_HEADS), np.ones((RWKV_HD, RWKV_HD))), BF16)
    route_tri = jnp.asarray(np.tril(np.ones((ROW_TILE, ROW_TILE)), -1), BF16)

    for li in range(depth):
        p2 = _inproj(xa, mod[li], norm1_g[li], _pad_w_in(w_in[li]), geom)
        p3 = p2.reshape(b, l, P_COLS)

        ret = _retention(p3, ret_cos, ret_sin, ret_consts, ctx_len).reshape(2, t, RET_W)

        mw = _mla_weights(mla_q_norm_g[li], mla_w_uq[li], mla_kv_norm_g[li], mla_w_ukv[li],
                          mla_qk_q_g[li], mla_qk_k_g[li])
        q2, k2, v2 = _mla_prep(p2, mw, mla_cos, mla_sin, l)
        mla = _attention(q2.reshape(b, l, 1024), k2.reshape(b, l, 1024), v2.reshape(b, l, MLA_W),
                         ctx_len).reshape(t, MLA_W)

        rw = dict(mup=rwkv_mu_prev[li].reshape(1, -1), mun=rwkv_mu_next[li].reshape(1, -1),
                  w0=rwkv_w0[li].reshape(2, 1, RWKV_W), w2=rwkv_w2[li].astype(BF16),
                  a0=rwkv_a0[li].reshape(2, 1, RWKV_W), a2=rwkv_a2[li].astype(BF16),
                  g2=rwkv_g2[li].astype(BF16), kk=rwkv_k_k[li].reshape(1, -1),
                  ka=rwkv_k_a[li].reshape(1, -1), rk=rwkv_r_k[li].reshape(1, -1), blk=head_blk)
        r_, v_, na_, kd_, bb_, lw_, g_, bonus_ = _rwkv_elem(p2, rw, l, ctx_len)
        yf, yb = _rwkv_chunk(dict(r=r_, v=v_, na=na_, kd=kd_, bb=bb_, lw=lw_), b, l, ctx_len, rwkv_consts)

        mg = dict(lng=rwkv_lnx_g[li].reshape(1, -1), lnb=rwkv_lnx_b[li].reshape(1, -1), blk=head_blk,
                  wr=w_br_ret[li].astype(BF16), wm=w_br_mla[li].astype(BF16),
                  ww=w_br_rwkv[li].astype(BF16), wo=w_out[li].astype(BF16))
        xa = _merge(xa, mod[li], p2, ret, mla, yf.reshape(t, RWKV_W), yb.reshape(t, RWKV_W), bonus_, g_,
                    mg, geom)

        wr_pad = jnp.pad(w_router[li], ((0, 0), (0, 128 - N_EXPERTS))).astype(BF16)
        br_pad = jnp.concatenate([b_router[li], jnp.full((128 - N_EXPERTS,), -1e30, F32)]).reshape(1, 128)
        h2, idx_rows, rank_rows, gate_rows, cnt = _route(xa, mod[li], norm2_g[li], wr_pad, br_pad,
                                                         route_tri, geom)
        dest, tile_expert, n_used, n_rows = _moe_layout(
            idx_rows[:, :TOP_K], rank_rows[:, :TOP_K], cnt[0, :N_EXPERTS].astype(jnp.int32), t)
        xs = _dispatch(dest, h2, n_rows)
        ys = _experts(xs, tile_expert, n_used, w_gu[li].astype(BF16), b_gu[li].reshape(N_EXPERTS, 1, -1),
                      w_down[li].astype(BF16), b_down[li].reshape(N_EXPERTS, 1, -1))
        xa = _collect(dest, gate_rows, xa, mod[li], ys, geom)

    return xa.reshape(b, l, d)[:, ctx_len:, :]
```

```python
import functools

import jax
import jax.numpy as jnp
import numpy as np
from jax import lax
from jax.experimental import pallas as pl
from jax.experimental.pallas import tpu as pltpu

F32 = jnp.float32
BF16 = jnp.bfloat16

D_MODEL = 1024
GRID_W = 64
RET_HEADS, RET_DK, RET_DV, RET_CHUNK = 4, 64, 128, 128
RET_W = RET_HEADS * RET_DV
MLA_HEADS, MLA_Q_RANK, MLA_KV_RANK, MLA_NOPE, MLA_ROPE, MLA_DV = 8, 256, 128, 64, 32, 64
MLA_QK = MLA_NOPE + MLA_ROPE
MLA_W = MLA_HEADS * MLA_DV
RWKV_HEADS, RWKV_HD = 8, 64
RWKV_W = RWKV_HEADS * RWKV_HD
RWKV_DECAY_LORA, RWKV_A_LORA, RWKV_GATE_LORA = 64, 64, 128
RWKV_GN_EPS = 64e-5
RWKV_IN = 3 * RWKV_W + 2 * RWKV_DECAY_LORA + 2 * RWKV_A_LORA + RWKV_GATE_LORA
N_EXPERTS, TOP_K = 32, 4
SWIGLU_LIMIT, SWIGLU_ALPHA = 7.0, 1.702
ROPE_BASE = 10000.0
NORM_EPS = 1e-6
HEAD_NORM_EPS = 1e-5

GATE_OFF = 0
RET_OFF = 3 * D_MODEL
MLA_OFF = RET_OFF + 2048
RWKV_OFF = MLA_OFF + 512 + 128
P_COLS = RWKV_OFF + RWKV_IN
IN_COL_TILE = 3840

ROW_TILE = 256
RWKV_CHUNK = 64
HALF = 256
MOE_SUB = 128
MOE_TILE = 256
VMEM_LIMIT = 48 * 1024 * 1024

NT_DIMS = (((1,), (1,)), ((), ()))
TN_DIMS = (((0,), (0,)), ((), ()))


def _params(*sem):
    return pltpu.CompilerParams(dimension_semantics=sem, vmem_limit_bytes=VMEM_LIMIT)


def _dot(a, b):
    return jnp.dot(a, b, preferred_element_type=F32)


def _dot_nt(a, b):
    return lax.dot_general(a, b, NT_DIMS, preferred_element_type=F32)


def _dot_tn(a, b):
    return lax.dot_general(a, b, TN_DIMS, preferred_element_type=F32)


def _split3(x):
    hi = x.astype(BF16)
    r1 = x - hi.astype(F32)
    mid = r1.astype(BF16)
    lo = (r1 - mid.astype(F32)).astype(BF16)
    return hi, mid, lo


def _dot_exact_rhs01(x, m01):
    hi, mid, lo = _split3(x)
    return _dot(hi, m01) + _dot(mid, m01) + _dot(lo, m01)


def _dot_exact_lhs01(m01, x):
    hi, mid, lo = _split3(x)
    return _dot(m01, hi) + _dot(m01, mid) + _dot(m01, lo)


def _sigmoid(x):
    return 1.0 / (1.0 + jnp.exp(-x))


def _silu(x):
    return x * _sigmoid(x)


def _row_to_col(row, n):
    eye = lax.broadcasted_iota(jnp.int32, (n, n), 0) == lax.broadcasted_iota(jnp.int32, (n, n), 1)
    return jnp.sum(jnp.where(eye, jnp.broadcast_to(row, (n, n)), 0.0), axis=1, keepdims=True)


def _ada_kernel(c_ref, w_ref, b_ref, o_ref):
    s = _silu(c_ref[...])
    o_ref[...] = _dot(s.astype(BF16), w_ref[...]) + b_ref[...]


def _ada_all(cc, ada_w_bf, ada_b):
    depth = ada_w_bf.shape[0]
    tn = 1536
    return pl.pallas_call(
        _ada_kernel,
        grid=(depth, 6 * D_MODEL // tn),
        in_specs=[
            pl.BlockSpec((16, D_MODEL), lambda l, j: (0, 0)),
            pl.BlockSpec((None, D_MODEL, tn), lambda l, j: (l, 0, j)),
            pl.BlockSpec((None, 1, tn), lambda l, j: (l, 0, j)),
        ],
        out_specs=pl.BlockSpec((None, 16, tn), lambda l, j: (l, 0, j)),
        out_shape=jax.ShapeDtypeStruct((depth, 16, 6 * D_MODEL), F32),
        compiler_params=_params("arbitrary", "arbitrary"),
        name="ada_mod",
    )(cc, ada_w_bf, ada_b.reshape(depth, 1, 6 * D_MODEL))


def _mod_row(i, tiles_per_b, ctx_tiles, batch):
    return jnp.where(i % tiles_per_b < ctx_tiles, batch, i // tiles_per_b)


def _inproj_kernel(x_ref, mod_ref, g_ref, w_ref, o_ref):
    x = x_ref[...]
    y = x * lax.rsqrt(jnp.mean(x * x, axis=-1, keepdims=True) + NORM_EPS) * g_ref[...]
    h = y * (1.0 + mod_ref[1:2, :]) + mod_ref[0:1, :]
    o_ref[...] = _dot(h.astype(BF16), w_ref[...])


def _inproj(x2, mod_l, g, w_pad, geom):
    t = x2.shape[0]
    tpb, ctx_tiles, batch = geom
    row = functools.partial(_mod_row, tiles_per_b=tpb, ctx_tiles=ctx_tiles, batch=batch)
    return pl.pallas_call(
        _inproj_kernel,
        grid=(P_COLS // IN_COL_TILE, t // ROW_TILE),
        in_specs=[
            pl.BlockSpec((ROW_TILE, D_MODEL), lambda j, i: (i, 0)),
            pl.BlockSpec((None, 6, D_MODEL), lambda j, i: (row(i), 0, 0)),
            pl.BlockSpec((1, D_MODEL), lambda j, i: (0, 0)),
            pl.BlockSpec((D_MODEL, IN_COL_TILE), lambda j, i: (0, j)),
        ],
        out_specs=pl.BlockSpec((ROW_TILE, IN_COL_TILE), lambda j, i: (i, j)),
        out_shape=jax.ShapeDtypeStruct((t, P_COLS), F32),
        compiler_params=_params("arbitrary", "arbitrary"),
        name="in_proj",
    )(x2, mod_l, g.reshape(1, D_MODEL), w_pad)


def _pad_w_in(w_in):
    d = w_in.shape[0]
    ret = w_in[:, 0:2048]
    cq = w_in[:, 2048:2304]
    ckv = w_in[:, 2304:2432]
    kpe = w_in[:, 2432:2464]
    rwkv = w_in[:, 2464:2464 + RWKV_IN]
    gate = w_in[:, 2464 + RWKV_IN:]
    z = lambda n: jnp.zeros((d, n), w_in.dtype)
    return jnp.concatenate([gate, ret, cq, ckv, z(64), kpe, z(32), z(128), rwkv], axis=1).astype(BF16)


def _scan_chunk(d, j, n_ctx, n_all):
    bwd = jnp.where(j < n_ctx, n_ctx - 1 - j, n_all - 1 - j + n_ctx)
    return jnp.where(d == 0, j, bwd)


def _ret_kernel(q_ref, k_ref, v_ref, g_ref, cos_ref, sin_ref, dmat_ref, qdec_ref, kdec_ref,
                o_ref, s_ref, *, cdec):
    @pl.when(pl.program_id(2) == 0)
    def _():
        s_ref[...] = jnp.zeros_like(s_ref)

    c = RET_CHUNK
    w = RET_HEADS * RET_DK
    cos = cos_ref[...]
    sin = sin_ref[...]
    first = (lax.broadcasted_iota(jnp.int32, (c, w), 1) & (RET_DK - 1)) < RET_DK // 2

    def rot(x):
        partner = jnp.where(first, pltpu.roll(x, w - RET_DK // 2, 1), pltpu.roll(x, RET_DK // 2, 1))
        return x * cos + partner * sin

    q = rot(q_ref[...])
    k = rot(k_ref[...]) * (RET_DK ** -0.5)
    kd_t = (k * kdec_ref[...]).T.astype(BF16)
    qb = q.astype(BF16)
    kb = k.astype(BF16)
    vb = v_ref[...].astype(BF16)
    g = g_ref[...]
    heads = range(RET_HEADS)
    qs = [qb[:, h * RET_DK:(h + 1) * RET_DK] for h in heads]
    ks = [kb[:, h * RET_DK:(h + 1) * RET_DK] for h in heads]
    vs = [vb[:, h * RET_DV:(h + 1) * RET_DV] for h in heads]
    s_old = [s_ref[h] for h in heads]
    atts = [_dot_nt(qs[h], ks[h]) * dmat_ref[h] for h in heads]
    inter = [_dot(qs[h], s_old[h].astype(BF16)) * qdec_ref[h] for h in heads]
    kvs = [_dot(kd_t[h * RET_DK:(h + 1) * RET_DK, :], vs[h]) for h in heads]
    outs = [_dot(atts[h].astype(BF16), vs[h]) + inter[h] for h in heads]
    for h in heads:
        s_ref[h] = s_old[h] * cdec[h] + kvs[h]
        o = outs[h]
        oc = o - jnp.mean(o, axis=-1, keepdims=True)
        y = oc * lax.rsqrt(jnp.mean(oc * oc, axis=-1, keepdims=True) + HEAD_NORM_EPS)
        o_ref[:, h * RET_DV:(h + 1) * RET_DV] = y * _silu(g[:, h * RET_DV:(h + 1) * RET_DV])


def _retention(p3, cos_t, sin_t, consts, ctx_len):
    b, l, _ = p3.shape
    n_all = l // RET_CHUNK
    n_ctx = ctx_len // RET_CHUNK
    dmat, qdec, kdec, cdec = consts
    ch = functools.partial(_scan_chunk, n_ctx=n_ctx, n_all=n_all)
    qb, vb = RET_OFF // 256, RET_OFF // 512
    return pl.pallas_call(
        functools.partial(_ret_kernel, cdec=cdec),
        grid=(b, 2, n_all),
        in_specs=[
            pl.BlockSpec((None, RET_CHUNK, 256), lambda bi, d, j: (bi, ch(d, j), qb)),
            pl.BlockSpec((None, RET_CHUNK, 256), lambda bi, d, j: (bi, ch(d, j), qb + 1)),
            pl.BlockSpec((None, RET_CHUNK, 512), lambda bi, d, j: (bi, ch(d, j), vb + 1)),
            pl.BlockSpec((None, RET_CHUNK, 512), lambda bi, d, j: (bi, ch(d, j), vb + 2 + d)),
            pl.BlockSpec((RET_CHUNK, 256), lambda bi, d, j: (ch(d, j), 0)),
            pl.BlockSpec((RET_CHUNK, 256), lambda bi, d, j: (ch(d, j), 0)),
            pl.BlockSpec((None, RET_HEADS, RET_CHUNK, RET_CHUNK), lambda bi, d, j: (d, 0, 0, 0)),
            pl.BlockSpec((None, RET_HEADS, RET_CHUNK, RET_DV), lambda bi, d, j: (d, 0, 0, 0)),
            pl.BlockSpec((None, RET_CHUNK, 256), lambda bi, d, j: (d, 0, 0)),
        ],
        out_specs=pl.BlockSpec((None, None, RET_CHUNK, RET_W), lambda bi, d, j: (d, bi, ch(d, j), 0)),
        out_shape=jax.ShapeDtypeStruct((2, b, l, RET_W), F32),
        scratch_shapes=[pltpu.VMEM((RET_HEADS, RET_DK, RET_DV), F32)],
        compiler_params=_params("arbitrary", "arbitrary", "arbitrary"),
        name="retention",
    )(p3, p3, p3, p3, cos_t, sin_t, dmat, qdec, kdec)


def _retention_consts():
    gamma = 1.0 - jnp.exp2(-5.0 - jnp.arange(RET_HEADS, dtype=F32))
    lg = jnp.log(gamma)
    idx = jnp.arange(RET_CHUNK, dtype=F32)
    diff = idx[:, None] - idx[None, :]
    lower = jnp.where(diff >= 0, jnp.exp(lg[:, None, None] * jnp.maximum(diff, 0.0)), 0.0)
    dmat = jnp.stack([lower, jnp.swapaxes(lower, 1, 2)])
    qd_f = jnp.exp((idx + 1.0)[None, :] * lg[:, None])
    qd_b = jnp.exp((RET_CHUNK - idx)[None, :] * lg[:, None])
    qdec = jnp.broadcast_to(jnp.stack([qd_f, qd_b])[..., None], (2, RET_HEADS, RET_CHUNK, RET_DV))
    kd_f = jnp.exp((RET_CHUNK - 1.0 - idx)[None, :] * lg[:, None])
    kd_b = jnp.exp(idx[None, :] * lg[:, None])
    kdec = jnp.repeat(jnp.swapaxes(jnp.stack([kd_f, kd_b]), 1, 2), RET_DK, axis=2)
    gam64 = 1.0 - np.exp2(-5.0 - np.arange(RET_HEADS))
    cdec = tuple(float(np.float32(np.exp(RET_CHUNK * np.log(np.float32(gv))))) for gv in gam64)
    return dmat, qdec, kdec, cdec


def _mla_prep_kernel(p_ref, qng_ref, wuq_ref, kvg_ref, wuk_ref, wuv_ref, qg_ref, kg_ref,
                     cos_ref, sin_ref, q_out, k_out, v_out):
    blk = p_ref[...]
    cq = blk[:, 0:MLA_Q_RANK]
    ckv = blk[:, MLA_Q_RANK:MLA_Q_RANK + MLA_KV_RANK]
    kpe = blk[:, MLA_Q_RANK + MLA_KV_RANK:]

    def rms(x, g, n):
        return x * lax.rsqrt(jnp.sum(x * x, axis=-1, keepdims=True) * (1.0 / n) + NORM_EPS) * g

    qn = rms(cq, qng_ref[...], MLA_Q_RANK).astype(BF16)
    kvn = rms(ckv, kvg_ref[...], MLA_KV_RANK).astype(BF16)
    q_raw = _dot(qn, wuq_ref[...])
    k_raw = _dot(kvn, wuk_ref[...])
    ones_col = (lax.broadcasted_iota(jnp.int32, (1, 1024), 1) & 127) == MLA_DV
    v_out[...] = (_dot(kvn, wuv_ref[...]) + jnp.where(ones_col, 1.0, 0.0)).astype(BF16)
    cos = cos_ref[...]
    sin = sin_ref[...]
    tm = blk.shape[0]
    first = (lax.broadcasted_iota(jnp.int32, (tm, 128), 1) & 15) < 8

    def rope(x):
        partner = jnp.where(first, pltpu.roll(x, 120, 1), pltpu.roll(x, 8, 1))
        return x * cos + partner * sin

    for h in range(MLA_HEADS):
        sl = slice(h * 128, (h + 1) * 128)
        q_out[:, sl] = rope(rms(q_raw[:, sl], qg_ref[...], MLA_QK)).astype(BF16)
        k_out[:, sl] = rope(rms(k_raw[:, sl] + kpe, kg_ref[...], MLA_QK)).astype(BF16)


def _mla_prep(p2, w, cos_t, sin_t, rows_per_b):
    t = p2.shape[0]
    tiles_per_b = rows_per_b // ROW_TILE
    const = lambda shape: pl.BlockSpec(shape, lambda i: (0,) * len(shape))
    return pl.pallas_call(
        _mla_prep_kernel,
        grid=(t // ROW_TILE,),
        in_specs=[
            pl.BlockSpec((ROW_TILE, 512), lambda i: (i, MLA_OFF // 512)),
            const((1, MLA_Q_RANK)), const((MLA_Q_RANK, 1024)),
            const((1, MLA_KV_RANK)), const((MLA_KV_RANK, 1024)), const((MLA_KV_RANK, 1024)),
            const((1, 128)), const((1, 128)),
            pl.BlockSpec((ROW_TILE, 128), lambda i: (i % tiles_per_b, 0)),
            pl.BlockSpec((ROW_TILE, 128), lambda i: (i % tiles_per_b, 0)),
        ],
        out_specs=[
            pl.BlockSpec((ROW_TILE, 1024), lambda i: (i, 0)),
            pl.BlockSpec((ROW_TILE, 1024), lambda i: (i, 0)),
            pl.BlockSpec((ROW_TILE, 1024), lambda i: (i, 0)),
        ],
        out_shape=[
            jax.ShapeDtypeStruct((t, 1024), BF16),
            jax.ShapeDtypeStruct((t, 1024), BF16),
            jax.ShapeDtypeStruct((t, 1024), BF16),
        ],
        compiler_params=_params("arbitrary"),
        name="mla_prep",
    )(p2, w["qng"], w["wuq"], w["kvg"], w["wuk"], w["wuv"], w["qg"], w["kg"], cos_t, sin_t)


def _mla_weights(q_norm_g, w_uq, kv_norm_g, w_ukv, qk_q_g, qk_k_g):
    wq = w_uq.reshape(MLA_Q_RANK, MLA_HEADS, MLA_QK)
    wq = jnp.pad(wq, ((0, 0), (0, 0), (0, 128 - MLA_QK))).reshape(MLA_Q_RANK, 1024)
    wkv = w_ukv.reshape(MLA_KV_RANK, MLA_HEADS, MLA_NOPE + MLA_DV)
    wk = jnp.pad(wkv[:, :, :MLA_NOPE], ((0, 0), (0, 0), (0, 128 - MLA_NOPE))).reshape(MLA_KV_RANK, 1024)
    wv = jnp.pad(wkv[:, :, MLA_NOPE:], ((0, 0), (0, 0), (0, 128 - MLA_DV))).reshape(MLA_KV_RANK, 1024)
    padg = lambda g: jnp.pad(g, (0, 128 - MLA_QK)).reshape(1, 128)
    return dict(qng=q_norm_g.reshape(1, -1), wuq=wq.astype(BF16), kvg=kv_norm_g.reshape(1, -1),
                wuk=wk.astype(BF16), wuv=wv.astype(BF16), qg=padg(qk_q_g), kg=padg(qk_k_g))


def _attn_kernel(q_ref, k_ref, v_ref, o_ref, *, ctx_len, ctx_tiles):
    scale_log2e = MLA_QK ** -0.5 * 1.4426950408889634

    def attend(kv_len):
        outs = []
        for hh in range(2):
            q = q_ref[:, hh * 128:(hh + 1) * 128]
            k = k_ref[0:kv_len, hh * 128:(hh + 1) * 128]
            v = v_ref[0:kv_len, hh * 128:(hh + 1) * 128]
            s = _dot_nt(q, k) * scale_log2e
            p = jnp.exp2((s - jnp.max(s, axis=-1, keepdims=True)).astype(BF16))
            o_aug = _dot(p, v)
            outs.append(o_aug[:, 0:MLA_DV] / o_aug[:, MLA_DV:MLA_DV + 1])
        o_ref[...] = jnp.concatenate(outs, axis=-1).astype(BF16)

    is_ctx = pl.program_id(2) < ctx_tiles

    @pl.when(is_ctx)
    def _():
        attend(ctx_len)

    @pl.when(jnp.logical_not(is_ctx))
    def _():
        attend(k_ref.shape[0])


def _attention(q3, k3, v3, ctx_len):
    b, l, _ = q3.shape
    tq = ROW_TILE
    return pl.pallas_call(
        functools.partial(_attn_kernel, ctx_len=ctx_len, ctx_tiles=ctx_len // tq),
        grid=(b, MLA_HEADS // 2, l // tq),
        in_specs=[
            pl.BlockSpec((None, tq, 256), lambda bi, h, i: (bi, i, h)),
            pl.BlockSpec((None, l, 256), lambda bi, h, i: (bi, 0, h)),
            pl.BlockSpec((None, l, 256), lambda bi, h, i: (bi, 0, h)),
        ],
        out_specs=pl.BlockSpec((None, tq, 128), lambda bi, h, i: (bi, i, h)),
        out_shape=jax.ShapeDtypeStruct((b, l, MLA_W), BF16),
        compiler_params=_params("arbitrary", "arbitrary", "arbitrary"),
        name="mla_attention",
    )(q3, k3, v3)


def _rwkv_elem_kernel(t_ref, prev_ref, next_ref, mup_ref, mun_ref, w0_ref, w2_ref, a0_ref, a2_ref,
                      g2_ref, kk_ref, ka_ref, rk_ref, blk_ref,
                      r_out, v_out, na_out, kd_out, bb_out, lw_out, g_out, bonus_out,
                      *, tiles_per_b, ctx_tiles):
    i = pl.program_id(0) % tiles_per_b
    starts = jnp.logical_or(i == 0, i == ctx_tiles)
    ends = jnp.logical_or(i == ctx_tiles - 1, i == tiles_per_b - 1)
    t = t_ref[...]
    tm = t.shape[0]
    prev_row = jnp.where(starts, 0.0, prev_ref[7:8, :])
    next_row = jnp.where(ends, 0.0, next_ref[0:1, :])
    rid = lax.broadcasted_iota(jnp.int32, t.shape, 0)
    prev = jnp.where(rid == 0, prev_row, pltpu.roll(t, 1, 0))
    nxt = jnp.where(rid == tm - 1, next_row, pltpu.roll(t, tm - 1, 0))
    t = t + mup_ref[...] * (prev - t) + mun_ref[...] * (nxt - t)

    w = RWKV_W
    r = t[:, 0:w]
    k = t[:, w:2 * w]
    v = t[:, 2 * w:3 * w]
    o = 3 * w
    wh = (t[:, o:o + 64], t[:, o + 64:o + 128])
    ah = (t[:, o + 128:o + 192], t[:, o + 192:o + 256])
    gh = t[:, o + 256:o + 384]
    blk = blk_ref[...]

    kk = k * kk_ref[...]
    kk = kk * lax.rsqrt(jnp.maximum(_dot_exact_rhs01(kk * kk, blk), 1e-24))
    r_out[...] = r
    v_out[...] = v
    na_out[...] = -kk
    g_out[...] = _dot(_sigmoid(gh).astype(BF16), g2_ref[...])
    rk = r * rk_ref[...]
    bonus = jnp.zeros_like(v)
    for d in range(2):
        z = w0_ref[d] + _dot(jnp.tanh(wh[d]).astype(BF16), w2_ref[d])
        nz = -z
        softplus = jnp.maximum(nz, 0.0) + jnp.log(1.0 + jnp.exp(-jnp.abs(nz)))
        lw_out[d] = -jnp.exp(-softplus - 0.5)
        a = _sigmoid(a0_ref[d] + _dot(ah[d].astype(BF16), a2_ref[d]))
        kd = k * (1.0 + (a - 1.0) * ka_ref[...])
        kd_out[d] = kd
        bb_out[d] = kk * a
        bonus = bonus + _dot_exact_rhs01(rk * kd, blk) * v
    bonus_out[...] = bonus


def _rwkv_elem(p2, w, rows_per_b, ctx_len):
    t = p2.shape[0]
    tm = ROW_TILE
    tiles_per_b = rows_per_b // tm
    n8 = t // 8
    cb = RWKV_OFF // RWKV_IN
    const = lambda shape: pl.BlockSpec(shape, lambda i: (0,) * len(shape))
    one = pl.BlockSpec((tm, RWKV_W), lambda i: (i, 0))
    two = pl.BlockSpec((2, tm, RWKV_W), lambda i: (0, i, 0))
    s1 = jax.ShapeDtypeStruct((t, RWKV_W), F32)
    s2 = jax.ShapeDtypeStruct((2, t, RWKV_W), F32)
    return pl.pallas_call(
        functools.partial(_rwkv_elem_kernel, tiles_per_b=tiles_per_b, ctx_tiles=ctx_len // tm),
        grid=(t // tm,),
        in_specs=[
            pl.BlockSpec((tm, RWKV_IN), lambda i: (i, cb)),
            pl.BlockSpec((8, RWKV_IN), lambda i: (jnp.maximum(i * (tm // 8) - 1, 0), cb)),
            pl.BlockSpec((8, RWKV_IN), lambda i: (jnp.minimum((i + 1) * (tm // 8), n8 - 1), cb)),
            const((1, RWKV_IN)), const((1, RWKV_IN)),
            const((2, 1, RWKV_W)), const((2, RWKV_DECAY_LORA, RWKV_W)),
            const((2, 1, RWKV_W)), const((2, RWKV_A_LORA, RWKV_W)),
            const((RWKV_GATE_LORA, RWKV_W)),
            const((1, RWKV_W)), const((1, RWKV_W)), const((1, RWKV_W)),
            const((RWKV_W, RWKV_W)),
        ],
        out_specs=[one, one, one, two, two, two, one, one],
        out_shape=[s1, s1, s1, s2, s2, s2, s1, s1],
        compiler_params=_params("arbitrary"),
        name="rwkv_elem",
    )(p2, p2, p2, w["mup"], w["mun"], w["w0"], w["w2"], w["a0"], w["a2"], w["g2"],
      w["kk"], w["ka"], w["rk"], w["blk"])


def _block_rows(x, bm):
    return jnp.where(bm, jnp.concatenate([x, x, x, x], axis=0), 0.0)


def _rwkv_chain(r, v, na, kd, bb, lw, tri, strict, incl, h_ref, o_ref, bi, d):
    c = RWKV_CHUNK
    cum = _dot_exact_lhs01(tri, lw)
    tot = cum[c - 1:c, :] if d == 0 else cum[0:1, :]
    e_pos = jnp.exp(cum)
    e_prev = jnp.exp(cum - lw)
    e_neg = jnp.exp(-cum)
    e_rel = jnp.exp(tot - cum)
    e_tot = jnp.exp(tot)
    a_t = na * e_prev
    r_t = r * e_pos
    b_t = bb * e_neg
    k_t = kd * e_neg
    b_p = bb * e_rel
    k_p = kd * e_rel

    assert c == 64 and RWKV_HD == 64
    bm = (lax.broadcasted_iota(jnp.int32, (HALF, HALF), 0) >> 6
          == lax.broadcasted_iota(jnp.int32, (HALF, HALF), 1) >> 6)
    eye_p = ((lax.broadcasted_iota(jnp.int32, (c, HALF), 1) & (c - 1))
             == lax.broadcasted_iota(jnp.int32, (c, HALF), 0))
    eye_s = (lax.broadcasted_iota(jnp.int32, (HALF, HALF), 0)
             == lax.broadcasted_iota(jnp.int32, (HALF, HALF), 1))
    bf = lambda x: x.astype(BF16)

    for g in range(2):
        sl = slice(g * HALF, (g + 1) * HALF)
        a_h, r_h, v_h = a_t[:, sl], r_t[:, sl], v[:, sl]
        lhs = bf(jnp.concatenate([a_h, r_h], axis=0))
        rhs = bf(jnp.concatenate([_block_rows(b_t[:, sl], bm), _block_rows(k_t[:, sl], bm)], axis=0))
        big = _dot_nt(lhs, rhs)
        a_ab = jnp.where(strict, big[0:c, 0:HALF], 0.0)
        a_ak = jnp.where(strict, big[0:c, HALF:], 0.0)
        a_rb = jnp.where(incl, big[c:, 0:HALF], 0.0)
        a_rk = jnp.where(incl, big[c:, HALF:], 0.0)
        v_bd = bf(_block_rows(v_h, bm))
        akv = _dot(bf(a_ak), v_bd)
        t_inv = jnp.where(eye_p, 1.0, 0.0) + a_ab
        pw = _dot(bf(a_ab), bf(_block_rows(a_ab, bm)))
        for it in range(5):
            prod = _dot(bf(jnp.concatenate([t_inv, pw], axis=0)), bf(_block_rows(pw, bm)))
            t_inv = t_inv + prod[0:c]
            pw = prod[c:]
        wu = _dot(bf(t_inv), bf(jnp.concatenate([_block_rows(a_h, bm), _block_rows(akv, bm)], axis=1)))
        w_m, u0 = wu[:, 0:HALF], wu[:, HALF:]
        gy_y0 = _dot(bf(a_rb), bf(jnp.concatenate([_block_rows(w_m, bm), _block_rows(u0, bm)], axis=1)))
        gy = r_h + gy_y0[:, 0:HALF]
        y0 = gy_y0[:, HALF:] + _dot(bf(a_rk), v_bd)
        bp_h, kp_h = bf(b_p[:, sl]), bf(k_p[:, sl])
        f_bd = jnp.where(bm, _dot_tn(bp_h, bf(w_m)), 0.0) + jnp.where(eye_s, e_tot[:, sl], 0.0)
        g_bd = jnp.where(bm, _dot_tn(bp_h, bf(u0)) + _dot_tn(kp_h, bf(v_h)), 0.0)
        h_old = bf(h_ref[bi, d, g])
        o_ref[bi, :, sl] = _dot(bf(gy), h_old) + y0
        h_ref[bi, d, g] = _dot(bf(f_bd), h_old) + g_bd


RWKV_ROWS = 2


def _rwkv_chunk_kernel(rf_ref, vf_ref, naf_ref, kdf_ref, bbf_ref, lwf_ref,
                       rb_ref, vb_ref, nab_ref, kdb_ref, bbb_ref, lwb_ref,
                       tri_ref, ms_ref, mi_ref, of_ref, ob_ref, h_ref):
    @pl.when(pl.program_id(1) == 0)
    def _():
        h_ref[...] = jnp.zeros_like(h_ref)

    dirs = ((rf_ref, vf_ref, naf_ref, kdf_ref, bbf_ref, lwf_ref, of_ref),
            (rb_ref, vb_ref, nab_ref, kdb_ref, bbb_ref, lwb_ref, ob_ref))
    c = RWKV_CHUNK
    assert c == 64 and RWKV_HD == 64
    bm = (lax.broadcasted_iota(jnp.int32, (HALF, HALF), 0) >> 6
          == lax.broadcasted_iota(jnp.int32, (HALF, HALF), 1) >> 6)
    eye_p = ((lax.broadcasted_iota(jnp.int32, (c, HALF), 1) & (c - 1))
             == lax.broadcasted_iota(jnp.int32, (c, HALF), 0))
    eye_s = (lax.broadcasted_iota(jnp.int32, (HALF, HALF), 0)
             == lax.broadcasted_iota(jnp.int32, (HALF, HALF), 1))
    bf = lambda x: x.astype(BF16)
    blk = lambda x: _block_rows(x, bm)
    cat0 = lambda xs: jnp.concatenate(xs, axis=0)
    cat1 = lambda xs: jnp.concatenate(xs, axis=1)

    chains = [(d, bi) + dirs[d] for d in range(2) for bi in range(RWKV_ROWS)]
    cums = [_dot_exact_lhs01(tri_ref[d], lw_ref[bi]) for d, bi, _, _, _, _, _, lw_ref, _ in chains]
    inst = []
    for (d, bi, r_ref, v_ref, na_ref, kd_ref, bb_ref, lw_ref, o_ref), cum in zip(chains, cums):
        lw = lw_ref[bi]
        tot = cum[c - 1:c, :] if d == 0 else cum[0:1, :]
        e_neg = jnp.exp(-cum)
        e_rel = jnp.exp(tot - cum)
        e_tot = jnp.exp(tot)
        a_t = na_ref[bi] * jnp.exp(cum - lw)
        r_t = r_ref[bi] * jnp.exp(cum)
        kd, bb, v = kd_ref[bi], bb_ref[bi], v_ref[bi]
        for g in range(2):
            sl = slice(g * HALF, (g + 1) * HALF)
            inst.append(dict(
                d=d, bi=bi, g=g, sl=sl, o_ref=o_ref, a=a_t[:, sl], r=r_t[:, sl], v=v[:, sl],
                bt=bb[:, sl] * e_neg[:, sl], kt=kd[:, sl] * e_neg[:, sl],
                bp=bf(bb[:, sl] * e_rel[:, sl]), kp=bf(kd[:, sl] * e_rel[:, sl]), etot=e_tot[:, sl]))
    strict = [ms_ref[0] > 0.5, ms_ref[1] > 0.5]
    incl = [mi_ref[0] > 0.5, mi_ref[1] > 0.5]

    bigs = [_dot_nt(bf(cat0([s["a"], s["r"]])), bf(cat0([blk(s["bt"]), blk(s["kt"])]))) for s in inst]
    for s, big in zip(inst, bigs):
        s["a_ab"] = jnp.where(strict[s["d"]], big[0:c, 0:HALF], 0.0)
        s["a_ak"] = bf(jnp.where(strict[s["d"]], big[0:c, HALF:], 0.0))
        s["a_rb"] = bf(jnp.where(incl[s["d"]], big[c:, 0:HALF], 0.0))
        s["a_rk"] = bf(jnp.where(incl[s["d"]], big[c:, HALF:], 0.0))
        s["v_bd"] = bf(blk(s["v"]))
        s["t"] = jnp.where(eye_p, 1.0, 0.0) + s["a_ab"]
    pws = [_dot(bf(s["a_ab"]), bf(blk(s["a_ab"]))) for s in inst]
    akvs = [_dot(s["a_ak"], s["v_bd"]) for s in inst]
    arkv = [_dot(s["a_rk"], s["v_bd"]) for s in inst]
    for it in range(5):
        prods = [_dot(bf(cat0([s["t"], pw])), bf(blk(pw))) for s, pw in zip(inst, pws)]
        for s, prod in zip(inst, prods):
            s["t"] = s["t"] + prod[0:c]
        pws = [prod[c:] for prod in prods]
    wus = [_dot(bf(s["t"]), bf(cat1([blk(s["a"]), blk(akv)]))) for s, akv in zip(inst, akvs)]
    gys = [_dot(s["a_rb"], bf(cat1([blk(wu[:, 0:HALF]), blk(wu[:, HALF:])]))) for s, wu in zip(inst, wus)]
    fs = [_dot_tn(s["bp"], bf(wu[:, 0:HALF])) for s, wu in zip(inst, wus)]
    gs = [_dot_tn(s["bp"], bf(wu[:, HALF:])) + _dot_tn(s["kp"], bf(s["v"])) for s, wu in zip(inst, wus)]
    h_olds = [bf(h_ref[s["bi"], s["d"], s["g"]]) for s in inst]
    outs = [_dot(bf(s["r"] + gy[:, 0:HALF]), h_old) + gy[:, HALF:] + ak
            for s, gy, h_old, ak in zip(inst, gys, h_olds, arkv)]
    h_news = [_dot(bf(jnp.where(bm, f, 0.0) + jnp.where(eye_s, s["etot"], 0.0)), h_old)
              + jnp.where(bm, g_, 0.0) for s, f, g_, h_old in zip(inst, fs, gs, h_olds)]
    for s, out, h_new in zip(inst, outs, h_news):
        s["o_ref"][s["bi"], :, s["sl"]] = out
        h_ref[s["bi"], s["d"], s["g"]] = h_new


def _rwkv_chunk(e, b, l, ctx_len, consts):
    c = RWKV_CHUNK
    rows = RWKV_ROWS
    assert b % rows == 0
    n_all, n_ctx = l // c, ctx_len // c
    ch = functools.partial(_scan_chunk, n_ctx=n_ctx, n_all=n_all)
    tri, ms, mi = consts
    r3, v3, na3 = (x.reshape(b, l, RWKV_W) for x in (e["r"], e["v"], e["na"]))
    kd4, bb4, lw4 = (x.reshape(2, b, l, RWKV_W) for x in (e["kd"], e["bb"], e["lw"]))

    def specs(d):
        one = pl.BlockSpec((rows, c, RWKV_W), lambda bi, j: (bi, ch(d, j), 0))
        two = pl.BlockSpec((None, rows, c, RWKV_W), lambda bi, j: (d, bi, ch(d, j), 0))
        return one, two

    one_f, two_f = specs(0)
    one_b, two_b = specs(1)
    const = lambda shape: pl.BlockSpec(shape, lambda bi, j: (0,) * len(shape))
    out = jax.ShapeDtypeStruct((b, l, RWKV_W), F32)
    return pl.pallas_call(
        _rwkv_chunk_kernel,
        grid=(b // rows, n_all),
        in_specs=[one_f, one_f, one_f, two_f, two_f, two_f,
                  one_b, one_b, one_b, two_b, two_b, two_b,
                  const((2, c, c)), const((2, c, HALF)), const((2, c, HALF))],
        out_specs=[one_f, one_b],
        out_shape=[out, out],
        scratch_shapes=[pltpu.VMEM((rows, 2, 2, HALF, HALF), F32)],
        compiler_params=_params("arbitrary", "arbitrary"),
        name="rwkv_chunk",
    )(r3, v3, na3, kd4, bb4, lw4, r3, v3, na3, kd4, bb4, lw4, tri, ms, mi)


def _rwkv_consts():
    c = RWKV_CHUNK
    idx = np.arange(c)
    lower = idx[None, :] <= idx[:, None]
    tri = np.stack([lower, lower.T]).astype(np.float32)
    s_in_head = np.tile(idx, HALF // c)[None, :]
    t_row = idx[:, None]
    ms = np.stack([s_in_head < t_row, s_in_head > t_row]).astype(np.float32)
    mi = np.stack([s_in_head <= t_row, s_in_head >= t_row]).astype(np.float32)
    return jnp.asarray(tri, BF16), jnp.asarray(ms), jnp.asarray(mi)


def _merge_kernel(x_ref, mod_ref, gate_ref, ret_ref, mla_ref, yf_ref, yb_ref, bonus_ref, g_ref,
                  lng_ref, lnb_ref, blk_ref, wr_ref, wm_ref, ww_ref, wo_ref, o_ref):
    blk = blk_ref[...]
    ret = ret_ref[0] + ret_ref[1]
    y = yf_ref[...] + yb_ref[...]
    mean = _dot_exact_rhs01(y, blk) * (1.0 / RWKV_HD)
    yc = y - mean
    var = _dot_exact_rhs01(yc * yc, blk) * (1.0 / RWKV_HD)
    y = yc * lax.rsqrt(var + RWKV_GN_EPS) * lng_ref[...] + lnb_ref[...] + bonus_ref[...]
    rwk = y * g_ref[...]
    gate = gate_ref[...]
    d = D_MODEL
    mix = (_sigmoid(gate[:, 0:d]) * _dot(ret.astype(BF16), wr_ref[...])
           + _sigmoid(gate[:, d:2 * d]) * _dot(mla_ref[...], wm_ref[...])
           + _sigmoid(gate[:, 2 * d:]) * _dot(rwk.astype(BF16), ww_ref[...]))
    o_ref[...] = x_ref[...] + mod_ref[2:3, :] * _dot(mix.astype(BF16), wo_ref[...])


def _merge(x2, mod_l, p2, ret, mla, yf, yb, bonus, g, w, geom):
    t = x2.shape[0]
    tm = ROW_TILE
    tpb, ctx_tiles, batch = geom
    row = functools.partial(_mod_row, tiles_per_b=tpb, ctx_tiles=ctx_tiles, batch=batch)
    const = lambda shape: pl.BlockSpec(shape, lambda i: (0,) * len(shape))
    return pl.pallas_call(
        _merge_kernel,
        grid=(t // tm,),
        in_specs=[
            pl.BlockSpec((tm, D_MODEL), lambda i: (i, 0)),
            pl.BlockSpec((None, 6, D_MODEL), lambda i: (row(i), 0, 0)),
            pl.BlockSpec((tm, 3 * D_MODEL), lambda i: (i, 0)),
            pl.BlockSpec((2, tm, RET_W), lambda i: (0, i, 0)),
            pl.BlockSpec((tm, MLA_W), lambda i: (i, 0)),
            pl.BlockSpec((tm, RWKV_W), lambda i: (i, 0)),
            pl.BlockSpec((tm, RWKV_W), lambda i: (i, 0)),
            pl.BlockSpec((tm, RWKV_W), lambda i: (i, 0)),
            pl.BlockSpec((tm, RWKV_W), lambda i: (i, 0)),
            const((1, RWKV_W)), const((1, RWKV_W)), const((RWKV_W, RWKV_W)),
            const((RET_W, D_MODEL)), const((MLA_W, D_MODEL)), const((RWKV_W, D_MODEL)),
            const((D_MODEL, D_MODEL)),
        ],
        out_specs=pl.BlockSpec((tm, D_MODEL), lambda i: (i, 0)),
        out_shape=jax.ShapeDtypeStruct((t, D_MODEL), F32),
        compiler_params=_params("arbitrary"),
        name="merge",
    )(x2, mod_l, p2, ret, mla, yf, yb, bonus, g, w["lng"], w["lnb"], w["blk"],
      w["wr"], w["wm"], w["ww"], w["wo"])


def _router_kernel(x_ref, mod_ref, g_ref, wr_ref, br_ref, h_out, idx_out, gate_out):
    x = x_ref[...]
    y = x * lax.rsqrt(jnp.mean(x * x, axis=-1, keepdims=True) + NORM_EPS) * g_ref[...]
    h = (y * (1.0 + mod_ref[4:5, :]) + mod_ref[3:4, :]).astype(BF16)
    h_out[...] = h
    logits = _dot(h, wr_ref[...]) + br_ref[...]
    lane = lax.broadcasted_iota(jnp.int32, logits.shape, 1).astype(F32)
    vals, idxs = [], []
    for _ in range(TOP_K):
        m = jnp.max(logits, axis=-1, keepdims=True)
        sel = jnp.min(jnp.where(logits == m, lane, 128.0), axis=-1, keepdims=True)
        vals.append(m)
        idxs.append(sel)
        logits = jnp.where(lane == sel, -jnp.inf, logits)
    es = [jnp.exp(vv - vals[0]) for vv in vals]
    den = es[0] + es[1] + es[2] + es[3]
    idx_row = jnp.zeros(lane.shape, F32)
    gate_row = jnp.zeros(lane.shape, F32)
    for kq in range(TOP_K):
        idx_row = jnp.where(lane == float(kq), idxs[kq], idx_row)
        gate_row = jnp.where(lane == float(kq), es[kq] / den, gate_row)
    idx_out[...] = idx_row.astype(jnp.int32)
    gate_out[...] = gate_row


def _router(x2, mod_l, g, wr_pad, br_pad, geom):
    t = x2.shape[0]
    tm = ROW_TILE
    tpb, ctx_tiles, batch = geom
    row = functools.partial(_mod_row, tiles_per_b=tpb, ctx_tiles=ctx_tiles, batch=batch)
    return pl.pallas_call(
        _router_kernel,
        grid=(t // tm,),
        in_specs=[
            pl.BlockSpec((tm, D_MODEL), lambda i: (i, 0)),
            pl.BlockSpec((None, 6, D_MODEL), lambda i: (row(i), 0, 0)),
            pl.BlockSpec((1, D_MODEL), lambda i: (0, 0)),
            pl.BlockSpec((D_MODEL, 128), lambda i: (0, 0)),
            pl.BlockSpec((1, 128), lambda i: (0, 0)),
        ],
        out_specs=[
            pl.BlockSpec((tm, D_MODEL), lambda i: (i, 0)),
            pl.BlockSpec((tm, 128), lambda i: (i, 0)),
            pl.BlockSpec((tm, 128), lambda i: (i, 0)),
        ],
        out_shape=[
            jax.ShapeDtypeStruct((t, D_MODEL), BF16),
            jax.ShapeDtypeStruct((t, 128), jnp.int32),
            jax.ShapeDtypeStruct((t, 128), F32),
        ],
        compiler_params=_params("arbitrary"),
        name="moe_router",
    )(x2, mod_l, g.reshape(1, D_MODEL), wr_pad, br_pad)


def _moe_plan(top_idx, top_gate, t, tb_rows):
    nk = t * TOP_K
    n_rows = -(-(nk + N_EXPERTS * (MOE_TILE - 1)) // MOE_TILE) * MOE_TILE
    n_sub = n_rows // MOE_SUB
    n_tb = t // tb_rows
    n_items = n_sub + N_EXPERTS * n_tb

    flat_e = top_idx.reshape(-1)
    flat_tok = jnp.arange(nk, dtype=jnp.int32) // TOP_K
    order = jnp.argsort(flat_e, stable=True)
    e_sorted = flat_e[order]
    counts = jnp.bincount(flat_e, length=N_EXPERTS)
    padded = (counts + MOE_TILE - 1) // MOE_TILE * MOE_TILE
    pad_end = jnp.cumsum(padded)
    start = jnp.cumsum(counts) - counts
    dest = (pad_end - padded)[e_sorted] + jnp.arange(nk, dtype=jnp.int32) - start[e_sorted]
    tok_buf = jnp.full((n_rows,), t, jnp.int32).at[dest].set(flat_tok[order])
    gate_buf = jnp.zeros((n_rows,), F32).at[dest].set(top_gate.reshape(-1)[order])
    tile_expert = jnp.minimum(
        jnp.searchsorted(pad_end // MOE_TILE, jnp.arange(n_rows // MOE_TILE), side="right"),
        N_EXPERTS - 1).astype(jnp.int32)

    tok_sub = tok_buf.reshape(n_sub, MOE_SUB)
    valid = tok_sub < t
    lo = jnp.min(jnp.where(valid, tok_sub, t), axis=1) // tb_rows
    hi = jnp.max(jnp.where(valid, tok_sub, -1), axis=1) // tb_rows
    has = jnp.any(valid, axis=1)
    lo = jnp.where(has, lo, 0).astype(jnp.int32)
    n_it = jnp.where(has, hi - lo + 1, 1).astype(jnp.int32)
    first_item = jnp.cumsum(n_it) - n_it
    total = jnp.sum(n_it)
    w = jnp.arange(n_items, dtype=jnp.int32)
    wv = jnp.minimum(w, total - 1)
    st = (jnp.searchsorted(first_item, wv, side="right") - 1).astype(jnp.int32)
    tb = lo[st] + wv - first_item[st]
    live = w < total
    is_first = jnp.logical_and(live, wv == first_item[st])
    is_last = jnp.logical_and(live, wv == first_item[st] + n_it[st] - 1)
    run_ffn = jnp.logical_and(is_last, st % 2 == 1)
    flags1 = (live.astype(jnp.int32) + 2 * is_first.astype(jnp.int32) + 4 * run_ffn.astype(jnp.int32))

    key = jnp.where(live, tb * n_sub + st, n_tb * n_sub)
    perm = jnp.argsort(key)
    live2 = live[perm]
    last_live = total - 1
    pos = jnp.minimum(w, last_live)
    st2 = st[perm][pos]
    tb2 = tb[perm][pos]
    prev_tb = jnp.concatenate([jnp.full((1,), -1, jnp.int32), tb2[:-1]])
    next_tb = jnp.concatenate([tb2[1:], jnp.full((1,), -1, jnp.int32)])
    first2 = jnp.logical_and(live2, tb2 != prev_tb)
    last2 = jnp.logical_and(live2, jnp.logical_or(tb2 != next_tb, w == last_live))
    flags2 = live2.astype(jnp.int32) + 2 * first2.astype(jnp.int32) + 4 * last2.astype(jnp.int32)

    return dict(n_rows=n_rows, n_sub=n_sub, n_items=n_items,
                tok_rows=tok_buf.reshape(n_rows // MOE_TILE, 2, MOE_SUB),
                gate_rows=gate_buf.reshape(n_rows // MOE_TILE, 2, MOE_SUB),
                tile_expert=tile_expert, st1=st, tb1=tb.astype(jnp.int32), flags1=flags1,
                st2=st2, tb2=tb2.astype(jnp.int32), flags2=flags2)


def _moe_ffn_kernel(st_ref, tb_ref, fl_ref, te_ref, h_ref, tok_ref, gate_ref, wgu_ref, bgu_ref,
                    wd_ref, bd_ref, o_ref, x_ref, *, tb_rows):
    w = pl.program_id(0)
    flags = fl_ref[w]
    half = st_ref[w] % 2
    row0 = pl.multiple_of(half * MOE_SUB, MOE_SUB)

    @pl.when(flags & 2 != 0)
    def _():
        x_ref[pl.ds(row0, MOE_SUB), :] = jnp.zeros((MOE_SUB, D_MODEL), F32)

    @pl.when(flags & 1 != 0)
    def _():
        tok_row = jnp.where(half == 0, tok_ref[0:1, :], tok_ref[1:2, :]).astype(F32)
        tok_col = _row_to_col(tok_row, MOE_SUB)
        base = (tb_ref[w] * tb_rows).astype(F32)
        col = lax.broadcasted_iota(jnp.int32, (MOE_SUB, tb_rows), 1).astype(F32) + base
        onehot = jnp.where(tok_col == col, 1.0, 0.0).astype(BF16)
        x_ref[pl.ds(row0, MOE_SUB), :] += _dot(onehot, h_ref[...])

    @pl.when(flags & 4 != 0)
    def _():
        gu = _dot(x_ref[...].astype(BF16), wgu_ref[...]) + bgu_ref[...]
        f = gu.shape[1] // 2
        g_lin = jnp.minimum(gu[:, 0:f], SWIGLU_LIMIT)
        up = jnp.clip(gu[:, f:], -SWIGLU_LIMIT, SWIGLU_LIMIT)
        act = g_lin * _sigmoid(SWIGLU_ALPHA * g_lin) * (up + 1.0)
        y = _dot(act.astype(BF16), wd_ref[...]) + bd_ref[...]
        gate_col = jnp.concatenate([_row_to_col(gate_ref[0:1, :], MOE_SUB),
                                    _row_to_col(gate_ref[1:2, :], MOE_SUB)], axis=0)
        o_ref[...] = y * gate_col


def _moe_ffn(plan, h2, wgu, bgu, wd, bd, tb_rows):
    n_rows = plan["n_rows"]
    f2 = wgu.shape[2]
    grid_spec = pltpu.PrefetchScalarGridSpec(
        num_scalar_prefetch=4,
        grid=(plan["n_items"],),
        in_specs=[
            pl.BlockSpec((tb_rows, D_MODEL), lambda w, st, tb, fl, te: (tb[w], 0)),
            pl.BlockSpec((None, 2, MOE_SUB), lambda w, st, tb, fl, te: (st[w] // 2, 0, 0)),
            pl.BlockSpec((None, 2, MOE_SUB), lambda w, st, tb, fl, te: (st[w] // 2, 0, 0)),
            pl.BlockSpec((None, D_MODEL, f2), lambda w, st, tb, fl, te: (te[st[w] // 2], 0, 0)),
            pl.BlockSpec((None, 1, f2), lambda w, st, tb, fl, te: (te[st[w] // 2], 0, 0)),
            pl.BlockSpec((None, f2 // 2, D_MODEL), lambda w, st, tb, fl, te: (te[st[w] // 2], 0, 0)),
            pl.BlockSpec((None, 1, D_MODEL), lambda w, st, tb, fl, te: (te[st[w] // 2], 0, 0)),
        ],
        out_specs=pl.BlockSpec((MOE_TILE, D_MODEL), lambda w, st, tb, fl, te: (st[w] // 2, 0)),
        scratch_shapes=[pltpu.VMEM((MOE_TILE, D_MODEL), F32)],
    )
    return pl.pallas_call(
        functools.partial(_moe_ffn_kernel, tb_rows=tb_rows),
        grid_spec=grid_spec,
        out_shape=jax.ShapeDtypeStruct((n_rows, D_MODEL), F32),
        compiler_params=_params("arbitrary"),
        name="moe_experts",
    )(plan["st1"], plan["tb1"], plan["flags1"], plan["tile_expert"],
      h2, plan["tok_rows"], plan["gate_rows"], wgu, bgu, wd, bd)


def _moe_combine_kernel(st_ref, tb_ref, fl_ref, y_ref, tok_ref, x_ref, modb_ref, modc_ref,
                        o_ref, acc_ref, *, tb_rows, blocks_per_b, ctx_len):
    w = pl.program_id(0)
    flags = fl_ref[w]

    @pl.when(flags & 2 != 0)
    def _():
        acc_ref[...] = jnp.zeros_like(acc_ref)

    @pl.when(flags & 1 != 0)
    def _():
        half = st_ref[w] % 2
        tok_row = jnp.where(half == 0, tok_ref[0:1, :], tok_ref[1:2, :])
        rows = lax.broadcasted_iota(jnp.int32, (tb_rows, MOE_SUB), 0) + tb_ref[w] * tb_rows
        onehot_t = jnp.where(rows == tok_row, 1.0, 0.0).astype(BF16)
        acc_ref[...] += _dot(onehot_t, y_ref[...].astype(BF16))

    @pl.when(flags & 4 != 0)
    def _():
        row_in_b = lax.broadcasted_iota(jnp.int32, (tb_rows, 1), 0) + (tb_ref[w] % blocks_per_b) * tb_rows
        in_ctx = row_in_b < ctx_len
        g2 = jnp.where(in_ctx, modc_ref[5:6, :], modb_ref[5:6, :])
        o_ref[...] = x_ref[...] + g2 * acc_ref[...]


def _moe_combine(plan, y_sorted, x2, mod_l, tb_rows, blocks_per_b, ctx_len, batch):
    t = x2.shape[0]
    grid_spec = pltpu.PrefetchScalarGridSpec(
        num_scalar_prefetch=3,
        grid=(plan["n_items"],),
        in_specs=[
            pl.BlockSpec((MOE_SUB, D_MODEL), lambda w, st, tb, fl: (st[w], 0)),
            pl.BlockSpec((None, 2, MOE_SUB), lambda w, st, tb, fl: (st[w] // 2, 0, 0)),
            pl.BlockSpec((tb_rows, D_MODEL), lambda w, st, tb, fl: (tb[w], 0)),
            pl.BlockSpec((None, 6, D_MODEL), lambda w, st, tb, fl: (tb[w] // blocks_per_b, 0, 0)),
            pl.BlockSpec((None, 6, D_MODEL), lambda w, st, tb, fl: (batch, 0, 0)),
        ],
        out_specs=pl.BlockSpec((tb_rows, D_MODEL), lambda w, st, tb, fl: (tb[w], 0)),
        scratch_shapes=[pltpu.VMEM((tb_rows, D_MODEL), F32)],
    )
    return pl.pallas_call(
        functools.partial(_moe_combine_kernel, tb_rows=tb_rows, blocks_per_b=blocks_per_b,
                          ctx_len=ctx_len),
        grid_spec=grid_spec,
        out_shape=jax.ShapeDtypeStruct((t, D_MODEL), F32),
        compiler_params=_params("arbitrary"),
        name="moe_combine",
    )(plan["st2"], plan["tb2"], plan["flags2"], y_sorted, plan["tok_rows"], x2, mod_l, mod_l)


def _route_kernel(x_ref, mod_ref, g_ref, wr_ref, br_ref, tri_ref,
                  h_out, idx_out, rank_out, gate_out, cnt_out, run_ref):
    @pl.when(pl.program_id(0) == 0)
    def _():
        run_ref[...] = jnp.zeros_like(run_ref)

    x = x_ref[...]
    y = x * lax.rsqrt(jnp.mean(x * x, axis=-1, keepdims=True) + NORM_EPS) * g_ref[...]
    h = y * (1.0 + mod_ref[4:5, :]) + mod_ref[3:4, :]
    h_out[...] = h
    logits = _dot(h.astype(BF16), wr_ref[...]) + br_ref[...]
    lane = lax.broadcasted_iota(jnp.int32, logits.shape, 1).astype(F32)
    vals, idxs = [], []
    for _ in range(TOP_K):
        m = jnp.max(logits, axis=-1, keepdims=True)
        sel = jnp.min(jnp.where(logits == m, lane, 128.0), axis=-1, keepdims=True)
        vals.append(m)
        idxs.append(sel)
        logits = jnp.where(lane == sel, -jnp.inf, logits)
    es = [jnp.exp(vv - vals[0]) for vv in vals]
    den = es[0] + es[1] + es[2] + es[3]
    chosen = jnp.zeros(lane.shape, F32)
    for kq in range(TOP_K):
        chosen = jnp.where(lane == idxs[kq], 1.0, chosen)
    before = _dot(tri_ref[...], chosen.astype(BF16)) + run_ref[...]
    idx_row = jnp.zeros(lane.shape, F32)
    rank_row = jnp.zeros(lane.shape, F32)
    gate_row = jnp.zeros(lane.shape, F32)
    for kq in range(TOP_K):
        rk = jnp.sum(jnp.where(lane == idxs[kq], before, 0.0), axis=-1, keepdims=True)
        idx_row = jnp.where(lane == float(kq), idxs[kq], idx_row)
        rank_row = jnp.where(lane == float(kq), rk, rank_row)
        gate_row = jnp.where(lane == float(kq), es[kq] / den, gate_row)
    idx_out[...] = idx_row.astype(jnp.int32)
    rank_out[...] = rank_row.astype(jnp.int32)
    gate_out[...] = gate_row
    run_ref[...] += jnp.sum(chosen, axis=0, keepdims=True)
    cnt_out[...] = run_ref[...]


def _route(x2, mod_l, g, wr_pad, br_pad, tri, geom):
    t = x2.shape[0]
    tm = ROW_TILE
    tpb, ctx_tiles, batch = geom
    row = functools.partial(_mod_row, tiles_per_b=tpb, ctx_tiles=ctx_tiles, batch=batch)
    lanes = lambda dt: jax.ShapeDtypeStruct((t, 128), dt)
    tile = pl.BlockSpec((tm, 128), lambda i: (i, 0))
    return pl.pallas_call(
        _route_kernel,
        grid=(t // tm,),
        in_specs=[
            pl.BlockSpec((tm, D_MODEL), lambda i: (i, 0)),
            pl.BlockSpec((None, 6, D_MODEL), lambda i: (row(i), 0, 0)),
            pl.BlockSpec((1, D_MODEL), lambda i: (0, 0)),
            pl.BlockSpec((D_MODEL, 128), lambda i: (0, 0)),
            pl.BlockSpec((1, 128), lambda i: (0, 0)),
            pl.BlockSpec((tm, tm), lambda i: (0, 0)),
        ],
        out_specs=[pl.BlockSpec((tm, D_MODEL), lambda i: (i, 0)), tile, tile, tile,
                   pl.BlockSpec((1, 128), lambda i: (0, 0))],
        out_shape=[jax.ShapeDtypeStruct((t, D_MODEL), F32), lanes(jnp.int32), lanes(jnp.int32),
                   lanes(F32), jax.ShapeDtypeStruct((1, 128), F32)],
        scratch_shapes=[pltpu.VMEM((1, 128), F32)],
        compiler_params=_params("arbitrary"),
        name="moe_router",
    )(x2, mod_l, g.reshape(1, D_MODEL), wr_pad, br_pad, tri)


def _moe_layout(idx, rank, counts, t):
    n_rows = -(-(t * TOP_K + N_EXPERTS * (MOE_TILE - 1)) // MOE_TILE) * MOE_TILE
    n_tiles = n_rows // MOE_TILE
    padded = (counts + MOE_TILE - 1) // MOE_TILE * MOE_TILE
    pad_end = jnp.cumsum(padded)
    pad_start = pad_end - padded
    onehot = idx[..., None] == jnp.arange(N_EXPERTS, dtype=jnp.int32)
    dest = jnp.sum(jnp.where(onehot, pad_start, 0), axis=-1) + rank
    tile_end = pad_end // MOE_TILE
    tile_expert = jnp.sum(jnp.arange(n_tiles, dtype=jnp.int32)[:, None] >= tile_end[None, :], axis=1)
    tile_expert = jnp.minimum(tile_expert, N_EXPERTS - 1).astype(jnp.int32)
    n_used = tile_end[-1:].astype(jnp.int32)
    return dest.astype(jnp.int32).reshape(t // ROW_TILE, 1, ROW_TILE * TOP_K), tile_expert, n_used, n_rows


def _dispatch_kernel(dest_ref, h_ref, xs_in_ref, xs_ref, sem):
    del xs_in_ref
    tm = h_ref.shape[0]

    def issue(tok, carry):
        src = h_ref.at[pl.ds(tok, 1), :]
        for kq in range(TOP_K):
            row = dest_ref[0, tok * TOP_K + kq]
            pltpu.make_async_copy(src, xs_ref.at[pl.ds(row, 1), :], sem.at[0]).start()
        return carry

    lax.fori_loop(0, tm, issue, 0, unroll=4)
    for _ in range(TOP_K):
        pltpu.make_async_copy(h_ref, xs_ref.at[pl.ds(0, tm), :], sem.at[0]).wait()


def _dispatch(dest, h2, xs_prev):
    t = h2.shape[0]
    tm = ROW_TILE
    n_rows = xs_prev.shape[0]
    return pl.pallas_call(
        _dispatch_kernel,
        grid=(t // tm,),
        in_specs=[
            pl.BlockSpec((None, 1, tm * TOP_K), lambda i: (i, 0, 0), memory_space=pltpu.SMEM),
            pl.BlockSpec((tm, D_MODEL), lambda i: (i, 0)),
            pl.BlockSpec(memory_space=pl.ANY),
        ],
        out_specs=pl.BlockSpec(memory_space=pl.ANY),
        out_shape=jax.ShapeDtypeStruct((n_rows, D_MODEL), F32),
        scratch_shapes=[pltpu.SemaphoreType.DMA((1,))],
        input_output_aliases={2: 0},
        compiler_params=_params("arbitrary"),
        name="moe_dispatch",
    )(dest, h2, xs_prev)


def _expert_kernel(te_ref, nu_ref, x_ref, wgu_ref, bgu_ref, wd_ref, bd_ref, o_ref):
    del te_ref
    used = pl.program_id(0) < nu_ref[0]

    @pl.when(jnp.logical_not(used))
    def _():
        o_ref[...] = jnp.zeros_like(o_ref)

    @pl.when(used)
    def _():
        gu = _dot(x_ref[...].astype(BF16), wgu_ref[...]) + bgu_ref[...]
        f = gu.shape[1] // 2
        g_lin = jnp.minimum(gu[:, 0:f], SWIGLU_LIMIT)
        up = jnp.clip(gu[:, f:], -SWIGLU_LIMIT, SWIGLU_LIMIT)
        act = g_lin * _sigmoid(SWIGLU_ALPHA * g_lin) * (up + 1.0)
        o_ref[...] = _dot(act.astype(BF16), wd_ref[...]) + bd_ref[...]


def _experts(xs, tile_expert, n_used, wgu, bgu, wd, bd):
    n_rows = xs.shape[0]
    f2 = wgu.shape[2]
    used = lambda g, nu: jnp.minimum(g, nu[0] - 1)
    grid_spec = pltpu.PrefetchScalarGridSpec(
        num_scalar_prefetch=2,
        grid=(n_rows // MOE_TILE,),
        in_specs=[
            pl.BlockSpec((MOE_TILE, D_MODEL), lambda g, te, nu: (used(g, nu), 0)),
            pl.BlockSpec((None, D_MODEL, f2), lambda g, te, nu: (te[used(g, nu)], 0, 0)),
            pl.BlockSpec((None, 1, f2), lambda g, te, nu: (te[used(g, nu)], 0, 0)),
            pl.BlockSpec((None, f2 // 2, D_MODEL), lambda g, te, nu: (te[used(g, nu)], 0, 0)),
            pl.BlockSpec((None, 1, D_MODEL), lambda g, te, nu: (te[used(g, nu)], 0, 0)),
        ],
        out_specs=pl.BlockSpec((MOE_TILE, D_MODEL), lambda g, te, nu: (g, 0)),
    )
    return pl.pallas_call(
        _expert_kernel,
        grid_spec=grid_spec,
        out_shape=jax.ShapeDtypeStruct((n_rows, D_MODEL), F32),
        compiler_params=_params("arbitrary"),
        name="moe_experts",
    )(tile_expert, n_used, xs, wgu, bgu, wd, bd)


def _collect_kernel(dest_ref, dest_next_ref, gate_ref, x_ref, mod_ref, y_ref, o_ref, buf_ref, sem):
    tm = x_ref.shape[0]
    step = pl.program_id(0)
    slot = step % 2

    def issue_tile(d_ref, to_slot):
        def issue(tok, carry):
            for kq in range(TOP_K):
                row = d_ref[0, tok * TOP_K + kq]
                pltpu.make_async_copy(y_ref.at[pl.ds(row, 1), :],
                                      buf_ref.at[to_slot, kq, pl.ds(tok, 1), :], sem.at[to_slot]).start()
            return carry

        lax.fori_loop(0, tm, issue, 0, unroll=4)

    @pl.when(step == 0)
    def _():
        issue_tile(dest_ref, slot)

    @pl.when(step + 1 < pl.num_programs(0))
    def _():
        issue_tile(dest_next_ref, 1 - slot)

    for kq in range(TOP_K):
        pltpu.make_async_copy(y_ref.at[pl.ds(0, tm), :], buf_ref.at[slot, kq], sem.at[slot]).wait()
    gate = gate_ref[...]
    acc = gate[:, 0:1] * buf_ref[slot, 0]
    for kq in range(1, TOP_K):
        acc = acc + gate[:, kq:kq + 1] * buf_ref[slot, kq]
    o_ref[...] = x_ref[...] + mod_ref[5:6, :] * acc


def _collect(dest, gate_rows, x2, mod_l, ys, geom, latent_only):
    t = x2.shape[0]
    tm = ROW_TILE
    tpb, ctx_tiles, batch = geom
    row = functools.partial(_mod_row, tiles_per_b=tpb, ctx_tiles=ctx_tiles, batch=batch)
    if latent_only:
        lat = tpb - ctx_tiles
        n_steps = batch * lat
        tile = lambda i: (i // lat) * tpb + ctx_tiles + i % lat
    else:
        n_steps = t // tm
        tile = lambda i: i
    nxt = lambda i: tile(jnp.minimum(i + 1, n_steps - 1))
    return pl.pallas_call(
        _collect_kernel,
        grid=(n_steps,),
        in_specs=[
            pl.BlockSpec((None, 1, tm * TOP_K), lambda i: (tile(i), 0, 0), memory_space=pltpu.SMEM),
            pl.BlockSpec((None, 1, tm * TOP_K), lambda i: (nxt(i), 0, 0), memory_space=pltpu.SMEM),
            pl.BlockSpec((tm, 128), lambda i: (tile(i), 0)),
            pl.BlockSpec((tm, D_MODEL), lambda i: (tile(i), 0)),
            pl.BlockSpec((None, 6, D_MODEL), lambda i: (row(tile(i)), 0, 0)),
            pl.BlockSpec(memory_space=pl.ANY),
        ],
        out_specs=pl.BlockSpec((tm, D_MODEL), lambda i: (i, 0)),
        out_shape=jax.ShapeDtypeStruct((n_steps * tm, D_MODEL), F32),
        scratch_shapes=[pltpu.VMEM((2, TOP_K, tm, D_MODEL), F32), pltpu.SemaphoreType.DMA((2,))],
        compiler_params=_params("arbitrary"),
        name="moe_collect",
    )(dest, dest, gate_rows, x2, mod_l, ys)


def _rope_angles(pos, dim):
    inv = ROPE_BASE ** (-jnp.arange(0, dim, 2, dtype=F32) / dim)
    return pos.astype(F32)[:, None] * inv[None, :]


def _rope_tables(seq, ctx_len):
    rows = seq // GRID_W
    pos = jnp.arange(seq, dtype=jnp.int32)
    ang = _rope_angles(pos, RET_DK)
    r_cos = jnp.tile(jnp.concatenate([jnp.cos(ang), jnp.cos(ang)], -1), (1, RET_HEADS))
    r_sin = jnp.tile(jnp.concatenate([-jnp.sin(ang), jnp.sin(ang)], -1), (1, RET_HEADS))
    ra = _rope_angles(pos // GRID_W, MLA_ROPE // 2)
    ca = _rope_angles(pos % GRID_W, MLA_ROPE // 2)
    one = jnp.ones((seq, MLA_NOPE), F32)
    zero = jnp.zeros((seq, MLA_NOPE), F32)
    m_cos = jnp.concatenate([one, jnp.cos(ra), jnp.cos(ra), jnp.cos(ca), jnp.cos(ca), one[:, :32]], -1)
    m_sin = jnp.concatenate([zero, -jnp.sin(ra), jnp.sin(ra), -jnp.sin(ca), jnp.sin(ca), zero[:, :32]], -1)
    del rows
    ctx1 = lambda n: jnp.ones((ctx_len, n), F32)
    ctx0 = lambda n: jnp.zeros((ctx_len, n), F32)
    return (jnp.concatenate([ctx1(256), r_cos], 0), jnp.concatenate([ctx0(256), r_sin], 0),
            jnp.concatenate([ctx1(128), m_cos], 0), jnp.concatenate([ctx0(128), m_sin], 0))


def kernel(x, c, ctx, c_ctx, ada_w, ada_b, norm1_g, norm2_g, w_in, mla_q_norm_g, mla_w_uq, mla_kv_norm_g, mla_w_ukv, mla_qk_q_g, mla_qk_k_g, rwkv_mu_prev, rwkv_mu_next, rwkv_w0, rwkv_w2, rwkv_a0, rwkv_a2, rwkv_g2, rwkv_k_k, rwkv_k_a, rwkv_r_k, rwkv_lnx_g, rwkv_lnx_b, w_br_ret, w_br_mla, w_br_rwkv, w_out, w_router, b_router, w_gu, b_gu, w_down, b_down):
    b, s, d = x.shape
    ctx_len = ctx.shape[1]
    depth = ada_w.shape[0]
    l = ctx_len + s
    t = b * l
    assert d == D_MODEL and b < 16 and ctx_len % ROW_TILE == 0 and s % ROW_TILE == 0
    geom = (l // ROW_TILE, ctx_len // ROW_TILE, b)
    xa =jnp.concatenate([ctx, x], axis=1).reshape(t, d)
    cc = jnp.zeros((16, d), F32).at[:b].set(c).at[b].set(c_ctx)
    mod = _ada_all(cc, ada_w.astype(BF16), ada_b).reshape(depth, 16, 6, d)

    ret_cos, ret_sin, mla_cos, mla_sin = _rope_tables(s, ctx_len)
    ret_consts = _retention_consts()
    rwkv_consts = _rwkv_consts()
    head_blk = jnp.asarray(np.kron(np.eye(RWKV_HEADS), np.ones((RWKV_HD, RWKV_HD))), BF16)
    route_tri = jnp.asarray(np.tril(np.ones((ROW_TILE, ROW_TILE)), -1), BF16)

    xs = None
    for li in range(depth):
        p2 = _inproj(xa, mod[li], norm1_g[li], _pad_w_in(w_in[li]), geom)
        p3 = p2.reshape(b, l, P_COLS)

        ret = _retention(p3, ret_cos, ret_sin, ret_consts, ctx_len).reshape(2, t, RET_W)

        mw = _mla_weights(mla_q_norm_g[li], mla_w_uq[li], mla_kv_norm_g[li], mla_w_ukv[li],
                          mla_qk_q_g[li], mla_qk_k_g[li])
        q2, k2, v2 = _mla_prep(p2, mw, mla_cos, mla_sin, l)
        mla = _attention(q2.reshape(b, l, 1024), k2.reshape(b, l, 1024), v2.reshape(b, l, 1024),
                         ctx_len).reshape(t, MLA_W)

        rw = dict(mup=rwkv_mu_prev[li].reshape(1, -1), mun=rwkv_mu_next[li].reshape(1, -1),
                  w0=rwkv_w0[li].reshape(2, 1, RWKV_W), w2=rwkv_w2[li].astype(BF16),
                  a0=rwkv_a0[li].reshape(2, 1, RWKV_W), a2=rwkv_a2[li].astype(BF16),
                  g2=rwkv_g2[li].astype(BF16), kk=rwkv_k_k[li].reshape(1, -1),
                  ka=rwkv_k_a[li].reshape(1, -1), rk=rwkv_r_k[li].reshape(1, -1), blk=head_blk)
        r_, v_, na_, kd_, bb_, lw_, g_, bonus_ = _rwkv_elem(p2, rw, l, ctx_len)
        yf, yb = _rwkv_chunk(dict(r=r_, v=v_, na=na_, kd=kd_, bb=bb_, lw=lw_), b, l, ctx_len, rwkv_consts)

        mg = dict(lng=rwkv_lnx_g[li].reshape(1, -1), lnb=rwkv_lnx_b[li].reshape(1, -1), blk=head_blk,
                  wr=w_br_ret[li].astype(BF16), wm=w_br_mla[li].astype(BF16),
                  ww=w_br_rwkv[li].astype(BF16), wo=w_out[li].astype(BF16))
        xa = _merge(xa, mod[li], p2, ret, mla, yf.reshape(t, RWKV_W), yb.reshape(t, RWKV_W), bonus_, g_,
                    mg, geom)

        wr_pad = jnp.pad(w_router[li], ((0, 0), (0, 128 - N_EXPERTS))).astype(BF16)
        br_pad = jnp.concatenate([b_router[li], jnp.full((128 - N_EXPERTS,), -1e30, F32)]).reshape(1, 128)
        h2, idx_rows, rank_rows, gate_rows, cnt = _route(xa, mod[li], norm2_g[li], wr_pad, br_pad,
                                                         route_tri, geom)
        dest, tile_expert, n_used, n_rows = _moe_layout(
            idx_rows[:, :TOP_K], rank_rows[:, :TOP_K], cnt[0, :N_EXPERTS].astype(jnp.int32), t)
        if xs is None:
            xs = jnp.zeros((n_rows, D_MODEL), F32)
        xs = _dispatch(dest, h2, xs)
        ys = _experts(xs, tile_expert, n_used, w_gu[li].astype(BF16), b_gu[li].reshape(N_EXPERTS, 1, -1),
                      w_down[li].astype(BF16), b_down[li].reshape(N_EXPERTS, 1, -1))
        xa = _collect(dest, gate_rows, xa, mod[li], ys, geom, latent_only=li == depth - 1)

    return xa.reshape(b, s, d)
```

```python
import functools

import jax
import jax.numpy as jnp
import numpy as np
from jax import lax
from jax.experimental import pallas as pl
from jax.experimental.pallas import tpu as pltpu

F32 = jnp.float32
BF16 = jnp.bfloat16

D_MODEL = 1024
GRID_W = 64
RET_HEADS, RET_DK, RET_DV, RET_CHUNK = 4, 64, 128, 128
RET_W = RET_HEADS * RET_DV
MLA_HEADS, MLA_Q_RANK, MLA_KV_RANK, MLA_NOPE, MLA_ROPE, MLA_DV = 8, 256, 128, 64, 32, 64
MLA_QK = MLA_NOPE + MLA_ROPE
MLA_W = MLA_HEADS * MLA_DV
RWKV_HEADS, RWKV_HD = 8, 64
RWKV_W = RWKV_HEADS * RWKV_HD
RWKV_DECAY_LORA, RWKV_A_LORA, RWKV_GATE_LORA = 64, 64, 128
RWKV_GN_EPS = 64e-5
RWKV_IN = 3 * RWKV_W + 2 * RWKV_DECAY_LORA + 2 * RWKV_A_LORA + RWKV_GATE_LORA
N_EXPERTS, TOP_K = 32, 4
SWIGLU_LIMIT, SWIGLU_ALPHA = 7.0, 1.702
ROPE_BASE = 10000.0
NORM_EPS = 1e-6
HEAD_NORM_EPS = 1e-5

GATE_OFF = 0
RET_OFF = 3 * D_MODEL
MLA_OFF = RET_OFF + 2048
RWKV_OFF = MLA_OFF + 512 + 128
P_COLS = RWKV_OFF + RWKV_IN
IN_COL_TILE = 3840

ROW_TILE = 256
RWKV_CHUNK = 64
HALF = 256
ATTN_Q_SCALE = MLA_QK ** -0.5 * 1.4426950408889634
ATTN_KV_SPLIT = 1
MOE_SUB = 128
MOE_TILE = 256
VMEM_LIMIT = 48 * 1024 * 1024

NT_DIMS = (((1,), (1,)), ((), ()))
TN_DIMS = (((0,), (0,)), ((), ()))


def _params(*sem):
    return pltpu.CompilerParams(dimension_semantics=sem, vmem_limit_bytes=VMEM_LIMIT)


def _dot(a, b):
    return jnp.dot(a, b, preferred_element_type=F32)


def _dot_nt(a, b):
    return lax.dot_general(a, b, NT_DIMS, preferred_element_type=F32)


def _dot_tn(a, b):
    return lax.dot_general(a, b, TN_DIMS, preferred_element_type=F32)


def _split3(x):
    hi = x.astype(BF16)
    r1 = x - hi.astype(F32)
    mid = r1.astype(BF16)
    lo = (r1 - mid.astype(F32)).astype(BF16)
    return hi, mid, lo


def _dot_exact_rhs01(x, m01):
    hi, mid, lo = _split3(x)
    return _dot(hi, m01) + _dot(mid, m01) + _dot(lo, m01)


def _dot_exact_lhs01(m01, x):
    hi, mid, lo = _split3(x)
    return _dot(m01, hi) + _dot(m01, mid) + _dot(m01, lo)


def _sigmoid(x):
    return 1.0 / (1.0 + jnp.exp(-x))


def _silu(x):
    return x * _sigmoid(x)


def _row_to_col(row, n):
    eye = lax.broadcasted_iota(jnp.int32, (n, n), 0) == lax.broadcasted_iota(jnp.int32, (n, n), 1)
    return jnp.sum(jnp.where(eye, jnp.broadcast_to(row, (n, n)), 0.0), axis=1, keepdims=True)


def _ada_kernel(c_ref, w_ref, b_ref, o_ref):
    s = _silu(c_ref[...])
    o_ref[...] = _dot(s.astype(BF16), w_ref[...]) + b_ref[...]


def _ada_all(cc, ada_w_bf, ada_b):
    depth = ada_w_bf.shape[0]
    tn = 1536
    return pl.pallas_call(
        _ada_kernel,
        grid=(depth, 6 * D_MODEL // tn),
        in_specs=[
            pl.BlockSpec((16, D_MODEL), lambda l, j: (0, 0)),
            pl.BlockSpec((None, D_MODEL, tn), lambda l, j: (l, 0, j)),
            pl.BlockSpec((None, 1, tn), lambda l, j: (l, 0, j)),
        ],
        out_specs=pl.BlockSpec((None, 16, tn), lambda l, j: (l, 0, j)),
        out_shape=jax.ShapeDtypeStruct((depth, 16, 6 * D_MODEL), F32),
        compiler_params=_params("arbitrary", "arbitrary"),
        name="ada_mod",
    )(cc, ada_w_bf, ada_b.reshape(depth, 1, 6 * D_MODEL))


def _mod_row(i, tiles_per_b, ctx_tiles, batch):
    return jnp.where(i % tiles_per_b < ctx_tiles, batch, i // tiles_per_b)


def _inproj_kernel(x_ref, mod_ref, g_ref, w_ref, o_ref):
    x = x_ref[...]
    y = x * lax.rsqrt(jnp.mean(x * x, axis=-1, keepdims=True) + NORM_EPS) * g_ref[...]
    h = y * (1.0 + mod_ref[1:2, :]) + mod_ref[0:1, :]
    o_ref[...] = _dot(h.astype(BF16), w_ref[...])


def _inproj(x2, mod_l, g, w_pad, geom):
    t = x2.shape[0]
    tpb, ctx_tiles, batch = geom
    row = functools.partial(_mod_row, tiles_per_b=tpb, ctx_tiles=ctx_tiles, batch=batch)
    return pl.pallas_call(
        _inproj_kernel,
        grid=(P_COLS // IN_COL_TILE, t // ROW_TILE),
        in_specs=[
            pl.BlockSpec((ROW_TILE, D_MODEL), lambda j, i: (i, 0)),
            pl.BlockSpec((None, 6, D_MODEL), lambda j, i: (row(i), 0, 0)),
            pl.BlockSpec((1, D_MODEL), lambda j, i: (0, 0)),
            pl.BlockSpec((D_MODEL, IN_COL_TILE), lambda j, i: (0, j)),
        ],
        out_specs=pl.BlockSpec((ROW_TILE, IN_COL_TILE), lambda j, i: (i, j)),
        out_shape=jax.ShapeDtypeStruct((t, P_COLS), F32),
        compiler_params=_params("arbitrary", "arbitrary"),
        name="in_proj",
    )(x2, mod_l, g.reshape(1, D_MODEL), w_pad)


def _pad_w_in(w_in):
    d = w_in.shape[0]
    ret = w_in[:, 0:2048]
    cq = w_in[:, 2048:2304]
    ckv = w_in[:, 2304:2432]
    kpe = w_in[:, 2432:2464]
    rwkv = w_in[:, 2464:2464 + RWKV_IN]
    gate = w_in[:, 2464 + RWKV_IN:]
    z = lambda n: jnp.zeros((d, n), w_in.dtype)
    return jnp.concatenate([gate, ret, cq, ckv, z(64), kpe, z(32), z(128), rwkv], axis=1).astype(BF16)


def _scan_chunk(d, j, n_ctx, n_all):
    bwd = jnp.where(j < n_ctx, n_ctx - 1 - j, n_all - 1 - j + n_ctx)
    return jnp.where(d == 0, j, bwd)


def _ret_kernel(q_ref, k_ref, v_ref, g_ref, cos_ref, sin_ref, dmat_ref, qdec_ref, kdec_ref,
                o_ref, s_ref, *, cdec):
    @pl.when(pl.program_id(2) == 0)
    def _():
        s_ref[...] = jnp.zeros_like(s_ref)

    c = RET_CHUNK
    w = RET_HEADS * RET_DK
    cos = cos_ref[...]
    sin = sin_ref[...]
    first = (lax.broadcasted_iota(jnp.int32, (c, w), 1) & (RET_DK - 1)) < RET_DK // 2

    def rot(x):
        partner = jnp.where(first, pltpu.roll(x, w - RET_DK // 2, 1), pltpu.roll(x, RET_DK // 2, 1))
        return x * cos + partner * sin

    q = rot(q_ref[...])
    k = rot(k_ref[...]) * (RET_DK ** -0.5)
    kd_t = (k * kdec_ref[...]).T.astype(BF16)
    qb = q.astype(BF16)
    kb = k.astype(BF16)
    vb = v_ref[...].astype(BF16)
    g = g_ref[...]
    heads = range(RET_HEADS)
    qs = [qb[:, h * RET_DK:(h + 1) * RET_DK] for h in heads]
    ks = [kb[:, h * RET_DK:(h + 1) * RET_DK] for h in heads]
    vs = [vb[:, h * RET_DV:(h + 1) * RET_DV] for h in heads]
    s_old = [s_ref[h] for h in heads]
    atts = [_dot_nt(qs[h], ks[h]) * dmat_ref[h] for h in heads]
    inter = [_dot(qs[h], s_old[h].astype(BF16)) * qdec_ref[h] for h in heads]
    kvs = [_dot(kd_t[h * RET_DK:(h + 1) * RET_DK, :], vs[h]) for h in heads]
    outs = [_dot(atts[h].astype(BF16), vs[h]) + inter[h] for h in heads]
    for h in heads:
        s_ref[h] = s_old[h] * cdec[h] + kvs[h]
        o = outs[h]
        oc = o - jnp.mean(o, axis=-1, keepdims=True)
        y = oc * lax.rsqrt(jnp.mean(oc * oc, axis=-1, keepdims=True) + HEAD_NORM_EPS)
        o_ref[:, h * RET_DV:(h + 1) * RET_DV] = y * _silu(g[:, h * RET_DV:(h + 1) * RET_DV])


def _ret_pair_kernel(qf_ref, kf_ref, vf_ref, gf_ref, cosf_ref, sinf_ref,
                     qb_ref, kb_ref, vb_ref, gb_ref, cosb_ref, sinb_ref,
                     dmat_ref, qdec_ref, kdec_ref, of_ref, ob_ref, s_ref, *, cdec):
    @pl.when(pl.program_id(1) == 0)
    def _():
        s_ref[...] = jnp.zeros_like(s_ref)

    c = RET_CHUNK
    w = RET_HEADS * RET_DK
    first = (lax.broadcasted_iota(jnp.int32, (c, w), 1) & (RET_DK - 1)) < RET_DK // 2

    def rot(x, cos, sin):
        partner = jnp.where(first, pltpu.roll(x, w - RET_DK // 2, 1), pltpu.roll(x, RET_DK // 2, 1))
        return x * cos + partner * sin

    dirs = ((qf_ref, kf_ref, vf_ref, gf_ref, cosf_ref, sinf_ref, of_ref),
            (qb_ref, kb_ref, vb_ref, gb_ref, cosb_ref, sinb_ref, ob_ref))
    inst = []
    for d, (q_ref, k_ref, v_ref, g_ref, cos_ref, sin_ref, o_ref) in enumerate(dirs):
        cos, sin = cos_ref[...], sin_ref[...]
        q = rot(q_ref[...], cos, sin).astype(BF16)
        k = rot(k_ref[...], cos, sin) * (RET_DK ** -0.5)
        kd_t = (k * kdec_ref[d]).T.astype(BF16)
        kb = k.astype(BF16)
        vb = v_ref[...].astype(BF16)
        for h in range(RET_HEADS):
            ksl = slice(h * RET_DK, (h + 1) * RET_DK)
            vsl = slice(h * RET_DV, (h + 1) * RET_DV)
            inst.append(dict(d=d, h=h, vsl=vsl, q=q[:, ksl], k=kb[:, ksl], v=vb[:, vsl], kd=kd_t[ksl, :],
                             g_ref=g_ref, o_ref=o_ref))
    s_old = [s_ref[s["d"], s["h"]] for s in inst]
    atts = [_dot_nt(s["q"], s["k"]) * dmat_ref[s["d"], s["h"]] for s in inst]
    inter = [_dot(s["q"], so.astype(BF16)) * qdec_ref[s["d"], s["h"]] for s, so in zip(inst, s_old)]
    kvs = [_dot(s["kd"], s["v"]) for s in inst]
    outs = [_dot(att.astype(BF16), s["v"]) + it for s, att, it in zip(inst, atts, inter)]
    for s, so, kv, o in zip(inst, s_old, kvs, outs):
        s_ref[s["d"], s["h"]] = so * cdec[s["h"]] + kv
        oc = o - jnp.mean(o, axis=-1, keepdims=True)
        y = oc * lax.rsqrt(jnp.mean(oc * oc, axis=-1, keepdims=True) + HEAD_NORM_EPS)
        s["o_ref"][:, s["vsl"]] = y * _silu(s["g_ref"][:, s["vsl"]])


def _retention(p3, cos_t, sin_t, consts, ctx_len):
    b, l, _ = p3.shape
    n_all = l // RET_CHUNK
    n_ctx = ctx_len // RET_CHUNK
    dmat, qdec, kdec, cdec = consts
    ch = functools.partial(_scan_chunk, n_ctx=n_ctx, n_all=n_all)
    qb, vb = RET_OFF // 256, RET_OFF // 512

    def specs(d):
        return [
            pl.BlockSpec((None, RET_CHUNK, 256), lambda bi, j: (bi, ch(d, j), qb)),
            pl.BlockSpec((None, RET_CHUNK, 256), lambda bi, j: (bi, ch(d, j), qb + 1)),
            pl.BlockSpec((None, RET_CHUNK, 512), lambda bi, j: (bi, ch(d, j), vb + 1)),
            pl.BlockSpec((None, RET_CHUNK, 512), lambda bi, j: (bi, ch(d, j), vb + 2 + d)),
            pl.BlockSpec((RET_CHUNK, 256), lambda bi, j: (ch(d, j), 0)),
            pl.BlockSpec((RET_CHUNK, 256), lambda bi, j: (ch(d, j), 0)),
        ]

    const = lambda shape: pl.BlockSpec(shape, lambda bi, j: (0,) * len(shape))
    out = jax.ShapeDtypeStruct((b, l, RET_W), F32)
    out_spec = lambda d: pl.BlockSpec((None, RET_CHUNK, RET_W), lambda bi, j: (bi, ch(d, j), 0))
    return pl.pallas_call(
        functools.partial(_ret_pair_kernel, cdec=cdec),
        grid=(b, n_all),
        in_specs=specs(0) + specs(1) + [
            const((2, RET_HEADS, RET_CHUNK, RET_CHUNK)), const((2, RET_HEADS, RET_CHUNK, RET_DV)),
            const((2, RET_CHUNK, 256))],
        out_specs=[out_spec(0), out_spec(1)],
        out_shape=[out, out],
        scratch_shapes=[pltpu.VMEM((2, RET_HEADS, RET_DK, RET_DV), F32)],
        compiler_params=_params("arbitrary", "arbitrary"),
        name="retention",
    )(p3, p3, p3, p3, cos_t, sin_t, p3, p3, p3, p3, cos_t, sin_t, dmat, qdec, kdec)


def _retention_consts():
    gamma = 1.0 - jnp.exp2(-5.0 - jnp.arange(RET_HEADS, dtype=F32))
    lg = jnp.log(gamma)
    idx = jnp.arange(RET_CHUNK, dtype=F32)
    diff = idx[:, None] - idx[None, :]
    lower = jnp.where(diff >= 0, jnp.exp(lg[:, None, None] * jnp.maximum(diff, 0.0)), 0.0)
    dmat = jnp.stack([lower, jnp.swapaxes(lower, 1, 2)])
    qd_f = jnp.exp((idx + 1.0)[None, :] * lg[:, None])
    qd_b = jnp.exp((RET_CHUNK - idx)[None, :] * lg[:, None])
    qdec = jnp.broadcast_to(jnp.stack([qd_f, qd_b])[..., None], (2, RET_HEADS, RET_CHUNK, RET_DV))
    kd_f = jnp.exp((RET_CHUNK - 1.0 - idx)[None, :] * lg[:, None])
    kd_b = jnp.exp(idx[None, :] * lg[:, None])
    kdec = jnp.repeat(jnp.swapaxes(jnp.stack([kd_f, kd_b]), 1, 2), RET_DK, axis=2)
    gam64 = 1.0 - np.exp2(-5.0 - np.arange(RET_HEADS))
    cdec = tuple(float(np.float32(np.exp(RET_CHUNK * np.log(np.float32(gv))))) for gv in gam64)
    return dmat, qdec, kdec, cdec


def _mla_prep_kernel(p_ref, qng_ref, wuq_ref, kvg_ref, wuk_ref, wuv_ref, qg_ref, kg_ref,
                     cos_ref, sin_ref, q_out, k_out, v_out):
    blk = p_ref[...]
    cq = blk[:, 0:MLA_Q_RANK]
    ckv = blk[:, MLA_Q_RANK:MLA_Q_RANK + MLA_KV_RANK]
    kpe = blk[:, MLA_Q_RANK + MLA_KV_RANK:]

    def rms(x, g, n):
        return x * lax.rsqrt(jnp.sum(x * x, axis=-1, keepdims=True) * (1.0 / n) + NORM_EPS) * g

    qn = rms(cq, qng_ref[...], MLA_Q_RANK).astype(BF16)
    kvn = rms(ckv, kvg_ref[...], MLA_KV_RANK).astype(BF16)
    q_raw = _dot(qn, wuq_ref[...])
    k_raw = _dot(kvn, wuk_ref[...])
    ones_col = (lax.broadcasted_iota(jnp.int32, (1, 1024), 1) & 127) == MLA_DV
    v_out[...] = (_dot(kvn, wuv_ref[...]) + jnp.where(ones_col, 1.0, 0.0)).astype(BF16)
    cos = cos_ref[...]
    sin = sin_ref[...]
    tm = blk.shape[0]
    first = (lax.broadcasted_iota(jnp.int32, (tm, 128), 1) & 15) < 8

    def rope(x):
        partner = jnp.where(first, pltpu.roll(x, 120, 1), pltpu.roll(x, 8, 1))
        return x * cos + partner * sin

    for h in range(MLA_HEADS):
        sl = slice(h * 128, (h + 1) * 128)
        q_out[:, sl] = (rope(rms(q_raw[:, sl], qg_ref[...], MLA_QK)) * ATTN_Q_SCALE).astype(BF16)
        k_out[:, sl] = rope(rms(k_raw[:, sl] + kpe, kg_ref[...], MLA_QK)).astype(BF16)


def _mla_prep(p2, w, cos_t, sin_t, rows_per_b):
    t = p2.shape[0]
    tiles_per_b = rows_per_b // ROW_TILE
    const = lambda shape: pl.BlockSpec(shape, lambda i: (0,) * len(shape))
    return pl.pallas_call(
        _mla_prep_kernel,
        grid=(t // ROW_TILE,),
        in_specs=[
            pl.BlockSpec((ROW_TILE, 512), lambda i: (i, MLA_OFF // 512)),
            const((1, MLA_Q_RANK)), const((MLA_Q_RANK, 1024)),
            const((1, MLA_KV_RANK)), const((MLA_KV_RANK, 1024)), const((MLA_KV_RANK, 1024)),
            const((1, 128)), const((1, 128)),
            pl.BlockSpec((ROW_TILE, 128), lambda i: (i % tiles_per_b, 0)),
            pl.BlockSpec((ROW_TILE, 128), lambda i: (i % tiles_per_b, 0)),
        ],
        out_specs=[
            pl.BlockSpec((ROW_TILE, 1024), lambda i: (i, 0)),
            pl.BlockSpec((ROW_TILE, 1024), lambda i: (i, 0)),
            pl.BlockSpec((ROW_TILE, 1024), lambda i: (i, 0)),
        ],
        out_shape=[
            jax.ShapeDtypeStruct((t, 1024), BF16),
            jax.ShapeDtypeStruct((t, 1024), BF16),
            jax.ShapeDtypeStruct((t, 1024), BF16),
        ],
        compiler_params=_params("arbitrary"),
        name="mla_prep",
    )(p2, w["qng"], w["wuq"], w["kvg"], w["wuk"], w["wuv"], w["qg"], w["kg"], cos_t, sin_t)


def _mla_weights(q_norm_g, w_uq, kv_norm_g, w_ukv, qk_q_g, qk_k_g):
    wq = w_uq.reshape(MLA_Q_RANK, MLA_HEADS, MLA_QK)
    wq = jnp.pad(wq, ((0, 0), (0, 0), (0, 128 - MLA_QK))).reshape(MLA_Q_RANK, 1024)
    wkv = w_ukv.reshape(MLA_KV_RANK, MLA_HEADS, MLA_NOPE + MLA_DV)
    wk = jnp.pad(wkv[:, :, :MLA_NOPE], ((0, 0), (0, 0), (0, 128 - MLA_NOPE))).reshape(MLA_KV_RANK, 1024)
    wv = jnp.pad(wkv[:, :, MLA_NOPE:], ((0, 0), (0, 0), (0, 128 - MLA_DV))).reshape(MLA_KV_RANK, 1024)
    padg = lambda g: jnp.pad(g, (0, 128 - MLA_QK)).reshape(1, 128)
    return dict(qng=q_norm_g.reshape(1, -1), wuq=wq.astype(BF16), kvg=kv_norm_g.reshape(1, -1),
                wuk=wk.astype(BF16), wuv=wv.astype(BF16), qg=padg(qk_q_g), kg=padg(qk_k_g))


def _attn_kernel(q_ref, k_ref, v_ref, o_ref, *, ctx_len, ctx_tiles):

    def attend(kv_len):
        n_split = ATTN_KV_SPLIT
        assert kv_len % (n_split * 128) == 0
        step_len = kv_len // n_split
        inst = [(hh, c * step_len) for hh in range(2) for c in range(n_split)]
        ss = [_dot_nt(q_ref[:, hh * 128:(hh + 1) * 128], k_ref[lo:lo + step_len, hh * 128:(hh + 1) * 128])
              for hh, lo in inst]
        ms = [jnp.max(s, axis=-1, keepdims=True) for s in ss]
        ps = [jnp.exp2((s - m).astype(BF16)) for s, m in zip(ss, ms)]
        parts = [_dot(p, v_ref[lo:lo + step_len, hh * 128:(hh + 1) * 128]) for p, (hh, lo) in zip(ps, inst)]
        outs = []
        for hh in range(2):
            sel = [i for i, (h2, _) in enumerate(inst) if h2 == hh]
            m_all = ms[sel[0]]
            for i in sel[1:]:
                m_all = jnp.maximum(m_all, ms[i])
            o = parts[sel[0]] * jnp.exp2(ms[sel[0]] - m_all)
            for i in sel[1:]:
                o = o + parts[i] * jnp.exp2(ms[i] - m_all)
            outs.append(o[:, 0:MLA_DV] / o[:, MLA_DV:MLA_DV + 1])
        o_ref[...] = jnp.concatenate(outs, axis=-1).astype(BF16)

    is_ctx = pl.program_id(2) < ctx_tiles

    @pl.when(is_ctx)
    def _():
        attend(ctx_len)

    @pl.when(jnp.logical_not(is_ctx))
    def _():
        attend(k_ref.shape[0])


def _attention(q3, k3, v3, ctx_len):
    b, l, _ = q3.shape
    tq = ROW_TILE
    return pl.pallas_call(
        functools.partial(_attn_kernel, ctx_len=ctx_len, ctx_tiles=ctx_len // tq),
        grid=(b, MLA_HEADS // 2, l // tq),
        in_specs=[
            pl.BlockSpec((None, tq, 256), lambda bi, h, i: (bi, i, h)),
            pl.BlockSpec((None, l, 256), lambda bi, h, i: (bi, 0, h)),
            pl.BlockSpec((None, l, 256), lambda bi, h, i: (bi, 0, h)),
        ],
        out_specs=pl.BlockSpec((None, tq, 128), lambda bi, h, i: (bi, i, h)),
        out_shape=jax.ShapeDtypeStruct((b, l, MLA_W), BF16),
        compiler_params=_params("arbitrary", "arbitrary", "arbitrary"),
        name="mla_attention",
    )(q3, k3, v3)


def _rwkv_elem_kernel(t_ref, prev_ref, next_ref, mup_ref, mun_ref, w0_ref, w2_ref, a0_ref, a2_ref,
                      g2_ref, kk_ref, ka_ref, rk_ref, blk_ref,
                      r_out, v_out, na_out, kd_out, bb_out, lw_out, g_out, bonus_out,
                      *, tiles_per_b, ctx_tiles):
    i = pl.program_id(0) % tiles_per_b
    starts = jnp.logical_or(i == 0, i == ctx_tiles)
    ends = jnp.logical_or(i == ctx_tiles - 1, i == tiles_per_b - 1)
    t = t_ref[...]
    tm = t.shape[0]
    prev_row = jnp.where(starts, 0.0, prev_ref[7:8, :])
    next_row = jnp.where(ends, 0.0, next_ref[0:1, :])
    rid = lax.broadcasted_iota(jnp.int32, t.shape, 0)
    prev = jnp.where(rid == 0, prev_row, pltpu.roll(t, 1, 0))
    nxt = jnp.where(rid == tm - 1, next_row, pltpu.roll(t, tm - 1, 0))
    t = t + mup_ref[...] * (prev - t) + mun_ref[...] * (nxt - t)

    w = RWKV_W
    r = t[:, 0:w]
    k = t[:, w:2 * w]
    v = t[:, 2 * w:3 * w]
    o = 3 * w
    wh = (t[:, o:o + 64], t[:, o + 64:o + 128])
    ah = (t[:, o + 128:o + 192], t[:, o + 192:o + 256])
    gh = t[:, o + 256:o + 384]
    blk = blk_ref[...]

    kk = k * kk_ref[...]
    kk = kk * lax.rsqrt(jnp.maximum(_dot_exact_rhs01(kk * kk, blk), 1e-24))
    r_out[...] = r
    v_out[...] = v
    na_out[...] = -kk
    g_out[...] = _dot(_sigmoid(gh).astype(BF16), g2_ref[...])
    rk = r * rk_ref[...]
    bonus = jnp.zeros_like(v)
    for d in range(2):
        z = w0_ref[d] + _dot(jnp.tanh(wh[d]).astype(BF16), w2_ref[d])
        nz = -z
        softplus = jnp.maximum(nz, 0.0) + jnp.log(1.0 + jnp.exp(-jnp.abs(nz)))
        lw_out[d] = -jnp.exp(-softplus - 0.5)
        a = _sigmoid(a0_ref[d] + _dot(ah[d].astype(BF16), a2_ref[d]))
        kd = k * (1.0 + (a - 1.0) * ka_ref[...])
        kd_out[d] = kd
        bb_out[d] = kk * a
        bonus = bonus + _dot_exact_rhs01(rk * kd, blk) * v
    bonus_out[...] = bonus


def _rwkv_elem(p2, w, rows_per_b, ctx_len):
    t = p2.shape[0]
    tm = ROW_TILE
    tiles_per_b = rows_per_b // tm
    n8 = t // 8
    cb = RWKV_OFF // RWKV_IN
    const = lambda shape: pl.BlockSpec(shape, lambda i: (0,) * len(shape))
    one = pl.BlockSpec((tm, RWKV_W), lambda i: (i, 0))
    two = pl.BlockSpec((2, tm, RWKV_W), lambda i: (0, i, 0))
    s1 = jax.ShapeDtypeStruct((t, RWKV_W), F32)
    s2 = jax.ShapeDtypeStruct((2, t, RWKV_W), F32)
    return pl.pallas_call(
        functools.partial(_rwkv_elem_kernel, tiles_per_b=tiles_per_b, ctx_tiles=ctx_len // tm),
        grid=(t // tm,),
        in_specs=[
            pl.BlockSpec((tm, RWKV_IN), lambda i: (i, cb)),
            pl.BlockSpec((8, RWKV_IN), lambda i: (jnp.maximum(i * (tm // 8) - 1, 0), cb)),
            pl.BlockSpec((8, RWKV_IN), lambda i: (jnp.minimum((i + 1) * (tm // 8), n8 - 1), cb)),
            const((1, RWKV_IN)), const((1, RWKV_IN)),
            const((2, 1, RWKV_W)), const((2, RWKV_DECAY_LORA, RWKV_W)),
            const((2, 1, RWKV_W)), const((2, RWKV_A_LORA, RWKV_W)),
            const((RWKV_GATE_LORA, RWKV_W)),
            const((1, RWKV_W)), const((1, RWKV_W)), const((1, RWKV_W)),
            const((RWKV_W, RWKV_W)),
        ],
        out_specs=[one, one, one, two, two, two, one, one],
        out_shape=[s1, s1, s1, s2, s2, s2, s1, s1],
        compiler_params=_params("arbitrary"),
        name="rwkv_elem",
    )(p2, p2, p2, w["mup"], w["mun"], w["w0"], w["w2"], w["a0"], w["a2"], w["g2"],
      w["kk"], w["ka"], w["rk"], w["blk"])


def _block_rows(x, bm):
    return jnp.where(bm, jnp.concatenate([x, x, x, x], axis=0), 0.0)


def _rwkv_chain(r, v, na, kd, bb, lw, tri, strict, incl, h_ref, o_ref, bi, d):
    c = RWKV_CHUNK
    cum = _dot_exact_lhs01(tri, lw)
    tot = cum[c - 1:c, :] if d == 0 else cum[0:1, :]
    e_pos = jnp.exp(cum)
    e_prev = jnp.exp(cum - lw)
    e_neg = jnp.exp(-cum)
    e_rel = jnp.exp(tot - cum)
    e_tot = jnp.exp(tot)
    a_t = na * e_prev
    r_t = r * e_pos
    b_t = bb * e_neg
    k_t = kd * e_neg
    b_p = bb * e_rel
    k_p = kd * e_rel

    assert c == 64 and RWKV_HD == 64
    bm = (lax.broadcasted_iota(jnp.int32, (HALF, HALF), 0) >> 6
          == lax.broadcasted_iota(jnp.int32, (HALF, HALF), 1) >> 6)
    eye_p = ((lax.broadcasted_iota(jnp.int32, (c, HALF), 1) & (c - 1))
             == lax.broadcasted_iota(jnp.int32, (c, HALF), 0))
    eye_s = (lax.broadcasted_iota(jnp.int32, (HALF, HALF), 0)
             == lax.broadcasted_iota(jnp.int32, (HALF, HALF), 1))
    bf = lambda x: x.astype(BF16)

    for g in range(2):
        sl = slice(g * HALF, (g + 1) * HALF)
        a_h, r_h, v_h = a_t[:, sl], r_t[:, sl], v[:, sl]
        lhs = bf(jnp.concatenate([a_h, r_h], axis=0))
        rhs = bf(jnp.concatenate([_block_rows(b_t[:, sl], bm), _block_rows(k_t[:, sl], bm)], axis=0))
        big = _dot_nt(lhs, rhs)
        a_ab = jnp.where(strict, big[0:c, 0:HALF], 0.0)
        a_ak = jnp.where(strict, big[0:c, HALF:], 0.0)
        a_rb = jnp.where(incl, big[c:, 0:HALF], 0.0)
        a_rk = jnp.where(incl, big[c:, HALF:], 0.0)
        v_bd = bf(_block_rows(v_h, bm))
        akv = _dot(bf(a_ak), v_bd)
        t_inv = jnp.where(eye_p, 1.0, 0.0) + a_ab
        pw = _dot(bf(a_ab), bf(_block_rows(a_ab, bm)))
        for it in range(5):
            prod = _dot(bf(jnp.concatenate([t_inv, pw], axis=0)), bf(_block_rows(pw, bm)))
            t_inv = t_inv + prod[0:c]
            pw = prod[c:]
        wu = _dot(bf(t_inv), bf(jnp.concatenate([_block_rows(a_h, bm), _block_rows(akv, bm)], axis=1)))
        w_m, u0 = wu[:, 0:HALF], wu[:, HALF:]
        gy_y0 = _dot(bf(a_rb), bf(jnp.concatenate([_block_rows(w_m, bm), _block_rows(u0, bm)], axis=1)))
        gy = r_h + gy_y0[:, 0:HALF]
        y0 = gy_y0[:, HALF:] + _dot(bf(a_rk), v_bd)
        bp_h, kp_h = bf(b_p[:, sl]), bf(k_p[:, sl])
        f_bd = jnp.where(bm, _dot_tn(bp_h, bf(w_m)), 0.0) + jnp.where(eye_s, e_tot[:, sl], 0.0)
        g_bd = jnp.where(bm, _dot_tn(bp_h, bf(u0)) + _dot_tn(kp_h, bf(v_h)), 0.0)
        h_old = bf(h_ref[bi, d, g])
        o_ref[bi, :, sl] = _dot(bf(gy), h_old) + y0
        h_ref[bi, d, g] = _dot(bf(f_bd), h_old) + g_bd


RWKV_ROWS = 2


def _rwkv_chunk_kernel(rf_ref, vf_ref, naf_ref, kdf_ref, bbf_ref, lwf_ref,
                       rb_ref, vb_ref, nab_ref, kdb_ref, bbb_ref, lwb_ref,
                       tri_ref, ms_ref, mi_ref, of_ref, ob_ref, h_ref):
    @pl.when(pl.program_id(1) == 0)
    def _():
        h_ref[...] = jnp.zeros_like(h_ref)

    dirs = ((rf_ref, vf_ref, naf_ref, kdf_ref, bbf_ref, lwf_ref, of_ref),
            (rb_ref, vb_ref, nab_ref, kdb_ref, bbb_ref, lwb_ref, ob_ref))
    c = RWKV_CHUNK
    assert c == 64 and RWKV_HD == 64
    bm = (lax.broadcasted_iota(jnp.int32, (HALF, HALF), 0) >> 6
          == lax.broadcasted_iota(jnp.int32, (HALF, HALF), 1) >> 6)
    eye_p = ((lax.broadcasted_iota(jnp.int32, (c, HALF), 1) & (c - 1))
             == lax.broadcasted_iota(jnp.int32, (c, HALF), 0))
    eye_s = (lax.broadcasted_iota(jnp.int32, (HALF, HALF), 0)
             == lax.broadcasted_iota(jnp.int32, (HALF, HALF), 1))
    bf = lambda x: x.astype(BF16)
    blk = lambda x: _block_rows(x, bm)
    cat0 = lambda xs: jnp.concatenate(xs, axis=0)
    cat1 = lambda xs: jnp.concatenate(xs, axis=1)

    chains = [(d, bi) + dirs[d] for d in range(2) for bi in range(RWKV_ROWS)]
    cums = [_dot_exact_lhs01(tri_ref[d], lw_ref[bi]) for d, bi, _, _, _, _, _, lw_ref, _ in chains]
    inst = []
    for (d, bi, r_ref, v_ref, na_ref, kd_ref, bb_ref, lw_ref, o_ref), cum in zip(chains, cums):
        lw = lw_ref[bi]
        tot = cum[c - 1:c, :] if d == 0 else cum[0:1, :]
        e_neg = jnp.exp(-cum)
        e_rel = jnp.exp(tot - cum)
        e_tot = jnp.exp(tot)
        a_t = na_ref[bi] * jnp.exp(cum - lw)
        r_t = r_ref[bi] * jnp.exp(cum)
        kd, bb, v = kd_ref[bi], bb_ref[bi], v_ref[bi]
        for g in range(2):
            sl = slice(g * HALF, (g + 1) * HALF)
            inst.append(dict(
                d=d, bi=bi, g=g, sl=sl, o_ref=o_ref, a=a_t[:, sl], r=r_t[:, sl], v=v[:, sl],
                bt=bb[:, sl] * e_neg[:, sl], kt=kd[:, sl] * e_neg[:, sl],
                bp=bf(bb[:, sl] * e_rel[:, sl]), kp=bf(kd[:, sl] * e_rel[:, sl]), etot=e_tot[:, sl]))
    strict = [ms_ref[0] > 0.5, ms_ref[1] > 0.5]
    incl = [mi_ref[0] > 0.5, mi_ref[1] > 0.5]

    bigs = [_dot_nt(bf(cat0([s["a"], s["r"]])), bf(cat0([blk(s["bt"]), blk(s["kt"])]))) for s in inst]
    for s, big in zip(inst, bigs):
        s["a_ab"] = jnp.where(strict[s["d"]], big[0:c, 0:HALF], 0.0)
        s["a_ak"] = bf(jnp.where(strict[s["d"]], big[0:c, HALF:], 0.0))
        s["a_rb"] = bf(jnp.where(incl[s["d"]], big[c:, 0:HALF], 0.0))
        s["a_rk"] = bf(jnp.where(incl[s["d"]], big[c:, HALF:], 0.0))
        s["v_bd"] = bf(blk(s["v"]))
        s["t"] = jnp.where(eye_p, 1.0, 0.0) + s["a_ab"]
    pws = [_dot(bf(s["a_ab"]), bf(blk(s["a_ab"]))) for s in inst]
    akvs = [_dot(s["a_ak"], s["v_bd"]) for s in inst]
    arkv = [_dot(s["a_rk"], s["v_bd"]) for s in inst]
    for it in range(5):
        prods = [_dot(bf(cat0([s["t"], pw])), bf(blk(pw))) for s, pw in zip(inst, pws)]
        for s, prod in zip(inst, prods):
            s["t"] = s["t"] + prod[0:c]
        pws = [prod[c:] for prod in prods]
    wus = [_dot(bf(s["t"]), bf(cat1([blk(s["a"]), blk(akv)]))) for s, akv in zip(inst, akvs)]
    gys = [_dot(s["a_rb"], bf(cat1([blk(wu[:, 0:HALF]), blk(wu[:, HALF:])]))) for s, wu in zip(inst, wus)]
    zero_half = jnp.zeros((c, HALF), BF16)
    fgs = [_dot_tn(cat0([s["bp"], s["kp"]]), cat0([bf(wu), cat1([zero_half, bf(s["v"])])]))
           for s, wu in zip(inst, wus)]
    fs = [fg[:, 0:HALF] for fg in fgs]
    gs = [fg[:, HALF:] for fg in fgs]
    h_olds = [bf(h_ref[s["bi"], s["d"], s["g"]]) for s in inst]
    outs = [_dot(bf(s["r"] + gy[:, 0:HALF]), h_old) + gy[:, HALF:] + ak
            for s, gy, h_old, ak in zip(inst, gys, h_olds, arkv)]
    h_news = [_dot(bf(jnp.where(bm, f, 0.0) + jnp.where(eye_s, s["etot"], 0.0)), h_old)
              + jnp.where(bm, g_, 0.0) for s, f, g_, h_old in zip(inst, fs, gs, h_olds)]
    for s, out, h_new in zip(inst, outs, h_news):
        s["o_ref"][s["bi"], :, s["sl"]] = out
        h_ref[s["bi"], s["d"], s["g"]] = h_new


def _rwkv_chunk(e, b, l, ctx_len, consts):
    c = RWKV_CHUNK
    rows = RWKV_ROWS
    assert b % rows == 0
    n_all, n_ctx = l // c, ctx_len // c
    ch = functools.partial(_scan_chunk, n_ctx=n_ctx, n_all=n_all)
    tri, ms, mi = consts
    r3, v3, na3 = (x.reshape(b, l, RWKV_W) for x in (e["r"], e["v"], e["na"]))
    kd4, bb4, lw4 = (x.reshape(2, b, l, RWKV_W) for x in (e["kd"], e["bb"], e["lw"]))

    def specs(d):
        one = pl.BlockSpec((rows, c, RWKV_W), lambda bi, j: (bi, ch(d, j), 0))
        two = pl.BlockSpec((None, rows, c, RWKV_W), lambda bi, j: (d, bi, ch(d, j), 0))
        return one, two

    one_f, two_f = specs(0)
    one_b, two_b = specs(1)
    const = lambda shape: pl.BlockSpec(shape, lambda bi, j: (0,) * len(shape))
    out = jax.ShapeDtypeStruct((b, l, RWKV_W), F32)
    return pl.pallas_call(
        _rwkv_chunk_kernel,
        grid=(b // rows, n_all),
        in_specs=[one_f, one_f, one_f, two_f, two_f, two_f,
                  one_b, one_b, one_b, two_b, two_b, two_b,
                  const((2, c, c)), const((2, c, HALF)), const((2, c, HALF))],
        out_specs=[one_f, one_b],
        out_shape=[out, out],
        scratch_shapes=[pltpu.VMEM((rows, 2, 2, HALF, HALF), F32)],
        compiler_params=_params("arbitrary", "arbitrary"),
        name="rwkv_chunk",
    )(r3, v3, na3, kd4, bb4, lw4, r3, v3, na3, kd4, bb4, lw4, tri, ms, mi)


def _rwkv_consts():
    c = RWKV_CHUNK
    idx = np.arange(c)
    lower = idx[None, :] <= idx[:, None]
    tri = np.stack([lower, lower.T]).astype(np.float32)
    s_in_head = np.tile(idx, HALF // c)[None, :]
    t_row = idx[:, None]
    ms = np.stack([s_in_head < t_row, s_in_head > t_row]).astype(np.float32)
    mi = np.stack([s_in_head <= t_row, s_in_head >= t_row]).astype(np.float32)
    return jnp.asarray(tri, BF16), jnp.asarray(ms), jnp.asarray(mi)


def _merge_kernel(x_ref, mod_ref, gate_ref, retf_ref, retb_ref, mla_ref, yf_ref, yb_ref, bonus_ref, g_ref,
                  lng_ref, lnb_ref, blk_ref, wr_ref, wm_ref, ww_ref, wo_ref, o_ref):
    blk = blk_ref[...]
    ret = retf_ref[...] + retb_ref[...]
    y = yf_ref[...] + yb_ref[...]
    mean = _dot_exact_rhs01(y, blk) * (1.0 / RWKV_HD)
    yc = y - mean
    var = _dot_exact_rhs01(yc * yc, blk) * (1.0 / RWKV_HD)
    y = yc * lax.rsqrt(var + RWKV_GN_EPS) * lng_ref[...] + lnb_ref[...] + bonus_ref[...]
    rwk = y * g_ref[...]
    gate = gate_ref[...]
    d = D_MODEL
    mix = (_sigmoid(gate[:, 0:d]) * _dot(ret.astype(BF16), wr_ref[...])
           + _sigmoid(gate[:, d:2 * d]) * _dot(mla_ref[...], wm_ref[...])
           + _sigmoid(gate[:, 2 * d:]) * _dot(rwk.astype(BF16), ww_ref[...]))
    o_ref[...] = x_ref[...] + mod_ref[2:3, :] * _dot(mix.astype(BF16), wo_ref[...])


def _merge(x2, mod_l, p2, retf, retb, mla, yf, yb, bonus, g, w, geom):
    t = x2.shape[0]
    tm = ROW_TILE
    tpb, ctx_tiles, batch = geom
    row = functools.partial(_mod_row, tiles_per_b=tpb, ctx_tiles=ctx_tiles, batch=batch)
    const = lambda shape: pl.BlockSpec(shape, lambda i: (0,) * len(shape))
    return pl.pallas_call(
        _merge_kernel,
        grid=(t // tm,),
        in_specs=[
            pl.BlockSpec((tm, D_MODEL), lambda i: (i, 0)),
            pl.BlockSpec((None, 6, D_MODEL), lambda i: (row(i), 0, 0)),
            pl.BlockSpec((tm, 3 * D_MODEL), lambda i: (i, 0)),
            pl.BlockSpec((tm, RET_W), lambda i: (i, 0)),
            pl.BlockSpec((tm, RET_W), lambda i: (i, 0)),
            pl.BlockSpec((tm, MLA_W), lambda i: (i, 0)),
            pl.BlockSpec((tm, RWKV_W), lambda i: (i, 0)),
            pl.BlockSpec((tm, RWKV_W), lambda i: (i, 0)),
            pl.BlockSpec((tm, RWKV_W), lambda i: (i, 0)),
            pl.BlockSpec((tm, RWKV_W), lambda i: (i, 0)),
            const((1, RWKV_W)), const((1, RWKV_W)), const((RWKV_W, RWKV_W)),
            const((RET_W, D_MODEL)), const((MLA_W, D_MODEL)), const((RWKV_W, D_MODEL)),
            const((D_MODEL, D_MODEL)),
        ],
        out_specs=pl.BlockSpec((tm, D_MODEL), lambda i: (i, 0)),
        out_shape=jax.ShapeDtypeStruct((t, D_MODEL), F32),
        compiler_params=_params("arbitrary"),
        name="merge",
    )(x2, mod_l, p2, retf, retb, mla, yf, yb, bonus, g, w["lng"], w["lnb"], w["blk"],
      w["wr"], w["wm"], w["ww"], w["wo"])


def _router_kernel(x_ref, mod_ref, g_ref, wr_ref, br_ref, h_out, idx_out, gate_out):
    x = x_ref[...]
    y = x * lax.rsqrt(jnp.mean(x * x, axis=-1, keepdims=True) + NORM_EPS) * g_ref[...]
    h = (y * (1.0 + mod_ref[4:5, :]) + mod_ref[3:4, :]).astype(BF16)
    h_out[...] = h
    logits = _dot(h, wr_ref[...]) + br_ref[...]
    lane = lax.broadcasted_iota(jnp.int32, logits.shape, 1).astype(F32)
    vals, idxs = [], []
    for _ in range(TOP_K):
        m = jnp.max(logits, axis=-1, keepdims=True)
        sel = jnp.min(jnp.where(logits == m, lane, 128.0), axis=-1, keepdims=True)
        vals.append(m)
        idxs.append(sel)
        logits = jnp.where(lane == sel, -jnp.inf, logits)
    es = [jnp.exp(vv - vals[0]) for vv in vals]
    den = es[0] + es[1] + es[2] + es[3]
    idx_row = jnp.zeros(lane.shape, F32)
    gate_row = jnp.zeros(lane.shape, F32)
    for kq in range(TOP_K):
        idx_row = jnp.where(lane == float(kq), idxs[kq], idx_row)
        gate_row = jnp.where(lane == float(kq), es[kq] / den, gate_row)
    idx_out[...] = idx_row.astype(jnp.int32)
    gate_out[...] = gate_row


def _router(x2, mod_l, g, wr_pad, br_pad, geom):
    t = x2.shape[0]
    tm = ROW_TILE
    tpb, ctx_tiles, batch = geom
    row = functools.partial(_mod_row, tiles_per_b=tpb, ctx_tiles=ctx_tiles, batch=batch)
    return pl.pallas_call(
        _router_kernel,
        grid=(t // tm,),
        in_specs=[
            pl.BlockSpec((tm, D_MODEL), lambda i: (i, 0)),
            pl.BlockSpec((None, 6, D_MODEL), lambda i: (row(i), 0, 0)),
            pl.BlockSpec((1, D_MODEL), lambda i: (0, 0)),
            pl.BlockSpec((D_MODEL, 128), lambda i: (0, 0)),
            pl.BlockSpec((1, 128), lambda i: (0, 0)),
        ],
        out_specs=[
            pl.BlockSpec((tm, D_MODEL), lambda i: (i, 0)),
            pl.BlockSpec((tm, 128), lambda i: (i, 0)),
            pl.BlockSpec((tm, 128), lambda i: (i, 0)),
        ],
        out_shape=[
            jax.ShapeDtypeStruct((t, D_MODEL), BF16),
            jax.ShapeDtypeStruct((t, 128), jnp.int32),
            jax.ShapeDtypeStruct((t, 128), F32),
        ],
        compiler_params=_params("arbitrary"),
        name="moe_router",
    )(x2, mod_l, g.reshape(1, D_MODEL), wr_pad, br_pad)


def _moe_plan(top_idx, top_gate, t, tb_rows):
    nk = t * TOP_K
    n_rows = -(-(nk + N_EXPERTS * (MOE_TILE - 1)) // MOE_TILE) * MOE_TILE
    n_sub = n_rows // MOE_SUB
    n_tb = t // tb_rows
    n_items = n_sub + N_EXPERTS * n_tb

    flat_e = top_idx.reshape(-1)
    flat_tok = jnp.arange(nk, dtype=jnp.int32) // TOP_K
    order = jnp.argsort(flat_e, stable=True)
    e_sorted = flat_e[order]
    counts = jnp.bincount(flat_e, length=N_EXPERTS)
    padded = (counts + MOE_TILE - 1) // MOE_TILE * MOE_TILE
    pad_end = jnp.cumsum(padded)
    start = jnp.cumsum(counts) - counts
    dest = (pad_end - padded)[e_sorted] + jnp.arange(nk, dtype=jnp.int32) - start[e_sorted]
    tok_buf = jnp.full((n_rows,), t, jnp.int32).at[dest].set(flat_tok[order])
    gate_buf = jnp.zeros((n_rows,), F32).at[dest].set(top_gate.reshape(-1)[order])
    tile_expert = jnp.minimum(
        jnp.searchsorted(pad_end // MOE_TILE, jnp.arange(n_rows // MOE_TILE), side="right"),
        N_EXPERTS - 1).astype(jnp.int32)

    tok_sub = tok_buf.reshape(n_sub, MOE_SUB)
    valid = tok_sub < t
    lo = jnp.min(jnp.where(valid, tok_sub, t), axis=1) // tb_rows
    hi = jnp.max(jnp.where(valid, tok_sub, -1), axis=1) // tb_rows
    has = jnp.any(valid, axis=1)
    lo = jnp.where(has, lo, 0).astype(jnp.int32)
    n_it = jnp.where(has, hi - lo + 1, 1).astype(jnp.int32)
    first_item = jnp.cumsum(n_it) - n_it
    total = jnp.sum(n_it)
    w = jnp.arange(n_items, dtype=jnp.int32)
    wv = jnp.minimum(w, total - 1)
    st = (jnp.searchsorted(first_item, wv, side="right") - 1).astype(jnp.int32)
    tb = lo[st] + wv - first_item[st]
    live = w < total
    is_first = jnp.logical_and(live, wv == first_item[st])
    is_last = jnp.logical_and(live, wv == first_item[st] + n_it[st] - 1)
    run_ffn = jnp.logical_and(is_last, st % 2 == 1)
    flags1 = (live.astype(jnp.int32) + 2 * is_first.astype(jnp.int32) + 4 * run_ffn.astype(jnp.int32))

    key = jnp.where(live, tb * n_sub + st, n_tb * n_sub)
    perm = jnp.argsort(key)
    live2 = live[perm]
    last_live = total - 1
    pos = jnp.minimum(w, last_live)
    st2 = st[perm][pos]
    tb2 = tb[perm][pos]
    prev_tb = jnp.concatenate([jnp.full((1,), -1, jnp.int32), tb2[:-1]])
    next_tb = jnp.concatenate([tb2[1:], jnp.full((1,), -1, jnp.int32)])
    first2 = jnp.logical_and(live2, tb2 != prev_tb)
    last2 = jnp.logical_and(live2, jnp.logical_or(tb2 != next_tb, w == last_live))
    flags2 = live2.astype(jnp.int32) + 2 * first2.astype(jnp.int32) + 4 * last2.astype(jnp.int32)

    return dict(n_rows=n_rows, n_sub=n_sub, n_items=n_items,
                tok_rows=tok_buf.reshape(n_rows // MOE_TILE, 2, MOE_SUB),
                gate_rows=gate_buf.reshape(n_rows // MOE_TILE, 2, MOE_SUB),
                tile_expert=tile_expert, st1=st, tb1=tb.astype(jnp.int32), flags1=flags1,
                st2=st2, tb2=tb2.astype(jnp.int32), flags2=flags2)


def _moe_ffn_kernel(st_ref, tb_ref, fl_ref, te_ref, h_ref, tok_ref, gate_ref, wgu_ref, bgu_ref,
                    wd_ref, bd_ref, o_ref, x_ref, *, tb_rows):
    w = pl.program_id(0)
    flags = fl_ref[w]
    half = st_ref[w] % 2
    row0 = pl.multiple_of(half * MOE_SUB, MOE_SUB)

    @pl.when(flags & 2 != 0)
    def _():
        x_ref[pl.ds(row0, MOE_SUB), :] = jnp.zeros((MOE_SUB, D_MODEL), F32)

    @pl.when(flags & 1 != 0)
    def _():
        tok_row = jnp.where(half == 0, tok_ref[0:1, :], tok_ref[1:2, :]).astype(F32)
        tok_col = _row_to_col(tok_row, MOE_SUB)
        base = (tb_ref[w] * tb_rows).astype(F32)
        col = lax.broadcasted_iota(jnp.int32, (MOE_SUB, tb_rows), 1).astype(F32) + base
        onehot = jnp.where(tok_col == col, 1.0, 0.0).astype(BF16)
        x_ref[pl.ds(row0, MOE_SUB), :] += _dot(onehot, h_ref[...])

    @pl.when(flags & 4 != 0)
    def _():
        gu = _dot(x_ref[...].astype(BF16), wgu_ref[...]) + bgu_ref[...]
        f = gu.shape[1] // 2
        g_lin = jnp.minimum(gu[:, 0:f], SWIGLU_LIMIT)
        up = jnp.clip(gu[:, f:], -SWIGLU_LIMIT, SWIGLU_LIMIT)
        act = g_lin * _sigmoid(SWIGLU_ALPHA * g_lin) * (up + 1.0)
        y = _dot(act.astype(BF16), wd_ref[...]) + bd_ref[...]
        gate_col = jnp.concatenate([_row_to_col(gate_ref[0:1, :], MOE_SUB),
                                    _row_to_col(gate_ref[1:2, :], MOE_SUB)], axis=0)
        o_ref[...] = y * gate_col


def _moe_ffn(plan, h2, wgu, bgu, wd, bd, tb_rows):
    n_rows = plan["n_rows"]
    f2 = wgu.shape[2]
    grid_spec = pltpu.PrefetchScalarGridSpec(
        num_scalar_prefetch=4,
        grid=(plan["n_items"],),
        in_specs=[
            pl.BlockSpec((tb_rows, D_MODEL), lambda w, st, tb, fl, te: (tb[w], 0)),
            pl.BlockSpec((None, 2, MOE_SUB), lambda w, st, tb, fl, te: (st[w] // 2, 0, 0)),
            pl.BlockSpec((None, 2, MOE_SUB), lambda w, st, tb, fl, te: (st[w] // 2, 0, 0)),
            pl.BlockSpec((None, D_MODEL, f2), lambda w, st, tb, fl, te: (te[st[w] // 2], 0, 0)),
            pl.BlockSpec((None, 1, f2), lambda w, st, tb, fl, te: (te[st[w] // 2], 0, 0)),
            pl.BlockSpec((None, f2 // 2, D_MODEL), lambda w, st, tb, fl, te: (te[st[w] // 2], 0, 0)),
            pl.BlockSpec((None, 1, D_MODEL), lambda w, st, tb, fl, te: (te[st[w] // 2], 0, 0)),
        ],
        out_specs=pl.BlockSpec((MOE_TILE, D_MODEL), lambda w, st, tb, fl, te: (st[w] // 2, 0)),
        scratch_shapes=[pltpu.VMEM((MOE_TILE, D_MODEL), F32)],
    )
    return pl.pallas_call(
        functools.partial(_moe_ffn_kernel, tb_rows=tb_rows),
        grid_spec=grid_spec,
        out_shape=jax.ShapeDtypeStruct((n_rows, D_MODEL), F32),
        compiler_params=_params("arbitrary"),
        name="moe_experts",
    )(plan["st1"], plan["tb1"], plan["flags1"], plan["tile_expert"],
      h2, plan["tok_rows"], plan["gate_rows"], wgu, bgu, wd, bd)


def _moe_combine_kernel(st_ref, tb_ref, fl_ref, y_ref, tok_ref, x_ref, modb_ref, modc_ref,
                        o_ref, acc_ref, *, tb_rows, blocks_per_b, ctx_len):
    w = pl.program_id(0)
    flags = fl_ref[w]

    @pl.when(flags & 2 != 0)
    def _():
        acc_ref[...] = jnp.zeros_like(acc_ref)

    @pl.when(flags & 1 != 0)
    def _():
        half = st_ref[w] % 2
        tok_row = jnp.where(half == 0, tok_ref[0:1, :], tok_ref[1:2, :])
        rows = lax.broadcasted_iota(jnp.int32, (tb_rows, MOE_SUB), 0) + tb_ref[w] * tb_rows
        onehot_t = jnp.where(rows == tok_row, 1.0, 0.0).astype(BF16)
        acc_ref[...] += _dot(onehot_t, y_ref[...].astype(BF16))

    @pl.when(flags & 4 != 0)
    def _():
        row_in_b = lax.broadcasted_iota(jnp.int32, (tb_rows, 1), 0) + (tb_ref[w] % blocks_per_b) * tb_rows
        in_ctx = row_in_b < ctx_len
        g2 = jnp.where(in_ctx, modc_ref[5:6, :], modb_ref[5:6, :])
        o_ref[...] = x_ref[...] + g2 * acc_ref[...]


def _moe_combine(plan, y_sorted, x2, mod_l, tb_rows, blocks_per_b, ctx_len, batch):
    t = x2.shape[0]
    grid_spec = pltpu.PrefetchScalarGridSpec(
        num_scalar_prefetch=3,
        grid=(plan["n_items"],),
        in_specs=[
            pl.BlockSpec((MOE_SUB, D_MODEL), lambda w, st, tb, fl: (st[w], 0)),
            pl.BlockSpec((None, 2, MOE_SUB), lambda w, st, tb, fl: (st[w] // 2, 0, 0)),
            pl.BlockSpec((tb_rows, D_MODEL), lambda w, st, tb, fl: (tb[w], 0)),
            pl.BlockSpec((None, 6, D_MODEL), lambda w, st, tb, fl: (tb[w] // blocks_per_b, 0, 0)),
            pl.BlockSpec((None, 6, D_MODEL), lambda w, st, tb, fl: (batch, 0, 0)),
        ],
        out_specs=pl.BlockSpec((tb_rows, D_MODEL), lambda w, st, tb, fl: (tb[w], 0)),
        scratch_shapes=[pltpu.VMEM((tb_rows, D_MODEL), F32)],
    )
    return pl.pallas_call(
        functools.partial(_moe_combine_kernel, tb_rows=tb_rows, blocks_per_b=blocks_per_b,
                          ctx_len=ctx_len),
        grid_spec=grid_spec,
        out_shape=jax.ShapeDtypeStruct((t, D_MODEL), F32),
        compiler_params=_params("arbitrary"),
        name="moe_combine",
    )(plan["st2"], plan["tb2"], plan["flags2"], y_sorted, plan["tok_rows"], x2, mod_l, mod_l)


def _route_kernel(x_ref, mod_ref, g_ref, wr_ref, br_ref, tri_ref,
                  h_out, idx_out, rank_out, gate_out, cnt_out, run_ref):
    @pl.when(pl.program_id(0) == 0)
    def _():
        run_ref[...] = jnp.zeros_like(run_ref)

    x = x_ref[...]
    y = x * lax.rsqrt(jnp.mean(x * x, axis=-1, keepdims=True) + NORM_EPS) * g_ref[...]
    h = y * (1.0 + mod_ref[4:5, :]) + mod_ref[3:4, :]
    h_out[...] = h
    logits = _dot(h.astype(BF16), wr_ref[...]) + br_ref[...]
    lane = lax.broadcasted_iota(jnp.int32, logits.shape, 1).astype(F32)
    vals, idxs = [], []
    for _ in range(TOP_K):
        m = jnp.max(logits, axis=-1, keepdims=True)
        sel = jnp.min(jnp.where(logits == m, lane, 128.0), axis=-1, keepdims=True)
        vals.append(m)
        idxs.append(sel)
        logits = jnp.where(lane == sel, -jnp.inf, logits)
    es = [jnp.exp(vv - vals[0]) for vv in vals]
    den = es[0] + es[1] + es[2] + es[3]
    chosen = jnp.zeros(lane.shape, F32)
    for kq in range(TOP_K):
        chosen = jnp.where(lane == idxs[kq], 1.0, chosen)
    before = _dot(tri_ref[...], chosen.astype(BF16)) + run_ref[...]
    idx_row = jnp.zeros(lane.shape, F32)
    rank_row = jnp.zeros(lane.shape, F32)
    gate_row = jnp.zeros(lane.shape, F32)
    for kq in range(TOP_K):
        rk = jnp.sum(jnp.where(lane == idxs[kq], before, 0.0), axis=-1, keepdims=True)
        idx_row = jnp.where(lane == float(kq), idxs[kq], idx_row)
        rank_row = jnp.where(lane == float(kq), rk, rank_row)
        gate_row = jnp.where(lane == float(kq), es[kq] / den, gate_row)
    idx_out[...] = idx_row.astype(jnp.int32)
    rank_out[...] = rank_row.astype(jnp.int32)
    gate_out[...] = gate_row
    run_ref[...] += jnp.sum(chosen, axis=0, keepdims=True)
    cnt_out[...] = run_ref[...]


def _route(x2, mod_l, g, wr_pad, br_pad, tri, geom):
    t = x2.shape[0]
    tm = ROW_TILE
    tpb, ctx_tiles, batch = geom
    row = functools.partial(_mod_row, tiles_per_b=tpb, ctx_tiles=ctx_tiles, batch=batch)
    lanes = lambda dt: jax.ShapeDtypeStruct((t, 128), dt)
    tile = pl.BlockSpec((tm, 128), lambda i: (i, 0))
    return pl.pallas_call(
        _route_kernel,
        grid=(t // tm,),
        in_specs=[
            pl.BlockSpec((tm, D_MODEL), lambda i: (i, 0)),
            pl.BlockSpec((None, 6, D_MODEL), lambda i: (row(i), 0, 0)),
            pl.BlockSpec((1, D_MODEL), lambda i: (0, 0)),
            pl.BlockSpec((D_MODEL, 128), lambda i: (0, 0)),
            pl.BlockSpec((1, 128), lambda i: (0, 0)),
            pl.BlockSpec((tm, tm), lambda i: (0, 0)),
        ],
        out_specs=[pl.BlockSpec((tm, D_MODEL), lambda i: (i, 0)), tile, tile, tile,
                   pl.BlockSpec((1, 128), lambda i: (0, 0))],
        out_shape=[jax.ShapeDtypeStruct((t, D_MODEL), F32), lanes(jnp.int32), lanes(jnp.int32),
                   lanes(F32), jax.ShapeDtypeStruct((1, 128), F32)],
        scratch_shapes=[pltpu.VMEM((1, 128), F32)],
        compiler_params=_params("arbitrary"),
        name="moe_router",
    )(x2, mod_l, g.reshape(1, D_MODEL), wr_pad, br_pad, tri)


def _moe_layout(idx, rank, counts, t):
    n_rows = -(-(t * TOP_K + N_EXPERTS * (MOE_TILE - 1)) // MOE_TILE) * MOE_TILE
    n_tiles = n_rows // MOE_TILE
    padded = (counts + MOE_TILE - 1) // MOE_TILE * MOE_TILE
    pad_end = jnp.cumsum(padded)
    pad_start = pad_end - padded
    onehot = idx[..., None] == jnp.arange(N_EXPERTS, dtype=jnp.int32)
    dest = jnp.sum(jnp.where(onehot, pad_start, 0), axis=-1) + rank
    tile_end = pad_end // MOE_TILE
    tile_expert = jnp.sum(jnp.arange(n_tiles, dtype=jnp.int32)[:, None] >= tile_end[None, :], axis=1)
    tile_expert = jnp.minimum(tile_expert, N_EXPERTS - 1).astype(jnp.int32)
    n_used = tile_end[-1:].astype(jnp.int32)
    return dest.astype(jnp.int32).reshape(t // ROW_TILE, 1, ROW_TILE * TOP_K), tile_expert, n_used, n_rows


def _dispatch_kernel(dest_ref, h_ref, xs_in_ref, xs_ref, sem):
    del xs_in_ref
    tm = h_ref.shape[0]

    def issue(tok, carry):
        src = h_ref.at[pl.ds(tok, 1), :]
        for kq in range(TOP_K):
            row = dest_ref[0, tok * TOP_K + kq]
            pltpu.make_async_copy(src, xs_ref.at[pl.ds(row, 1), :], sem.at[0]).start()
        return carry

    lax.fori_loop(0, tm, issue, 0, unroll=4)
    for _ in range(TOP_K):
        pltpu.make_async_copy(h_ref, xs_ref.at[pl.ds(0, tm), :], sem.at[0]).wait()


def _dispatch(dest, h2, xs_prev):
    t = h2.shape[0]
    tm = ROW_TILE
    n_rows = xs_prev.shape[0]
    return pl.pallas_call(
        _dispatch_kernel,
        grid=(t // tm,),
        in_specs=[
            pl.BlockSpec((None, 1, tm * TOP_K), lambda i: (i, 0, 0), memory_space=pltpu.SMEM),
            pl.BlockSpec((tm, D_MODEL), lambda i: (i, 0)),
            pl.BlockSpec(memory_space=pl.ANY),
        ],
        out_specs=pl.BlockSpec(memory_space=pl.ANY),
        out_shape=jax.ShapeDtypeStruct((n_rows, D_MODEL), F32),
        scratch_shapes=[pltpu.SemaphoreType.DMA((1,))],
        input_output_aliases={2: 0},
        compiler_params=_params("arbitrary"),
        name="moe_dispatch",
    )(dest, h2, xs_prev)


def _expert_kernel(te_ref, nu_ref, x_ref, wgu_ref, bgu_ref, wd_ref, bd_ref, o_ref):
    del te_ref
    used = pl.program_id(0) < nu_ref[0]

    @pl.when(jnp.logical_not(used))
    def _():
        o_ref[...] = jnp.zeros_like(o_ref)

    @pl.when(used)
    def _():
        gu = _dot(x_ref[...].astype(BF16), wgu_ref[...]) + bgu_ref[...]
        f = gu.shape[1] // 2
        g_lin = jnp.minimum(gu[:, 0:f], SWIGLU_LIMIT)
        up = jnp.clip(gu[:, f:], -SWIGLU_LIMIT, SWIGLU_LIMIT)
        act = g_lin * _sigmoid(SWIGLU_ALPHA * g_lin) * (up + 1.0)
        o_ref[...] = _dot(act.astype(BF16), wd_ref[...]) + bd_ref[...]


def _experts(xs, tile_expert, n_used, wgu, bgu, wd, bd):
    n_rows = xs.shape[0]
    f2 = wgu.shape[2]
    used = lambda g, nu: jnp.minimum(g, nu[0] - 1)
    grid_spec = pltpu.PrefetchScalarGridSpec(
        num_scalar_prefetch=2,
        grid=(n_rows // MOE_TILE,),
        in_specs=[
            pl.BlockSpec((MOE_TILE, D_MODEL), lambda g, te, nu: (used(g, nu), 0)),
            pl.BlockSpec((None, D_MODEL, f2), lambda g, te, nu: (te[used(g, nu)], 0, 0)),
            pl.BlockSpec((None, 1, f2), lambda g, te, nu: (te[used(g, nu)], 0, 0)),
            pl.BlockSpec((None, f2 // 2, D_MODEL), lambda g, te, nu: (te[used(g, nu)], 0, 0)),
            pl.BlockSpec((None, 1, D_MODEL), lambda g, te, nu: (te[used(g, nu)], 0, 0)),
        ],
        out_specs=pl.BlockSpec((MOE_TILE, D_MODEL), lambda g, te, nu: (g, 0)),
    )
    return pl.pallas_call(
        _expert_kernel,
        grid_spec=grid_spec,
        out_shape=jax.ShapeDtypeStruct((n_rows, D_MODEL), F32),
        compiler_params=_params("arbitrary"),
        name="moe_experts",
    )(tile_expert, n_used, xs, wgu, bgu, wd, bd)


def _collect_kernel(dest_ref, dest_next_ref, gate_ref, x_ref, mod_ref, y_ref, o_ref, buf_ref, sem):
    tm = x_ref.shape[0]
    step = pl.program_id(0)
    slot = step % 2

    def issue_tile(d_ref, to_slot):
        def issue(tok, carry):
            for kq in range(TOP_K):
                row = d_ref[0, tok * TOP_K + kq]
                pltpu.make_async_copy(y_ref.at[pl.ds(row, 1), :],
                                      buf_ref.at[to_slot, kq, pl.ds(tok, 1), :], sem.at[to_slot]).start()
            return carry

        lax.fori_loop(0, tm, issue, 0, unroll=4)

    @pl.when(step == 0)
    def _():
        issue_tile(dest_ref, slot)

    @pl.when(step + 1 < pl.num_programs(0))
    def _():
        issue_tile(dest_next_ref, 1 - slot)

    for kq in range(TOP_K):
        pltpu.make_async_copy(y_ref.at[pl.ds(0, tm), :], buf_ref.at[slot, kq], sem.at[slot]).wait()
    gate = gate_ref[...]
    acc = gate[:, 0:1] * buf_ref[slot, 0]
    for kq in range(1, TOP_K):
        acc = acc + gate[:, kq:kq + 1] * buf_ref[slot, kq]
    o_ref[...] = x_ref[...] + mod_ref[5:6, :] * acc


def _collect(dest, gate_rows, x2, mod_l, ys, geom, latent_only):
    t = x2.shape[0]
    tm = ROW_TILE
    tpb, ctx_tiles, batch = geom
    row = functools.partial(_mod_row, tiles_per_b=tpb, ctx_tiles=ctx_tiles, batch=batch)
    if latent_only:
        lat = tpb - ctx_tiles
        n_steps = batch * lat
        tile = lambda i: (i // lat) * tpb + ctx_tiles + i % lat
    else:
        n_steps = t // tm
        tile = lambda i: i
    nxt = lambda i: tile(jnp.minimum(i + 1, n_steps - 1))
    return pl.pallas_call(
        _collect_kernel,
        grid=(n_steps,),
        in_specs=[
            pl.BlockSpec((None, 1, tm * TOP_K), lambda i: (tile(i), 0, 0), memory_space=pltpu.SMEM),
            pl.BlockSpec((None, 1, tm * TOP_K), lambda i: (nxt(i), 0, 0), memory_space=pltpu.SMEM),
            pl.BlockSpec((tm, 128), lambda i: (tile(i), 0)),
            pl.BlockSpec((tm, D_MODEL), lambda i: (tile(i), 0)),
            pl.BlockSpec((None, 6, D_MODEL), lambda i: (row(tile(i)), 0, 0)),
            pl.BlockSpec(memory_space=pl.ANY),
        ],
        out_specs=pl.BlockSpec((tm, D_MODEL), lambda i: (i, 0)),
        out_shape=jax.ShapeDtypeStruct((n_steps * tm, D_MODEL), F32),
        scratch_shapes=[pltpu.VMEM((2, TOP_K, tm, D_MODEL), F32), pltpu.SemaphoreType.DMA((2,))],
        compiler_params=_params("arbitrary"),
        name="moe_collect",
    )(dest, dest, gate_rows, x2, mod_l, ys)


def _rope_angles(pos, dim):
    inv = ROPE_BASE ** (-jnp.arange(0, dim, 2, dtype=F32) / dim)
    return pos.astype(F32)[:, None] * inv[None, :]


def _rope_tables(seq, ctx_len):
    rows = seq // GRID_W
    pos = jnp.arange(seq, dtype=jnp.int32)
    ang = _rope_angles(pos, RET_DK)
    r_cos = jnp.tile(jnp.concatenate([jnp.cos(ang), jnp.cos(ang)], -1), (1, RET_HEADS))
    r_sin = jnp.tile(jnp.concatenate([-jnp.sin(ang), jnp.sin(ang)], -1), (1, RET_HEADS))
    ra = _rope_angles(pos // GRID_W, MLA_ROPE // 2)
    ca = _rope_angles(pos % GRID_W, MLA_ROPE // 2)
    one = jnp.ones((seq, MLA_NOPE), F32)
    zero = jnp.zeros((seq, MLA_NOPE), F32)
    m_cos = jnp.concatenate([one, jnp.cos(ra), jnp.cos(ra), jnp.cos(ca), jnp.cos(ca), one[:, :32]], -1)
    m_sin = jnp.concatenate([zero, -jnp.sin(ra), jnp.sin(ra), -jnp.sin(ca), jnp.sin(ca), zero[:, :32]], -1)
    del rows
    ctx1 = lambda n: jnp.ones((ctx_len, n), F32)
    ctx0 = lambda n: jnp.zeros((ctx_len, n), F32)
    return (jnp.concatenate([ctx1(256), r_cos], 0), jnp.concatenate([ctx0(256), r_sin], 0),
            jnp.concatenate([ctx1(128), m_cos], 0), jnp.concatenate([ctx0(128), m_sin], 0))


def kernel(x, c, ctx, c_ctx, ada_w, ada_b, norm1_g, norm2_g, w_in, mla_q_norm_g, mla_w_uq, mla_kv_norm_g, mla_w_ukv, mla_qk_q_g, mla_qk_k_g, rwkv_mu_prev, rwkv_mu_next, rwkv_w0, rwkv_w2, rwkv_a0, rwkv_a2, rwkv_g2, rwkv_k_k, rwkv_k_a, rwkv_r_k, rwkv_lnx_g, rwkv_lnx_b, w_br_ret, w_br_mla, w_br_rwkv, w_out, w_router, b_router, w_gu, b_gu, w_down, b_down):
    b, s, d = x.shape
    ctx_len = ctx.shape[1]
    depth = ada_w.shape[0]
    l = ctx_len + s
    t = b * l
    assert d == D_MODEL and b < 16 and ctx_len % ROW_TILE == 0 and s % ROW_TILE == 0
    geom = (l // ROW_TILE, ctx_len // ROW_TILE, b)
    xa =jnp.concatenate([ctx, x], axis=1).reshape(t, d)
    cc = jnp.zeros((16, d), F32).at[:b].set(c).at[b].set(c_ctx)
    mod = _ada_all(cc, ada_w.astype(BF16), ada_b).reshape(depth, 16, 6, d)

    ret_cos, ret_sin, mla_cos, mla_sin = _rope_tables(s, ctx_len)
    ret_consts = _retention_consts()
    rwkv_consts = _rwkv_consts()
    head_blk = jnp.asarray(np.kron(np.eye(RWKV_HEADS), np.ones((RWKV_HD, RWKV_HD))), BF16)
    route_tri = jnp.asarray(np.tril(np.ones((ROW_TILE, ROW_TILE)), -1), BF16)

    xs = None
    for li in range(depth):
        p2 = _inproj(xa, mod[li], norm1_g[li], _pad_w_in(w_in[li]), geom)
        p3 = p2.reshape(b, l, P_COLS)

        retf, retb = (r.reshape(t, RET_W) for r in _retention(p3, ret_cos, ret_sin, ret_consts, ctx_len))

        mw = _mla_weights(mla_q_norm_g[li], mla_w_uq[li], mla_kv_norm_g[li], mla_w_ukv[li],
                          mla_qk_q_g[li], mla_qk_k_g[li])
        q2, k2, v2 = _mla_prep(p2, mw, mla_cos, mla_sin, l)
        mla = _attention(q2.reshape(b, l, 1024), k2.reshape(b, l, 1024), v2.reshape(b, l, 1024),
                         ctx_len).reshape(t, MLA_W)

        rw = dict(mup=rwkv_mu_prev[li].reshape(1, -1), mun=rwkv_mu_next[li].reshape(1, -1),
                  w0=rwkv_w0[li].reshape(2, 1, RWKV_W), w2=rwkv_w2[li].astype(BF16),
                  a0=rwkv_a0[li].reshape(2, 1, RWKV_W), a2=rwkv_a2[li].astype(BF16),
                  g2=rwkv_g2[li].astype(BF16), kk=rwkv_k_k[li].reshape(1, -1),
                  ka=rwkv_k_a[li].reshape(1, -1), rk=rwkv_r_k[li].reshape(1, -1), blk=head_blk)
        r_, v_, na_, kd_, bb_, lw_, g_, bonus_ = _rwkv_elem(p2, rw, l, ctx_len)
        yf, yb = _rwkv_chunk(dict(r=r_, v=v_, na=na_, kd=kd_, bb=bb_, lw=lw_), b, l, ctx_len, rwkv_consts)

        mg = dict(lng=rwkv_lnx_g[li].reshape(1, -1), lnb=rwkv_lnx_b[li].reshape(1, -1), blk=head_blk,
                  wr=w_br_ret[li].astype(BF16), wm=w_br_mla[li].astype(BF16),
                  ww=w_br_rwkv[li].astype(BF16), wo=w_out[li].astype(BF16))
        xa = _merge(xa, mod[li], p2, retf, retb, mla, yf.reshape(t, RWKV_W), yb.reshape(t, RWKV_W), bonus_, g_,
                    mg, geom)

        wr_pad = jnp.pad(w_router[li], ((0, 0), (0, 128 - N_EXPERTS))).astype(BF16)
        br_pad = jnp.concatenate([b_router[li], jnp.full((128 - N_EXPERTS,), -1e30, F32)]).reshape(1, 128)
        h2, idx_rows, rank_rows, gate_rows, cnt = _route(xa, mod[li], norm2_g[li], wr_pad, br_pad,
                                                         route_tri, geom)
        dest, tile_expert, n_used, n_rows = _moe_layout(
            idx_rows[:, :TOP_K], rank_rows[:, :TOP_K], cnt[0, :N_EXPERTS].astype(jnp.int32), t)
        if xs is None:
            xs = jnp.zeros((n_rows, D_MODEL), F32)
        xs = _dispatch(dest, h2, xs)
        ys = _experts(xs, tile_expert, n_used, w_gu[li].astype(BF16), b_gu[li].reshape(N_EXPERTS, 1, -1),
                      w_down[li].astype(BF16), b_down[li].reshape(N_EXPERTS, 1, -1))
        xa = _collect(dest, gate_rows, xa, mod[li], ys, geom, latent_only=li == depth - 1)

    return xa.reshape(b, s, d)
```

```python
import functools

import jax
import jax.numpy as jnp
import numpy as np
from jax import lax
from jax.experimental import pallas as pl
from jax.experimental.pallas import tpu as pltpu

F32 = jnp.float32
BF16 = jnp.bfloat16

D_MODEL = 1024
GRID_W = 64
RET_HEADS, RET_DK, RET_DV, RET_CHUNK = 4, 64, 128, 128
RET_W = RET_HEADS * RET_DV
MLA_HEADS, MLA_Q_RANK, MLA_KV_RANK, MLA_NOPE, MLA_ROPE, MLA_DV = 8, 256, 128, 64, 32, 64
MLA_QK = MLA_NOPE + MLA_ROPE
MLA_W = MLA_HEADS * MLA_DV
MLA_PAD_W = MLA_HEADS * 128
RWKV_HEADS, RWKV_HD = 8, 64
RWKV_W = RWKV_HEADS * RWKV_HD
RWKV_DECAY_LORA, RWKV_A_LORA, RWKV_GATE_LORA = 64, 64, 128
RWKV_GN_EPS = 64e-5
RWKV_IN = 3 * RWKV_W + 2 * RWKV_DECAY_LORA + 2 * RWKV_A_LORA + RWKV_GATE_LORA
N_EXPERTS, TOP_K = 32, 4
SWIGLU_LIMIT, SWIGLU_ALPHA = 7.0, 1.702
ROPE_BASE = 10000.0
NORM_EPS = 1e-6
HEAD_NORM_EPS = 1e-5

GATE_OFF = 0
RET_OFF = 3 * D_MODEL
MLA_OFF = RET_OFF + 2048
RWKV_OFF = MLA_OFF + 512 + 128
P_COLS = RWKV_OFF + RWKV_IN
IN_COL_TILE = 3840

ROW_TILE = 256
RWKV_CHUNK = 64
RWKV_ROWS = 2
HALF = 256
ATTN_Q_SCALE = MLA_QK ** -0.5 * 1.4426950408889634
MOE_TILE = 256
VMEM_LIMIT = 48 * 1024 * 1024

NT_DIMS = (((1,), (1,)), ((), ()))
TN_DIMS = (((0,), (0,)), ((), ()))


def _params(*sem):
    return pltpu.CompilerParams(dimension_semantics=sem, vmem_limit_bytes=VMEM_LIMIT)


def _dot(a, b):
    return jnp.dot(a, b, preferred_element_type=F32)


def _dot_nt(a, b):
    return lax.dot_general(a, b, NT_DIMS, preferred_element_type=F32)


def _dot_tn(a, b):
    return lax.dot_general(a, b, TN_DIMS, preferred_element_type=F32)


def _split3(x):
    hi = x.astype(BF16)
    r1 = x - hi.astype(F32)
    mid = r1.astype(BF16)
    lo = (r1 - mid.astype(F32)).astype(BF16)
    return hi, mid, lo


def _dot_exact_rhs01(x, m01):
    hi, mid, lo = _split3(x)
    return _dot(hi, m01) + _dot(mid, m01) + _dot(lo, m01)


def _dot_exact_lhs01(m01, x):
    hi, mid, lo = _split3(x)
    return _dot(m01, hi) + _dot(m01, mid) + _dot(m01, lo)


def _sigmoid(x):
    return 1.0 / (1.0 + jnp.exp(-x))


def _silu(x):
    return x * _sigmoid(x)


def _ada_kernel(c_ref, w_ref, b_ref, o_ref):
    s = _silu(c_ref[...])
    o_ref[...] = _dot(s.astype(BF16), w_ref[...]) + b_ref[...]


def _ada_all(cc, ada_w_bf, ada_b):
    depth = ada_w_bf.shape[0]
    tn = 1536
    return pl.pallas_call(
        _ada_kernel,
        grid=(depth, 6 * D_MODEL // tn),
        in_specs=[
            pl.BlockSpec((16, D_MODEL), lambda l, j: (0, 0)),
            pl.BlockSpec((None, D_MODEL, tn), lambda l, j: (l, 0, j)),
            pl.BlockSpec((None, 1, tn), lambda l, j: (l, 0, j)),
        ],
        out_specs=pl.BlockSpec((None, 16, tn), lambda l, j: (l, 0, j)),
        out_shape=jax.ShapeDtypeStruct((depth, 16, 6 * D_MODEL), F32),
        compiler_params=_params("arbitrary", "arbitrary"),
        name="ada_mod",
    )(cc, ada_w_bf, ada_b.reshape(depth, 1, 6 * D_MODEL))


def _mod_row(i, tiles_per_b, ctx_tiles, batch):
    return jnp.where(i % tiles_per_b < ctx_tiles, batch, i // tiles_per_b)


def _inproj_kernel(x_ref, mod_ref, g_ref, w_ref, o_ref):
    x = x_ref[...]
    y = x * lax.rsqrt(jnp.mean(x * x, axis=-1, keepdims=True) + NORM_EPS) * g_ref[...]
    h = y * (1.0 + mod_ref[1:2, :]) + mod_ref[0:1, :]
    o_ref[...] = _dot(h.astype(BF16), w_ref[...])


def _inproj(x2, mod_l, g, w_pad, geom):
    t = x2.shape[0]
    tpb, ctx_tiles, batch = geom
    row = functools.partial(_mod_row, tiles_per_b=tpb, ctx_tiles=ctx_tiles, batch=batch)
    return pl.pallas_call(
        _inproj_kernel,
        grid=(P_COLS // IN_COL_TILE, t // ROW_TILE),
        in_specs=[
            pl.BlockSpec((ROW_TILE, D_MODEL), lambda j, i: (i, 0)),
            pl.BlockSpec((None, 6, D_MODEL), lambda j, i: (row(i), 0, 0)),
            pl.BlockSpec((1, D_MODEL), lambda j, i: (0, 0)),
            pl.BlockSpec((D_MODEL, IN_COL_TILE), lambda j, i: (0, j)),
        ],
        out_specs=pl.BlockSpec((ROW_TILE, IN_COL_TILE), lambda j, i: (i, j)),
        out_shape=jax.ShapeDtypeStruct((t, P_COLS), F32),
        compiler_params=_params("arbitrary", "arbitrary"),
        name="in_proj",
    )(x2, mod_l, g.reshape(1, D_MODEL), w_pad)


def _pad_w_in(w_in):
    d = w_in.shape[0]
    ret = w_in[:, 0:2048]
    cq = w_in[:, 2048:2304]
    ckv = w_in[:, 2304:2432]
    kpe = w_in[:, 2432:2464]
    rwkv = w_in[:, 2464:2464 + RWKV_IN]
    gate = w_in[:, 2464 + RWKV_IN:]
    z = lambda n: jnp.zeros((d, n), w_in.dtype)
    return jnp.concatenate([gate, ret, cq, ckv, z(64), kpe, z(32), z(128), rwkv], axis=1).astype(BF16)


def _scan_chunk(d, j, n_ctx, n_all):
    bwd = jnp.where(j < n_ctx, n_ctx - 1 - j, n_all - 1 - j + n_ctx)
    return jnp.where(d == 0, j, bwd)


def _ret_pair_kernel(qf_ref, kf_ref, vf_ref, gf_ref, cosf_ref, sinf_ref,
                     qb_ref, kb_ref, vb_ref, gb_ref, cosb_ref, sinb_ref,
                     dmat_ref, qdec_ref, kdec_ref, of_ref, ob_ref, s_ref, *, cdec):
    @pl.when(pl.program_id(1) == 0)
    def _():
        s_ref[...] = jnp.zeros_like(s_ref)

    c = RET_CHUNK
    w = RET_HEADS * RET_DK
    first = (lax.broadcasted_iota(jnp.int32, (c, w), 1) & (RET_DK - 1)) < RET_DK // 2

    def rot(x, cos, sin):
        partner = jnp.where(first, pltpu.roll(x, w - RET_DK // 2, 1), pltpu.roll(x, RET_DK // 2, 1))
        return x * cos + partner * sin

    dirs = ((qf_ref, kf_ref, vf_ref, gf_ref, cosf_ref, sinf_ref, of_ref),
            (qb_ref, kb_ref, vb_ref, gb_ref, cosb_ref, sinb_ref, ob_ref))
    inst = []
    for d, (q_ref, k_ref, v_ref, g_ref, cos_ref, sin_ref, o_ref) in enumerate(dirs):
        cos, sin = cos_ref[...], sin_ref[...]
        q = rot(q_ref[...], cos, sin).astype(BF16)
        k = rot(k_ref[...], cos, sin) * (RET_DK ** -0.5)
        kd_t = (k * kdec_ref[d]).T.astype(BF16)
        kb = k.astype(BF16)
        vb = v_ref[...].astype(BF16)
        for h in range(RET_HEADS):
            ksl = slice(h * RET_DK, (h + 1) * RET_DK)
            vsl = slice(h * RET_DV, (h + 1) * RET_DV)
            inst.append(dict(d=d, h=h, vsl=vsl, q=q[:, ksl], k=kb[:, ksl], v=vb[:, vsl], kd=kd_t[ksl, :],
                             g_ref=g_ref, o_ref=o_ref))
    s_old = [s_ref[s["d"], s["h"]] for s in inst]
    atts = [_dot_nt(s["q"], s["k"]) * dmat_ref[s["d"], s["h"]] for s in inst]
    inter = [_dot(s["q"], so.astype(BF16)) * qdec_ref[s["d"], s["h"]] for s, so in zip(inst, s_old)]
    kvs = [_dot(s["kd"], s["v"]) for s in inst]
    outs = [_dot(att.astype(BF16), s["v"]) + it for s, att, it in zip(inst, atts, inter)]
    for s, so, kv, o in zip(inst, s_old, kvs, outs):
        s_ref[s["d"], s["h"]] = so * cdec[s["h"]] + kv
        oc = o - jnp.mean(o, axis=-1, keepdims=True)
        y = oc * lax.rsqrt(jnp.mean(oc * oc, axis=-1, keepdims=True) + HEAD_NORM_EPS)
        s["o_ref"][:, s["vsl"]] = y * _silu(s["g_ref"][:, s["vsl"]])


def _retention(p3, cos_t, sin_t, consts, ctx_len):
    b, l, _ = p3.shape
    n_all = l // RET_CHUNK
    n_ctx = ctx_len // RET_CHUNK
    dmat, qdec, kdec, cdec = consts
    ch = functools.partial(_scan_chunk, n_ctx=n_ctx, n_all=n_all)
    qb, vb = RET_OFF // 256, RET_OFF // 512

    def specs(d):
        return [
            pl.BlockSpec((None, RET_CHUNK, 256), lambda bi, j: (bi, ch(d, j), qb)),
            pl.BlockSpec((None, RET_CHUNK, 256), lambda bi, j: (bi, ch(d, j), qb + 1)),
            pl.BlockSpec((None, RET_CHUNK, 512), lambda bi, j: (bi, ch(d, j), vb + 1)),
            pl.BlockSpec((None, RET_CHUNK, 512), lambda bi, j: (bi, ch(d, j), vb + 2 + d)),
            pl.BlockSpec((RET_CHUNK, 256), lambda bi, j: (ch(d, j), 0)),
            pl.BlockSpec((RET_CHUNK, 256), lambda bi, j: (ch(d, j), 0)),
        ]

    const = lambda shape: pl.BlockSpec(shape, lambda bi, j: (0,) * len(shape))
    out = jax.ShapeDtypeStruct((b, l, RET_W), F32)
    out_spec = lambda d: pl.BlockSpec((None, RET_CHUNK, RET_W), lambda bi, j: (bi, ch(d, j), 0))
    return pl.pallas_call(
        functools.partial(_ret_pair_kernel, cdec=cdec),
        grid=(b, n_all),
        in_specs=specs(0) + specs(1) + [
            const((2, RET_HEADS, RET_CHUNK, RET_CHUNK)), const((2, RET_HEADS, RET_CHUNK, RET_DV)),
            const((2, RET_CHUNK, 256))],
        out_specs=[out_spec(0), out_spec(1)],
        out_shape=[out, out],
        scratch_shapes=[pltpu.VMEM((2, RET_HEADS, RET_DK, RET_DV), F32)],
        compiler_params=_params("arbitrary", "arbitrary"),
        name="retention",
    )(p3, p3, p3, p3, cos_t, sin_t, p3, p3, p3, p3, cos_t, sin_t, dmat, qdec, kdec)


def _retention_consts():
    gamma = 1.0 - jnp.exp2(-5.0 - jnp.arange(RET_HEADS, dtype=F32))
    lg = jnp.log(gamma)
    idx = jnp.arange(RET_CHUNK, dtype=F32)
    diff = idx[:, None] - idx[None, :]
    lower = jnp.where(diff >= 0, jnp.exp(lg[:, None, None] * jnp.maximum(diff, 0.0)), 0.0)
    dmat = jnp.stack([lower, jnp.swapaxes(lower, 1, 2)])
    qd_f = jnp.exp((idx + 1.0)[None, :] * lg[:, None])
    qd_b = jnp.exp((RET_CHUNK - idx)[None, :] * lg[:, None])
    qdec = jnp.broadcast_to(jnp.stack([qd_f, qd_b])[..., None], (2, RET_HEADS, RET_CHUNK, RET_DV))
    kd_f = jnp.exp((RET_CHUNK - 1.0 - idx)[None, :] * lg[:, None])
    kd_b = jnp.exp(idx[None, :] * lg[:, None])
    kdec = jnp.repeat(jnp.swapaxes(jnp.stack([kd_f, kd_b]), 1, 2), RET_DK, axis=2)
    gam64 = 1.0 - np.exp2(-5.0 - np.arange(RET_HEADS))
    cdec = tuple(float(np.float32(np.exp(RET_CHUNK * np.log(np.float32(gv))))) for gv in gam64)
    return dmat, qdec, kdec, cdec


def _mla_prep_kernel(p_ref, qng_ref, wuq_ref, kvg_ref, wuk_ref, wuv_ref, qg_ref, kg_ref,
                     cos_ref, sin_ref, q_out, k_out, v_out):
    blk = p_ref[...]
    cq = blk[:, 0:MLA_Q_RANK]
    ckv = blk[:, MLA_Q_RANK:MLA_Q_RANK + MLA_KV_RANK]
    kpe = blk[:, MLA_Q_RANK + MLA_KV_RANK:]

    def rms(x, g, n):
        return x * lax.rsqrt(jnp.sum(x * x, axis=-1, keepdims=True) * (1.0 / n) + NORM_EPS) * g

    qn = rms(cq, qng_ref[...], MLA_Q_RANK).astype(BF16)
    kvn = rms(ckv, kvg_ref[...], MLA_KV_RANK).astype(BF16)
    q_raw = _dot(qn, wuq_ref[...])
    k_raw = _dot(kvn, wuk_ref[...])
    ones_col = (lax.broadcasted_iota(jnp.int32, (1, MLA_PAD_W), 1) & 127) == MLA_DV
    v_out[...] = (_dot(kvn, wuv_ref[...]) + jnp.where(ones_col, 1.0, 0.0)).astype(BF16)
    cos = cos_ref[...]
    sin = sin_ref[...]
    tm = blk.shape[0]
    first = (lax.broadcasted_iota(jnp.int32, (tm, 128), 1) & 15) < 8

    def rope(x):
        partner = jnp.where(first, pltpu.roll(x, 120, 1), pltpu.roll(x, 8, 1))
        return x * cos + partner * sin

    heads = [slice(h * 128, (h + 1) * 128) for h in range(MLA_HEADS)]
    qs = [rms(q_raw[:, sl], qg_ref[...], MLA_QK) for sl in heads]
    ks = [rms(k_raw[:, sl] + kpe, kg_ref[...], MLA_QK) for sl in heads]
    for sl, qh, kh in zip(heads, qs, ks):
        q_out[:, sl] = (rope(qh) * ATTN_Q_SCALE).astype(BF16)
        k_out[:, sl] = rope(kh).astype(BF16)


def _mla_prep(p2, w, cos_t, sin_t, rows_per_b):
    t = p2.shape[0]
    tiles_per_b = rows_per_b // ROW_TILE
    const = lambda shape: pl.BlockSpec(shape, lambda i: (0,) * len(shape))
    wide = pl.BlockSpec((ROW_TILE, MLA_PAD_W), lambda i: (i, 0))
    out = jax.ShapeDtypeStruct((t, MLA_PAD_W), BF16)
    return pl.pallas_call(
        _mla_prep_kernel,
        grid=(t // ROW_TILE,),
        in_specs=[
            pl.BlockSpec((ROW_TILE, 512), lambda i: (i, MLA_OFF // 512)),
            const((1, MLA_Q_RANK)), const((MLA_Q_RANK, MLA_PAD_W)),
            const((1, MLA_KV_RANK)), const((MLA_KV_RANK, MLA_PAD_W)), const((MLA_KV_RANK, MLA_PAD_W)),
            const((1, 128)), const((1, 128)),
            pl.BlockSpec((ROW_TILE, 128), lambda i: (i % tiles_per_b, 0)),
            pl.BlockSpec((ROW_TILE, 128), lambda i: (i % tiles_per_b, 0)),
        ],
        out_specs=[wide, wide, wide],
        out_shape=[out, out, out],
        compiler_params=_params("arbitrary"),
        name="mla_prep",
    )(p2, w["qng"], w["wuq"], w["kvg"], w["wuk"], w["wuv"], w["qg"], w["kg"], cos_t, sin_t)


def _mla_weights(q_norm_g, w_uq, kv_norm_g, w_ukv, qk_q_g, qk_k_g):
    wq = w_uq.reshape(MLA_Q_RANK, MLA_HEADS, MLA_QK)
    wq = jnp.pad(wq, ((0, 0), (0, 0), (0, 128 - MLA_QK))).reshape(MLA_Q_RANK, MLA_PAD_W)
    wkv = w_ukv.reshape(MLA_KV_RANK, MLA_HEADS, MLA_NOPE + MLA_DV)
    wk = jnp.pad(wkv[:, :, :MLA_NOPE], ((0, 0), (0, 0), (0, 128 - MLA_NOPE))).reshape(MLA_KV_RANK, MLA_PAD_W)
    wv = jnp.pad(wkv[:, :, MLA_NOPE:], ((0, 0), (0, 0), (0, 128 - MLA_DV))).reshape(MLA_KV_RANK, MLA_PAD_W)
    padg = lambda g: jnp.pad(g, (0, 128 - MLA_QK)).reshape(1, 128)
    return dict(qng=q_norm_g.reshape(1, -1), wuq=wq.astype(BF16), kvg=kv_norm_g.reshape(1, -1),
                wuk=wk.astype(BF16), wuv=wv.astype(BF16), qg=padg(qk_q_g), kg=padg(qk_k_g))


def _attn_kernel(q_ref, k_ref, v_ref, o_ref, *, ctx_len, ctx_tiles):
    def attend(kv_len):
        hs = [slice(hh * 128, (hh + 1) * 128) for hh in range(2)]
        ss = [_dot_nt(q_ref[:, sl], k_ref[0:kv_len, sl]) for sl in hs]
        ps = [jnp.exp2((s - jnp.max(s, axis=-1, keepdims=True)).astype(BF16)) for s in ss]
        o_aug = [_dot(p, v_ref[0:kv_len, sl]) for p, sl in zip(ps, hs)]
        outs = [o[:, 0:MLA_DV] / o[:, MLA_DV:MLA_DV + 1] for o in o_aug]
        o_ref[...] = jnp.concatenate(outs, axis=-1).astype(BF16)

    is_ctx = pl.program_id(2) < ctx_tiles

    @pl.when(is_ctx)
    def _():
        attend(ctx_len)

    @pl.when(jnp.logical_not(is_ctx))
    def _():
        attend(k_ref.shape[0])


def _attention(q3, k3, v3, ctx_len):
    b, l, _ = q3.shape
    tq = ROW_TILE
    return pl.pallas_call(
        functools.partial(_attn_kernel, ctx_len=ctx_len, ctx_tiles=ctx_len // tq),
        grid=(b, MLA_HEADS // 2, l // tq),
        in_specs=[
            pl.BlockSpec((None, tq, 256), lambda bi, h, i: (bi, i, h)),
            pl.BlockSpec((None, l, 256), lambda bi, h, i: (bi, 0, h)),
            pl.BlockSpec((None, l, 256), lambda bi, h, i: (bi, 0, h)),
        ],
        out_specs=pl.BlockSpec((None, tq, 128), lambda bi, h, i: (bi, i, h)),
        out_shape=jax.ShapeDtypeStruct((b, l, MLA_W), BF16),
        compiler_params=_params("arbitrary", "arbitrary", "arbitrary"),
        name="mla_attention",
    )(q3, k3, v3)


def _rwkv_elem_kernel(t_ref, prev_ref, next_ref, mup_ref, mun_ref, w0_ref, w2_ref, a0_ref, a2_ref,
                      g2_ref, kk_ref, ka_ref, rk_ref, blk_ref,
                      r_out, v_out, na_out, kd_out, bb_out, lw_out, g_out, bonus_out,
                      *, tiles_per_b, ctx_tiles):
    i = pl.program_id(0) % tiles_per_b
    starts = jnp.logical_or(i == 0, i == ctx_tiles)
    ends = jnp.logical_or(i == ctx_tiles - 1, i == tiles_per_b - 1)
    t = t_ref[...]
    tm = t.shape[0]
    prev_row = jnp.where(starts, 0.0, prev_ref[7:8, :])
    next_row = jnp.where(ends, 0.0, next_ref[0:1, :])
    rid = lax.broadcasted_iota(jnp.int32, t.shape, 0)
    prev = jnp.where(rid == 0, prev_row, pltpu.roll(t, 1, 0))
    nxt = jnp.where(rid == tm - 1, next_row, pltpu.roll(t, tm - 1, 0))
    t = t + mup_ref[...] * (prev - t) + mun_ref[...] * (nxt - t)

    w = RWKV_W
    r = t[:, 0:w]
    k = t[:, w:2 * w]
    v = t[:, 2 * w:3 * w]
    o = 3 * w
    wh = (t[:, o:o + 64], t[:, o + 64:o + 128])
    ah = (t[:, o + 128:o + 192], t[:, o + 192:o + 256])
    gh = t[:, o + 256:o + 384]
    blk = blk_ref[...]

    kk = k * kk_ref[...]
    kk = kk * lax.rsqrt(jnp.maximum(_dot_exact_rhs01(kk * kk, blk), 1e-24))
    r_out[...] = r
    v_out[...] = v
    na_out[...] = -kk
    g_out[...] = _dot(_sigmoid(gh).astype(BF16), g2_ref[...])
    rk = r * rk_ref[...]
    bonus = jnp.zeros_like(v)
    for d in range(2):
        z = w0_ref[d] + _dot(jnp.tanh(wh[d]).astype(BF16), w2_ref[d])
        nz = -z
        softplus = jnp.maximum(nz, 0.0) + jnp.log(1.0 + jnp.exp(-jnp.abs(nz)))
        lw_out[d] = -jnp.exp(-softplus - 0.5)
        a = _sigmoid(a0_ref[d] + _dot(ah[d].astype(BF16), a2_ref[d]))
        kd = k * (1.0 + (a - 1.0) * ka_ref[...])
        kd_out[d] = kd
        bb_out[d] = kk * a
        bonus = bonus + _dot_exact_rhs01(rk * kd, blk) * v
    bonus_out[...] = bonus


def _rwkv_elem(p2, w, rows_per_b, ctx_len):
    t = p2.shape[0]
    tm = ROW_TILE
    tiles_per_b = rows_per_b // tm
    n8 = t // 8
    cb = RWKV_OFF // RWKV_IN
    const = lambda shape: pl.BlockSpec(shape, lambda i: (0,) * len(shape))
    one = pl.BlockSpec((tm, RWKV_W), lambda i: (i, 0))
    two = pl.BlockSpec((2, tm, RWKV_W), lambda i: (0, i, 0))
    s1 = jax.ShapeDtypeStruct((t, RWKV_W), F32)
    s2 = jax.ShapeDtypeStruct((2, t, RWKV_W), F32)
    return pl.pallas_call(
        functools.partial(_rwkv_elem_kernel, tiles_per_b=tiles_per_b, ctx_tiles=ctx_len // tm),
        grid=(t // tm,),
        in_specs=[
            pl.BlockSpec((tm, RWKV_IN), lambda i: (i, cb)),
            pl.BlockSpec((8, RWKV_IN), lambda i: (jnp.maximum(i * (tm // 8) - 1, 0), cb)),
            pl.BlockSpec((8, RWKV_IN), lambda i: (jnp.minimum((i + 1) * (tm // 8), n8 - 1), cb)),
            const((1, RWKV_IN)), const((1, RWKV_IN)),
            const((2, 1, RWKV_W)), const((2, RWKV_DECAY_LORA, RWKV_W)),
            const((2, 1, RWKV_W)), const((2, RWKV_A_LORA, RWKV_W)),
            const((RWKV_GATE_LORA, RWKV_W)),
            const((1, RWKV_W)), const((1, RWKV_W)), const((1, RWKV_W)),
            const((RWKV_W, RWKV_W)),
        ],
        out_specs=[one, one, one, two, two, two, one, one],
        out_shape=[s1, s1, s1, s2, s2, s2, s1, s1],
        compiler_params=_params("arbitrary"),
        name="rwkv_elem",
    )(p2, p2, p2, w["mup"], w["mun"], w["w0"], w["w2"], w["a0"], w["a2"], w["g2"],
      w["kk"], w["ka"], w["rk"], w["blk"])


def _block_rows(x, bm):
    return jnp.where(bm, jnp.concatenate([x, x, x, x], axis=0), 0.0)


def _rwkv_chunk_kernel(rf_ref, vf_ref, naf_ref, kdf_ref, bbf_ref, lwf_ref,
                       rb_ref, vb_ref, nab_ref, kdb_ref, bbb_ref, lwb_ref,
                       tri_ref, ms_ref, mi_ref, of_ref, ob_ref, h_ref):
    @pl.when(pl.program_id(1) == 0)
    def _():
        h_ref[...] = jnp.zeros_like(h_ref)

    dirs = ((rf_ref, vf_ref, naf_ref, kdf_ref, bbf_ref, lwf_ref, of_ref),
            (rb_ref, vb_ref, nab_ref, kdb_ref, bbb_ref, lwb_ref, ob_ref))
    c = RWKV_CHUNK
    assert c == 64 and RWKV_HD == 64
    bm = (lax.broadcasted_iota(jnp.int32, (HALF, HALF), 0) >> 6
          == lax.broadcasted_iota(jnp.int32, (HALF, HALF), 1) >> 6)
    eye_p = ((lax.broadcasted_iota(jnp.int32, (c, HALF), 1) & (c - 1))
             == lax.broadcasted_iota(jnp.int32, (c, HALF), 0))
    eye_s = (lax.broadcasted_iota(jnp.int32, (HALF, HALF), 0)
             == lax.broadcasted_iota(jnp.int32, (HALF, HALF), 1))
    bf = lambda x: x.astype(BF16)
    blk = lambda x: _block_rows(x, bm)
    cat0 = lambda xs: jnp.concatenate(xs, axis=0)
    cat1 = lambda xs: jnp.concatenate(xs, axis=1)

    chains = [(d, bi) + dirs[d] for d in range(2) for bi in range(RWKV_ROWS)]
    cums = [_dot_exact_lhs01(tri_ref[d], lw_ref[bi]) for d, bi, _, _, _, _, _, lw_ref, _ in chains]
    inst = []
    for (d, bi, r_ref, v_ref, na_ref, kd_ref, bb_ref, lw_ref, o_ref), cum in zip(chains, cums):
        lw = lw_ref[bi]
        tot = cum[c - 1:c, :] if d == 0 else cum[0:1, :]
        e_neg = jnp.exp(-cum)
        e_rel = jnp.exp(tot - cum)
        e_tot = jnp.exp(tot)
        a_t = na_ref[bi] * jnp.exp(cum - lw)
        r_t = r_ref[bi] * jnp.exp(cum)
        kd, bb, v = kd_ref[bi], bb_ref[bi], v_ref[bi]
        for g in range(2):
            sl = slice(g * HALF, (g + 1) * HALF)
            inst.append(dict(
                d=d, bi=bi, g=g, sl=sl, o_ref=o_ref, a=a_t[:, sl], r=bf(r_t[:, sl]), v=v[:, sl],
                bt=bb[:, sl] * e_neg[:, sl], kt=kd[:, sl] * e_neg[:, sl],
                bk_p=bf(cat0([bb[:, sl] * e_rel[:, sl], kd[:, sl] * e_rel[:, sl]])), etot=e_tot[:, sl]))
    strict = [ms_ref[0] > 0.5, ms_ref[1] > 0.5]
    incl = [mi_ref[0] > 0.5, mi_ref[1] > 0.5]

    bigs = [_dot_nt(bf(cat0([s["a"], s["r"].astype(F32)])), bf(cat0([blk(s["bt"]), blk(s["kt"])])))
            for s in inst]
    for s, big in zip(inst, bigs):
        s["a_ab"] = jnp.where(strict[s["d"]], big[0:c, 0:HALF], 0.0)
        s["a_ak"] = bf(jnp.where(strict[s["d"]], big[0:c, HALF:], 0.0))
        s["a_rb"] = bf(jnp.where(incl[s["d"]], big[c:, 0:HALF], 0.0))
        s["a_rk"] = bf(jnp.where(incl[s["d"]], big[c:, HALF:], 0.0))
        s["v_bd"] = bf(blk(s["v"]))
        s["t"] = jnp.where(eye_p, 1.0, 0.0) + s["a_ab"]
    pws = [_dot(bf(s["a_ab"]), bf(blk(s["a_ab"]))) for s in inst]
    akvs = [_dot(s["a_ak"], s["v_bd"]) for s in inst]
    arkv = [_dot(s["a_rk"], s["v_bd"]) for s in inst]
    for it in range(5):
        prods = [_dot(bf(cat0([s["t"], pw])), bf(blk(pw))) for s, pw in zip(inst, pws)]
        for s, prod in zip(inst, prods):
            s["t"] = s["t"] + prod[0:c]
        pws = [prod[c:] for prod in prods]
    wus = [_dot(bf(s["t"]), bf(cat1([blk(s["a"]), blk(akv)]))) for s, akv in zip(inst, akvs)]
    h_f32 = [h_ref[s["bi"], s["d"], s["g"]] for s in inst]
    h_old = [bf(h) for h in h_f32]
    us = [_dot(bf(wu[:, 0:HALF]), h) + wu[:, HALF:] for wu, h in zip(wus, h_old)]
    rhs = [_dot(s["r"], h) for s, h in zip(inst, h_old)]
    arbu = [_dot(s["a_rb"], bf(blk(u))) for s, u in zip(inst, us)]
    incs = [_dot_tn(s["bk_p"], bf(cat0([u, s["v"]]))) for s, u in zip(inst, us)]
    for s, rh, au, ak, inc, h in zip(inst, rhs, arbu, arkv, incs, h_f32):
        s["o_ref"][s["bi"], :, s["sl"]] = rh + au + ak
        decay_col = jnp.sum(jnp.where(eye_s, s["etot"], 0.0), axis=1, keepdims=True)
        h_ref[s["bi"], s["d"], s["g"]] = decay_col * h + jnp.where(bm, inc, 0.0)


def _rwkv_chunk(e, b, l, ctx_len, consts):
    c = RWKV_CHUNK
    rows = RWKV_ROWS
    assert b % rows == 0
    n_all, n_ctx = l // c, ctx_len // c
    ch = functools.partial(_scan_chunk, n_ctx=n_ctx, n_all=n_all)
    tri, ms, mi = consts
    r3, v3, na3 = (x.reshape(b, l, RWKV_W) for x in (e["r"], e["v"], e["na"]))
    kd4, bb4, lw4 = (x.reshape(2, b, l, RWKV_W) for x in (e["kd"], e["bb"], e["lw"]))

    def specs(d):
        one = pl.BlockSpec((rows, c, RWKV_W), lambda bi, j: (bi, ch(d, j), 0))
        two = pl.BlockSpec((None, rows, c, RWKV_W), lambda bi, j: (d, bi, ch(d, j), 0))
        return one, two

    one_f, two_f = specs(0)
    one_b, two_b = specs(1)
    const = lambda shape: pl.BlockSpec(shape, lambda bi, j: (0,) * len(shape))
    out = jax.ShapeDtypeStruct((b, l, RWKV_W), F32)
    return pl.pallas_call(
        _rwkv_chunk_kernel,
        grid=(b // rows, n_all),
        in_specs=[one_f, one_f, one_f, two_f, two_f, two_f,
                  one_b, one_b, one_b, two_b, two_b, two_b,
                  const((2, c, c)), const((2, c, HALF)), const((2, c, HALF))],
        out_specs=[one_f, one_b],
        out_shape=[out, out],
        scratch_shapes=[pltpu.VMEM((rows, 2, 2, HALF, HALF), F32)],
        compiler_params=_params("arbitrary", "arbitrary"),
        name="rwkv_chunk",
    )(r3, v3, na3, kd4, bb4, lw4, r3, v3, na3, kd4, bb4, lw4, tri, ms, mi)


def _rwkv_consts():
    c = RWKV_CHUNK
    idx = np.arange(c)
    lower = idx[None, :] <= idx[:, None]
    tri = np.stack([lower, lower.T]).astype(np.float32)
    s_in_head = np.tile(idx, HALF // c)[None, :]
    t_row = idx[:, None]
    ms = np.stack([s_in_head < t_row, s_in_head > t_row]).astype(np.float32)
    mi = np.stack([s_in_head <= t_row, s_in_head >= t_row]).astype(np.float32)
    return jnp.asarray(tri, BF16), jnp.asarray(ms), jnp.asarray(mi)


def _merge_kernel(x_ref, mod_ref, gate_ref, retf_ref, retb_ref, mla_ref, yf_ref, yb_ref, bonus_ref, g_ref,
                  lng_ref, lnb_ref, blk_ref, wr_ref, wm_ref, ww_ref, wo_ref, o_ref):
    blk = blk_ref[...]
    ret = retf_ref[...] + retb_ref[...]
    y = yf_ref[...] + yb_ref[...]
    mean = _dot_exact_rhs01(y, blk) * (1.0 / RWKV_HD)
    yc = y - mean
    var = _dot_exact_rhs01(yc * yc, blk) * (1.0 / RWKV_HD)
    y = yc * lax.rsqrt(var + RWKV_GN_EPS) * lng_ref[...] + lnb_ref[...] + bonus_ref[...]
    rwk = y * g_ref[...]
    gate = gate_ref[...]
    d = D_MODEL
    mix = (_sigmoid(gate[:, 0:d]) * _dot(ret.astype(BF16), wr_ref[...])
           + _sigmoid(gate[:, d:2 * d]) * _dot(mla_ref[...], wm_ref[...])
           + _sigmoid(gate[:, 2 * d:]) * _dot(rwk.astype(BF16), ww_ref[...]))
    o_ref[...] = x_ref[...] + mod_ref[2:3, :] * _dot(mix.astype(BF16), wo_ref[...])


def _merge(x2, mod_l, p2, retf, retb, mla, yf, yb, bonus, g, w, geom):
    t = x2.shape[0]
    tm = ROW_TILE
    tpb, ctx_tiles, batch = geom
    row = functools.partial(_mod_row, tiles_per_b=tpb, ctx_tiles=ctx_tiles, batch=batch)
    const = lambda shape: pl.BlockSpec(shape, lambda i: (0,) * len(shape))
    rows = lambda width: pl.BlockSpec((tm, width), lambda i: (i, 0))
    return pl.pallas_call(
        _merge_kernel,
        grid=(t // tm,),
        in_specs=[
            rows(D_MODEL),
            pl.BlockSpec((None, 6, D_MODEL), lambda i: (row(i), 0, 0)),
            rows(3 * D_MODEL),
            rows(RET_W), rows(RET_W), rows(MLA_W), rows(RWKV_W), rows(RWKV_W), rows(RWKV_W), rows(RWKV_W),
            const((1, RWKV_W)), const((1, RWKV_W)), const((RWKV_W, RWKV_W)),
            const((RET_W, D_MODEL)), const((MLA_W, D_MODEL)), const((RWKV_W, D_MODEL)),
            const((D_MODEL, D_MODEL)),
        ],
        out_specs=rows(D_MODEL),
        out_shape=jax.ShapeDtypeStruct((t, D_MODEL), F32),
        compiler_params=_params("arbitrary"),
        name="merge",
    )(x2, mod_l, p2, retf, retb, mla, yf, yb, bonus, g, w["lng"], w["lnb"], w["blk"],
      w["wr"], w["wm"], w["ww"], w["wo"])


def _route_kernel(x_ref, mod_ref, g_ref, wr_ref, br_ref, tri_ref,
                  h_out, idx_out, rank_out, gate_out, cnt_out, run_ref):
    @pl.when(pl.program_id(0) == 0)
    def _():
        run_ref[...] = jnp.zeros_like(run_ref)

    x = x_ref[...]
    y = x * lax.rsqrt(jnp.mean(x * x, axis=-1, keepdims=True) + NORM_EPS) * g_ref[...]
    h = y * (1.0 + mod_ref[4:5, :]) + mod_ref[3:4, :]
    h_out[...] = h
    logits = _dot(h.astype(BF16), wr_ref[...]) + br_ref[...]
    lane = lax.broadcasted_iota(jnp.int32, logits.shape, 1).astype(F32)
    vals, idxs = [], []
    for _ in range(TOP_K):
        m = jnp.max(logits, axis=-1, keepdims=True)
        sel = jnp.min(jnp.where(logits == m, lane, 128.0), axis=-1, keepdims=True)
        vals.append(m)
        idxs.append(sel)
        logits = jnp.where(lane == sel, -jnp.inf, logits)
    es = [jnp.exp(vv - vals[0]) for vv in vals]
    den = es[0] + es[1] + es[2] + es[3]
    chosen = jnp.zeros(lane.shape, F32)
    for kq in range(TOP_K):
        chosen = jnp.where(lane == idxs[kq], 1.0, chosen)
    before = _dot(tri_ref[...], chosen.astype(BF16)) + run_ref[...]
    idx_row = jnp.zeros(lane.shape, F32)
    rank_row = jnp.zeros(lane.shape, F32)
    gate_row = jnp.zeros(lane.shape, F32)
    for kq in range(TOP_K):
        rk = jnp.sum(jnp.where(lane == idxs[kq], before, 0.0), axis=-1, keepdims=True)
        idx_row = jnp.where(lane == float(kq), idxs[kq], idx_row)
        rank_row = jnp.where(lane == float(kq), rk, rank_row)
        gate_row = jnp.where(lane == float(kq), es[kq] / den, gate_row)
    idx_out[...] = idx_row.astype(jnp.int32)
    rank_out[...] = rank_row.astype(jnp.int32)
    gate_out[...] = gate_row
    run_ref[...] += jnp.sum(chosen, axis=0, keepdims=True)
    cnt_out[...] = run_ref[...]


def _route(x2, mod_l, g, wr_pad, br_pad, tri, geom):
    t = x2.shape[0]
    tm = ROW_TILE
    tpb, ctx_tiles, batch = geom
    row = functools.partial(_mod_row, tiles_per_b=tpb, ctx_tiles=ctx_tiles, batch=batch)
    lanes = lambda dt: jax.ShapeDtypeStruct((t, 128), dt)
    tile = pl.BlockSpec((tm, 128), lambda i: (i, 0))
    return pl.pallas_call(
        _route_kernel,
        grid=(t // tm,),
        in_specs=[
            pl.BlockSpec((tm, D_MODEL), lambda i: (i, 0)),
            pl.BlockSpec((None, 6, D_MODEL), lambda i: (row(i), 0, 0)),
            pl.BlockSpec((1, D_MODEL), lambda i: (0, 0)),
            pl.BlockSpec((D_MODEL, 128), lambda i: (0, 0)),
            pl.BlockSpec((1, 128), lambda i: (0, 0)),
            pl.BlockSpec((tm, tm), lambda i: (0, 0)),
        ],
        out_specs=[pl.BlockSpec((tm, D_MODEL), lambda i: (i, 0)), tile, tile, tile,
                   pl.BlockSpec((1, 128), lambda i: (0, 0))],
        out_shape=[jax.ShapeDtypeStruct((t, D_MODEL), F32), lanes(jnp.int32), lanes(jnp.int32),
                   lanes(F32), jax.ShapeDtypeStruct((1, 128), F32)],
        scratch_shapes=[pltpu.VMEM((1, 128), F32)],
        compiler_params=_params("arbitrary"),
        name="moe_router",
    )(x2, mod_l, g.reshape(1, D_MODEL), wr_pad, br_pad, tri)


def _moe_layout(idx, rank, counts, t):
    n_rows = -(-(t * TOP_K + N_EXPERTS * (MOE_TILE - 1)) // MOE_TILE) * MOE_TILE
    n_tiles = n_rows // MOE_TILE
    padded = (counts + MOE_TILE - 1) // MOE_TILE * MOE_TILE
    pad_end = jnp.cumsum(padded)
    pad_start = pad_end - padded
    onehot = idx[..., None] == jnp.arange(N_EXPERTS, dtype=jnp.int32)
    dest = jnp.sum(jnp.where(onehot, pad_start, 0), axis=-1) + rank
    tile_end = pad_end // MOE_TILE
    tile_expert = jnp.sum(jnp.arange(n_tiles, dtype=jnp.int32)[:, None] >= tile_end[None, :], axis=1)
    tile_expert = jnp.minimum(tile_expert, N_EXPERTS - 1).astype(jnp.int32)
    n_used = tile_end[-1:].astype(jnp.int32)
    return dest.astype(jnp.int32).reshape(t // ROW_TILE, 1, ROW_TILE * TOP_K), tile_expert, n_used, n_rows


def _dispatch_kernel(dest_ref, h_ref, xs_in_ref, xs_ref, sem):
    del xs_in_ref
    tm = h_ref.shape[0]

    def issue(tok, carry):
        src = h_ref.at[pl.ds(tok, 1), :]
        for kq in range(TOP_K):
            row = dest_ref[0, tok * TOP_K + kq]
            pltpu.make_async_copy(src, xs_ref.at[pl.ds(row, 1), :], sem.at[0]).start()
        return carry

    lax.fori_loop(0, tm, issue, 0, unroll=4)
    for _ in range(TOP_K):
        pltpu.make_async_copy(h_ref, xs_ref.at[pl.ds(0, tm), :], sem.at[0]).wait()


def _dispatch(dest, h2, xs_prev):
    t = h2.shape[0]
    tm = ROW_TILE
    n_rows = xs_prev.shape[0]
    return pl.pallas_call(
        _dispatch_kernel,
        grid=(t // tm,),
        in_specs=[
            pl.BlockSpec((None, 1, tm * TOP_K), lambda i: (i, 0, 0), memory_space=pltpu.SMEM),
            pl.BlockSpec((tm, D_MODEL), lambda i: (i, 0)),
            pl.BlockSpec(memory_space=pl.ANY),
        ],
        out_specs=pl.BlockSpec(memory_space=pl.ANY),
        out_shape=jax.ShapeDtypeStruct((n_rows, D_MODEL), F32),
        scratch_shapes=[pltpu.SemaphoreType.DMA((1,))],
        input_output_aliases={2: 0},
        compiler_params=_params("arbitrary"),
        name="moe_dispatch",
    )(dest, h2, xs_prev)


def _expert_kernel(te_ref, nu_ref, x_ref, wgu_ref, bgu_ref, wd_ref, bd_ref, o_ref):
    del te_ref
    used = pl.program_id(0) < nu_ref[0]

    @pl.when(jnp.logical_not(used))
    def _():
        o_ref[...] = jnp.zeros_like(o_ref)

    @pl.when(used)
    def _():
        gu = _dot(x_ref[...].astype(BF16), wgu_ref[...]) + bgu_ref[...]
        f = gu.shape[1] // 2
        g_lin = jnp.minimum(gu[:, 0:f], SWIGLU_LIMIT)
        up = jnp.clip(gu[:, f:], -SWIGLU_LIMIT, SWIGLU_LIMIT)
        act = g_lin * _sigmoid(SWIGLU_ALPHA * g_lin) * (up + 1.0)
        o_ref[...] = _dot(act.astype(BF16), wd_ref[...]) + bd_ref[...]


def _experts(xs, tile_expert, n_used, wgu, bgu, wd, bd):
    n_rows = xs.shape[0]
    f2 = wgu.shape[2]
    used = lambda g, nu: jnp.minimum(g, nu[0] - 1)
    grid_spec = pltpu.PrefetchScalarGridSpec(
        num_scalar_prefetch=2,
        grid=(n_rows // MOE_TILE,),
        in_specs=[
            pl.BlockSpec((MOE_TILE, D_MODEL), lambda g, te, nu: (used(g, nu), 0)),
            pl.BlockSpec((None, D_MODEL, f2), lambda g, te, nu: (te[used(g, nu)], 0, 0)),
            pl.BlockSpec((None, 1, f2), lambda g, te, nu: (te[used(g, nu)], 0, 0)),
            pl.BlockSpec((None, f2 // 2, D_MODEL), lambda g, te, nu: (te[used(g, nu)], 0, 0)),
            pl.BlockSpec((None, 1, D_MODEL), lambda g, te, nu: (te[used(g, nu)], 0, 0)),
        ],
        out_specs=pl.BlockSpec((MOE_TILE, D_MODEL), lambda g, te, nu: (g, 0)),
    )
    return pl.pallas_call(
        _expert_kernel,
        grid_spec=grid_spec,
        out_shape=jax.ShapeDtypeStruct((n_rows, D_MODEL), F32),
        compiler_params=_params("arbitrary"),
        name="moe_experts",
    )(tile_expert, n_used, xs, wgu, bgu, wd, bd)


def _collect_kernel(dest_ref, dest_next_ref, gate_ref, x_ref, mod_ref, y_ref, o_ref, buf_ref, sem):
    tm = x_ref.shape[0]
    step = pl.program_id(0)
    slot = step % 2

    def issue_tile(d_ref, to_slot):
        def issue(tok, carry):
            for kq in range(TOP_K):
                row = d_ref[0, tok * TOP_K + kq]
                pltpu.make_async_copy(y_ref.at[pl.ds(row, 1), :],
                                      buf_ref.at[to_slot, kq, pl.ds(tok, 1), :], sem.at[to_slot]).start()
            return carry

        lax.fori_loop(0, tm, issue, 0, unroll=4)

    @pl.when(step == 0)
    def _():
        issue_tile(dest_ref, slot)

    @pl.when(step + 1 < pl.num_programs(0))
    def _():
        issue_tile(dest_next_ref, 1 - slot)

    for kq in range(TOP_K):
        pltpu.make_async_copy(y_ref.at[pl.ds(0, tm), :], buf_ref.at[slot, kq], sem.at[slot]).wait()
    gate = gate_ref[...]
    acc = gate[:, 0:1] * buf_ref[slot, 0]
    for kq in range(1, TOP_K):
        acc = acc + gate[:, kq:kq + 1] * buf_ref[slot, kq]
    o_ref[...] = x_ref[...] + mod_ref[5:6, :] * acc


def _collect(dest, gate_rows, x2, mod_l, ys, geom, latent_only):
    t = x2.shape[0]
    tm = ROW_TILE
    tpb, ctx_tiles, batch = geom
    row = functools.partial(_mod_row, tiles_per_b=tpb, ctx_tiles=ctx_tiles, batch=batch)
    if latent_only:
        lat = tpb - ctx_tiles
        n_steps = batch * lat
        tile = lambda i: (i // lat) * tpb + ctx_tiles + i % lat
    else:
        n_steps = t // tm
        tile = lambda i: i
    nxt = lambda i: tile(jnp.minimum(i + 1, n_steps - 1))
    return pl.pallas_call(
        _collect_kernel,
        grid=(n_steps,),
        in_specs=[
            pl.BlockSpec((None, 1, tm * TOP_K), lambda i: (tile(i), 0, 0), memory_space=pltpu.SMEM),
            pl.BlockSpec((None, 1, tm * TOP_K), lambda i: (nxt(i), 0, 0), memory_space=pltpu.SMEM),
            pl.BlockSpec((tm, 128), lambda i: (tile(i), 0)),
            pl.BlockSpec((tm, D_MODEL), lambda i: (tile(i), 0)),
            pl.BlockSpec((None, 6, D_MODEL), lambda i: (row(tile(i)), 0, 0)),
            pl.BlockSpec(memory_space=pl.ANY),
        ],
        out_specs=pl.BlockSpec((tm, D_MODEL), lambda i: (i, 0)),
        out_shape=jax.ShapeDtypeStruct((n_steps * tm, D_MODEL), F32),
        scratch_shapes=[pltpu.VMEM((2, TOP_K, tm, D_MODEL), F32), pltpu.SemaphoreType.DMA((2,))],
        compiler_params=_params("arbitrary"),
        name="moe_collect",
    )(dest, dest, gate_rows, x2, mod_l, ys)


def _rope_angles(pos, dim):
    inv = ROPE_BASE ** (-jnp.arange(0, dim, 2, dtype=F32) / dim)
    return pos.astype(F32)[:, None] * inv[None, :]


def _rope_tables(seq, ctx_len):
    pos = jnp.arange(seq, dtype=jnp.int32)
    ang = _rope_angles(pos, RET_DK)
    r_cos = jnp.tile(jnp.concatenate([jnp.cos(ang), jnp.cos(ang)], -1), (1, RET_HEADS))
    r_sin = jnp.tile(jnp.concatenate([-jnp.sin(ang), jnp.sin(ang)], -1), (1, RET_HEADS))
    ra = _rope_angles(pos // GRID_W, MLA_ROPE // 2)
    ca = _rope_angles(pos % GRID_W, MLA_ROPE // 2)
    one = jnp.ones((seq, MLA_NOPE), F32)
    zero = jnp.zeros((seq, MLA_NOPE), F32)
    m_cos = jnp.concatenate([one, jnp.cos(ra), jnp.cos(ra), jnp.cos(ca), jnp.cos(ca), one[:, :32]], -1)
    m_sin = jnp.concatenate([zero, -jnp.sin(ra), jnp.sin(ra), -jnp.sin(ca), jnp.sin(ca), zero[:, :32]], -1)
    ctx1 = lambda n: jnp.ones((ctx_len, n), F32)
    ctx0 = lambda n: jnp.zeros((ctx_len, n), F32)
    return (jnp.concatenate([ctx1(256), r_cos], 0), jnp.concatenate([ctx0(256), r_sin], 0),
            jnp.concatenate([ctx1(128), m_cos], 0), jnp.concatenate([ctx0(128), m_sin], 0))


def kernel(x, c, ctx, c_ctx, ada_w, ada_b, norm1_g, norm2_g, w_in, mla_q_norm_g, mla_w_uq, mla_kv_norm_g, mla_w_ukv, mla_qk_q_g, mla_qk_k_g, rwkv_mu_prev, rwkv_mu_next, rwkv_w0, rwkv_w2, rwkv_a0, rwkv_a2, rwkv_g2, rwkv_k_k, rwkv_k_a, rwkv_r_k, rwkv_lnx_g, rwkv_lnx_b, w_br_ret, w_br_mla, w_br_rwkv, w_out, w_router, b_router, w_gu, b_gu, w_down, b_down):
    b, s, d = x.shape
    ctx_len = ctx.shape[1]
    depth = ada_w.shape[0]
    l = ctx_len + s
    t = b * l
    assert d == D_MODEL and b < 16 and ctx_len % ROW_TILE == 0 and s % ROW_TILE == 0
    geom = (l // ROW_TILE, ctx_len // ROW_TILE, b)

    xa = jnp.concatenate([ctx, x], axis=1).reshape(t, d)
    cc = jnp.zeros((16, d), F32).at[:b].set(c).at[b].set(c_ctx)
    mod = _ada_all(cc, ada_w.astype(BF16), ada_b).reshape(depth, 16, 6, d)

    ret_cos, ret_sin, mla_cos, mla_sin = _rope_tables(s, ctx_len)
    ret_consts = _retention_consts()
    rwkv_consts = _rwkv_consts()
    head_blk = jnp.asarray(np.kron(np.eye(RWKV_HEADS), np.ones((RWKV_HD, RWKV_HD))), BF16)
    route_tri = jnp.asarray(np.tril(np.ones((ROW_TILE, ROW_TILE)), -1), BF16)

    xs = None
    for li in range(depth):
        p2 = _inproj(xa, mod[li], norm1_g[li], _pad_w_in(w_in[li]), geom)
        p3 = p2.reshape(b, l, P_COLS)

        retf, retb = (r.reshape(t, RET_W) for r in _retention(p3, ret_cos, ret_sin, ret_consts, ctx_len))

        mw = _mla_weights(mla_q_norm_g[li], mla_w_uq[li], mla_kv_norm_g[li], mla_w_ukv[li],
                          mla_qk_q_g[li], mla_qk_k_g[li])
        q2, k2, v2 = (a.reshape(b, l, MLA_PAD_W) for a in _mla_prep(p2, mw, mla_cos, mla_sin, l))
        mla = _attention(q2, k2, v2, ctx_len).reshape(t, MLA_W)

        rw = dict(mup=rwkv_mu_prev[li].reshape(1, -1), mun=rwkv_mu_next[li].reshape(1, -1),
                  w0=rwkv_w0[li].reshape(2, 1, RWKV_W), w2=rwkv_w2[li].astype(BF16),
                  a0=rwkv_a0[li].reshape(2, 1, RWKV_W), a2=rwkv_a2[li].astype(BF16),
                  g2=rwkv_g2[li].astype(BF16), kk=rwkv_k_k[li].reshape(1, -1),
                  ka=rwkv_k_a[li].reshape(1, -1), rk=rwkv_r_k[li].reshape(1, -1), blk=head_blk)
        r_, v_, na_, kd_, bb_, lw_, g_, bonus_ = _rwkv_elem(p2, rw, l, ctx_len)
        yf, yb = _rwkv_chunk(dict(r=r_, v=v_, na=na_, kd=kd_, bb=bb_, lw=lw_), b, l, ctx_len, rwkv_consts)

        mg = dict(lng=rwkv_lnx_g[li].reshape(1, -1), lnb=rwkv_lnx_b[li].reshape(1, -1), blk=head_blk,
                  wr=w_br_ret[li].astype(BF16), wm=w_br_mla[li].astype(BF16),
                  ww=w_br_rwkv[li].astype(BF16), wo=w_out[li].astype(BF16))
        xa = _merge(xa, mod[li], p2, retf, retb, mla, yf.reshape(t, RWKV_W), yb.reshape(t, RWKV_W), bonus_, g_,
                    mg, geom)

        wr_pad = jnp.pad(w_router[li], ((0, 0), (0, 128 - N_EXPERTS))).astype(BF16)
        br_pad = jnp.concatenate([b_router[li], jnp.full((128 - N_EXPERTS,), -1e30, F32)]).reshape(1, 128)
        h2, idx_rows, rank_rows, gate_rows, cnt = _route(xa, mod[li], norm2_g[li], wr_pad, br_pad,
                                                         route_tri, geom)
        counts = cnt[0, :N_EXPERTS].astype(jnp.int32)
        dest, tile_expert, n_used, n_rows = _moe_layout(idx_rows[:, :TOP_K], rank_rows[:, :TOP_K], counts, t)
        if xs is None:
            xs = jnp.zeros((n_rows, D_MODEL), F32)
        xs = _dispatch(dest, h2, xs)
        ys = _experts(xs, tile_expert, n_used, w_gu[li].astype(BF16), b_gu[li].reshape(N_EXPERTS, 1, -1),
                      w_down[li].astype(BF16), b_down[li].reshape(N_EXPERTS, 1, -1))
        xa = _collect(dest, gate_rows, xa, mod[li], ys, geom, latent_only=li == depth - 1)

    return xa.reshape(b, s, d)
```

```python
import functools

import jax
import jax.numpy as jnp
import numpy as np
from jax import lax
from jax.experimental import pallas as pl
from jax.experimental.pallas import tpu as pltpu

F32 = jnp.float32
BF16 = jnp.bfloat16

D_MODEL = 1024
GRID_W = 64
RET_HEADS, RET_DK, RET_DV, RET_CHUNK = 4, 64, 128, 128
RET_W = RET_HEADS * RET_DV
MLA_HEADS, MLA_Q_RANK, MLA_KV_RANK, MLA_NOPE, MLA_ROPE, MLA_DV = 8, 256, 128, 64, 32, 64
MLA_QK = MLA_NOPE + MLA_ROPE
MLA_W = MLA_HEADS * MLA_DV
MLA_PAD_W = MLA_HEADS * 128
RWKV_HEADS, RWKV_HD = 8, 64
RWKV_W = RWKV_HEADS * RWKV_HD
RWKV_DECAY_LORA, RWKV_A_LORA, RWKV_GATE_LORA = 64, 64, 128
RWKV_GN_EPS = 64e-5
RWKV_IN = 3 * RWKV_W + 2 * RWKV_DECAY_LORA + 2 * RWKV_A_LORA + RWKV_GATE_LORA
N_EXPERTS, TOP_K = 32, 4
SWIGLU_LIMIT, SWIGLU_ALPHA = 7.0, 1.702
ROPE_BASE = 10000.0
NORM_EPS = 1e-6
HEAD_NORM_EPS = 1e-5

GATE_OFF = 0
RET_OFF = 3 * D_MODEL
MLA_OFF = RET_OFF + 2048
RWKV_OFF = MLA_OFF + 512 + 128
P_COLS = RWKV_OFF + RWKV_IN
IN_COL_TILE = 3840

ROW_TILE = 256
RET_ROWS = 1
RWKV_CHUNK = 64
RWKV_ROWS = 4
HALF = 256
ATTN_Q_SCALE = MLA_QK ** -0.5 * 1.4426950408889634
ATTN_HEADS_PER_STEP = 4
MOE_TILE = 256
VMEM_LIMIT = 48 * 1024 * 1024

NT_DIMS = (((1,), (1,)), ((), ()))
TN_DIMS = (((0,), (0,)), ((), ()))


def _params(*sem):
    return pltpu.CompilerParams(dimension_semantics=sem, vmem_limit_bytes=VMEM_LIMIT)


def _dot(a, b):
    return jnp.dot(a, b, preferred_element_type=F32)


def _dot_nt(a, b):
    return lax.dot_general(a, b, NT_DIMS, preferred_element_type=F32)


def _dot_tn(a, b):
    return lax.dot_general(a, b, TN_DIMS, preferred_element_type=F32)


def _split3(x):
    hi = x.astype(BF16)
    r1 = x - hi.astype(F32)
    mid = r1.astype(BF16)
    lo = (r1 - mid.astype(F32)).astype(BF16)
    return hi, mid, lo


def _dot_exact_rhs01(x, m01):
    hi, mid, lo = _split3(x)
    return _dot(hi, m01) + _dot(mid, m01) + _dot(lo, m01)


def _dot_exact_lhs01(m01, x):
    hi, mid, lo = _split3(x)
    return _dot(m01, hi) + _dot(m01, mid) + _dot(m01, lo)


def _sigmoid(x):
    return 1.0 / (1.0 + jnp.exp(-x))


def _silu(x):
    return x * _sigmoid(x)


def _ada_kernel(c_ref, w_ref, b_ref, o_ref):
    s = _silu(c_ref[...])
    o_ref[...] = _dot(s.astype(BF16), w_ref[...]) + b_ref[...]


def _ada_all(cc, ada_w_bf, ada_b):
    depth = ada_w_bf.shape[0]
    tn = 1536
    return pl.pallas_call(
        _ada_kernel,
        grid=(depth, 6 * D_MODEL // tn),
        in_specs=[
            pl.BlockSpec((16, D_MODEL), lambda l, j: (0, 0)),
            pl.BlockSpec((None, D_MODEL, tn), lambda l, j: (l, 0, j)),
            pl.BlockSpec((None, 1, tn), lambda l, j: (l, 0, j)),
        ],
        out_specs=pl.BlockSpec((None, 16, tn), lambda l, j: (l, 0, j)),
        out_shape=jax.ShapeDtypeStruct((depth, 16, 6 * D_MODEL), F32),
        compiler_params=_params("arbitrary", "arbitrary"),
        name="ada_mod",
    )(cc, ada_w_bf, ada_b.reshape(depth, 1, 6 * D_MODEL))


def _mod_row(i, tiles_per_b, ctx_tiles, batch):
    return jnp.where(i % tiles_per_b < ctx_tiles, batch, i // tiles_per_b)


def _inproj_kernel(x_ref, mod_ref, g_ref, w_ref, o_ref):
    x = x_ref[...]
    y = x * lax.rsqrt(jnp.mean(x * x, axis=-1, keepdims=True) + NORM_EPS) * g_ref[...]
    h = y * (1.0 + mod_ref[1:2, :]) + mod_ref[0:1, :]
    o_ref[...] = _dot(h.astype(BF16), w_ref[...])


def _inproj(x2, mod_l, g, w_pad, geom):
    t = x2.shape[0]
    tpb, ctx_tiles, batch = geom
    row = functools.partial(_mod_row, tiles_per_b=tpb, ctx_tiles=ctx_tiles, batch=batch)
    return pl.pallas_call(
        _inproj_kernel,
        grid=(P_COLS // IN_COL_TILE, t // ROW_TILE),
        in_specs=[
            pl.BlockSpec((ROW_TILE, D_MODEL), lambda j, i: (i, 0)),
            pl.BlockSpec((None, 6, D_MODEL), lambda j, i: (row(i), 0, 0)),
            pl.BlockSpec((1, D_MODEL), lambda j, i: (0, 0)),
            pl.BlockSpec((D_MODEL, IN_COL_TILE), lambda j, i: (0, j)),
        ],
        out_specs=pl.BlockSpec((ROW_TILE, IN_COL_TILE), lambda j, i: (i, j)),
        out_shape=jax.ShapeDtypeStruct((t, P_COLS), F32),
        compiler_params=_params("arbitrary", "arbitrary"),
        name="in_proj",
    )(x2, mod_l, g.reshape(1, D_MODEL), w_pad)


def _pad_w_in(w_in):
    d = w_in.shape[0]
    ret = w_in[:, 0:2048]
    cq = w_in[:, 2048:2304]
    ckv = w_in[:, 2304:2432]
    kpe = w_in[:, 2432:2464]
    rwkv = w_in[:, 2464:2464 + RWKV_IN]
    gate = w_in[:, 2464 + RWKV_IN:]
    z = lambda n: jnp.zeros((d, n), w_in.dtype)
    return jnp.concatenate([gate, ret, cq, ckv, z(64), kpe, z(32), z(128), rwkv], axis=1).astype(BF16)


def _scan_chunk(d, j, n_ctx, n_all):
    bwd = jnp.where(j < n_ctx, n_ctx - 1 - j, n_all - 1 - j + n_ctx)
    return jnp.where(d == 0, j, bwd)


def _ret_pair_kernel(qf_ref, kf_ref, vf_ref, gf_ref, cosf_ref, sinf_ref,
                     qb_ref, kb_ref, vb_ref, gb_ref, cosb_ref, sinb_ref,
                     dmat_ref, qdec_ref, kdec_ref, of_ref, ob_ref, s_ref, *, cdec):
    @pl.when(pl.program_id(1) == 0)
    def _():
        s_ref[...] = jnp.zeros_like(s_ref)

    c = RET_CHUNK
    w = RET_HEADS * RET_DK
    first = (lax.broadcasted_iota(jnp.int32, (c, w), 1) & (RET_DK - 1)) < RET_DK // 2

    def rot(x, cos, sin):
        partner = jnp.where(first, pltpu.roll(x, w - RET_DK // 2, 1), pltpu.roll(x, RET_DK // 2, 1))
        return x * cos + partner * sin

    dirs = ((qf_ref, kf_ref, vf_ref, gf_ref, cosf_ref, sinf_ref, of_ref),
            (qb_ref, kb_ref, vb_ref, gb_ref, cosb_ref, sinb_ref, ob_ref))
    inst = []
    for d, (q_ref, k_ref, v_ref, g_ref, cos_ref, sin_ref, o_ref) in enumerate(dirs):
        cos, sin = cos_ref[...], sin_ref[...]
        for bi in range(RET_ROWS):
            q = rot(q_ref[bi], cos, sin).astype(BF16)
            k = rot(k_ref[bi], cos, sin) * (RET_DK ** -0.5)
            kd_t = (k * kdec_ref[d]).T.astype(BF16)
            kb = k.astype(BF16)
            vb = v_ref[bi].astype(BF16)
            for h in range(RET_HEADS):
                ksl = slice(h * RET_DK, (h + 1) * RET_DK)
                vsl = slice(h * RET_DV, (h + 1) * RET_DV)
                inst.append(dict(d=d, bi=bi, h=h, vsl=vsl, q=q[:, ksl], k=kb[:, ksl], v=vb[:, vsl],
                                 kd=kd_t[ksl, :], g_ref=g_ref, o_ref=o_ref))
    s_old = [s_ref[s["bi"], s["d"], s["h"]] for s in inst]
    atts = [_dot_nt(s["q"], s["k"]) * dmat_ref[s["d"], s["h"]] for s in inst]
    inter = [_dot(s["q"], so.astype(BF16)) * qdec_ref[s["d"], s["h"]] for s, so in zip(inst, s_old)]
    kvs = [_dot(s["kd"], s["v"]) for s in inst]
    outs = [_dot(att.astype(BF16), s["v"]) + it for s, att, it in zip(inst, atts, inter)]
    for s, so, kv, o in zip(inst, s_old, kvs, outs):
        s_ref[s["bi"], s["d"], s["h"]] = so * cdec[s["h"]] + kv
        oc = o - jnp.mean(o, axis=-1, keepdims=True)
        y = oc * lax.rsqrt(jnp.mean(oc * oc, axis=-1, keepdims=True) + HEAD_NORM_EPS)
        s["o_ref"][s["bi"], :, s["vsl"]] = y * _silu(s["g_ref"][s["bi"], :, s["vsl"]])


def _retention(p3, cos_t, sin_t, consts, ctx_len):
    b, l, _ = p3.shape
    n_all = l // RET_CHUNK
    n_ctx = ctx_len // RET_CHUNK
    dmat, qdec, kdec, cdec = consts
    ch = functools.partial(_scan_chunk, n_ctx=n_ctx, n_all=n_all)
    qb, vb = RET_OFF // 256, RET_OFF // 512

    rows = RET_ROWS
    assert b % rows == 0

    def specs(d):
        return [
            pl.BlockSpec((rows, RET_CHUNK, 256), lambda bi, j: (bi, ch(d, j), qb)),
            pl.BlockSpec((rows, RET_CHUNK, 256), lambda bi, j: (bi, ch(d, j), qb + 1)),
            pl.BlockSpec((rows, RET_CHUNK, 512), lambda bi, j: (bi, ch(d, j), vb + 1)),
            pl.BlockSpec((rows, RET_CHUNK, 512), lambda bi, j: (bi, ch(d, j), vb + 2 + d)),
            pl.BlockSpec((RET_CHUNK, 256), lambda bi, j: (ch(d, j), 0)),
            pl.BlockSpec((RET_CHUNK, 256), lambda bi, j: (ch(d, j), 0)),
        ]

    const = lambda shape: pl.BlockSpec(shape, lambda bi, j: (0,) * len(shape))
    out = jax.ShapeDtypeStruct((b, l, RET_W), F32)
    out_spec = lambda d: pl.BlockSpec((rows, RET_CHUNK, RET_W), lambda bi, j: (bi, ch(d, j), 0))
    return pl.pallas_call(
        functools.partial(_ret_pair_kernel, cdec=cdec),
        grid=(b // rows, n_all),
        in_specs=specs(0) + specs(1) + [
            const((2, RET_HEADS, RET_CHUNK, RET_CHUNK)), const((2, RET_HEADS, RET_CHUNK, RET_DV)),
            const((2, RET_CHUNK, 256))],
        out_specs=[out_spec(0), out_spec(1)],
        out_shape=[out, out],
        scratch_shapes=[pltpu.VMEM((rows, 2, RET_HEADS, RET_DK, RET_DV), F32)],
        compiler_params=_params("arbitrary", "arbitrary"),
        name="retention",
    )(p3, p3, p3, p3, cos_t, sin_t, p3, p3, p3, p3, cos_t, sin_t, dmat, qdec, kdec)


def _retention_consts():
    gamma = 1.0 - jnp.exp2(-5.0 - jnp.arange(RET_HEADS, dtype=F32))
    lg = jnp.log(gamma)
    idx = jnp.arange(RET_CHUNK, dtype=F32)
    diff = idx[:, None] - idx[None, :]
    lower = jnp.where(diff >= 0, jnp.exp(lg[:, None, None] * jnp.maximum(diff, 0.0)), 0.0)
    dmat = jnp.stack([lower, jnp.swapaxes(lower, 1, 2)])
    qd_f = jnp.exp((idx + 1.0)[None, :] * lg[:, None])
    qd_b = jnp.exp((RET_CHUNK - idx)[None, :] * lg[:, None])
    qdec = jnp.broadcast_to(jnp.stack([qd_f, qd_b])[..., None], (2, RET_HEADS, RET_CHUNK, RET_DV))
    kd_f = jnp.exp((RET_CHUNK - 1.0 - idx)[None, :] * lg[:, None])
    kd_b = jnp.exp(idx[None, :] * lg[:, None])
    kdec = jnp.repeat(jnp.swapaxes(jnp.stack([kd_f, kd_b]), 1, 2), RET_DK, axis=2)
    gam64 = 1.0 - np.exp2(-5.0 - np.arange(RET_HEADS))
    cdec = tuple(float(np.float32(np.exp(RET_CHUNK * np.log(np.float32(gv))))) for gv in gam64)
    return dmat, qdec, kdec, cdec


def _mla_prep_kernel(p_ref, qng_ref, wuq_ref, kvg_ref, wuk_ref, wuv_ref, qg_ref, kg_ref,
                     cos_ref, sin_ref, q_out, k_out, v_out):
    blk = p_ref[...]
    cq = blk[:, 0:MLA_Q_RANK]
    ckv = blk[:, MLA_Q_RANK:MLA_Q_RANK + MLA_KV_RANK]
    kpe = blk[:, MLA_Q_RANK + MLA_KV_RANK:]

    def rms(x, g, n):
        return x * lax.rsqrt(jnp.sum(x * x, axis=-1, keepdims=True) * (1.0 / n) + NORM_EPS) * g

    qn = rms(cq, qng_ref[...], MLA_Q_RANK).astype(BF16)
    kvn = rms(ckv, kvg_ref[...], MLA_KV_RANK).astype(BF16)
    q_raw = _dot(qn, wuq_ref[...])
    k_raw = _dot(kvn, wuk_ref[...])
    ones_col = (lax.broadcasted_iota(jnp.int32, (1, MLA_PAD_W), 1) & 127) == MLA_DV
    v_out[...] = (_dot(kvn, wuv_ref[...]) + jnp.where(ones_col, 1.0, 0.0)).astype(BF16)
    cos = cos_ref[...]
    sin = sin_ref[...]
    tm = blk.shape[0]
    first = (lax.broadcasted_iota(jnp.int32, (tm, 128), 1) & 15) < 8

    def rope(x):
        partner = jnp.where(first, pltpu.roll(x, 120, 1), pltpu.roll(x, 8, 1))
        return x * cos + partner * sin

    heads = [slice(h * 128, (h + 1) * 128) for h in range(MLA_HEADS)]
    qs = [rms(q_raw[:, sl], qg_ref[...], MLA_QK) for sl in heads]
    ks = [rms(k_raw[:, sl] + kpe, kg_ref[...], MLA_QK) for sl in heads]
    for sl, qh, kh in zip(heads, qs, ks):
        q_out[:, sl] = (rope(qh) * ATTN_Q_SCALE).astype(BF16)
        k_out[:, sl] = rope(kh).astype(BF16)


def _mla_prep(p2, w, cos_t, sin_t, rows_per_b):
    t = p2.shape[0]
    tiles_per_b = rows_per_b // ROW_TILE
    const = lambda shape: pl.BlockSpec(shape, lambda i: (0,) * len(shape))
    wide = pl.BlockSpec((ROW_TILE, MLA_PAD_W), lambda i: (i, 0))
    out = jax.ShapeDtypeStruct((t, MLA_PAD_W), BF16)
    return pl.pallas_call(
        _mla_prep_kernel,
        grid=(t // ROW_TILE,),
        in_specs=[
            pl.BlockSpec((ROW_TILE, 512), lambda i: (i, MLA_OFF // 512)),
            const((1, MLA_Q_RANK)), const((MLA_Q_RANK, MLA_PAD_W)),
            const((1, MLA_KV_RANK)), const((MLA_KV_RANK, MLA_PAD_W)), const((MLA_KV_RANK, MLA_PAD_W)),
            const((1, 128)), const((1, 128)),
            pl.BlockSpec((ROW_TILE, 128), lambda i: (i % tiles_per_b, 0)),
            pl.BlockSpec((ROW_TILE, 128), lambda i: (i % tiles_per_b, 0)),
        ],
        out_specs=[wide, wide, wide],
        out_shape=[out, out, out],
        compiler_params=_params("arbitrary"),
        name="mla_prep",
    )(p2, w["qng"], w["wuq"], w["kvg"], w["wuk"], w["wuv"], w["qg"], w["kg"], cos_t, sin_t)


def _mla_weights(q_norm_g, w_uq, kv_norm_g, w_ukv, qk_q_g, qk_k_g):
    wq = w_uq.reshape(MLA_Q_RANK, MLA_HEADS, MLA_QK)
    wq = jnp.pad(wq, ((0, 0), (0, 0), (0, 128 - MLA_QK))).reshape(MLA_Q_RANK, MLA_PAD_W)
    wkv = w_ukv.reshape(MLA_KV_RANK, MLA_HEADS, MLA_NOPE + MLA_DV)
    wk = jnp.pad(wkv[:, :, :MLA_NOPE], ((0, 0), (0, 0), (0, 128 - MLA_NOPE))).reshape(MLA_KV_RANK, MLA_PAD_W)
    wv = jnp.pad(wkv[:, :, MLA_NOPE:], ((0, 0), (0, 0), (0, 128 - MLA_DV))).reshape(MLA_KV_RANK, MLA_PAD_W)
    padg = lambda g: jnp.pad(g, (0, 128 - MLA_QK)).reshape(1, 128)
    return dict(qng=q_norm_g.reshape(1, -1), wuq=wq.astype(BF16), kvg=kv_norm_g.reshape(1, -1),
                wuk=wk.astype(BF16), wuv=wv.astype(BF16), qg=padg(qk_q_g), kg=padg(qk_k_g))


def _attn_kernel(q_ref, k_ref, v_ref, o_ref, *, ctx_len, ctx_tiles):
    def attend(kv_len):
        hs = [slice(hh * 128, (hh + 1) * 128) for hh in range(ATTN_HEADS_PER_STEP)]
        ss = [_dot_nt(q_ref[:, sl], k_ref[0:kv_len, sl]) for sl in hs]
        ps = [jnp.exp2((s - jnp.max(s, axis=-1, keepdims=True)).astype(BF16)) for s in ss]
        o_aug = [_dot(p, v_ref[0:kv_len, sl]) for p, sl in zip(ps, hs)]
        outs = [o[:, 0:MLA_DV] / o[:, MLA_DV:MLA_DV + 1] for o in o_aug]
        o_ref[...] = jnp.concatenate(outs, axis=-1).astype(BF16)

    is_ctx = pl.program_id(2) < ctx_tiles

    @pl.when(is_ctx)
    def _():
        attend(ctx_len)

    @pl.when(jnp.logical_not(is_ctx))
    def _():
        attend(k_ref.shape[0])


def _attention(q3, k3, v3, ctx_len):
    b, l, _ = q3.shape
    tq = ROW_TILE
    hps = ATTN_HEADS_PER_STEP
    return pl.pallas_call(
        functools.partial(_attn_kernel, ctx_len=ctx_len, ctx_tiles=ctx_len // tq),
        grid=(b, MLA_HEADS // hps, l // tq),
        in_specs=[
            pl.BlockSpec((None, tq, hps * 128), lambda bi, h, i: (bi, i, h)),
            pl.BlockSpec((None, l, hps * 128), lambda bi, h, i: (bi, 0, h), pipeline_mode=pl.Buffered(1)),
            pl.BlockSpec((None, l, hps * 128), lambda bi, h, i: (bi, 0, h), pipeline_mode=pl.Buffered(1)),
        ],
        out_specs=pl.BlockSpec((None, tq, hps * MLA_DV), lambda bi, h, i: (bi, i, h)),
        out_shape=jax.ShapeDtypeStruct((b, l, MLA_W), BF16),
        compiler_params=_params("arbitrary", "arbitrary", "arbitrary"),
        name="mla_attention",
    )(q3, k3, v3)


def _rwkv_elem_kernel(t_ref, prev_ref, next_ref, mup_ref, mun_ref, w0_ref, w2_ref, a0_ref, a2_ref,
                      g2_ref, kk_ref, ka_ref, rk_ref, blk_ref,
                      r_out, v_out, na_out, kd_out, bb_out, lw_out, g_out, bonus_out,
                      *, tiles_per_b, ctx_tiles):
    i = pl.program_id(0) % tiles_per_b
    starts = jnp.logical_or(i == 0, i == ctx_tiles)
    ends = jnp.logical_or(i == ctx_tiles - 1, i == tiles_per_b - 1)
    t = t_ref[...]
    tm = t.shape[0]
    prev_row = jnp.where(starts, 0.0, prev_ref[7:8, :])
    next_row = jnp.where(ends, 0.0, next_ref[0:1, :])
    rid = lax.broadcasted_iota(jnp.int32, t.shape, 0)
    prev = jnp.where(rid == 0, prev_row, pltpu.roll(t, 1, 0))
    nxt = jnp.where(rid == tm - 1, next_row, pltpu.roll(t, tm - 1, 0))
    t = t + mup_ref[...] * (prev - t) + mun_ref[...] * (nxt - t)

    w = RWKV_W
    r = t[:, 0:w]
    k = t[:, w:2 * w]
    v = t[:, 2 * w:3 * w]
    o = 3 * w
    wh = (t[:, o:o + 64], t[:, o + 64:o + 128])
    ah = (t[:, o + 128:o + 192], t[:, o + 192:o + 256])
    gh = t[:, o + 256:o + 384]
    blk = blk_ref[...]

    kk = k * kk_ref[...]
    kk = kk * lax.rsqrt(jnp.maximum(_dot_exact_rhs01(kk * kk, blk), 1e-24))
    r_out[...] = r
    v_out[...] = v
    na_out[...] = -kk
    g_out[...] = _dot(_sigmoid(gh).astype(BF16), g2_ref[...])
    rk = r * rk_ref[...]
    bonus = jnp.zeros_like(v)
    for d in range(2):
        z = w0_ref[d] + _dot(jnp.tanh(wh[d]).astype(BF16), w2_ref[d])
        nz = -z
        softplus = jnp.maximum(nz, 0.0) + jnp.log(1.0 + jnp.exp(-jnp.abs(nz)))
        lw_out[d] = -jnp.exp(-softplus - 0.5)
        a = _sigmoid(a0_ref[d] + _dot(ah[d].astype(BF16), a2_ref[d]))
        kd = k * (1.0 + (a - 1.0) * ka_ref[...])
        kd_out[d] = kd
        bb_out[d] = kk * a
        bonus = bonus + _dot_exact_rhs01(rk * kd, blk) * v
    bonus_out[...] = bonus


def _rwkv_elem(p2, w, rows_per_b, ctx_len):
    t = p2.shape[0]
    tm = ROW_TILE
    tiles_per_b = rows_per_b // tm
    n8 = t // 8
    cb = RWKV_OFF // RWKV_IN
    const = lambda shape: pl.BlockSpec(shape, lambda i: (0,) * len(shape))
    one = pl.BlockSpec((tm, RWKV_W), lambda i: (i, 0))
    two = pl.BlockSpec((2, tm, RWKV_W), lambda i: (0, i, 0))
    s1 = jax.ShapeDtypeStruct((t, RWKV_W), F32)
    s2 = jax.ShapeDtypeStruct((2, t, RWKV_W), F32)
    return pl.pallas_call(
        functools.partial(_rwkv_elem_kernel, tiles_per_b=tiles_per_b, ctx_tiles=ctx_len // tm),
        grid=(t // tm,),
        in_specs=[
            pl.BlockSpec((tm, RWKV_IN), lambda i: (i, cb)),
            pl.BlockSpec((8, RWKV_IN), lambda i: (jnp.maximum(i * (tm // 8) - 1, 0), cb)),
            pl.BlockSpec((8, RWKV_IN), lambda i: (jnp.minimum((i + 1) * (tm // 8), n8 - 1), cb)),
            const((1, RWKV_IN)), const((1, RWKV_IN)),
            const((2, 1, RWKV_W)), const((2, RWKV_DECAY_LORA, RWKV_W)),
            const((2, 1, RWKV_W)), const((2, RWKV_A_LORA, RWKV_W)),
            const((RWKV_GATE_LORA, RWKV_W)),
            const((1, RWKV_W)), const((1, RWKV_W)), const((1, RWKV_W)),
            const((RWKV_W, RWKV_W)),
        ],
        out_specs=[one, one, one, two, two, two, one, one],
        out_shape=[s1, s1, s1, s2, s2, s2, s1, s1],
        compiler_params=_params("arbitrary"),
        name="rwkv_elem",
    )(p2, p2, p2, w["mup"], w["mun"], w["w0"], w["w2"], w["a0"], w["a2"], w["g2"],
      w["kk"], w["ka"], w["rk"], w["blk"])


def _block_rows(x, bm):
    return jnp.where(bm, jnp.concatenate([x, x, x, x], axis=0), 0.0)


def _rwkv_chunk_kernel(rf_ref, vf_ref, naf_ref, kdf_ref, bbf_ref, lwf_ref,
                       rb_ref, vb_ref, nab_ref, kdb_ref, bbb_ref, lwb_ref,
                       tri_ref, ms_ref, mi_ref, of_ref, ob_ref, h_ref):
    @pl.when(pl.program_id(1) == 0)
    def _():
        h_ref[...] = jnp.zeros_like(h_ref)

    dirs = ((rf_ref, vf_ref, naf_ref, kdf_ref, bbf_ref, lwf_ref, of_ref),
            (rb_ref, vb_ref, nab_ref, kdb_ref, bbb_ref, lwb_ref, ob_ref))
    c = RWKV_CHUNK
    assert c == 64 and RWKV_HD == 64
    bm = (lax.broadcasted_iota(jnp.int32, (HALF, HALF), 0) >> 6
          == lax.broadcasted_iota(jnp.int32, (HALF, HALF), 1) >> 6)
    eye_p = ((lax.broadcasted_iota(jnp.int32, (c, HALF), 1) & (c - 1))
             == lax.broadcasted_iota(jnp.int32, (c, HALF), 0))
    eye_s = (lax.broadcasted_iota(jnp.int32, (HALF, HALF), 0)
             == lax.broadcasted_iota(jnp.int32, (HALF, HALF), 1))
    bf = lambda x: x.astype(BF16)
    blk = lambda x: _block_rows(x, bm)
    cat0 = lambda xs: jnp.concatenate(xs, axis=0)
    cat1 = lambda xs: jnp.concatenate(xs, axis=1)

    chains = [(d, bi) + dirs[d] for d in range(2) for bi in range(RWKV_ROWS)]
    cums = [_dot_exact_lhs01(tri_ref[d], lw_ref[bi]) for d, bi, _, _, _, _, _, lw_ref, _ in chains]
    inst = []
    for (d, bi, r_ref, v_ref, na_ref, kd_ref, bb_ref, lw_ref, o_ref), cum in zip(chains, cums):
        lw = lw_ref[bi]
        tot = cum[c - 1:c, :] if d == 0 else cum[0:1, :]
        e_neg = jnp.exp(-cum)
        e_rel = jnp.exp(tot - cum)
        e_tot = jnp.exp(tot)
        a_t = na_ref[bi] * jnp.exp(cum - lw)
        r_t = r_ref[bi] * jnp.exp(cum)
        kd, bb, v = kd_ref[bi], bb_ref[bi], v_ref[bi]
        for g in range(2):
            sl = slice(g * HALF, (g + 1) * HALF)
            inst.append(dict(
                d=d, bi=bi, g=g, sl=sl, o_ref=o_ref, a=a_t[:, sl], r=bf(r_t[:, sl]), v=v[:, sl],
                bt=bb[:, sl] * e_neg[:, sl], kt=kd[:, sl] * e_neg[:, sl],
                bk_p=bf(cat0([bb[:, sl] * e_rel[:, sl], kd[:, sl] * e_rel[:, sl]])), etot=e_tot[:, sl]))
    strict = [ms_ref[0] > 0.5, ms_ref[1] > 0.5]
    incl = [mi_ref[0] > 0.5, mi_ref[1] > 0.5]

    bigs = [_dot_nt(bf(cat0([s["a"], s["r"].astype(F32)])), bf(cat0([blk(s["bt"]), blk(s["kt"])])))
            for s in inst]
    for s, big in zip(inst, bigs):
        s["a_ab"] = jnp.where(strict[s["d"]], big[0:c, 0:HALF], 0.0)
        s["a_ak"] = bf(jnp.where(strict[s["d"]], big[0:c, HALF:], 0.0))
        s["a_rb"] = bf(jnp.where(incl[s["d"]], big[c:, 0:HALF], 0.0))
        s["a_rk"] = bf(jnp.where(incl[s["d"]], big[c:, HALF:], 0.0))
        s["v_bd"] = bf(blk(s["v"]))
        s["t"] = jnp.where(eye_p, 1.0, 0.0) + s["a_ab"]
    pws = [_dot(bf(s["a_ab"]), bf(blk(s["a_ab"]))) for s in inst]
    akvs = [_dot(s["a_ak"], s["v_bd"]) for s in inst]
    arkv = [_dot(s["a_rk"], s["v_bd"]) for s in inst]
    for it in range(5):
        prods = [_dot(bf(cat0([s["t"], pw])), bf(blk(pw))) for s, pw in zip(inst, pws)]
        for s, prod in zip(inst, prods):
            s["t"] = s["t"] + prod[0:c]
        pws = [prod[c:] for prod in prods]
    wus = [_dot(bf(s["t"]), bf(cat1([blk(s["a"]), blk(akv)]))) for s, akv in zip(inst, akvs)]
    h_f32 = [h_ref[s["bi"], s["d"], s["g"]] for s in inst]
    h_old = [bf(h) for h in h_f32]
    us = [_dot(bf(wu[:, 0:HALF]), h) + wu[:, HALF:] for wu, h in zip(wus, h_old)]
    rhs = [_dot(s["r"], h) for s, h in zip(inst, h_old)]
    arbu = [_dot(s["a_rb"], bf(blk(u))) for s, u in zip(inst, us)]
    incs = [_dot_tn(s["bk_p"], bf(cat0([u, s["v"]]))) for s, u in zip(inst, us)]
    for s, rh, au, ak, inc, h in zip(inst, rhs, arbu, arkv, incs, h_f32):
        s["o_ref"][s["bi"], :, s["sl"]] = rh + au + ak
        decay_col = jnp.sum(jnp.where(eye_s, s["etot"], 0.0), axis=1, keepdims=True)
        h_ref[s["bi"], s["d"], s["g"]] = decay_col * h + jnp.where(bm, inc, 0.0)


def _rwkv_chunk(e, b, l, ctx_len, consts):
    c = RWKV_CHUNK
    rows = RWKV_ROWS
    assert b % rows == 0
    n_all, n_ctx = l // c, ctx_len // c
    ch = functools.partial(_scan_chunk, n_ctx=n_ctx, n_all=n_all)
    tri, ms, mi = consts
    r3, v3, na3 = (x.reshape(b, l, RWKV_W) for x in (e["r"], e["v"], e["na"]))
    kd4, bb4, lw4 = (x.reshape(2, b, l, RWKV_W) for x in (e["kd"], e["bb"], e["lw"]))

    def specs(d):
        one = pl.BlockSpec((rows, c, RWKV_W), lambda bi, j: (bi, ch(d, j), 0))
        two = pl.BlockSpec((None, rows, c, RWKV_W), lambda bi, j: (d, bi, ch(d, j), 0))
        return one, two

    one_f, two_f = specs(0)
    one_b, two_b = specs(1)
    const = lambda shape: pl.BlockSpec(shape, lambda bi, j: (0,) * len(shape))
    out = jax.ShapeDtypeStruct((b, l, RWKV_W), F32)
    return pl.pallas_call(
        _rwkv_chunk_kernel,
        grid=(b // rows, n_all),
        in_specs=[one_f, one_f, one_f, two_f, two_f, two_f,
                  one_b, one_b, one_b, two_b, two_b, two_b,
                  const((2, c, c)), const((2, c, HALF)), const((2, c, HALF))],
        out_specs=[one_f, one_b],
        out_shape=[out, out],
        scratch_shapes=[pltpu.VMEM((rows, 2, 2, HALF, HALF), F32)],
        compiler_params=_params("arbitrary", "arbitrary"),
        name="rwkv_chunk",
    )(r3, v3, na3, kd4, bb4, lw4, r3, v3, na3, kd4, bb4, lw4, tri, ms, mi)


def _rwkv_consts():
    c = RWKV_CHUNK
    idx = np.arange(c)
    lower = idx[None, :] <= idx[:, None]
    tri = np.stack([lower, lower.T]).astype(np.float32)
    s_in_head = np.tile(idx, HALF // c)[None, :]
    t_row = idx[:, None]
    ms = np.stack([s_in_head < t_row, s_in_head > t_row]).astype(np.float32)
    mi = np.stack([s_in_head <= t_row, s_in_head >= t_row]).astype(np.float32)
    return jnp.asarray(tri, BF16), jnp.asarray(ms), jnp.asarray(mi)


def _merge_kernel(x_ref, mod_ref, gate_ref, retf_ref, retb_ref, mla_ref, yf_ref, yb_ref, bonus_ref, g_ref,
                  lng_ref, lnb_ref, blk_ref, wr_ref, wm_ref, ww_ref, wo_ref, o_ref):
    blk = blk_ref[...]
    ret = retf_ref[...] + retb_ref[...]
    y = yf_ref[...] + yb_ref[...]
    mean = _dot_exact_rhs01(y, blk) * (1.0 / RWKV_HD)
    yc = y - mean
    var = _dot_exact_rhs01(yc * yc, blk) * (1.0 / RWKV_HD)
    y = yc * lax.rsqrt(var + RWKV_GN_EPS) * lng_ref[...] + lnb_ref[...] + bonus_ref[...]
    rwk = y * g_ref[...]
    gate = gate_ref[...]
    d = D_MODEL
    mix = (_sigmoid(gate[:, 0:d]) * _dot(ret.astype(BF16), wr_ref[...])
           + _sigmoid(gate[:, d:2 * d]) * _dot(mla_ref[...], wm_ref[...])
           + _sigmoid(gate[:, 2 * d:]) * _dot(rwk.astype(BF16), ww_ref[...]))
    o_ref[...] = x_ref[...] + mod_ref[2:3, :] * _dot(mix.astype(BF16), wo_ref[...])


def _merge(x2, mod_l, p2, retf, retb, mla, yf, yb, bonus, g, w, geom):
    t = x2.shape[0]
    tm = ROW_TILE
    tpb, ctx_tiles, batch = geom
    row = functools.partial(_mod_row, tiles_per_b=tpb, ctx_tiles=ctx_tiles, batch=batch)
    const = lambda shape: pl.BlockSpec(shape, lambda i: (0,) * len(shape))
    rows = lambda width: pl.BlockSpec((tm, width), lambda i: (i, 0))
    return pl.pallas_call(
        _merge_kernel,
        grid=(t // tm,),
        in_specs=[
            rows(D_MODEL),
            pl.BlockSpec((None, 6, D_MODEL), lambda i: (row(i), 0, 0)),
            rows(3 * D_MODEL),
            rows(RET_W), rows(RET_W), rows(MLA_W), rows(RWKV_W), rows(RWKV_W), rows(RWKV_W), rows(RWKV_W),
            const((1, RWKV_W)), const((1, RWKV_W)), const((RWKV_W, RWKV_W)),
            const((RET_W, D_MODEL)), const((MLA_W, D_MODEL)), const((RWKV_W, D_MODEL)),
            const((D_MODEL, D_MODEL)),
        ],
        out_specs=rows(D_MODEL),
        out_shape=jax.ShapeDtypeStruct((t, D_MODEL), F32),
        compiler_params=_params("arbitrary"),
        name="merge",
    )(x2, mod_l, p2, retf, retb, mla, yf, yb, bonus, g, w["lng"], w["lnb"], w["blk"],
      w["wr"], w["wm"], w["ww"], w["wo"])


def _route_kernel(x_ref, mod_ref, g_ref, wr_ref, br_ref, tri_ref,
                  h_out, idx_out, rank_out, gate_out, cnt_out, run_ref):
    @pl.when(pl.program_id(0) == 0)
    def _():
        run_ref[...] = jnp.zeros_like(run_ref)

    x = x_ref[...]
    y = x * lax.rsqrt(jnp.mean(x * x, axis=-1, keepdims=True) + NORM_EPS) * g_ref[...]
    h = y * (1.0 + mod_ref[4:5, :]) + mod_ref[3:4, :]
    h_out[...] = h
    logits = _dot(h.astype(BF16), wr_ref[...]) + br_ref[...]
    lane = lax.broadcasted_iota(jnp.int32, logits.shape, 1).astype(F32)
    vals, idxs = [], []
    for _ in range(TOP_K):
        m = jnp.max(logits, axis=-1, keepdims=True)
        sel = jnp.min(jnp.where(logits == m, lane, 128.0), axis=-1, keepdims=True)
        vals.append(m)
        idxs.append(sel)
        logits = jnp.where(lane == sel, -jnp.inf, logits)
    es = [jnp.exp(vv - vals[0]) for vv in vals]
    den = es[0] + es[1] + es[2] + es[3]
    chosen = jnp.zeros(lane.shape, F32)
    for kq in range(TOP_K):
        chosen = jnp.where(lane == idxs[kq], 1.0, chosen)
    before = _dot(tri_ref[...], chosen.astype(BF16)) + run_ref[...]
    idx_row = jnp.zeros(lane.shape, F32)
    rank_row = jnp.zeros(lane.shape, F32)
    gate_row = jnp.zeros(lane.shape, F32)
    for kq in range(TOP_K):
        rk = jnp.sum(jnp.where(lane == idxs[kq], before, 0.0), axis=-1, keepdims=True)
        idx_row = jnp.where(lane == float(kq), idxs[kq], idx_row)
        rank_row = jnp.where(lane == float(kq), rk, rank_row)
        gate_row = jnp.where(lane == float(kq), es[kq] / den, gate_row)
    idx_out[...] = idx_row.astype(jnp.int32)
    rank_out[...] = rank_row.astype(jnp.int32)
    gate_out[...] = gate_row
    run_ref[...] += jnp.sum(chosen, axis=0, keepdims=True)
    cnt_out[...] = run_ref[...]


def _route(x2, mod_l, g, wr_pad, br_pad, tri, geom):
    t = x2.shape[0]
    tm = ROW_TILE
    tpb, ctx_tiles, batch = geom
    row = functools.partial(_mod_row, tiles_per_b=tpb, ctx_tiles=ctx_tiles, batch=batch)
    lanes = lambda dt: jax.ShapeDtypeStruct((t, 128), dt)
    tile = pl.BlockSpec((tm, 128), lambda i: (i, 0))
    return pl.pallas_call(
        _route_kernel,
        grid=(t // tm,),
        in_specs=[
            pl.BlockSpec((tm, D_MODEL), lambda i: (i, 0)),
            pl.BlockSpec((None, 6, D_MODEL), lambda i: (row(i), 0, 0)),
            pl.BlockSpec((1, D_MODEL), lambda i: (0, 0)),
            pl.BlockSpec((D_MODEL, 128), lambda i: (0, 0)),
            pl.BlockSpec((1, 128), lambda i: (0, 0)),
            pl.BlockSpec((tm, tm), lambda i: (0, 0)),
        ],
        out_specs=[pl.BlockSpec((tm, D_MODEL), lambda i: (i, 0)), tile, tile, tile,
                   pl.BlockSpec((1, 128), lambda i: (0, 0))],
        out_shape=[jax.ShapeDtypeStruct((t, D_MODEL), F32), lanes(jnp.int32), lanes(jnp.int32),
                   lanes(F32), jax.ShapeDtypeStruct((1, 128), F32)],
        scratch_shapes=[pltpu.VMEM((1, 128), F32)],
        compiler_params=_params("arbitrary"),
        name="moe_router",
    )(x2, mod_l, g.reshape(1, D_MODEL), wr_pad, br_pad, tri)


def _moe_layout(idx, rank, counts, t):
    n_rows = -(-(t * TOP_K + N_EXPERTS * (MOE_TILE - 1)) // MOE_TILE) * MOE_TILE
    n_tiles = n_rows // MOE_TILE
    padded = (counts + MOE_TILE - 1) // MOE_TILE * MOE_TILE
    pad_end = jnp.cumsum(padded)
    pad_start = pad_end - padded
    onehot = idx[..., None] == jnp.arange(N_EXPERTS, dtype=jnp.int32)
    dest = jnp.sum(jnp.where(onehot, pad_start, 0), axis=-1) + rank
    tile_end = pad_end // MOE_TILE
    tile_expert = jnp.sum(jnp.arange(n_tiles, dtype=jnp.int32)[:, None] >= tile_end[None, :], axis=1)
    tile_expert = jnp.minimum(tile_expert, N_EXPERTS - 1).astype(jnp.int32)
    n_used = tile_end[-1:].astype(jnp.int32)
    return dest.astype(jnp.int32).reshape(t // ROW_TILE, 1, ROW_TILE * TOP_K), tile_expert, n_used, n_rows


def _dispatch_kernel(dest_ref, h_ref, xs_in_ref, xs_ref, sem):
    del xs_in_ref
    tm = h_ref.shape[0]

    def issue(tok, carry):
        src = h_ref.at[pl.ds(tok, 1), :]
        for kq in range(TOP_K):
            row = dest_ref[0, tok * TOP_K + kq]
            pltpu.make_async_copy(src, xs_ref.at[pl.ds(row, 1), :], sem.at[0]).start()
        return carry

    lax.fori_loop(0, tm, issue, 0, unroll=4)
    for _ in range(TOP_K):
        pltpu.make_async_copy(h_ref, xs_ref.at[pl.ds(0, tm), :], sem.at[0]).wait()


def _dispatch(dest, h2, xs_prev):
    t = h2.shape[0]
    tm = ROW_TILE
    n_rows = xs_prev.shape[0]
    return pl.pallas_call(
        _dispatch_kernel,
        grid=(t // tm,),
        in_specs=[
            pl.BlockSpec((None, 1, tm * TOP_K), lambda i: (i, 0, 0), memory_space=pltpu.SMEM),
            pl.BlockSpec((tm, D_MODEL), lambda i: (i, 0)),
            pl.BlockSpec(memory_space=pl.ANY),
        ],
        out_specs=pl.BlockSpec(memory_space=pl.ANY),
        out_shape=jax.ShapeDtypeStruct((n_rows, D_MODEL), F32),
        scratch_shapes=[pltpu.SemaphoreType.DMA((1,))],
        input_output_aliases={2: 0},
        compiler_params=_params("arbitrary"),
        name="moe_dispatch",
    )(dest, h2, xs_prev)


def _expert_kernel(te_ref, nu_ref, x_ref, wgu_ref, bgu_ref, wd_ref, bd_ref, o_ref):
    del te_ref
    used = pl.program_id(0) < nu_ref[0]

    @pl.when(jnp.logical_not(used))
    def _():
        o_ref[...] = jnp.zeros_like(o_ref)

    @pl.when(used)
    def _():
        gu = _dot(x_ref[...].astype(BF16), wgu_ref[...]) + bgu_ref[...]
        f = gu.shape[1] // 2
        g_lin = jnp.minimum(gu[:, 0:f], SWIGLU_LIMIT)
        up = jnp.clip(gu[:, f:], -SWIGLU_LIMIT, SWIGLU_LIMIT)
        act = g_lin * _sigmoid(SWIGLU_ALPHA * g_lin) * (up + 1.0)
        o_ref[...] = _dot(act.astype(BF16), wd_ref[...]) + bd_ref[...]


def _experts(xs, tile_expert, n_used, wgu, bgu, wd, bd):
    n_rows = xs.shape[0]
    f2 = wgu.shape[2]
    used = lambda g, nu: jnp.minimum(g, nu[0] - 1)
    grid_spec = pltpu.PrefetchScalarGridSpec(
        num_scalar_prefetch=2,
        grid=(n_rows // MOE_TILE,),
        in_specs=[
            pl.BlockSpec((MOE_TILE, D_MODEL), lambda g, te, nu: (used(g, nu), 0)),
            pl.BlockSpec((None, D_MODEL, f2), lambda g, te, nu: (te[used(g, nu)], 0, 0)),
            pl.BlockSpec((None, 1, f2), lambda g, te, nu: (te[used(g, nu)], 0, 0)),
            pl.BlockSpec((None, f2 // 2, D_MODEL), lambda g, te, nu: (te[used(g, nu)], 0, 0)),
            pl.BlockSpec((None, 1, D_MODEL), lambda g, te, nu: (te[used(g, nu)], 0, 0)),
        ],
        out_specs=pl.BlockSpec((MOE_TILE, D_MODEL), lambda g, te, nu: (g, 0)),
    )
    return pl.pallas_call(
        _expert_kernel,
        grid_spec=grid_spec,
        out_shape=jax.ShapeDtypeStruct((n_rows, D_MODEL), F32),
        compiler_params=_params("arbitrary"),
        name="moe_experts",
    )(tile_expert, n_used, xs, wgu, bgu, wd, bd)


def _collect_kernel(dest_ref, dest_next_ref, gate_ref, x_ref, mod_ref, y_ref, o_ref, buf_ref, sem):
    tm = x_ref.shape[0]
    step = pl.program_id(0)
    slot = step % 2

    def issue_tile(d_ref, to_slot):
        def issue(tok, carry):
            for kq in range(TOP_K):
                row = d_ref[0, tok * TOP_K + kq]
                pltpu.make_async_copy(y_ref.at[pl.ds(row, 1), :],
                                      buf_ref.at[to_slot, kq, pl.ds(tok, 1), :], sem.at[to_slot]).start()
            return carry

        lax.fori_loop(0, tm, issue, 0, unroll=4)

    @pl.when(step == 0)
    def _():
        issue_tile(dest_ref, slot)

    @pl.when(step + 1 < pl.num_programs(0))
    def _():
        issue_tile(dest_next_ref, 1 - slot)

    for kq in range(TOP_K):
        pltpu.make_async_copy(y_ref.at[pl.ds(0, tm), :], buf_ref.at[slot, kq], sem.at[slot]).wait()
    gate = gate_ref[...]
    acc = gate[:, 0:1] * buf_ref[slot, 0]
    for kq in range(1, TOP_K):
        acc = acc + gate[:, kq:kq + 1] * buf_ref[slot, kq]
    o_ref[...] = x_ref[...] + mod_ref[5:6, :] * acc


def _collect(dest, gate_rows, x2, mod_l, ys, geom, latent_only):
    t = x2.shape[0]
    tm = ROW_TILE
    tpb, ctx_tiles, batch = geom
    row = functools.partial(_mod_row, tiles_per_b=tpb, ctx_tiles=ctx_tiles, batch=batch)
    if latent_only:
        lat = tpb - ctx_tiles
        n_steps = batch * lat
        tile = lambda i: (i // lat) * tpb + ctx_tiles + i % lat
    else:
        n_steps = t // tm
        tile = lambda i: i
    nxt = lambda i: tile(jnp.minimum(i + 1, n_steps - 1))
    return pl.pallas_call(
        _collect_kernel,
        grid=(n_steps,),
        in_specs=[
            pl.BlockSpec((None, 1, tm * TOP_K), lambda i: (tile(i), 0, 0), memory_space=pltpu.SMEM),
            pl.BlockSpec((None, 1, tm * TOP_K), lambda i: (nxt(i), 0, 0), memory_space=pltpu.SMEM),
            pl.BlockSpec((tm, 128), lambda i: (tile(i), 0)),
            pl.BlockSpec((tm, D_MODEL), lambda i: (tile(i), 0)),
            pl.BlockSpec((None, 6, D_MODEL), lambda i: (row(tile(i)), 0, 0)),
            pl.BlockSpec(memory_space=pl.ANY),
        ],
        out_specs=pl.BlockSpec((tm, D_MODEL), lambda i: (i, 0)),
        out_shape=jax.ShapeDtypeStruct((n_steps * tm, D_MODEL), F32),
        scratch_shapes=[pltpu.VMEM((2, TOP_K, tm, D_MODEL), F32), pltpu.SemaphoreType.DMA((2,))],
        compiler_params=_params("arbitrary"),
        name="moe_collect",
    )(dest, dest, gate_rows, x2, mod_l, ys)


def _rope_angles(pos, dim):
    inv = ROPE_BASE ** (-jnp.arange(0, dim, 2, dtype=F32) / dim)
    return pos.astype(F32)[:, None] * inv[None, :]


def _rope_tables(seq, ctx_len):
    pos = jnp.arange(seq, dtype=jnp.int32)
    ang = _rope_angles(pos, RET_DK)
    r_cos = jnp.tile(jnp.concatenate([jnp.cos(ang), jnp.cos(ang)], -1), (1, RET_HEADS))
    r_sin = jnp.tile(jnp.concatenate([-jnp.sin(ang), jnp.sin(ang)], -1), (1, RET_HEADS))
    ra = _rope_angles(pos // GRID_W, MLA_ROPE // 2)
    ca = _rope_angles(pos % GRID_W, MLA_ROPE // 2)
    one = jnp.ones((seq, MLA_NOPE), F32)
    zero = jnp.zeros((seq, MLA_NOPE), F32)
    m_cos = jnp.concatenate([one, jnp.cos(ra), jnp.cos(ra), jnp.cos(ca), jnp.cos(ca), one[:, :32]], -1)
    m_sin = jnp.concatenate([zero, -jnp.sin(ra), jnp.sin(ra), -jnp.sin(ca), jnp.sin(ca), zero[:, :32]], -1)
    ctx1 = lambda n: jnp.ones((ctx_len, n), F32)
    ctx0 = lambda n: jnp.zeros((ctx_len, n), F32)
    return (jnp.concatenate([ctx1(256), r_cos], 0), jnp.concatenate([ctx0(256), r_sin], 0),
            jnp.concatenate([ctx1(128), m_cos], 0), jnp.concatenate([ctx0(128), m_sin], 0))


def kernel(x, c, ctx, c_ctx, ada_w, ada_b, norm1_g, norm2_g, w_in, mla_q_norm_g, mla_w_uq, mla_kv_norm_g, mla_w_ukv, mla_qk_q_g, mla_qk_k_g, rwkv_mu_prev, rwkv_mu_next, rwkv_w0, rwkv_w2, rwkv_a0, rwkv_a2, rwkv_g2, rwkv_k_k, rwkv_k_a, rwkv_r_k, rwkv_lnx_g, rwkv_lnx_b, w_br_ret, w_br_mla, w_br_rwkv, w_out, w_router, b_router, w_gu, b_gu, w_down, b_down):
    b, s, d = x.shape
    ctx_len = ctx.shape[1]
    depth = ada_w.shape[0]
    l = ctx_len + s
    t = b * l
    assert d == D_MODEL and b < 16 and ctx_len % ROW_TILE == 0 and s % ROW_TILE == 0
    geom = (l // ROW_TILE, ctx_len // ROW_TILE, b)

    xa = jnp.concatenate([ctx, x], axis=1).reshape(t, d)
    cc = jnp.zeros((16, d), F32).at[:b].set(c).at[b].set(c_ctx)
    mod = _ada_all(cc, ada_w.astype(BF16), ada_b).reshape(depth, 16, 6, d)

    ret_cos, ret_sin, mla_cos, mla_sin = _rope_tables(s, ctx_len)
    ret_consts = _retention_consts()
    rwkv_consts = _rwkv_consts()
    head_blk = jnp.asarray(np.kron(np.eye(RWKV_HEADS), np.ones((RWKV_HD, RWKV_HD))), BF16)
    route_tri = jnp.asarray(np.tril(np.ones((ROW_TILE, ROW_TILE)), -1), BF16)

    xs = None
    for li in range(depth):
        p2 = _inproj(xa, mod[li], norm1_g[li], _pad_w_in(w_in[li]), geom)
        p3 = p2.reshape(b, l, P_COLS)

        retf, retb = (r.reshape(t, RET_W) for r in _retention(p3, ret_cos, ret_sin, ret_consts, ctx_len))

        mw = _mla_weights(mla_q_norm_g[li], mla_w_uq[li], mla_kv_norm_g[li], mla_w_ukv[li],
                          mla_qk_q_g[li], mla_qk_k_g[li])
        q2, k2, v2 = (a.reshape(b, l, MLA_PAD_W) for a in _mla_prep(p2, mw, mla_cos, mla_sin, l))
        mla = _attention(q2, k2, v2, ctx_len).reshape(t, MLA_W)

        rw = dict(mup=rwkv_mu_prev[li].reshape(1, -1), mun=rwkv_mu_next[li].reshape(1, -1),
                  w0=rwkv_w0[li].reshape(2, 1, RWKV_W), w2=rwkv_w2[li].astype(BF16),
                  a0=rwkv_a0[li].reshape(2, 1, RWKV_W), a2=rwkv_a2[li].astype(BF16),
                  g2=rwkv_g2[li].astype(BF16), kk=rwkv_k_k[li].reshape(1, -1),
                  ka=rwkv_k_a[li].reshape(1, -1), rk=rwkv_r_k[li].reshape(1, -1), blk=head_blk)
        r_, v_, na_, kd_, bb_, lw_, g_, bonus_ = _rwkv_elem(p2, rw, l, ctx_len)
        yf, yb = _rwkv_chunk(dict(r=r_, v=v_, na=na_, kd=kd_, bb=bb_, lw=lw_), b, l, ctx_len, rwkv_consts)

        mg = dict(lng=rwkv_lnx_g[li].reshape(1, -1), lnb=rwkv_lnx_b[li].reshape(1, -1), blk=head_blk,
                  wr=w_br_ret[li].astype(BF16), wm=w_br_mla[li].astype(BF16),
                  ww=w_br_rwkv[li].astype(BF16), wo=w_out[li].astype(BF16))
        xa = _merge(xa, mod[li], p2, retf, retb, mla, yf.reshape(t, RWKV_W), yb.reshape(t, RWKV_W), bonus_, g_,
                    mg, geom)

        wr_pad = jnp.pad(w_router[li], ((0, 0), (0, 128 - N_EXPERTS))).astype(BF16)
        br_pad = jnp.concatenate([b_router[li], jnp.full((128 - N_EXPERTS,), -1e30, F32)]).reshape(1, 128)
        h2, idx_rows, rank_rows, gate_rows, cnt = _route(xa, mod[li], norm2_g[li], wr_pad, br_pad,
                                                         route_tri, geom)
        counts = cnt[0, :N_EXPERTS].astype(jnp.int32)
        dest, tile_expert, n_used, n_rows = _moe_layout(idx_rows[:, :TOP_K], rank_rows[:, :TOP_K], counts, t)
        if xs is None:
            xs = jnp.zeros((n_rows, D_MODEL), F32)
        xs = _dispatch(dest, h2, xs)
        ys = _experts(xs, tile_expert, n_used, w_gu[li].astype(BF16), b_gu[li].reshape(N_EXPERTS, 1, -1),
                      w_down[li].astype(BF16), b_down[li].reshape(N_EXPERTS, 1, -1))
        xa = _collect(dest, gate_rows, xa, mod[li], ys, geom, latent_only=li == depth - 1)

    return xa.reshape(b, s, d)
```

```python
import functools

import jax
import jax.numpy as jnp
import numpy as np
from jax import lax
from jax.experimental import pallas as pl
from jax.experimental.pallas import tpu as pltpu

F32 = jnp.float32
BF16 = jnp.bfloat16

D_MODEL = 1024
GRID_W = 64
RET_HEADS, RET_DK, RET_DV, RET_CHUNK = 4, 64, 128, 128
RET_W = RET_HEADS * RET_DV
MLA_HEADS, MLA_Q_RANK, MLA_KV_RANK, MLA_NOPE, MLA_ROPE, MLA_DV = 8, 256, 128, 64, 32, 64
MLA_QK = MLA_NOPE + MLA_ROPE
MLA_W = MLA_HEADS * MLA_DV
MLA_PAD_W = MLA_HEADS * 128
RWKV_HEADS, RWKV_HD = 8, 64
RWKV_W = RWKV_HEADS * RWKV_HD
RWKV_DECAY_LORA, RWKV_A_LORA, RWKV_GATE_LORA = 64, 64, 128
RWKV_GN_EPS = 64e-5
RWKV_IN = 3 * RWKV_W + 2 * RWKV_DECAY_LORA + 2 * RWKV_A_LORA + RWKV_GATE_LORA
N_EXPERTS, TOP_K = 32, 4
SWIGLU_LIMIT, SWIGLU_ALPHA = 7.0, 1.702
ROPE_BASE = 10000.0
NORM_EPS = 1e-6
HEAD_NORM_EPS = 1e-5

GATE_OFF = 0
RET_OFF = 3 * D_MODEL
MLA_OFF = RET_OFF + 2048
RWKV_OFF = MLA_OFF + 512 + 128
P_COLS = RWKV_OFF + RWKV_IN
IN_COL_TILE = 3840

ROW_TILE = 256
RET_ROWS = 1
RWKV_CHUNK = 64
RWKV_ROWS = 4
HALF = 256
ATTN_Q_SCALE = MLA_QK ** -0.5 * 1.4426950408889634
ATTN_HEADS_PER_STEP = 4
MOE_TILE = 256
VMEM_LIMIT = 48 * 1024 * 1024

NT_DIMS = (((1,), (1,)), ((), ()))
TN_DIMS = (((0,), (0,)), ((), ()))


def _params(*sem):
    return pltpu.CompilerParams(dimension_semantics=sem, vmem_limit_bytes=VMEM_LIMIT)


def _dot(a, b):
    return jnp.dot(a, b, preferred_element_type=F32)


def _dot_nt(a, b):
    return lax.dot_general(a, b, NT_DIMS, preferred_element_type=F32)


def _dot_tn(a, b):
    return lax.dot_general(a, b, TN_DIMS, preferred_element_type=F32)


def _split3(x):
    hi = x.astype(BF16)
    r1 = x - hi.astype(F32)
    mid = r1.astype(BF16)
    lo = (r1 - mid.astype(F32)).astype(BF16)
    return hi, mid, lo


def _dot_exact_rhs01(x, m01):
    hi = x.astype(BF16)
    lo = (x - hi.astype(F32)).astype(BF16)
    return _dot(hi, m01) + _dot(lo, m01)


def _dot_exact_lhs01(m01, x):
    hi, mid, lo = _split3(x)
    return _dot(m01, hi) + _dot(m01, mid) + _dot(m01, lo)


def _sigmoid(x):
    return 1.0 / (1.0 + jnp.exp(-x))


def _silu(x):
    return x * _sigmoid(x)


def _ada_kernel(c_ref, w_ref, b_ref, o_ref):
    s = _silu(c_ref[...])
    o_ref[...] = _dot(s.astype(BF16), w_ref[...]) + b_ref[...]


def _ada_all(cc, ada_w_bf, ada_b):
    depth = ada_w_bf.shape[0]
    tn = 1536
    return pl.pallas_call(
        _ada_kernel,
        grid=(depth, 6 * D_MODEL // tn),
        in_specs=[
            pl.BlockSpec((16, D_MODEL), lambda l, j: (0, 0)),
            pl.BlockSpec((None, D_MODEL, tn), lambda l, j: (l, 0, j)),
            pl.BlockSpec((None, 1, tn), lambda l, j: (l, 0, j)),
        ],
        out_specs=pl.BlockSpec((None, 16, tn), lambda l, j: (l, 0, j)),
        out_shape=jax.ShapeDtypeStruct((depth, 16, 6 * D_MODEL), F32),
        compiler_params=_params("arbitrary", "arbitrary"),
        name="ada_mod",
    )(cc, ada_w_bf, ada_b.reshape(depth, 1, 6 * D_MODEL))


def _mod_row(i, tiles_per_b, ctx_tiles, batch):
    return jnp.where(i % tiles_per_b < ctx_tiles, batch, i // tiles_per_b)


def _inproj_kernel(x_ref, mod_ref, g_ref, w_ref, o_ref):
    x = x_ref[...]
    y = x * lax.rsqrt(jnp.mean(x * x, axis=-1, keepdims=True) + NORM_EPS) * g_ref[...]
    h = y * (1.0 + mod_ref[1:2, :]) + mod_ref[0:1, :]
    o_ref[...] = _dot(h.astype(BF16), w_ref[...])


def _inproj(x2, mod_l, g, w_pad, geom):
    t = x2.shape[0]
    tpb, ctx_tiles, batch = geom
    row = functools.partial(_mod_row, tiles_per_b=tpb, ctx_tiles=ctx_tiles, batch=batch)
    return pl.pallas_call(
        _inproj_kernel,
        grid=(P_COLS // IN_COL_TILE, t // ROW_TILE),
        in_specs=[
            pl.BlockSpec((ROW_TILE, D_MODEL), lambda j, i: (i, 0)),
            pl.BlockSpec((None, 6, D_MODEL), lambda j, i: (row(i), 0, 0)),
            pl.BlockSpec((1, D_MODEL), lambda j, i: (0, 0)),
            pl.BlockSpec((D_MODEL, IN_COL_TILE), lambda j, i: (0, j)),
        ],
        out_specs=pl.BlockSpec((ROW_TILE, IN_COL_TILE), lambda j, i: (i, j)),
        out_shape=jax.ShapeDtypeStruct((t, P_COLS), F32),
        compiler_params=_params("arbitrary", "arbitrary"),
        name="in_proj",
    )(x2, mod_l, g.reshape(1, D_MODEL), w_pad)


def _pad_w_in(w_in):
    d = w_in.shape[0]
    ret = w_in[:, 0:2048]
    cq = w_in[:, 2048:2304]
    ckv = w_in[:, 2304:2432]
    kpe = w_in[:, 2432:2464]
    rwkv = w_in[:, 2464:2464 + RWKV_IN]
    gate = w_in[:, 2464 + RWKV_IN:]
    z = lambda n: jnp.zeros((d, n), w_in.dtype)
    return jnp.concatenate([gate, ret, cq, ckv, z(64), kpe, z(32), z(128), rwkv], axis=1).astype(BF16)


def _scan_chunk(d, j, n_ctx, n_all):
    bwd = jnp.where(j < n_ctx, n_ctx - 1 - j, n_all - 1 - j + n_ctx)
    return jnp.where(d == 0, j, bwd)


def _ret_pair_kernel(qf_ref, kf_ref, vf_ref, gf_ref, cosf_ref, sinf_ref,
                     qb_ref, kb_ref, vb_ref, gb_ref, cosb_ref, sinb_ref,
                     dmat_ref, qdec_ref, kdec_ref, of_ref, ob_ref, s_ref, *, cdec):
    @pl.when(pl.program_id(1) == 0)
    def _():
        s_ref[...] = jnp.zeros_like(s_ref)

    c = RET_CHUNK
    w = RET_HEADS * RET_DK
    first = (lax.broadcasted_iota(jnp.int32, (c, w), 1) & (RET_DK - 1)) < RET_DK // 2

    def rot(x, cos, sin):
        partner = jnp.where(first, pltpu.roll(x, w - RET_DK // 2, 1), pltpu.roll(x, RET_DK // 2, 1))
        return x * cos + partner * sin

    dirs = ((qf_ref, kf_ref, vf_ref, gf_ref, cosf_ref, sinf_ref, of_ref),
            (qb_ref, kb_ref, vb_ref, gb_ref, cosb_ref, sinb_ref, ob_ref))
    inst = []
    for d, (q_ref, k_ref, v_ref, g_ref, cos_ref, sin_ref, o_ref) in enumerate(dirs):
        cos, sin = cos_ref[...], sin_ref[...]
        for bi in range(RET_ROWS):
            q = rot(q_ref[bi], cos, sin).astype(BF16)
            k = rot(k_ref[bi], cos, sin) * (RET_DK ** -0.5)
            kd_t = (k * kdec_ref[d]).T.astype(BF16)
            kb = k.astype(BF16)
            vb = v_ref[bi].astype(BF16)
            for h in range(RET_HEADS):
                ksl = slice(h * RET_DK, (h + 1) * RET_DK)
                vsl = slice(h * RET_DV, (h + 1) * RET_DV)
                inst.append(dict(d=d, bi=bi, h=h, vsl=vsl, q=q[:, ksl], k=kb[:, ksl], v=vb[:, vsl],
                                 kd=kd_t[ksl, :], g_ref=g_ref, o_ref=o_ref))
    s_old = [s_ref[s["bi"], s["d"], s["h"]] for s in inst]
    atts = [_dot_nt(s["q"], s["k"]) * dmat_ref[s["d"], s["h"]] for s in inst]
    inter = [_dot(s["q"], so.astype(BF16)) * qdec_ref[s["d"], s["h"]] for s, so in zip(inst, s_old)]
    kvs = [_dot(s["kd"], s["v"]) for s in inst]
    outs = [_dot(att.astype(BF16), s["v"]) + it for s, att, it in zip(inst, atts, inter)]
    for s, so, kv, o in zip(inst, s_old, kvs, outs):
        s_ref[s["bi"], s["d"], s["h"]] = so * cdec[s["h"]] + kv
        oc = o - jnp.mean(o, axis=-1, keepdims=True)
        y = oc * lax.rsqrt(jnp.mean(oc * oc, axis=-1, keepdims=True) + HEAD_NORM_EPS)
        s["o_ref"][s["bi"], :, s["vsl"]] = y * _silu(s["g_ref"][s["bi"], :, s["vsl"]])


def _retention(p3, cos_t, sin_t, consts, ctx_len):
    b, l, _ = p3.shape
    n_all = l // RET_CHUNK
    n_ctx = ctx_len // RET_CHUNK
    dmat, qdec, kdec, cdec = consts
    ch = functools.partial(_scan_chunk, n_ctx=n_ctx, n_all=n_all)
    qb, vb = RET_OFF // 256, RET_OFF // 512

    rows = RET_ROWS
    assert b % rows == 0

    def specs(d):
        return [
            pl.BlockSpec((rows, RET_CHUNK, 256), lambda bi, j: (bi, ch(d, j), qb)),
            pl.BlockSpec((rows, RET_CHUNK, 256), lambda bi, j: (bi, ch(d, j), qb + 1)),
            pl.BlockSpec((rows, RET_CHUNK, 512), lambda bi, j: (bi, ch(d, j), vb + 1)),
            pl.BlockSpec((rows, RET_CHUNK, 512), lambda bi, j: (bi, ch(d, j), vb + 2 + d)),
            pl.BlockSpec((RET_CHUNK, 256), lambda bi, j: (ch(d, j), 0)),
            pl.BlockSpec((RET_CHUNK, 256), lambda bi, j: (ch(d, j), 0)),
        ]

    const = lambda shape: pl.BlockSpec(shape, lambda bi, j: (0,) * len(shape))
    out = jax.ShapeDtypeStruct((b, l, RET_W), F32)
    out_spec = lambda d: pl.BlockSpec((rows, RET_CHUNK, RET_W), lambda bi, j: (bi, ch(d, j), 0))
    return pl.pallas_call(
        functools.partial(_ret_pair_kernel, cdec=cdec),
        grid=(b // rows, n_all),
        in_specs=specs(0) + specs(1) + [
            const((2, RET_HEADS, RET_CHUNK, RET_CHUNK)), const((2, RET_HEADS, RET_CHUNK, RET_DV)),
            const((2, RET_CHUNK, 256))],
        out_specs=[out_spec(0), out_spec(1)],
        out_shape=[out, out],
        scratch_shapes=[pltpu.VMEM((rows, 2, RET_HEADS, RET_DK, RET_DV), F32)],
        compiler_params=_params("arbitrary", "arbitrary"),
        name="retention",
    )(p3, p3, p3, p3, cos_t, sin_t, p3, p3, p3, p3, cos_t, sin_t, dmat, qdec, kdec)


def _retention_consts():
    gamma = 1.0 - jnp.exp2(-5.0 - jnp.arange(RET_HEADS, dtype=F32))
    lg = jnp.log(gamma)
    idx = jnp.arange(RET_CHUNK, dtype=F32)
    diff = idx[:, None] - idx[None, :]
    lower = jnp.where(diff >= 0, jnp.exp(lg[:, None, None] * jnp.maximum(diff, 0.0)), 0.0)
    dmat = jnp.stack([lower, jnp.swapaxes(lower, 1, 2)])
    qd_f = jnp.exp((idx + 1.0)[None, :] * lg[:, None])
    qd_b = jnp.exp((RET_CHUNK - idx)[None, :] * lg[:, None])
    qdec = jnp.broadcast_to(jnp.stack([qd_f, qd_b])[..., None], (2, RET_HEADS, RET_CHUNK, RET_DV))
    kd_f = jnp.exp((RET_CHUNK - 1.0 - idx)[None, :] * lg[:, None])
    kd_b = jnp.exp(idx[None, :] * lg[:, None])
    kdec = jnp.repeat(jnp.swapaxes(jnp.stack([kd_f, kd_b]), 1, 2), RET_DK, axis=2)
    gam64 = 1.0 - np.exp2(-5.0 - np.arange(RET_HEADS))
    cdec = tuple(float(np.float32(np.exp(RET_CHUNK * np.log(np.float32(gv))))) for gv in gam64)
    return dmat, qdec, kdec, cdec


def _mla_prep_kernel(p_ref, qng_ref, wuq_ref, kvg_ref, wuk_ref, wuv_ref, qg_ref, kg_ref,
                     cos_ref, sin_ref, q_out, k_out, v_out):
    blk = p_ref[...]
    cq = blk[:, 0:MLA_Q_RANK]
    ckv = blk[:, MLA_Q_RANK:MLA_Q_RANK + MLA_KV_RANK]
    kpe = blk[:, MLA_Q_RANK + MLA_KV_RANK:]

    def rms(x, g, n):
        return x * lax.rsqrt(jnp.sum(x * x, axis=-1, keepdims=True) * (1.0 / n) + NORM_EPS) * g

    qn = rms(cq, qng_ref[...], MLA_Q_RANK).astype(BF16)
    kvn = rms(ckv, kvg_ref[...], MLA_KV_RANK).astype(BF16)
    q_raw = _dot(qn, wuq_ref[...])
    k_raw = _dot(kvn, wuk_ref[...])
    ones_col = (lax.broadcasted_iota(jnp.int32, (1, MLA_PAD_W), 1) & 127) == MLA_DV
    v_out[...] = (_dot(kvn, wuv_ref[...]) + jnp.where(ones_col, 1.0, 0.0)).astype(BF16)
    cos = cos_ref[...]
    sin = sin_ref[...]
    tm = blk.shape[0]
    first = (lax.broadcasted_iota(jnp.int32, (tm, 128), 1) & 15) < 8

    def rope(x):
        partner = jnp.where(first, pltpu.roll(x, 120, 1), pltpu.roll(x, 8, 1))
        return x * cos + partner * sin

    heads = [slice(h * 128, (h + 1) * 128) for h in range(MLA_HEADS)]
    qs = [rms(q_raw[:, sl], qg_ref[...], MLA_QK) for sl in heads]
    ks = [rms(k_raw[:, sl] + kpe, kg_ref[...], MLA_QK) for sl in heads]
    for sl, qh, kh in zip(heads, qs, ks):
        q_out[:, sl] = (rope(qh) * ATTN_Q_SCALE).astype(BF16)
        k_out[:, sl] = rope(kh).astype(BF16)


def _mla_prep(p2, w, cos_t, sin_t, rows_per_b):
    t = p2.shape[0]
    tiles_per_b = rows_per_b // ROW_TILE
    const = lambda shape: pl.BlockSpec(shape, lambda i: (0,) * len(shape))
    wide = pl.BlockSpec((ROW_TILE, MLA_PAD_W), lambda i: (i, 0))
    out = jax.ShapeDtypeStruct((t, MLA_PAD_W), BF16)
    return pl.pallas_call(
        _mla_prep_kernel,
        grid=(t // ROW_TILE,),
        in_specs=[
            pl.BlockSpec((ROW_TILE, 512), lambda i: (i, MLA_OFF // 512)),
            const((1, MLA_Q_RANK)), const((MLA_Q_RANK, MLA_PAD_W)),
            const((1, MLA_KV_RANK)), const((MLA_KV_RANK, MLA_PAD_W)), const((MLA_KV_RANK, MLA_PAD_W)),
            const((1, 128)), const((1, 128)),
            pl.BlockSpec((ROW_TILE, 128), lambda i: (i % tiles_per_b, 0)),
            pl.BlockSpec((ROW_TILE, 128), lambda i: (i % tiles_per_b, 0)),
        ],
        out_specs=[wide, wide, wide],
        out_shape=[out, out, out],
        compiler_params=_params("arbitrary"),
        name="mla_prep",
    )(p2, w["qng"], w["wuq"], w["kvg"], w["wuk"], w["wuv"], w["qg"], w["kg"], cos_t, sin_t)


def _mla_weights(q_norm_g, w_uq, kv_norm_g, w_ukv, qk_q_g, qk_k_g):
    wq = w_uq.reshape(MLA_Q_RANK, MLA_HEADS, MLA_QK)
    wq = jnp.pad(wq, ((0, 0), (0, 0), (0, 128 - MLA_QK))).reshape(MLA_Q_RANK, MLA_PAD_W)
    wkv = w_ukv.reshape(MLA_KV_RANK, MLA_HEADS, MLA_NOPE + MLA_DV)
    wk = jnp.pad(wkv[:, :, :MLA_NOPE], ((0, 0), (0, 0), (0, 128 - MLA_NOPE))).reshape(MLA_KV_RANK, MLA_PAD_W)
    wv = jnp.pad(wkv[:, :, MLA_NOPE:], ((0, 0), (0, 0), (0, 128 - MLA_DV))).reshape(MLA_KV_RANK, MLA_PAD_W)
    padg = lambda g: jnp.pad(g, (0, 128 - MLA_QK)).reshape(1, 128)
    return dict(qng=q_norm_g.reshape(1, -1), wuq=wq.astype(BF16), kvg=kv_norm_g.reshape(1, -1),
                wuk=wk.astype(BF16), wuv=wv.astype(BF16), qg=padg(qk_q_g), kg=padg(qk_k_g))


def _attn_kernel(q_ref, k_ref, v_ref, o_ref, *, ctx_len, ctx_tiles):
    def attend(kv_len):
        hs = [slice(hh * 128, (hh + 1) * 128) for hh in range(ATTN_HEADS_PER_STEP)]
        ss = [_dot_nt(q_ref[:, sl], k_ref[0:kv_len, sl]) for sl in hs]
        ps = [jnp.exp2((s - jnp.max(s, axis=-1, keepdims=True)).astype(BF16)) for s in ss]
        o_aug = [_dot(p, v_ref[0:kv_len, sl]) for p, sl in zip(ps, hs)]
        outs = [o[:, 0:MLA_DV] / o[:, MLA_DV:MLA_DV + 1] for o in o_aug]
        o_ref[...] = jnp.concatenate(outs, axis=-1).astype(BF16)

    is_ctx = pl.program_id(2) < ctx_tiles

    @pl.when(is_ctx)
    def _():
        attend(ctx_len)

    @pl.when(jnp.logical_not(is_ctx))
    def _():
        attend(k_ref.shape[0])


def _attention(q3, k3, v3, ctx_len):
    b, l, _ = q3.shape
    tq = ROW_TILE
    hps = ATTN_HEADS_PER_STEP
    return pl.pallas_call(
        functools.partial(_attn_kernel, ctx_len=ctx_len, ctx_tiles=ctx_len // tq),
        grid=(b, MLA_HEADS // hps, l // tq),
        in_specs=[
            pl.BlockSpec((None, tq, hps * 128), lambda bi, h, i: (bi, i, h)),
            pl.BlockSpec((None, l, hps * 128), lambda bi, h, i: (bi, 0, h), pipeline_mode=pl.Buffered(1)),
            pl.BlockSpec((None, l, hps * 128), lambda bi, h, i: (bi, 0, h), pipeline_mode=pl.Buffered(1)),
        ],
        out_specs=pl.BlockSpec((None, tq, hps * MLA_DV), lambda bi, h, i: (bi, i, h)),
        out_shape=jax.ShapeDtypeStruct((b, l, MLA_W), BF16),
        compiler_params=_params("arbitrary", "arbitrary", "arbitrary"),
        name="mla_attention",
    )(q3, k3, v3)


def _rwkv_elem_kernel(t_ref, prev_ref, next_ref, mup_ref, mun_ref, w0_ref, w2_ref, a0_ref, a2_ref,
                      g2_ref, kk_ref, ka_ref, rk_ref, blk_ref,
                      r_out, v_out, na_out, kd_out, bb_out, lw_out, g_out, bonus_out,
                      *, tiles_per_b, ctx_tiles):
    i = pl.program_id(0) % tiles_per_b
    starts = jnp.logical_or(i == 0, i == ctx_tiles)
    ends = jnp.logical_or(i == ctx_tiles - 1, i == tiles_per_b - 1)
    t = t_ref[...]
    tm = t.shape[0]
    prev_row = jnp.where(starts, 0.0, prev_ref[7:8, :])
    next_row = jnp.where(ends, 0.0, next_ref[0:1, :])
    rid = lax.broadcasted_iota(jnp.int32, t.shape, 0)
    prev = jnp.where(rid == 0, prev_row, pltpu.roll(t, 1, 0))
    nxt = jnp.where(rid == tm - 1, next_row, pltpu.roll(t, tm - 1, 0))
    t = t + mup_ref[...] * (prev - t) + mun_ref[...] * (nxt - t)

    w = RWKV_W
    r = t[:, 0:w]
    k = t[:, w:2 * w]
    v = t[:, 2 * w:3 * w]
    o = 3 * w
    wh = (t[:, o:o + 64], t[:, o + 64:o + 128])
    ah = (t[:, o + 128:o + 192], t[:, o + 192:o + 256])
    gh = t[:, o + 256:o + 384]
    blk = blk_ref[...]

    kk = k * kk_ref[...]
    kk = kk * lax.rsqrt(jnp.maximum(_dot_exact_rhs01(kk * kk, blk), 1e-24))
    r_out[...] = r
    v_out[...] = v
    na_out[...] = -kk
    g_out[...] = _dot(_sigmoid(gh).astype(BF16), g2_ref[...])
    rk = r * rk_ref[...]
    bonus = jnp.zeros_like(v)
    for d in range(2):
        z = w0_ref[d] + _dot(jnp.tanh(wh[d]).astype(BF16), w2_ref[d])
        nz = -z
        softplus = jnp.maximum(nz, 0.0) + jnp.log(1.0 + jnp.exp(-jnp.abs(nz)))
        lw_out[d] = -jnp.exp(-softplus - 0.5)
        a = _sigmoid(a0_ref[d] + _dot(ah[d].astype(BF16), a2_ref[d]))
        kd = k * (1.0 + (a - 1.0) * ka_ref[...])
        kd_out[d] = kd
        bb_out[d] = kk * a
        bonus = bonus + _dot_exact_rhs01(rk * kd, blk) * v
    bonus_out[...] = bonus


def _rwkv_elem(p2, w, rows_per_b, ctx_len):
    t = p2.shape[0]
    tm = ROW_TILE
    tiles_per_b = rows_per_b // tm
    n8 = t // 8
    cb = RWKV_OFF // RWKV_IN
    const = lambda shape: pl.BlockSpec(shape, lambda i: (0,) * len(shape))
    one = pl.BlockSpec((tm, RWKV_W), lambda i: (i, 0))
    two = pl.BlockSpec((2, tm, RWKV_W), lambda i: (0, i, 0))
    s1 = jax.ShapeDtypeStruct((t, RWKV_W), F32)
    s2 = jax.ShapeDtypeStruct((2, t, RWKV_W), F32)
    return pl.pallas_call(
        functools.partial(_rwkv_elem_kernel, tiles_per_b=tiles_per_b, ctx_tiles=ctx_len // tm),
        grid=(t // tm,),
        in_specs=[
            pl.BlockSpec((tm, RWKV_IN), lambda i: (i, cb)),
            pl.BlockSpec((8, RWKV_IN), lambda i: (jnp.maximum(i * (tm // 8) - 1, 0), cb)),
            pl.BlockSpec((8, RWKV_IN), lambda i: (jnp.minimum((i + 1) * (tm // 8), n8 - 1), cb)),
            const((1, RWKV_IN)), const((1, RWKV_IN)),
            const((2, 1, RWKV_W)), const((2, RWKV_DECAY_LORA, RWKV_W)),
            const((2, 1, RWKV_W)), const((2, RWKV_A_LORA, RWKV_W)),
            const((RWKV_GATE_LORA, RWKV_W)),
            const((1, RWKV_W)), const((1, RWKV_W)), const((1, RWKV_W)),
            const((RWKV_W, RWKV_W)),
        ],
        out_specs=[one, one, one, two, two, two, one, one],
        out_shape=[s1, s1, s1, s2, s2, s2, s1, s1],
        compiler_params=_params("arbitrary"),
        name="rwkv_elem",
    )(p2, p2, p2, w["mup"], w["mun"], w["w0"], w["w2"], w["a0"], w["a2"], w["g2"],
      w["kk"], w["ka"], w["rk"], w["blk"])


def _block_rows(x, bm):
    return jnp.where(bm, jnp.concatenate([x, x, x, x], axis=0), 0.0)


def _rwkv_chunk_kernel(rf_ref, vf_ref, naf_ref, kdf_ref, bbf_ref, lwf_ref,
                       rb_ref, vb_ref, nab_ref, kdb_ref, bbb_ref, lwb_ref,
                       tri_ref, ms_ref, mi_ref, of_ref, ob_ref, h_ref):
    @pl.when(pl.program_id(1) == 0)
    def _():
        h_ref[...] = jnp.zeros_like(h_ref)

    dirs = ((rf_ref, vf_ref, naf_ref, kdf_ref, bbf_ref, lwf_ref, of_ref),
            (rb_ref, vb_ref, nab_ref, kdb_ref, bbb_ref, lwb_ref, ob_ref))
    c = RWKV_CHUNK
    assert c == 64 and RWKV_HD == 64
    bm = (lax.broadcasted_iota(jnp.int32, (HALF, HALF), 0) >> 6
          == lax.broadcasted_iota(jnp.int32, (HALF, HALF), 1) >> 6)
    eye_p = ((lax.broadcasted_iota(jnp.int32, (c, HALF), 1) & (c - 1))
             == lax.broadcasted_iota(jnp.int32, (c, HALF), 0))
    eye_s = (lax.broadcasted_iota(jnp.int32, (HALF, HALF), 0)
             == lax.broadcasted_iota(jnp.int32, (HALF, HALF), 1))
    bf = lambda x: x.astype(BF16)
    blk = lambda x: _block_rows(x, bm)
    cat0 = lambda xs: jnp.concatenate(xs, axis=0)
    cat1 = lambda xs: jnp.concatenate(xs, axis=1)

    chains = [(d, bi) + dirs[d] for d in range(2) for bi in range(RWKV_ROWS)]
    cums = [_dot_exact_lhs01(tri_ref[d], lw_ref[bi]) for d, bi, _, _, _, _, _, lw_ref, _ in chains]
    inst = []
    for (d, bi, r_ref, v_ref, na_ref, kd_ref, bb_ref, lw_ref, o_ref), cum in zip(chains, cums):
        lw = lw_ref[bi]
        tot = cum[c - 1:c, :] if d == 0 else cum[0:1, :]
        e_neg = jnp.exp(-cum)
        e_rel = jnp.exp(tot - cum)
        e_tot = jnp.exp(tot)
        a_t = na_ref[bi] * jnp.exp(cum - lw)
        r_t = r_ref[bi] * jnp.exp(cum)
        kd, bb, v = kd_ref[bi], bb_ref[bi], v_ref[bi]
        for g in range(2):
            sl = slice(g * HALF, (g + 1) * HALF)
            inst.append(dict(
                d=d, bi=bi, g=g, sl=sl, o_ref=o_ref, a=a_t[:, sl], r=bf(r_t[:, sl]), v=v[:, sl],
                bt=bb[:, sl] * e_neg[:, sl], kt=kd[:, sl] * e_neg[:, sl],
                bk_p=bf(cat0([bb[:, sl] * e_rel[:, sl], kd[:, sl] * e_rel[:, sl]])), etot=e_tot[:, sl]))
    strict = [ms_ref[0] > 0.5, ms_ref[1] > 0.5]
    incl = [mi_ref[0] > 0.5, mi_ref[1] > 0.5]

    bigs = [_dot_nt(bf(cat0([s["a"], s["r"].astype(F32)])), bf(cat0([blk(s["bt"]), blk(s["kt"])])))
            for s in inst]
    for s, big in zip(inst, bigs):
        s["a_ab"] = jnp.where(strict[s["d"]], big[0:c, 0:HALF], 0.0)
        s["a_ak"] = bf(jnp.where(strict[s["d"]], big[0:c, HALF:], 0.0))
        s["a_rb"] = bf(jnp.where(incl[s["d"]], big[c:, 0:HALF], 0.0))
        s["a_rk"] = bf(jnp.where(incl[s["d"]], big[c:, HALF:], 0.0))
        s["v_bd"] = bf(blk(s["v"]))
        s["t"] = jnp.where(eye_p, 1.0, 0.0) + s["a_ab"]
    pws = [_dot(bf(s["a_ab"]), bf(blk(s["a_ab"]))) for s in inst]
    akvs = [_dot(s["a_ak"], s["v_bd"]) for s in inst]
    arkv = [_dot(s["a_rk"], s["v_bd"]) for s in inst]
    for it in range(5):
        prods = [_dot(bf(cat0([s["t"], pw])), bf(blk(pw))) for s, pw in zip(inst, pws)]
        for s, prod in zip(inst, prods):
            s["t"] = s["t"] + prod[0:c]
        pws = [prod[c:] for prod in prods]
    wus = [_dot(bf(s["t"]), bf(cat1([blk(s["a"]), blk(akv)]))) for s, akv in zip(inst, akvs)]
    h_f32 = [h_ref[s["bi"], s["d"], s["g"]] for s in inst]
    h_old = [bf(h) for h in h_f32]
    us = [_dot(bf(wu[:, 0:HALF]), h) + wu[:, HALF:] for wu, h in zip(wus, h_old)]
    rhs = [_dot(s["r"], h) for s, h in zip(inst, h_old)]
    arbu = [_dot(s["a_rb"], bf(blk(u))) for s, u in zip(inst, us)]
    incs = [_dot_tn(s["bk_p"], bf(cat0([u, s["v"]]))) for s, u in zip(inst, us)]
    for s, rh, au, ak, inc, h in zip(inst, rhs, arbu, arkv, incs, h_f32):
        s["o_ref"][s["bi"], :, s["sl"]] = rh + au + ak
        decay_col = jnp.sum(jnp.where(eye_s, s["etot"], 0.0), axis=1, keepdims=True)
        h_ref[s["bi"], s["d"], s["g"]] = decay_col * h + jnp.where(bm, inc, 0.0)


def _rwkv_chunk(e, b, l, ctx_len, consts):
    c = RWKV_CHUNK
    rows = RWKV_ROWS
    assert b % rows == 0
    n_all, n_ctx = l // c, ctx_len // c
    ch = functools.partial(_scan_chunk, n_ctx=n_ctx, n_all=n_all)
    tri, ms, mi = consts
    r3, v3, na3 = (x.reshape(b, l, RWKV_W) for x in (e["r"], e["v"], e["na"]))
    kd4, bb4, lw4 = (x.reshape(2, b, l, RWKV_W) for x in (e["kd"], e["bb"], e["lw"]))

    def specs(d):
        one = pl.BlockSpec((rows, c, RWKV_W), lambda bi, j: (bi, ch(d, j), 0))
        two = pl.BlockSpec((None, rows, c, RWKV_W), lambda bi, j: (d, bi, ch(d, j), 0))
        return one, two

    one_f, two_f = specs(0)
    one_b, two_b = specs(1)
    const = lambda shape: pl.BlockSpec(shape, lambda bi, j: (0,) * len(shape))
    out = jax.ShapeDtypeStruct((b, l, RWKV_W), F32)
    return pl.pallas_call(
        _rwkv_chunk_kernel,
        grid=(b // rows, n_all),
        in_specs=[one_f, one_f, one_f, two_f, two_f, two_f,
                  one_b, one_b, one_b, two_b, two_b, two_b,
                  const((2, c, c)), const((2, c, HALF)), const((2, c, HALF))],
        out_specs=[one_f, one_b],
        out_shape=[out, out],
        scratch_shapes=[pltpu.VMEM((rows, 2, 2, HALF, HALF), F32)],
        compiler_params=_params("arbitrary", "arbitrary"),
        name="rwkv_chunk",
    )(r3, v3, na3, kd4, bb4, lw4, r3, v3, na3, kd4, bb4, lw4, tri, ms, mi)


def _rwkv_consts():
    c = RWKV_CHUNK
    idx = np.arange(c)
    lower = idx[None, :] <= idx[:, None]
    tri = np.stack([lower, lower.T]).astype(np.float32)
    s_in_head = np.tile(idx, HALF // c)[None, :]
    t_row = idx[:, None]
    ms = np.stack([s_in_head < t_row, s_in_head > t_row]).astype(np.float32)
    mi = np.stack([s_in_head <= t_row, s_in_head >= t_row]).astype(np.float32)
    return jnp.asarray(tri, BF16), jnp.asarray(ms), jnp.asarray(mi)


def _merge_kernel(x_ref, mod_ref, gate_ref, retf_ref, retb_ref, mla_ref, yf_ref, yb_ref, bonus_ref, g_ref,
                  lng_ref, lnb_ref, blk_ref, wr_ref, wm_ref, ww_ref, wo_ref, o_ref):
    blk = blk_ref[...]
    ret = retf_ref[...] + retb_ref[...]
    y = yf_ref[...] + yb_ref[...]
    mean = _dot_exact_rhs01(y, blk) * (1.0 / RWKV_HD)
    yc = y - mean
    var = _dot_exact_rhs01(yc * yc, blk) * (1.0 / RWKV_HD)
    y = yc * lax.rsqrt(var + RWKV_GN_EPS) * lng_ref[...] + lnb_ref[...] + bonus_ref[...]
    rwk = y * g_ref[...]
    gate = gate_ref[...]
    d = D_MODEL
    mix = (_sigmoid(gate[:, 0:d]) * _dot(ret.astype(BF16), wr_ref[...])
           + _sigmoid(gate[:, d:2 * d]) * _dot(mla_ref[...], wm_ref[...])
           + _sigmoid(gate[:, 2 * d:]) * _dot(rwk.astype(BF16), ww_ref[...]))
    o_ref[...] = x_ref[...] + mod_ref[2:3, :] * _dot(mix.astype(BF16), wo_ref[...])


def _merge(x2, mod_l, p2, retf, retb, mla, yf, yb, bonus, g, w, geom):
    t = x2.shape[0]
    tm = ROW_TILE
    tpb, ctx_tiles, batch = geom
    row = functools.partial(_mod_row, tiles_per_b=tpb, ctx_tiles=ctx_tiles, batch=batch)
    const = lambda shape: pl.BlockSpec(shape, lambda i: (0,) * len(shape))
    rows = lambda width: pl.BlockSpec((tm, width), lambda i: (i, 0))
    return pl.pallas_call(
        _merge_kernel,
        grid=(t // tm,),
        in_specs=[
            rows(D_MODEL),
            pl.BlockSpec((None, 6, D_MODEL), lambda i: (row(i), 0, 0)),
            rows(3 * D_MODEL),
            rows(RET_W), rows(RET_W), rows(MLA_W), rows(RWKV_W), rows(RWKV_W), rows(RWKV_W), rows(RWKV_W),
            const((1, RWKV_W)), const((1, RWKV_W)), const((RWKV_W, RWKV_W)),
            const((RET_W, D_MODEL)), const((MLA_W, D_MODEL)), const((RWKV_W, D_MODEL)),
            const((D_MODEL, D_MODEL)),
        ],
        out_specs=rows(D_MODEL),
        out_shape=jax.ShapeDtypeStruct((t, D_MODEL), F32),
        compiler_params=_params("arbitrary"),
        name="merge",
    )(x2, mod_l, p2, retf, retb, mla, yf, yb, bonus, g, w["lng"], w["lnb"], w["blk"],
      w["wr"], w["wm"], w["ww"], w["wo"])


def _route_kernel(x_ref, mod_ref, g_ref, wr_ref, br_ref, tri_ref,
                  h_out, idx_out, rank_out, gate_out, cnt_out, run_ref):
    @pl.when(pl.program_id(0) == 0)
    def _():
        run_ref[...] = jnp.zeros_like(run_ref)

    x = x_ref[...]
    y = x * lax.rsqrt(jnp.mean(x * x, axis=-1, keepdims=True) + NORM_EPS) * g_ref[...]
    h = y * (1.0 + mod_ref[4:5, :]) + mod_ref[3:4, :]
    h_out[...] = h
    logits = _dot(h.astype(BF16), wr_ref[...]) + br_ref[...]
    lane = lax.broadcasted_iota(jnp.int32, logits.shape, 1).astype(F32)
    vals, idxs = [], []
    for _ in range(TOP_K):
        m = jnp.max(logits, axis=-1, keepdims=True)
        sel = jnp.min(jnp.where(logits == m, lane, 128.0), axis=-1, keepdims=True)
        vals.append(m)
        idxs.append(sel)
        logits = jnp.where(lane == sel, -jnp.inf, logits)
    es = [jnp.exp(vv - vals[0]) for vv in vals]
    den = es[0] + es[1] + es[2] + es[3]
    chosen = jnp.zeros(lane.shape, F32)
    for kq in range(TOP_K):
        chosen = jnp.where(lane == idxs[kq], 1.0, chosen)
    before = _dot(tri_ref[...], chosen.astype(BF16)) + run_ref[...]
    idx_row = jnp.zeros(lane.shape, F32)
    rank_row = jnp.zeros(lane.shape, F32)
    gate_row = jnp.zeros(lane.shape, F32)
    for kq in range(TOP_K):
        rk = jnp.sum(jnp.where(lane == idxs[kq], before, 0.0), axis=-1, keepdims=True)
        idx_row = jnp.where(lane == float(kq), idxs[kq], idx_row)
        rank_row = jnp.where(lane == float(kq), rk, rank_row)
        gate_row = jnp.where(lane == float(kq), es[kq] / den, gate_row)
    idx_out[...] = idx_row.astype(jnp.int32)
    rank_out[...] = rank_row.astype(jnp.int32)
    gate_out[...] = gate_row
    run_ref[...] += jnp.sum(chosen, axis=0, keepdims=True)
    cnt_out[...] = run_ref[...]


def _route(x2, mod_l, g, wr_pad, br_pad, tri, geom):
    t = x2.shape[0]
    tm = ROW_TILE
    tpb, ctx_tiles, batch = geom
    row = functools.partial(_mod_row, tiles_per_b=tpb, ctx_tiles=ctx_tiles, batch=batch)
    lanes = lambda dt: jax.ShapeDtypeStruct((t, 128), dt)
    tile = pl.BlockSpec((tm, 128), lambda i: (i, 0))
    return pl.pallas_call(
        _route_kernel,
        grid=(t // tm,),
        in_specs=[
            pl.BlockSpec((tm, D_MODEL), lambda i: (i, 0)),
            pl.BlockSpec((None, 6, D_MODEL), lambda i: (row(i), 0, 0)),
            pl.BlockSpec((1, D_MODEL), lambda i: (0, 0)),
            pl.BlockSpec((D_MODEL, 128), lambda i: (0, 0)),
            pl.BlockSpec((1, 128), lambda i: (0, 0)),
            pl.BlockSpec((tm, tm), lambda i: (0, 0)),
        ],
        out_specs=[pl.BlockSpec((tm, D_MODEL), lambda i: (i, 0)), tile, tile, tile,
                   pl.BlockSpec((1, 128), lambda i: (0, 0))],
        out_shape=[jax.ShapeDtypeStruct((t, D_MODEL), F32), lanes(jnp.int32), lanes(jnp.int32),
                   lanes(F32), jax.ShapeDtypeStruct((1, 128), F32)],
        scratch_shapes=[pltpu.VMEM((1, 128), F32)],
        compiler_params=_params("arbitrary"),
        name="moe_router",
    )(x2, mod_l, g.reshape(1, D_MODEL), wr_pad, br_pad, tri)


def _moe_layout(idx, rank, counts, t):
    n_rows = -(-(t * TOP_K + N_EXPERTS * (MOE_TILE - 1)) // MOE_TILE) * MOE_TILE
    n_tiles = n_rows // MOE_TILE
    padded = (counts + MOE_TILE - 1) // MOE_TILE * MOE_TILE
    pad_end = jnp.cumsum(padded)
    pad_start = pad_end - padded
    onehot = idx[..., None] == jnp.arange(N_EXPERTS, dtype=jnp.int32)
    dest = jnp.sum(jnp.where(onehot, pad_start, 0), axis=-1) + rank
    tile_end = pad_end // MOE_TILE
    tile_expert = jnp.sum(jnp.arange(n_tiles, dtype=jnp.int32)[:, None] >= tile_end[None, :], axis=1)
    tile_expert = jnp.minimum(tile_expert, N_EXPERTS - 1).astype(jnp.int32)
    n_used = tile_end[-1:].astype(jnp.int32)
    return dest.astype(jnp.int32).reshape(t // ROW_TILE, 1, ROW_TILE * TOP_K), tile_expert, n_used, n_rows


def _dispatch_kernel(dest_ref, h_ref, xs_in_ref, xs_ref, sem):
    del xs_in_ref
    tm = h_ref.shape[0]

    def issue(tok, carry):
        src = h_ref.at[pl.ds(tok, 1), :]
        for kq in range(TOP_K):
            row = dest_ref[0, tok * TOP_K + kq]
            pltpu.make_async_copy(src, xs_ref.at[pl.ds(row, 1), :], sem.at[0]).start()
        return carry

    lax.fori_loop(0, tm, issue, 0, unroll=4)
    for _ in range(TOP_K):
        pltpu.make_async_copy(h_ref, xs_ref.at[pl.ds(0, tm), :], sem.at[0]).wait()


def _dispatch(dest, h2, xs_prev):
    t = h2.shape[0]
    tm = ROW_TILE
    n_rows = xs_prev.shape[0]
    return pl.pallas_call(
        _dispatch_kernel,
        grid=(t // tm,),
        in_specs=[
            pl.BlockSpec((None, 1, tm * TOP_K), lambda i: (i, 0, 0), memory_space=pltpu.SMEM),
            pl.BlockSpec((tm, D_MODEL), lambda i: (i, 0)),
            pl.BlockSpec(memory_space=pl.ANY),
        ],
        out_specs=pl.BlockSpec(memory_space=pl.ANY),
        out_shape=jax.ShapeDtypeStruct((n_rows, D_MODEL), F32),
        scratch_shapes=[pltpu.SemaphoreType.DMA((1,))],
        input_output_aliases={2: 0},
        compiler_params=_params("arbitrary"),
        name="moe_dispatch",
    )(dest, h2, xs_prev)


def _expert_kernel(te_ref, nu_ref, x_ref, wgu_ref, bgu_ref, wd_ref, bd_ref, o_ref):
    del te_ref
    used = pl.program_id(0) < nu_ref[0]

    @pl.when(jnp.logical_not(used))
    def _():
        o_ref[...] = jnp.zeros_like(o_ref)

    @pl.when(used)
    def _():
        gu = _dot(x_ref[...].astype(BF16), wgu_ref[...]) + bgu_ref[...]
        f = gu.shape[1] // 2
        g_lin = jnp.minimum(gu[:, 0:f], SWIGLU_LIMIT)
        up = jnp.clip(gu[:, f:], -SWIGLU_LIMIT, SWIGLU_LIMIT)
        act = g_lin * _sigmoid(SWIGLU_ALPHA * g_lin) * (up + 1.0)
        o_ref[...] = _dot(act.astype(BF16), wd_ref[...]) + bd_ref[...]


def _experts(xs, tile_expert, n_used, wgu, bgu, wd, bd):
    n_rows = xs.shape[0]
    f2 = wgu.shape[2]
    used = lambda g, nu: jnp.minimum(g, nu[0] - 1)
    grid_spec = pltpu.PrefetchScalarGridSpec(
        num_scalar_prefetch=2,
        grid=(n_rows // MOE_TILE,),
        in_specs=[
            pl.BlockSpec((MOE_TILE, D_MODEL), lambda g, te, nu: (used(g, nu), 0)),
            pl.BlockSpec((None, D_MODEL, f2), lambda g, te, nu: (te[used(g, nu)], 0, 0)),
            pl.BlockSpec((None, 1, f2), lambda g, te, nu: (te[used(g, nu)], 0, 0)),
            pl.BlockSpec((None, f2 // 2, D_MODEL), lambda g, te, nu: (te[used(g, nu)], 0, 0)),
            pl.BlockSpec((None, 1, D_MODEL), lambda g, te, nu: (te[used(g, nu)], 0, 0)),
        ],
        out_specs=pl.BlockSpec((MOE_TILE, D_MODEL), lambda g, te, nu: (g, 0)),
    )
    return pl.pallas_call(
        _expert_kernel,
        grid_spec=grid_spec,
        out_shape=jax.ShapeDtypeStruct((n_rows, D_MODEL), F32),
        compiler_params=_params("arbitrary"),
        name="moe_experts",
    )(tile_expert, n_used, xs, wgu, bgu, wd, bd)


def _collect_kernel(dest_ref, dest_next_ref, gate_ref, x_ref, mod_ref, y_ref, o_ref, buf_ref, sem):
    tm = x_ref.shape[0]
    step = pl.program_id(0)
    slot = step % 2

    def issue_tile(d_ref, to_slot):
        def issue(tok, carry):
            for kq in range(TOP_K):
                row = d_ref[0, tok * TOP_K + kq]
                pltpu.make_async_copy(y_ref.at[pl.ds(row, 1), :],
                                      buf_ref.at[to_slot, kq, pl.ds(tok, 1), :], sem.at[to_slot]).start()
            return carry

        lax.fori_loop(0, tm, issue, 0, unroll=4)

    @pl.when(step == 0)
    def _():
        issue_tile(dest_ref, slot)

    @pl.when(step + 1 < pl.num_programs(0))
    def _():
        issue_tile(dest_next_ref, 1 - slot)

    for kq in range(TOP_K):
        pltpu.make_async_copy(y_ref.at[pl.ds(0, tm), :], buf_ref.at[slot, kq], sem.at[slot]).wait()
    gate = gate_ref[...]
    acc = gate[:, 0:1] * buf_ref[slot, 0]
    for kq in range(1, TOP_K):
        acc = acc + gate[:, kq:kq + 1] * buf_ref[slot, kq]
    o_ref[...] = x_ref[...] + mod_ref[5:6, :] * acc


def _collect(dest, gate_rows, x2, mod_l, ys, geom, latent_only):
    t = x2.shape[0]
    tm = ROW_TILE
    tpb, ctx_tiles, batch = geom
    row = functools.partial(_mod_row, tiles_per_b=tpb, ctx_tiles=ctx_tiles, batch=batch)
    if latent_only:
        lat = tpb - ctx_tiles
        n_steps = batch * lat
        tile = lambda i: (i // lat) * tpb + ctx_tiles + i % lat
    else:
        n_steps = t // tm
        tile = lambda i: i
    nxt = lambda i: tile(jnp.minimum(i + 1, n_steps - 1))
    return pl.pallas_call(
        _collect_kernel,
        grid=(n_steps,),
        in_specs=[
            pl.BlockSpec((None, 1, tm * TOP_K), lambda i: (tile(i), 0, 0), memory_space=pltpu.SMEM),
            pl.BlockSpec((None, 1, tm * TOP_K), lambda i: (nxt(i), 0, 0), memory_space=pltpu.SMEM),
            pl.BlockSpec((tm, 128), lambda i: (tile(i), 0)),
            pl.BlockSpec((tm, D_MODEL), lambda i: (tile(i), 0)),
            pl.BlockSpec((None, 6, D_MODEL), lambda i: (row(tile(i)), 0, 0)),
            pl.BlockSpec(memory_space=pl.ANY),
        ],
        out_specs=pl.BlockSpec((tm, D_MODEL), lambda i: (i, 0)),
        out_shape=jax.ShapeDtypeStruct((n_steps * tm, D_MODEL), F32),
        scratch_shapes=[pltpu.VMEM((2, TOP_K, tm, D_MODEL), F32), pltpu.SemaphoreType.DMA((2,))],
        compiler_params=_params("arbitrary"),
        name="moe_collect",
    )(dest, dest, gate_rows, x2, mod_l, ys)


def _rope_angles(pos, dim):
    inv = ROPE_BASE ** (-jnp.arange(0, dim, 2, dtype=F32) / dim)
    return pos.astype(F32)[:, None] * inv[None, :]


def _rope_tables(seq, ctx_len):
    pos = jnp.arange(seq, dtype=jnp.int32)
    ang = _rope_angles(pos, RET_DK)
    r_cos = jnp.tile(jnp.concatenate([jnp.cos(ang), jnp.cos(ang)], -1), (1, RET_HEADS))
    r_sin = jnp.tile(jnp.concatenate([-jnp.sin(ang), jnp.sin(ang)], -1), (1, RET_HEADS))
    ra = _rope_angles(pos // GRID_W, MLA_ROPE // 2)
    ca = _rope_angles(pos % GRID_W, MLA_ROPE // 2)
    one = jnp.ones((seq, MLA_NOPE), F32)
    zero = jnp.zeros((seq, MLA_NOPE), F32)
    m_cos = jnp.concatenate([one, jnp.cos(ra), jnp.cos(ra), jnp.cos(ca), jnp.cos(ca), one[:, :32]], -1)
    m_sin = jnp.concatenate([zero, -jnp.sin(ra), jnp.sin(ra), -jnp.sin(ca), jnp.sin(ca), zero[:, :32]], -1)
    ctx1 = lambda n: jnp.ones((ctx_len, n), F32)
    ctx0 = lambda n: jnp.zeros((ctx_len, n), F32)
    return (jnp.concatenate([ctx1(256), r_cos], 0), jnp.concatenate([ctx0(256), r_sin], 0),
            jnp.concatenate([ctx1(128), m_cos], 0), jnp.concatenate([ctx0(128), m_sin], 0))


def kernel(x, c, ctx, c_ctx, ada_w, ada_b, norm1_g, norm2_g, w_in, mla_q_norm_g, mla_w_uq, mla_kv_norm_g, mla_w_ukv, mla_qk_q_g, mla_qk_k_g, rwkv_mu_prev, rwkv_mu_next, rwkv_w0, rwkv_w2, rwkv_a0, rwkv_a2, rwkv_g2, rwkv_k_k, rwkv_k_a, rwkv_r_k, rwkv_lnx_g, rwkv_lnx_b, w_br_ret, w_br_mla, w_br_rwkv, w_out, w_router, b_router, w_gu, b_gu, w_down, b_down):
    b, s, d = x.shape
    ctx_len = ctx.shape[1]
    depth = ada_w.shape[0]
    l = ctx_len + s
    t = b * l
    assert d == D_MODEL and b < 16 and ctx_len % ROW_TILE == 0 and s % ROW_TILE == 0
    geom = (l // ROW_TILE, ctx_len // ROW_TILE, b)

    xa = jnp.concatenate([ctx, x], axis=1).reshape(t, d)
    cc = jnp.zeros((16, d), F32).at[:b].set(c).at[b].set(c_ctx)
    mod = _ada_all(cc, ada_w.astype(BF16), ada_b).reshape(depth, 16, 6, d)

    ret_cos, ret_sin, mla_cos, mla_sin = _rope_tables(s, ctx_len)
    ret_consts = _retention_consts()
    rwkv_consts = _rwkv_consts()
    head_blk = jnp.asarray(np.kron(np.eye(RWKV_HEADS), np.ones((RWKV_HD, RWKV_HD))), BF16)
    route_tri = jnp.asarray(np.tril(np.ones((ROW_TILE, ROW_TILE)), -1), BF16)

    xs = None
    for li in range(depth):
        p2 = _inproj(xa, mod[li], norm1_g[li], _pad_w_in(w_in[li]), geom)
        p3 = p2.reshape(b, l, P_COLS)

        retf, retb = (r.reshape(t, RET_W) for r in _retention(p3, ret_cos, ret_sin, ret_consts, ctx_len))

        mw = _mla_weights(mla_q_norm_g[li], mla_w_uq[li], mla_kv_norm_g[li], mla_w_ukv[li],
                          mla_qk_q_g[li], mla_qk_k_g[li])
        q2, k2, v2 = (a.reshape(b, l, MLA_PAD_W) for a in _mla_prep(p2, mw, mla_cos, mla_sin, l))
        mla = _attention(q2, k2, v2, ctx_len).reshape(t, MLA_W)

        rw = dict(mup=rwkv_mu_prev[li].reshape(1, -1), mun=rwkv_mu_next[li].reshape(1, -1),
                  w0=rwkv_w0[li].reshape(2, 1, RWKV_W), w2=rwkv_w2[li].astype(BF16),
                  a0=rwkv_a0[li].reshape(2, 1, RWKV_W), a2=rwkv_a2[li].astype(BF16),
                  g2=rwkv_g2[li].astype(BF16), kk=rwkv_k_k[li].reshape(1, -1),
                  ka=rwkv_k_a[li].reshape(1, -1), rk=rwkv_r_k[li].reshape(1, -1), blk=head_blk)
        r_, v_, na_, kd_, bb_, lw_, g_, bonus_ = _rwkv_elem(p2, rw, l, ctx_len)
        yf, yb = _rwkv_chunk(dict(r=r_, v=v_, na=na_, kd=kd_, bb=bb_, lw=lw_), b, l, ctx_len, rwkv_consts)

        mg = dict(lng=rwkv_lnx_g[li].reshape(1, -1), lnb=rwkv_lnx_b[li].reshape(1, -1), blk=head_blk,
                  wr=w_br_ret[li].astype(BF16), wm=w_br_mla[li].astype(BF16),
                  ww=w_br_rwkv[li].astype(BF16), wo=w_out[li].astype(BF16))
        xa = _merge(xa, mod[li], p2, retf, retb, mla, yf.reshape(t, RWKV_W), yb.reshape(t, RWKV_W), bonus_, g_,
                    mg, geom)

        wr_pad = jnp.pad(w_router[li], ((0, 0), (0, 128 - N_EXPERTS))).astype(BF16)
        br_pad = jnp.concatenate([b_router[li], jnp.full((128 - N_EXPERTS,), -1e30, F32)]).reshape(1, 128)
        h2, idx_rows, rank_rows, gate_rows, cnt = _route(xa, mod[li], norm2_g[li], wr_pad, br_pad,
                                                         route_tri, geom)
        counts = cnt[0, :N_EXPERTS].astype(jnp.int32)
        dest, tile_expert, n_used, n_rows = _moe_layout(idx_rows[:, :TOP_K], rank_rows[:, :TOP_K], counts, t)
        if xs is None:
            xs = jnp.zeros((n_rows, D_MODEL), F32)
        xs = _dispatch(dest, h2, xs)
        ys = _experts(xs, tile_expert, n_used, w_gu[li].astype(BF16), b_gu[li].reshape(N_EXPERTS, 1, -1),
                      w_down[li].astype(BF16), b_down[li].reshape(N_EXPERTS, 1, -1))
        xa = _collect(dest, gate_rows, xa, mod[li], ys, geom, latent_only=li == depth - 1)

    return xa.reshape(b, s, d)
```

```python
import functools

import jax
import jax.numpy as jnp
import numpy as np
from jax import lax
from jax.experimental import pallas as pl
from jax.experimental.pallas import tpu as pltpu

F32 = jnp.float32
BF16 = jnp.bfloat16

D_MODEL = 1024
GRID_W = 64
RET_HEADS, RET_DK, RET_DV, RET_CHUNK = 4, 64, 128, 128
RET_W = RET_HEADS * RET_DV
MLA_HEADS, MLA_Q_RANK, MLA_KV_RANK, MLA_NOPE, MLA_ROPE, MLA_DV = 8, 256, 128, 64, 32, 64
MLA_QK = MLA_NOPE + MLA_ROPE
MLA_W = MLA_HEADS * MLA_DV
MLA_PAD_W = MLA_HEADS * 128
RWKV_HEADS, RWKV_HD = 8, 64
RWKV_W = RWKV_HEADS * RWKV_HD
RWKV_DECAY_LORA, RWKV_A_LORA, RWKV_GATE_LORA = 64, 64, 128
RWKV_GN_EPS = 64e-5
RWKV_IN = 3 * RWKV_W + 2 * RWKV_DECAY_LORA + 2 * RWKV_A_LORA + RWKV_GATE_LORA
N_EXPERTS, TOP_K = 32, 4
SWIGLU_LIMIT, SWIGLU_ALPHA = 7.0, 1.702
ROPE_BASE = 10000.0
NORM_EPS = 1e-6
HEAD_NORM_EPS = 1e-5

GATE_OFF = 0
RET_OFF = 3 * D_MODEL
MLA_OFF = RET_OFF + 2048
RWKV_OFF = MLA_OFF + 512 + 128
P_COLS = RWKV_OFF + RWKV_IN
IN_COL_TILE = 3840

ROW_TILE = 256
RET_ROWS = 1
RWKV_CHUNK = 64
RWKV_ROWS = 4
HALF = 256
ATTN_Q_SCALE = MLA_QK ** -0.5 * 1.4426950408889634
ATTN_HEADS_PER_STEP = 4
MOE_TILE = 256
VMEM_LIMIT = 48 * 1024 * 1024

NT_DIMS = (((1,), (1,)), ((), ()))
TN_DIMS = (((0,), (0,)), ((), ()))


def _params(*sem):
    return pltpu.CompilerParams(dimension_semantics=sem, vmem_limit_bytes=VMEM_LIMIT)


def _dot(a, b):
    return jnp.dot(a, b, preferred_element_type=F32)


def _dot_nt(a, b):
    return lax.dot_general(a, b, NT_DIMS, preferred_element_type=F32)


def _dot_tn(a, b):
    return lax.dot_general(a, b, TN_DIMS, preferred_element_type=F32)


def _split3(x):
    hi = x.astype(BF16)
    r1 = x - hi.astype(F32)
    mid = r1.astype(BF16)
    lo = (r1 - mid.astype(F32)).astype(BF16)
    return hi, mid, lo


def _dot_exact_rhs01(x, m01):
    hi = x.astype(BF16)
    lo = (x - hi.astype(F32)).astype(BF16)
    return _dot(hi, m01) + _dot(lo, m01)


def _dot_exact_lhs01(m01, x):
    hi, mid, lo = _split3(x)
    return _dot(m01, hi) + _dot(m01, mid) + _dot(m01, lo)


def _sigmoid(x):
    return 1.0 / (1.0 + jnp.exp(-x))


def _silu(x):
    return x * _sigmoid(x)


def _ada_kernel(c_ref, w_ref, b_ref, o_ref):
    s = _silu(c_ref[...])
    o_ref[...] = _dot(s.astype(BF16), w_ref[...]) + b_ref[...]


def _ada_all(cc, ada_w_bf, ada_b):
    depth = ada_w_bf.shape[0]
    tn = 1536
    return pl.pallas_call(
        _ada_kernel,
        grid=(depth, 6 * D_MODEL // tn),
        in_specs=[
            pl.BlockSpec((16, D_MODEL), lambda l, j: (0, 0)),
            pl.BlockSpec((None, D_MODEL, tn), lambda l, j: (l, 0, j)),
            pl.BlockSpec((None, 1, tn), lambda l, j: (l, 0, j)),
        ],
        out_specs=pl.BlockSpec((None, 16, tn), lambda l, j: (l, 0, j)),
        out_shape=jax.ShapeDtypeStruct((depth, 16, 6 * D_MODEL), F32),
        compiler_params=_params("arbitrary", "arbitrary"),
        name="ada_mod",
    )(cc, ada_w_bf, ada_b.reshape(depth, 1, 6 * D_MODEL))


def _mod_row(i, tiles_per_b, ctx_tiles, batch):
    return jnp.where(i % tiles_per_b < ctx_tiles, batch, i // tiles_per_b)


def _inproj_kernel(x_ref, mod_ref, g_ref, w_ref, o_ref):
    x = x_ref[...]
    y = x * lax.rsqrt(jnp.mean(x * x, axis=-1, keepdims=True) + NORM_EPS) * g_ref[...]
    h = y * (1.0 + mod_ref[1:2, :]) + mod_ref[0:1, :]
    o_ref[...] = _dot(h.astype(BF16), w_ref[...])


def _inproj(x2, mod_l, g, w_pad, geom):
    t = x2.shape[0]
    tpb, ctx_tiles, batch = geom
    row = functools.partial(_mod_row, tiles_per_b=tpb, ctx_tiles=ctx_tiles, batch=batch)
    return pl.pallas_call(
        _inproj_kernel,
        grid=(P_COLS // IN_COL_TILE, t // ROW_TILE),
        in_specs=[
            pl.BlockSpec((ROW_TILE, D_MODEL), lambda j, i: (i, 0)),
            pl.BlockSpec((None, 6, D_MODEL), lambda j, i: (row(i), 0, 0)),
            pl.BlockSpec((1, D_MODEL), lambda j, i: (0, 0)),
            pl.BlockSpec((D_MODEL, IN_COL_TILE), lambda j, i: (0, j)),
        ],
        out_specs=pl.BlockSpec((ROW_TILE, IN_COL_TILE), lambda j, i: (i, j)),
        out_shape=jax.ShapeDtypeStruct((t, P_COLS), F32),
        compiler_params=_params("arbitrary", "arbitrary"),
        name="in_proj",
    )(x2, mod_l, g.reshape(1, D_MODEL), w_pad)


def _pad_w_in(w_in):
    d = w_in.shape[0]
    ret = w_in[:, 0:2048]
    cq = w_in[:, 2048:2304]
    ckv = w_in[:, 2304:2432]
    kpe = w_in[:, 2432:2464]
    rwkv = w_in[:, 2464:2464 + RWKV_IN]
    gate = w_in[:, 2464 + RWKV_IN:]
    z = lambda n: jnp.zeros((d, n), w_in.dtype)
    return jnp.concatenate([gate, ret, cq, ckv, z(64), kpe, z(32), z(128), rwkv], axis=1).astype(BF16)


def _scan_chunk(d, j, n_ctx, n_all):
    bwd = jnp.where(j < n_ctx, n_ctx - 1 - j, n_all - 1 - j + n_ctx)
    return jnp.where(d == 0, j, bwd)


def _ret_pair_kernel(qf_ref, kf_ref, vf_ref, gf_ref, cosf_ref, sinf_ref,
                     qb_ref, kb_ref, vb_ref, gb_ref, cosb_ref, sinb_ref,
                     dmat_ref, qdec_ref, kdec_ref, of_ref, ob_ref, s_ref, *, cdec):
    @pl.when(pl.program_id(1) == 0)
    def _():
        s_ref[...] = jnp.zeros_like(s_ref)

    c = RET_CHUNK
    w = RET_HEADS * RET_DK
    first = (lax.broadcasted_iota(jnp.int32, (c, w), 1) & (RET_DK - 1)) < RET_DK // 2

    def rot(x, cos, sin):
        partner = jnp.where(first, pltpu.roll(x, w - RET_DK // 2, 1), pltpu.roll(x, RET_DK // 2, 1))
        return x * cos + partner * sin

    dirs = ((qf_ref, kf_ref, vf_ref, gf_ref, cosf_ref, sinf_ref, of_ref),
            (qb_ref, kb_ref, vb_ref, gb_ref, cosb_ref, sinb_ref, ob_ref))
    inst = []
    for d, (q_ref, k_ref, v_ref, g_ref, cos_ref, sin_ref, o_ref) in enumerate(dirs):
        cos, sin = cos_ref[...], sin_ref[...]
        for bi in range(RET_ROWS):
            q = rot(q_ref[bi], cos, sin).astype(BF16)
            k = rot(k_ref[bi], cos, sin) * (RET_DK ** -0.5)
            kd_t = (k * kdec_ref[d]).T.astype(BF16)
            kb = k.astype(BF16)
            vb = v_ref[bi].astype(BF16)
            for h in range(RET_HEADS):
                ksl = slice(h * RET_DK, (h + 1) * RET_DK)
                vsl = slice(h * RET_DV, (h + 1) * RET_DV)
                inst.append(dict(d=d, bi=bi, h=h, vsl=vsl, q=q[:, ksl], k=kb[:, ksl], v=vb[:, vsl],
                                 kd=kd_t[ksl, :], g_ref=g_ref, o_ref=o_ref))
    s_old = [s_ref[s["bi"], s["d"], s["h"]] for s in inst]
    atts = [_dot_nt(s["q"], s["k"]) * dmat_ref[s["d"], s["h"]] for s in inst]
    inter = [_dot(s["q"], so.astype(BF16)) * qdec_ref[s["d"], s["h"]] for s, so in zip(inst, s_old)]
    kvs = [_dot(s["kd"], s["v"]) for s in inst]
    outs = [_dot(att.astype(BF16), s["v"]) + it for s, att, it in zip(inst, atts, inter)]
    for s, so, kv, o in zip(inst, s_old, kvs, outs):
        s_ref[s["bi"], s["d"], s["h"]] = so * cdec[s["h"]] + kv
        oc = o - jnp.mean(o, axis=-1, keepdims=True)
        y = oc * lax.rsqrt(jnp.mean(oc * oc, axis=-1, keepdims=True) + HEAD_NORM_EPS)
        s["o_ref"][s["bi"], :, s["vsl"]] = y * _silu(s["g_ref"][s["bi"], :, s["vsl"]])


def _retention(p3, cos_t, sin_t, consts, ctx_len):
    b, l, _ = p3.shape
    n_all = l // RET_CHUNK
    n_ctx = ctx_len // RET_CHUNK
    dmat, qdec, kdec, cdec = consts
    ch = functools.partial(_scan_chunk, n_ctx=n_ctx, n_all=n_all)
    qb, vb = RET_OFF // 256, RET_OFF // 512

    rows = RET_ROWS
    assert b % rows == 0

    def specs(d):
        return [
            pl.BlockSpec((rows, RET_CHUNK, 256), lambda bi, j: (bi, ch(d, j), qb)),
            pl.BlockSpec((rows, RET_CHUNK, 256), lambda bi, j: (bi, ch(d, j), qb + 1)),
            pl.BlockSpec((rows, RET_CHUNK, 512), lambda bi, j: (bi, ch(d, j), vb + 1)),
            pl.BlockSpec((rows, RET_CHUNK, 512), lambda bi, j: (bi, ch(d, j), vb + 2 + d)),
            pl.BlockSpec((RET_CHUNK, 256), lambda bi, j: (ch(d, j), 0)),
            pl.BlockSpec((RET_CHUNK, 256), lambda bi, j: (ch(d, j), 0)),
        ]

    const = lambda shape: pl.BlockSpec(shape, lambda bi, j: (0,) * len(shape))
    out = jax.ShapeDtypeStruct((b, l, RET_W), F32)
    out_spec = lambda d: pl.BlockSpec((rows, RET_CHUNK, RET_W), lambda bi, j: (bi, ch(d, j), 0))
    return pl.pallas_call(
        functools.partial(_ret_pair_kernel, cdec=cdec),
        grid=(b // rows, n_all),
        in_specs=specs(0) + specs(1) + [
            const((2, RET_HEADS, RET_CHUNK, RET_CHUNK)), const((2, RET_HEADS, RET_CHUNK, RET_DV)),
            const((2, RET_CHUNK, 256))],
        out_specs=[out_spec(0), out_spec(1)],
        out_shape=[out, out],
        scratch_shapes=[pltpu.VMEM((rows, 2, RET_HEADS, RET_DK, RET_DV), F32)],
        compiler_params=_params("arbitrary", "arbitrary"),
        name="retention",
    )(p3, p3, p3, p3, cos_t, sin_t, p3, p3, p3, p3, cos_t, sin_t, dmat, qdec, kdec)


def _retention_consts():
    gamma = 1.0 - jnp.exp2(-5.0 - jnp.arange(RET_HEADS, dtype=F32))
    lg = jnp.log(gamma)
    idx = jnp.arange(RET_CHUNK, dtype=F32)
    diff = idx[:, None] - idx[None, :]
    lower = jnp.where(diff >= 0, jnp.exp(lg[:, None, None] * jnp.maximum(diff, 0.0)), 0.0)
    dmat = jnp.stack([lower, jnp.swapaxes(lower, 1, 2)])
    qd_f = jnp.exp((idx + 1.0)[None, :] * lg[:, None])
    qd_b = jnp.exp((RET_CHUNK - idx)[None, :] * lg[:, None])
    qdec = jnp.broadcast_to(jnp.stack([qd_f, qd_b])[..., None], (2, RET_HEADS, RET_CHUNK, RET_DV))
    kd_f = jnp.exp((RET_CHUNK - 1.0 - idx)[None, :] * lg[:, None])
    kd_b = jnp.exp(idx[None, :] * lg[:, None])
    kdec = jnp.repeat(jnp.swapaxes(jnp.stack([kd_f, kd_b]), 1, 2), RET_DK, axis=2)
    gam64 = 1.0 - np.exp2(-5.0 - np.arange(RET_HEADS))
    cdec = tuple(float(np.float32(np.exp(RET_CHUNK * np.log(np.float32(gv))))) for gv in gam64)
    return dmat, qdec, kdec, cdec


def _mla_prep_kernel(p_ref, qng_ref, wuq_ref, kvg_ref, wuk_ref, wuv_ref, qg_ref, kg_ref,
                     cos_ref, sin_ref, q_out, k_out, v_out):
    blk = p_ref[...]
    cq = blk[:, 0:MLA_Q_RANK]
    ckv = blk[:, MLA_Q_RANK:MLA_Q_RANK + MLA_KV_RANK]
    kpe = blk[:, MLA_Q_RANK + MLA_KV_RANK:]

    def rms(x, g, n):
        return x * lax.rsqrt(jnp.sum(x * x, axis=-1, keepdims=True) * (1.0 / n) + NORM_EPS) * g

    qn = rms(cq, qng_ref[...], MLA_Q_RANK).astype(BF16)
    kvn = rms(ckv, kvg_ref[...], MLA_KV_RANK).astype(BF16)
    q_raw = _dot(qn, wuq_ref[...])
    k_raw = _dot(kvn, wuk_ref[...])
    ones_col = (lax.broadcasted_iota(jnp.int32, (1, MLA_PAD_W), 1) & 127) == MLA_DV
    v_out[...] = (_dot(kvn, wuv_ref[...]) + jnp.where(ones_col, 1.0, 0.0)).astype(BF16)
    cos = cos_ref[...]
    sin = sin_ref[...]
    tm = blk.shape[0]
    first = (lax.broadcasted_iota(jnp.int32, (tm, 128), 1) & 15) < 8

    def rope(x):
        partner = jnp.where(first, pltpu.roll(x, 120, 1), pltpu.roll(x, 8, 1))
        return x * cos + partner * sin

    heads = [slice(h * 128, (h + 1) * 128) for h in range(MLA_HEADS)]
    qs = [rms(q_raw[:, sl], qg_ref[...], MLA_QK) for sl in heads]
    ks = [rms(k_raw[:, sl] + kpe, kg_ref[...], MLA_QK) for sl in heads]
    for sl, qh, kh in zip(heads, qs, ks):
        q_out[:, sl] = (rope(qh) * ATTN_Q_SCALE).astype(BF16)
        k_out[:, sl] = rope(kh).astype(BF16)


def _mla_prep(p2, w, cos_t, sin_t, rows_per_b):
    t = p2.shape[0]
    tiles_per_b = rows_per_b // ROW_TILE
    const = lambda shape: pl.BlockSpec(shape, lambda i: (0,) * len(shape))
    wide = pl.BlockSpec((ROW_TILE, MLA_PAD_W), lambda i: (i, 0))
    out = jax.ShapeDtypeStruct((t, MLA_PAD_W), BF16)
    return pl.pallas_call(
        _mla_prep_kernel,
        grid=(t // ROW_TILE,),
        in_specs=[
            pl.BlockSpec((ROW_TILE, 512), lambda i: (i, MLA_OFF // 512)),
            const((1, MLA_Q_RANK)), const((MLA_Q_RANK, MLA_PAD_W)),
            const((1, MLA_KV_RANK)), const((MLA_KV_RANK, MLA_PAD_W)), const((MLA_KV_RANK, MLA_PAD_W)),
            const((1, 128)), const((1, 128)),
            pl.BlockSpec((ROW_TILE, 128), lambda i: (i % tiles_per_b, 0)),
            pl.BlockSpec((ROW_TILE, 128), lambda i: (i % tiles_per_b, 0)),
        ],
        out_specs=[wide, wide, wide],
        out_shape=[out, out, out],
        compiler_params=_params("arbitrary"),
        name="mla_prep",
    )(p2, w["qng"], w["wuq"], w["kvg"], w["wuk"], w["wuv"], w["qg"], w["kg"], cos_t, sin_t)


def _mla_weights(q_norm_g, w_uq, kv_norm_g, w_ukv, qk_q_g, qk_k_g):
    wq = w_uq.reshape(MLA_Q_RANK, MLA_HEADS, MLA_QK)
    wq = jnp.pad(wq, ((0, 0), (0, 0), (0, 128 - MLA_QK))).reshape(MLA_Q_RANK, MLA_PAD_W)
    wkv = w_ukv.reshape(MLA_KV_RANK, MLA_HEADS, MLA_NOPE + MLA_DV)
    wk = jnp.pad(wkv[:, :, :MLA_NOPE], ((0, 0), (0, 0), (0, 128 - MLA_NOPE))).reshape(MLA_KV_RANK, MLA_PAD_W)
    wv = jnp.pad(wkv[:, :, MLA_NOPE:], ((0, 0), (0, 0), (0, 128 - MLA_DV))).reshape(MLA_KV_RANK, MLA_PAD_W)
    padg = lambda g: jnp.pad(g, (0, 128 - MLA_QK)).reshape(1, 128)
    return dict(qng=q_norm_g.reshape(1, -1), wuq=wq.astype(BF16), kvg=kv_norm_g.reshape(1, -1),
                wuk=wk.astype(BF16), wuv=wv.astype(BF16), qg=padg(qk_q_g), kg=padg(qk_k_g))


def _attn_kernel(q_ref, k_ref, v_ref, o_ref, *, ctx_len, ctx_tiles):
    def attend(kv_len):
        hs = [slice(hh * 128, (hh + 1) * 128) for hh in range(ATTN_HEADS_PER_STEP)]
        ss = [_dot_nt(q_ref[:, sl], k_ref[0:kv_len, sl]) for sl in hs]
        ps = [jnp.exp2((s - jnp.max(s, axis=-1, keepdims=True)).astype(BF16)) for s in ss]
        o_aug = [_dot(p, v_ref[0:kv_len, sl]) for p, sl in zip(ps, hs)]
        outs = [o[:, 0:MLA_DV] / o[:, MLA_DV:MLA_DV + 1] for o in o_aug]
        o_ref[...] = jnp.concatenate(outs, axis=-1).astype(BF16)

    is_ctx = pl.program_id(2) < ctx_tiles

    @pl.when(is_ctx)
    def _():
        attend(ctx_len)

    @pl.when(jnp.logical_not(is_ctx))
    def _():
        attend(k_ref.shape[0])


def _attention(q3, k3, v3, ctx_len):
    b, l, _ = q3.shape
    tq = ROW_TILE
    hps = ATTN_HEADS_PER_STEP
    return pl.pallas_call(
        functools.partial(_attn_kernel, ctx_len=ctx_len, ctx_tiles=ctx_len // tq),
        grid=(b, MLA_HEADS // hps, l // tq),
        in_specs=[
            pl.BlockSpec((None, tq, hps * 128), lambda bi, h, i: (bi, i, h)),
            pl.BlockSpec((None, l, hps * 128), lambda bi, h, i: (bi, 0, h), pipeline_mode=pl.Buffered(1)),
            pl.BlockSpec((None, l, hps * 128), lambda bi, h, i: (bi, 0, h), pipeline_mode=pl.Buffered(1)),
        ],
        out_specs=pl.BlockSpec((None, tq, hps * MLA_DV), lambda bi, h, i: (bi, i, h)),
        out_shape=jax.ShapeDtypeStruct((b, l, MLA_W), BF16),
        compiler_params=_params("arbitrary", "arbitrary", "arbitrary"),
        name="mla_attention",
    )(q3, k3, v3)


def _rwkv_elem_kernel(t_ref, prev_ref, next_ref, mup_ref, mun_ref, w0_ref, w2_ref, a0_ref, a2_ref,
                      g2_ref, kk_ref, ka_ref, rk_ref, blk_ref,
                      r_out, v_out, na_out, kd_out, bb_out, lw_out, g_out, bonus_out,
                      *, tiles_per_b, ctx_tiles):
    i = pl.program_id(0) % tiles_per_b
    starts = jnp.logical_or(i == 0, i == ctx_tiles)
    ends = jnp.logical_or(i == ctx_tiles - 1, i == tiles_per_b - 1)
    t = t_ref[...]
    tm = t.shape[0]
    prev_row = jnp.where(starts, 0.0, prev_ref[7:8, :])
    next_row = jnp.where(ends, 0.0, next_ref[0:1, :])
    rid = lax.broadcasted_iota(jnp.int32, t.shape, 0)
    prev = jnp.where(rid == 0, prev_row, pltpu.roll(t, 1, 0))
    nxt = jnp.where(rid == tm - 1, next_row, pltpu.roll(t, tm - 1, 0))
    t = t + mup_ref[...] * (prev - t) + mun_ref[...] * (nxt - t)

    w = RWKV_W
    r = t[:, 0:w]
    k = t[:, w:2 * w]
    v = t[:, 2 * w:3 * w]
    o = 3 * w
    wh = (t[:, o:o + 64], t[:, o + 64:o + 128])
    ah = (t[:, o + 128:o + 192], t[:, o + 192:o + 256])
    gh = t[:, o + 256:o + 384]
    blk = blk_ref[...]

    kk = k * kk_ref[...]
    kk = kk * lax.rsqrt(jnp.maximum(_dot_exact_rhs01(kk * kk, blk), 1e-24))
    r_out[...] = r
    v_out[...] = v
    na_out[...] = -kk
    g_out[...] = _dot(_sigmoid(gh).astype(BF16), g2_ref[...])
    rk = r * rk_ref[...]
    bonus = jnp.zeros_like(v)
    for d in range(2):
        z = w0_ref[d] + _dot(jnp.tanh(wh[d]).astype(BF16), w2_ref[d])
        nz = -z
        softplus = jnp.maximum(nz, 0.0) + jnp.log(1.0 + jnp.exp(-jnp.abs(nz)))
        lw_out[d] = -jnp.exp(-softplus - 0.5)
        a = _sigmoid(a0_ref[d] + _dot(ah[d].astype(BF16), a2_ref[d]))
        kd = k * (1.0 + (a - 1.0) * ka_ref[...])
        kd_out[d] = kd
        bb_out[d] = kk * a
        bonus = bonus + _dot_exact_rhs01(rk * kd, blk) * v
    bonus_out[...] = bonus


def _rwkv_elem(p2, w, rows_per_b, ctx_len):
    t = p2.shape[0]
    tm = ROW_TILE
    tiles_per_b = rows_per_b // tm
    n8 = t // 8
    cb = RWKV_OFF // RWKV_IN
    const = lambda shape: pl.BlockSpec(shape, lambda i: (0,) * len(shape))
    one = pl.BlockSpec((tm, RWKV_W), lambda i: (i, 0))
    two = pl.BlockSpec((2, tm, RWKV_W), lambda i: (0, i, 0))
    s1 = jax.ShapeDtypeStruct((t, RWKV_W), F32)
    s2 = jax.ShapeDtypeStruct((2, t, RWKV_W), F32)
    return pl.pallas_call(
        functools.partial(_rwkv_elem_kernel, tiles_per_b=tiles_per_b, ctx_tiles=ctx_len // tm),
        grid=(t // tm,),
        in_specs=[
            pl.BlockSpec((tm, RWKV_IN), lambda i: (i, cb)),
            pl.BlockSpec((8, RWKV_IN), lambda i: (jnp.maximum(i * (tm // 8) - 1, 0), cb)),
            pl.BlockSpec((8, RWKV_IN), lambda i: (jnp.minimum((i + 1) * (tm // 8), n8 - 1), cb)),
            const((1, RWKV_IN)), const((1, RWKV_IN)),
            const((2, 1, RWKV_W)), const((2, RWKV_DECAY_LORA, RWKV_W)),
            const((2, 1, RWKV_W)), const((2, RWKV_A_LORA, RWKV_W)),
            const((RWKV_GATE_LORA, RWKV_W)),
            const((1, RWKV_W)), const((1, RWKV_W)), const((1, RWKV_W)),
            const((RWKV_W, RWKV_W)),
        ],
        out_specs=[one, one, one, two, two, two, one, one],
        out_shape=[s1, s1, s1, s2, s2, s2, s1, s1],
        compiler_params=_params("arbitrary"),
        name="rwkv_elem",
    )(p2, p2, p2, w["mup"], w["mun"], w["w0"], w["w2"], w["a0"], w["a2"], w["g2"],
      w["kk"], w["ka"], w["rk"], w["blk"])


def _block_rows(x, bm):
    return jnp.where(bm, jnp.concatenate([x, x, x, x], axis=0), 0.0)


def _rwkv_chunk_kernel(rf_ref, vf_ref, naf_ref, kdf_ref, bbf_ref, lwf_ref,
                       rb_ref, vb_ref, nab_ref, kdb_ref, bbb_ref, lwb_ref,
                       tri_ref, ms_ref, mi_ref, of_ref, ob_ref, h_ref):
    @pl.when(pl.program_id(1) == 0)
    def _():
        h_ref[...] = jnp.zeros_like(h_ref)

    dirs = ((rf_ref, vf_ref, naf_ref, kdf_ref, bbf_ref, lwf_ref, of_ref),
            (rb_ref, vb_ref, nab_ref, kdb_ref, bbb_ref, lwb_ref, ob_ref))
    c = RWKV_CHUNK
    assert c == 64 and RWKV_HD == 64
    bm = (lax.broadcasted_iota(jnp.int32, (HALF, HALF), 0) >> 6
          == lax.broadcasted_iota(jnp.int32, (HALF, HALF), 1) >> 6)
    eye_p = ((lax.broadcasted_iota(jnp.int32, (c, HALF), 1) & (c - 1))
             == lax.broadcasted_iota(jnp.int32, (c, HALF), 0))
    eye_s = (lax.broadcasted_iota(jnp.int32, (HALF, HALF), 0)
             == lax.broadcasted_iota(jnp.int32, (HALF, HALF), 1))
    bf = lambda x: x.astype(BF16)
    blk = lambda x: _block_rows(x, bm)
    cat0 = lambda xs: jnp.concatenate(xs, axis=0)
    cat1 = lambda xs: jnp.concatenate(xs, axis=1)

    chains = [(d, bi) + dirs[d] for d in range(2) for bi in range(RWKV_ROWS)]
    cums = [_dot_exact_lhs01(tri_ref[d], lw_ref[bi]) for d, bi, _, _, _, _, _, lw_ref, _ in chains]
    inst = []
    for (d, bi, r_ref, v_ref, na_ref, kd_ref, bb_ref, lw_ref, o_ref), cum in zip(chains, cums):
        lw = lw_ref[bi]
        tot = cum[c - 1:c, :] if d == 0 else cum[0:1, :]
        e_neg = jnp.exp(-cum)
        e_rel = jnp.exp(tot - cum)
        e_tot = jnp.exp(tot)
        a_t = na_ref[bi] * jnp.exp(cum - lw)
        r_t = r_ref[bi] * jnp.exp(cum)
        kd, bb, v = kd_ref[bi], bb_ref[bi], v_ref[bi]
        for g in range(2):
            sl = slice(g * HALF, (g + 1) * HALF)
            inst.append(dict(
                d=d, bi=bi, g=g, sl=sl, o_ref=o_ref, a=a_t[:, sl], r=bf(r_t[:, sl]), v=v[:, sl],
                bt=bb[:, sl] * e_neg[:, sl], kt=kd[:, sl] * e_neg[:, sl],
                bk_p=bf(cat0([bb[:, sl] * e_rel[:, sl], kd[:, sl] * e_rel[:, sl]])), etot=e_tot[:, sl]))
    strict = [ms_ref[0] > 0.5, ms_ref[1] > 0.5]
    incl = [mi_ref[0] > 0.5, mi_ref[1] > 0.5]

    bigs = [_dot_nt(bf(cat0([s["a"], s["r"].astype(F32)])), bf(cat0([blk(s["bt"]), blk(s["kt"])])))
            for s in inst]
    for s, big in zip(inst, bigs):
        s["a_ab"] = jnp.where(strict[s["d"]], big[0:c, 0:HALF], 0.0)
        s["a_ak"] = bf(jnp.where(strict[s["d"]], big[0:c, HALF:], 0.0))
        s["a_rb"] = bf(jnp.where(incl[s["d"]], big[c:, 0:HALF], 0.0))
        s["a_rk"] = bf(jnp.where(incl[s["d"]], big[c:, HALF:], 0.0))
        s["v_bd"] = bf(blk(s["v"]))
        s["t"] = jnp.where(eye_p, 1.0, 0.0) + s["a_ab"]
    pws = [_dot(bf(s["a_ab"]), bf(blk(s["a_ab"]))) for s in inst]
    akvs = [_dot(s["a_ak"], s["v_bd"]) for s in inst]
    arkv = [_dot(s["a_rk"], s["v_bd"]) for s in inst]
    for it in range(5):
        prods = [_dot(bf(cat0([s["t"], pw])), bf(blk(pw))) for s, pw in zip(inst, pws)]
        for s, prod in zip(inst, prods):
            s["t"] = s["t"] + prod[0:c]
        pws = [prod[c:] for prod in prods]
    wus = [_dot(bf(s["t"]), bf(cat1([blk(s["a"]), blk(akv)]))) for s, akv in zip(inst, akvs)]
    h_f32 = [h_ref[s["bi"], s["d"], s["g"]] for s in inst]
    h_old = [bf(h) for h in h_f32]
    us = [_dot(bf(wu[:, 0:HALF]), h) + wu[:, HALF:] for wu, h in zip(wus, h_old)]
    rhs = [_dot(s["r"], h) for s, h in zip(inst, h_old)]
    arbu = [_dot(s["a_rb"], bf(blk(u))) for s, u in zip(inst, us)]
    incs = [_dot_tn(s["bk_p"], bf(cat0([u, s["v"]]))) for s, u in zip(inst, us)]
    for s, rh, au, ak, inc, h in zip(inst, rhs, arbu, arkv, incs, h_f32):
        s["o_ref"][s["bi"], :, s["sl"]] = rh + au + ak
        decay_col = jnp.sum(jnp.where(eye_s, s["etot"], 0.0), axis=1, keepdims=True)
        h_ref[s["bi"], s["d"], s["g"]] = decay_col * h + jnp.where(bm, inc, 0.0)


def _rwkv_chunk(e, b, l, ctx_len, consts):
    c = RWKV_CHUNK
    rows = RWKV_ROWS
    assert b % rows == 0
    n_all, n_ctx = l // c, ctx_len // c
    ch = functools.partial(_scan_chunk, n_ctx=n_ctx, n_all=n_all)
    tri, ms, mi = consts
    r3, v3, na3 = (x.reshape(b, l, RWKV_W) for x in (e["r"], e["v"], e["na"]))
    kd4, bb4, lw4 = (x.reshape(2, b, l, RWKV_W) for x in (e["kd"], e["bb"], e["lw"]))

    def specs(d):
        one = pl.BlockSpec((rows, c, RWKV_W), lambda bi, j: (bi, ch(d, j), 0))
        two = pl.BlockSpec((None, rows, c, RWKV_W), lambda bi, j: (d, bi, ch(d, j), 0))
        return one, two

    one_f, two_f = specs(0)
    one_b, two_b = specs(1)
    const = lambda shape: pl.BlockSpec(shape, lambda bi, j: (0,) * len(shape))
    out = jax.ShapeDtypeStruct((b, l, RWKV_W), F32)
    return pl.pallas_call(
        _rwkv_chunk_kernel,
        grid=(b // rows, n_all),
        in_specs=[one_f, one_f, one_f, two_f, two_f, two_f,
                  one_b, one_b, one_b, two_b, two_b, two_b,
                  const((2, c, c)), const((2, c, HALF)), const((2, c, HALF))],
        out_specs=[one_f, one_b],
        out_shape=[out, out],
        scratch_shapes=[pltpu.VMEM((rows, 2, 2, HALF, HALF), F32)],
        compiler_params=_params("arbitrary", "arbitrary"),
        name="rwkv_chunk",
    )(r3, v3, na3, kd4, bb4, lw4, r3, v3, na3, kd4, bb4, lw4, tri, ms, mi)


def _rwkv_consts():
    c = RWKV_CHUNK
    idx = np.arange(c)
    lower = idx[None, :] <= idx[:, None]
    tri = np.stack([lower, lower.T]).astype(np.float32)
    s_in_head = np.tile(idx, HALF // c)[None, :]
    t_row = idx[:, None]
    ms = np.stack([s_in_head < t_row, s_in_head > t_row]).astype(np.float32)
    mi = np.stack([s_in_head <= t_row, s_in_head >= t_row]).astype(np.float32)
    return jnp.asarray(tri, BF16), jnp.asarray(ms), jnp.asarray(mi)


def _merge_kernel(x_ref, mod_ref, gate_ref, retf_ref, retb_ref, mla_ref, yf_ref, yb_ref, bonus_ref, g_ref,
                  lng_ref, lnb_ref, blk_ref, wr_ref, wm_ref, ww_ref, wo_ref, o_ref):
    blk = blk_ref[...]
    ret = retf_ref[...] + retb_ref[...]
    y = yf_ref[...] + yb_ref[...]
    mean = _dot_exact_rhs01(y, blk) * (1.0 / RWKV_HD)
    yc = y - mean
    var = _dot_exact_rhs01(yc * yc, blk) * (1.0 / RWKV_HD)
    y = yc * lax.rsqrt(var + RWKV_GN_EPS) * lng_ref[...] + lnb_ref[...] + bonus_ref[...]
    rwk = y * g_ref[...]
    gate = gate_ref[...]
    d = D_MODEL
    mix = (_sigmoid(gate[:, 0:d]) * _dot(ret.astype(BF16), wr_ref[...])
           + _sigmoid(gate[:, d:2 * d]) * _dot(mla_ref[...], wm_ref[...])
           + _sigmoid(gate[:, 2 * d:]) * _dot(rwk.astype(BF16), ww_ref[...]))
    o_ref[...] = x_ref[...] + mod_ref[2:3, :] * _dot(mix.astype(BF16), wo_ref[...])


def _merge(x2, mod_l, p2, retf, retb, mla, yf, yb, bonus, g, w, geom):
    t = x2.shape[0]
    tm = ROW_TILE
    tpb, ctx_tiles, batch = geom
    row = functools.partial(_mod_row, tiles_per_b=tpb, ctx_tiles=ctx_tiles, batch=batch)
    const = lambda shape: pl.BlockSpec(shape, lambda i: (0,) * len(shape))
    rows = lambda width: pl.BlockSpec((tm, width), lambda i: (i, 0))
    return pl.pallas_call(
        _merge_kernel,
        grid=(t // tm,),
        in_specs=[
            rows(D_MODEL),
            pl.BlockSpec((None, 6, D_MODEL), lambda i: (row(i), 0, 0)),
            rows(3 * D_MODEL),
            rows(RET_W), rows(RET_W), rows(MLA_W), rows(RWKV_W), rows(RWKV_W), rows(RWKV_W), rows(RWKV_W),
            const((1, RWKV_W)), const((1, RWKV_W)), const((RWKV_W, RWKV_W)),
            const((RET_W, D_MODEL)), const((MLA_W, D_MODEL)), const((RWKV_W, D_MODEL)),
            const((D_MODEL, D_MODEL)),
        ],
        out_specs=rows(D_MODEL),
        out_shape=jax.ShapeDtypeStruct((t, D_MODEL), F32),
        compiler_params=_params("arbitrary"),
        name="merge",
    )(x2, mod_l, p2, retf, retb, mla, yf, yb, bonus, g, w["lng"], w["lnb"], w["blk"],
      w["wr"], w["wm"], w["ww"], w["wo"])


def _route_kernel(x_ref, mod_ref, g_ref, wr_ref, br_ref, tri_ref,
                  h_out, idx_out, rank_out, gate_out, cnt_out, run_ref):
    @pl.when(pl.program_id(0) == 0)
    def _():
        run_ref[...] = jnp.zeros_like(run_ref)

    x = x_ref[...]
    y = x * lax.rsqrt(jnp.mean(x * x, axis=-1, keepdims=True) + NORM_EPS) * g_ref[...]
    h = y * (1.0 + mod_ref[4:5, :]) + mod_ref[3:4, :]
    h_out[...] = h
    logits = _dot(h.astype(BF16), wr_ref[...]) + br_ref[...]
    lane = lax.broadcasted_iota(jnp.int32, logits.shape, 1).astype(F32)
    vals, idxs = [], []
    for _ in range(TOP_K):
        m = jnp.max(logits, axis=-1, keepdims=True)
        sel = jnp.min(jnp.where(logits == m, lane, 128.0), axis=-1, keepdims=True)
        vals.append(m)
        idxs.append(sel)
        logits = jnp.where(lane == sel, -jnp.inf, logits)
    es = [jnp.exp(vv - vals[0]) for vv in vals]
    den = es[0] + es[1] + es[2] + es[3]
    chosen = jnp.zeros(lane.shape, F32)
    for kq in range(TOP_K):
        chosen = jnp.where(lane == idxs[kq], 1.0, chosen)
    before = _dot(tri_ref[...], chosen.astype(BF16)) + run_ref[...]
    idx_row = jnp.zeros(lane.shape, F32)
    rank_row = jnp.zeros(lane.shape, F32)
    gate_row = jnp.zeros(lane.shape, F32)
    for kq in range(TOP_K):
        rk = jnp.sum(jnp.where(lane == idxs[kq], before, 0.0), axis=-1, keepdims=True)
        idx_row = jnp.where(lane == float(kq), idxs[kq], idx_row)
        rank_row = jnp.where(lane == float(kq), rk, rank_row)
        gate_row = jnp.where(lane == float(kq), es[kq] / den, gate_row)
    idx_out[...] = idx_row.astype(jnp.int32)
    rank_out[...] = rank_row.astype(jnp.int32)
    gate_out[...] = gate_row
    run_ref[...] += jnp.sum(chosen, axis=0, keepdims=True)
    cnt_out[...] = run_ref[...]


def _route(x2, mod_l, g, wr_pad, br_pad, tri, geom):
    t = x2.shape[0]
    tm = ROW_TILE
    tpb, ctx_tiles, batch = geom
    row = functools.partial(_mod_row, tiles_per_b=tpb, ctx_tiles=ctx_tiles, batch=batch)
    lanes = lambda dt: jax.ShapeDtypeStruct((t, 128), dt)
    tile = pl.BlockSpec((tm, 128), lambda i: (i, 0))
    return pl.pallas_call(
        _route_kernel,
        grid=(t // tm,),
        in_specs=[
            pl.BlockSpec((tm, D_MODEL), lambda i: (i, 0)),
            pl.BlockSpec((None, 6, D_MODEL), lambda i: (row(i), 0, 0)),
            pl.BlockSpec((1, D_MODEL), lambda i: (0, 0)),
            pl.BlockSpec((D_MODEL, 128), lambda i: (0, 0)),
            pl.BlockSpec((1, 128), lambda i: (0, 0)),
            pl.BlockSpec((tm, tm), lambda i: (0, 0)),
        ],
        out_specs=[pl.BlockSpec((tm, D_MODEL), lambda i: (i, 0)), tile, tile, tile,
                   pl.BlockSpec((1, 128), lambda i: (0, 0))],
        out_shape=[jax.ShapeDtypeStruct((t, D_MODEL), F32), lanes(jnp.int32), lanes(jnp.int32),
                   lanes(F32), jax.ShapeDtypeStruct((1, 128), F32)],
        scratch_shapes=[pltpu.VMEM((1, 128), F32)],
        compiler_params=_params("arbitrary"),
        name="moe_router",
    )(x2, mod_l, g.reshape(1, D_MODEL), wr_pad, br_pad, tri)


def _moe_layout(idx, rank, counts, t):
    n_rows = -(-(t * TOP_K + N_EXPERTS * (MOE_TILE - 1)) // MOE_TILE) * MOE_TILE
    n_tiles = n_rows // MOE_TILE
    padded = (counts + MOE_TILE - 1) // MOE_TILE * MOE_TILE
    pad_end = jnp.cumsum(padded)
    pad_start = pad_end - padded
    onehot = idx[..., None] == jnp.arange(N_EXPERTS, dtype=jnp.int32)
    dest = jnp.sum(jnp.where(onehot, pad_start, 0), axis=-1) + rank
    tile_end = pad_end // MOE_TILE
    tile_expert = jnp.sum(jnp.arange(n_tiles, dtype=jnp.int32)[:, None] >= tile_end[None, :], axis=1)
    tile_expert = jnp.minimum(tile_expert, N_EXPERTS - 1).astype(jnp.int32)
    n_used = tile_end[-1:].astype(jnp.int32)
    return dest.astype(jnp.int32).reshape(t // ROW_TILE, 1, ROW_TILE * TOP_K), tile_expert, n_used, n_rows


def _dispatch_kernel(dest_ref, h_ref, xs_in_ref, xs_ref, sem):
    del xs_in_ref
    tm = h_ref.shape[0]

    def issue(tok, carry):
        src = h_ref.at[pl.ds(tok, 1), :]
        for kq in range(TOP_K):
            row = dest_ref[0, tok * TOP_K + kq]
            pltpu.make_async_copy(src, xs_ref.at[pl.ds(row, 1), :], sem.at[0]).start()
        return carry

    lax.fori_loop(0, tm, issue, 0, unroll=4)
    for _ in range(TOP_K):
        pltpu.make_async_copy(h_ref, xs_ref.at[pl.ds(0, tm), :], sem.at[0]).wait()


def _dispatch(dest, h2, xs_prev):
    t = h2.shape[0]
    tm = ROW_TILE
    n_rows = xs_prev.shape[0]
    return pl.pallas_call(
        _dispatch_kernel,
        grid=(t // tm,),
        in_specs=[
            pl.BlockSpec((None, 1, tm * TOP_K), lambda i: (i, 0, 0), memory_space=pltpu.SMEM),
            pl.BlockSpec((tm, D_MODEL), lambda i: (i, 0)),
            pl.BlockSpec(memory_space=pl.ANY),
        ],
        out_specs=pl.BlockSpec(memory_space=pl.ANY),
        out_shape=jax.ShapeDtypeStruct((n_rows, D_MODEL), F32),
        scratch_shapes=[pltpu.SemaphoreType.DMA((1,))],
        input_output_aliases={2: 0},
        compiler_params=_params("arbitrary"),
        name="moe_dispatch",
    )(dest, h2, xs_prev)


def _expert_kernel(te_ref, nu_ref, x_ref, wgu_ref, bgu_ref, wd_ref, bd_ref, o_ref, wgu_bf, wd_bf):
    g = pl.program_id(0)
    used = g < nu_ref[0]

    @pl.when(jnp.logical_not(used))
    def _():
        o_ref[...] = jnp.zeros_like(o_ref)

    @pl.when(jnp.logical_and(used, jnp.logical_or(g == 0, te_ref[g] != te_ref[jnp.maximum(g - 1, 0)])))
    def _():
        wgu_bf[...] = wgu_ref[...].astype(BF16)
        wd_bf[...] = wd_ref[...].astype(BF16)

    @pl.when(used)
    def _():
        gu = _dot(x_ref[...].astype(BF16), wgu_bf[...]) + bgu_ref[...]
        f = gu.shape[1] // 2
        g_lin = jnp.minimum(gu[:, 0:f], SWIGLU_LIMIT)
        up = jnp.clip(gu[:, f:], -SWIGLU_LIMIT, SWIGLU_LIMIT)
        act = g_lin * _sigmoid(SWIGLU_ALPHA * g_lin) * (up + 1.0)
        o_ref[...] = _dot(act.astype(BF16), wd_bf[...]) + bd_ref[...]


def _experts(xs, tile_expert, n_used, wgu, bgu, wd, bd):
    n_rows = xs.shape[0]
    f2 = wgu.shape[2]
    used = lambda g, nu: jnp.minimum(g, nu[0] - 1)
    grid_spec = pltpu.PrefetchScalarGridSpec(
        num_scalar_prefetch=2,
        grid=(n_rows // MOE_TILE,),
        in_specs=[
            pl.BlockSpec((MOE_TILE, D_MODEL), lambda g, te, nu: (used(g, nu), 0)),
            pl.BlockSpec((None, D_MODEL, f2), lambda g, te, nu: (te[used(g, nu)], 0, 0)),
            pl.BlockSpec((None, 1, f2), lambda g, te, nu: (te[used(g, nu)], 0, 0)),
            pl.BlockSpec((None, f2 // 2, D_MODEL), lambda g, te, nu: (te[used(g, nu)], 0, 0)),
            pl.BlockSpec((None, 1, D_MODEL), lambda g, te, nu: (te[used(g, nu)], 0, 0)),
        ],
        out_specs=pl.BlockSpec((MOE_TILE, D_MODEL), lambda g, te, nu: (g, 0)),
        scratch_shapes=[pltpu.VMEM((D_MODEL, f2), BF16), pltpu.VMEM((f2 // 2, D_MODEL), BF16)],
    )
    return pl.pallas_call(
        _expert_kernel,
        grid_spec=grid_spec,
        out_shape=jax.ShapeDtypeStruct((n_rows, D_MODEL), F32),
        compiler_params=_params("arbitrary"),
        name="moe_experts",
    )(tile_expert, n_used, xs, wgu, bgu, wd, bd)


def _collect_kernel(dest_ref, dest_next_ref, gate_ref, x_ref, mod_ref, y_ref, o_ref, buf_ref, sem):
    tm = x_ref.shape[0]
    step = pl.program_id(0)
    slot = step % 2

    def issue_tile(d_ref, to_slot):
        def issue(tok, carry):
            for kq in range(TOP_K):
                row = d_ref[0, tok * TOP_K + kq]
                pltpu.make_async_copy(y_ref.at[pl.ds(row, 1), :],
                                      buf_ref.at[to_slot, kq, pl.ds(tok, 1), :], sem.at[to_slot]).start()
            return carry

        lax.fori_loop(0, tm, issue, 0, unroll=4)

    @pl.when(step == 0)
    def _():
        issue_tile(dest_ref, slot)

    @pl.when(step + 1 < pl.num_programs(0))
    def _():
        issue_tile(dest_next_ref, 1 - slot)

    for kq in range(TOP_K):
        pltpu.make_async_copy(y_ref.at[pl.ds(0, tm), :], buf_ref.at[slot, kq], sem.at[slot]).wait()
    gate = gate_ref[...]
    acc = gate[:, 0:1] * buf_ref[slot, 0]
    for kq in range(1, TOP_K):
        acc = acc + gate[:, kq:kq + 1] * buf_ref[slot, kq]
    o_ref[...] = x_ref[...] + mod_ref[5:6, :] * acc


def _collect(dest, gate_rows, x2, mod_l, ys, geom, latent_only):
    t = x2.shape[0]
    tm = ROW_TILE
    tpb, ctx_tiles, batch = geom
    row = functools.partial(_mod_row, tiles_per_b=tpb, ctx_tiles=ctx_tiles, batch=batch)
    if latent_only:
        lat = tpb - ctx_tiles
        n_steps = batch * lat
        tile = lambda i: (i // lat) * tpb + ctx_tiles + i % lat
    else:
        n_steps = t // tm
        tile = lambda i: i
    nxt = lambda i: tile(jnp.minimum(i + 1, n_steps - 1))
    return pl.pallas_call(
        _collect_kernel,
        grid=(n_steps,),
        in_specs=[
            pl.BlockSpec((None, 1, tm * TOP_K), lambda i: (tile(i), 0, 0), memory_space=pltpu.SMEM),
            pl.BlockSpec((None, 1, tm * TOP_K), lambda i: (nxt(i), 0, 0), memory_space=pltpu.SMEM),
            pl.BlockSpec((tm, 128), lambda i: (tile(i), 0)),
            pl.BlockSpec((tm, D_MODEL), lambda i: (tile(i), 0)),
            pl.BlockSpec((None, 6, D_MODEL), lambda i: (row(tile(i)), 0, 0)),
            pl.BlockSpec(memory_space=pl.ANY),
        ],
        out_specs=pl.BlockSpec((tm, D_MODEL), lambda i: (i, 0)),
        out_shape=jax.ShapeDtypeStruct((n_steps * tm, D_MODEL), F32),
        scratch_shapes=[pltpu.VMEM((2, TOP_K, tm, D_MODEL), F32), pltpu.SemaphoreType.DMA((2,))],
        compiler_params=_params("arbitrary"),
        name="moe_collect",
    )(dest, dest, gate_rows, x2, mod_l, ys)


def _rope_angles(pos, dim):
    inv = ROPE_BASE ** (-jnp.arange(0, dim, 2, dtype=F32) / dim)
    return pos.astype(F32)[:, None] * inv[None, :]


def _rope_tables(seq, ctx_len):
    pos = jnp.arange(seq, dtype=jnp.int32)
    ang = _rope_angles(pos, RET_DK)
    r_cos = jnp.tile(jnp.concatenate([jnp.cos(ang), jnp.cos(ang)], -1), (1, RET_HEADS))
    r_sin = jnp.tile(jnp.concatenate([-jnp.sin(ang), jnp.sin(ang)], -1), (1, RET_HEADS))
    ra = _rope_angles(pos // GRID_W, MLA_ROPE // 2)
    ca = _rope_angles(pos % GRID_W, MLA_ROPE // 2)
    one = jnp.ones((seq, MLA_NOPE), F32)
    zero = jnp.zeros((seq, MLA_NOPE), F32)
    m_cos = jnp.concatenate([one, jnp.cos(ra), jnp.cos(ra), jnp.cos(ca), jnp.cos(ca), one[:, :32]], -1)
    m_sin = jnp.concatenate([zero, -jnp.sin(ra), jnp.sin(ra), -jnp.sin(ca), jnp.sin(ca), zero[:, :32]], -1)
    ctx1 = lambda n: jnp.ones((ctx_len, n), F32)
    ctx0 = lambda n: jnp.zeros((ctx_len, n), F32)
    return (jnp.concatenate([ctx1(256), r_cos], 0), jnp.concatenate([ctx0(256), r_sin], 0),
            jnp.concatenate([ctx1(128), m_cos], 0), jnp.concatenate([ctx0(128), m_sin], 0))


def kernel(x, c, ctx, c_ctx, ada_w, ada_b, norm1_g, norm2_g, w_in, mla_q_norm_g, mla_w_uq, mla_kv_norm_g, mla_w_ukv, mla_qk_q_g, mla_qk_k_g, rwkv_mu_prev, rwkv_mu_next, rwkv_w0, rwkv_w2, rwkv_a0, rwkv_a2, rwkv_g2, rwkv_k_k, rwkv_k_a, rwkv_r_k, rwkv_lnx_g, rwkv_lnx_b, w_br_ret, w_br_mla, w_br_rwkv, w_out, w_router, b_router, w_gu, b_gu, w_down, b_down):
    b, s, d = x.shape
    ctx_len = ctx.shape[1]
    depth = ada_w.shape[0]
    l = ctx_len + s
    t = b * l
    assert d == D_MODEL and b < 16 and ctx_len % ROW_TILE == 0 and s % ROW_TILE == 0
    geom = (l // ROW_TILE, ctx_len // ROW_TILE, b)

    xa = jnp.concatenate([ctx, x], axis=1).reshape(t, d)
    cc = jnp.zeros((16, d), F32).at[:b].set(c).at[b].set(c_ctx)
    mod = _ada_all(cc, ada_w.astype(BF16), ada_b).reshape(depth, 16, 6, d)

    ret_cos, ret_sin, mla_cos, mla_sin = _rope_tables(s, ctx_len)
    ret_consts = _retention_consts()
    rwkv_consts = _rwkv_consts()
    head_blk = jnp.asarray(np.kron(np.eye(RWKV_HEADS), np.ones((RWKV_HD, RWKV_HD))), BF16)
    route_tri = jnp.asarray(np.tril(np.ones((ROW_TILE, ROW_TILE)), -1), BF16)

    xs = None
    for li in range(depth):
        p2 = _inproj(xa, mod[li], norm1_g[li], _pad_w_in(w_in[li]), geom)
        p3 = p2.reshape(b, l, P_COLS)

        retf, retb = (r.reshape(t, RET_W) for r in _retention(p3, ret_cos, ret_sin, ret_consts, ctx_len))

        mw = _mla_weights(mla_q_norm_g[li], mla_w_uq[li], mla_kv_norm_g[li], mla_w_ukv[li],
                          mla_qk_q_g[li], mla_qk_k_g[li])
        q2, k2, v2 = (a.reshape(b, l, MLA_PAD_W) for a in _mla_prep(p2, mw, mla_cos, mla_sin, l))
        mla = _attention(q2, k2, v2, ctx_len).reshape(t, MLA_W)

        rw = dict(mup=rwkv_mu_prev[li].reshape(1, -1), mun=rwkv_mu_next[li].reshape(1, -1),
                  w0=rwkv_w0[li].reshape(2, 1, RWKV_W), w2=rwkv_w2[li].astype(BF16),
                  a0=rwkv_a0[li].reshape(2, 1, RWKV_W), a2=rwkv_a2[li].astype(BF16),
                  g2=rwkv_g2[li].astype(BF16), kk=rwkv_k_k[li].reshape(1, -1),
                  ka=rwkv_k_a[li].reshape(1, -1), rk=rwkv_r_k[li].reshape(1, -1), blk=head_blk)
        r_, v_, na_, kd_, bb_, lw_, g_, bonus_ = _rwkv_elem(p2, rw, l, ctx_len)
        yf, yb = _rwkv_chunk(dict(r=r_, v=v_, na=na_, kd=kd_, bb=bb_, lw=lw_), b, l, ctx_len, rwkv_consts)

        mg = dict(lng=rwkv_lnx_g[li].reshape(1, -1), lnb=rwkv_lnx_b[li].reshape(1, -1), blk=head_blk,
                  wr=w_br_ret[li].astype(BF16), wm=w_br_mla[li].astype(BF16),
                  ww=w_br_rwkv[li].astype(BF16), wo=w_out[li].astype(BF16))
        xa = _merge(xa, mod[li], p2, retf, retb, mla, yf.reshape(t, RWKV_W), yb.reshape(t, RWKV_W), bonus_, g_,
                    mg, geom)

        wr_pad = jnp.pad(w_router[li], ((0, 0), (0, 128 - N_EXPERTS))).astype(BF16)
        br_pad = jnp.concatenate([b_router[li], jnp.full((128 - N_EXPERTS,), -1e30, F32)]).reshape(1, 128)
        h2, idx_rows, rank_rows, gate_rows, cnt = _route(xa, mod[li], norm2_g[li], wr_pad, br_pad,
                                                         route_tri, geom)
        counts = cnt[0, :N_EXPERTS].astype(jnp.int32)
        dest, tile_expert, n_used, n_rows = _moe_layout(idx_rows[:, :TOP_K], rank_rows[:, :TOP_K], counts, t)
        if xs is None:
            xs = jnp.zeros((n_rows, D_MODEL), F32)
        xs = _dispatch(dest, h2, xs)
        ys = _experts(xs, tile_expert, n_used, w_gu[li], b_gu[li].reshape(N_EXPERTS, 1, -1),
                      w_down[li], b_down[li].reshape(N_EXPERTS, 1, -1))
        xa = _collect(dest, gate_rows, xa, mod[li], ys, geom, latent_only=li == depth - 1)

    return xa.reshape(b, s, d)
```

```python
import functools

import jax
import jax.numpy as jnp
import numpy as np
from jax import lax
from jax.experimental import pallas as pl
from jax.experimental.pallas import tpu as pltpu

F32 = jnp.float32
BF16 = jnp.bfloat16

D_MODEL = 1024
GRID_W = 64
RET_HEADS, RET_DK, RET_DV, RET_CHUNK = 4, 64, 128, 128
RET_W = RET_HEADS * RET_DV
MLA_HEADS, MLA_Q_RANK, MLA_KV_RANK, MLA_NOPE, MLA_ROPE, MLA_DV = 8, 256, 128, 64, 32, 64
MLA_QK = MLA_NOPE + MLA_ROPE
MLA_W = MLA_HEADS * MLA_DV
MLA_PAD_W = MLA_HEADS * 128
RWKV_HEADS, RWKV_HD = 8, 64
RWKV_W = RWKV_HEADS * RWKV_HD
RWKV_DECAY_LORA, RWKV_A_LORA, RWKV_GATE_LORA = 64, 64, 128
RWKV_GN_EPS = 64e-5
RWKV_IN = 3 * RWKV_W + 2 * RWKV_DECAY_LORA + 2 * RWKV_A_LORA + RWKV_GATE_LORA
N_EXPERTS, TOP_K = 32, 4
SWIGLU_LIMIT, SWIGLU_ALPHA = 7.0, 1.702
ROPE_BASE = 10000.0
NORM_EPS = 1e-6
HEAD_NORM_EPS = 1e-5

GATE_OFF = 0
RET_OFF = 3 * D_MODEL
MLA_OFF = RET_OFF + 2048
RWKV_OFF = MLA_OFF + 512 + 128
P_COLS = RWKV_OFF + RWKV_IN
IN_COL_TILE = 3840

ROW_TILE = 256
RET_ROWS = 1
RWKV_CHUNK = 64
RWKV_ROWS = 4
HALF = 256
ATTN_Q_SCALE = MLA_QK ** -0.5 * 1.4426950408889634
ATTN_HEADS_PER_STEP = 4
MOE_TILE = 256
VMEM_LIMIT = 48 * 1024 * 1024

NT_DIMS = (((1,), (1,)), ((), ()))
TN_DIMS = (((0,), (0,)), ((), ()))


def _params(*sem):
    return pltpu.CompilerParams(dimension_semantics=sem, vmem_limit_bytes=VMEM_LIMIT)


def _dot(a, b):
    return jnp.dot(a, b, preferred_element_type=F32)


def _dot_nt(a, b):
    return lax.dot_general(a, b, NT_DIMS, preferred_element_type=F32)


def _dot_tn(a, b):
    return lax.dot_general(a, b, TN_DIMS, preferred_element_type=F32)


def _split3(x):
    hi = x.astype(BF16)
    r1 = x - hi.astype(F32)
    mid = r1.astype(BF16)
    lo = (r1 - mid.astype(F32)).astype(BF16)
    return hi, mid, lo


def _dot_exact_rhs01(x, m01):
    hi = x.astype(BF16)
    lo = (x - hi.astype(F32)).astype(BF16)
    return _dot(hi, m01) + _dot(lo, m01)


def _dot_exact_lhs01(m01, x):
    hi, mid, lo = _split3(x)
    return _dot(m01, hi) + _dot(m01, mid) + _dot(m01, lo)


def _sigmoid(x):
    return 1.0 / (1.0 + jnp.exp(-x))


def _silu(x):
    return x * _sigmoid(x)


def _ada_kernel(c_ref, w_ref, b_ref, o_ref):
    s = _silu(c_ref[...])
    o_ref[...] = _dot(s.astype(BF16), w_ref[...]) + b_ref[...]


def _ada_all(cc, ada_w_bf, ada_b):
    depth = ada_w_bf.shape[0]
    tn = 1536
    return pl.pallas_call(
        _ada_kernel,
        grid=(depth, 6 * D_MODEL // tn),
        in_specs=[
            pl.BlockSpec((16, D_MODEL), lambda l, j: (0, 0)),
            pl.BlockSpec((None, D_MODEL, tn), lambda l, j: (l, 0, j)),
            pl.BlockSpec((None, 1, tn), lambda l, j: (l, 0, j)),
        ],
        out_specs=pl.BlockSpec((None, 16, tn), lambda l, j: (l, 0, j)),
        out_shape=jax.ShapeDtypeStruct((depth, 16, 6 * D_MODEL), F32),
        compiler_params=_params("arbitrary", "arbitrary"),
        name="ada_mod",
    )(cc, ada_w_bf, ada_b.reshape(depth, 1, 6 * D_MODEL))


def _mod_row(i, tiles_per_b, ctx_tiles, batch):
    return jnp.where(i % tiles_per_b < ctx_tiles, batch, i // tiles_per_b)


def _inproj_kernel(x_ref, mod_ref, g_ref, w_ref, o_ref):
    x = x_ref[...]
    y = x * lax.rsqrt(jnp.mean(x * x, axis=-1, keepdims=True) + NORM_EPS) * g_ref[...]
    h = y * (1.0 + mod_ref[1:2, :]) + mod_ref[0:1, :]
    o_ref[...] = _dot(h.astype(BF16), w_ref[...])


def _inproj(x2, mod_l, g, w_pad, geom):
    t = x2.shape[0]
    tpb, ctx_tiles, batch = geom
    row = functools.partial(_mod_row, tiles_per_b=tpb, ctx_tiles=ctx_tiles, batch=batch)
    return pl.pallas_call(
        _inproj_kernel,
        grid=(P_COLS // IN_COL_TILE, t // ROW_TILE),
        in_specs=[
            pl.BlockSpec((ROW_TILE, D_MODEL), lambda j, i: (i, 0)),
            pl.BlockSpec((None, 6, D_MODEL), lambda j, i: (row(i), 0, 0)),
            pl.BlockSpec((1, D_MODEL), lambda j, i: (0, 0)),
            pl.BlockSpec((D_MODEL, IN_COL_TILE), lambda j, i: (0, j)),
        ],
        out_specs=pl.BlockSpec((ROW_TILE, IN_COL_TILE), lambda j, i: (i, j)),
        out_shape=jax.ShapeDtypeStruct((t, P_COLS), F32),
        compiler_params=_params("arbitrary", "arbitrary"),
        name="in_proj",
    )(x2, mod_l, g.reshape(1, D_MODEL), w_pad)


def _pad_w_in(w_in):
    d = w_in.shape[0]
    ret = w_in[:, 0:2048]
    cq = w_in[:, 2048:2304]
    ckv = w_in[:, 2304:2432]
    kpe = w_in[:, 2432:2464]
    rwkv = w_in[:, 2464:2464 + RWKV_IN]
    gate = w_in[:, 2464 + RWKV_IN:]
    z = lambda n: jnp.zeros((d, n), w_in.dtype)
    return jnp.concatenate([gate, ret, cq, ckv, z(64), kpe, z(32), z(128), rwkv], axis=1).astype(BF16)


def _scan_chunk(d, j, n_ctx, n_all):
    bwd = jnp.where(j < n_ctx, n_ctx - 1 - j, n_all - 1 - j + n_ctx)
    return jnp.where(d == 0, j, bwd)


def _ret_pair_kernel(qf_ref, kf_ref, vf_ref, gf_ref, cosf_ref, sinf_ref,
                     qb_ref, kb_ref, vb_ref, gb_ref, cosb_ref, sinb_ref,
                     dmat_ref, qdec_ref, kdec_ref, of_ref, ob_ref, s_ref, *, cdec):
    @pl.when(pl.program_id(1) == 0)
    def _():
        s_ref[...] = jnp.zeros_like(s_ref)

    c = RET_CHUNK
    w = RET_HEADS * RET_DK
    first = (lax.broadcasted_iota(jnp.int32, (c, w), 1) & (RET_DK - 1)) < RET_DK // 2

    def rot(x, cos, sin):
        partner = jnp.where(first, pltpu.roll(x, w - RET_DK // 2, 1), pltpu.roll(x, RET_DK // 2, 1))
        return x * cos + partner * sin

    dirs = ((qf_ref, kf_ref, vf_ref, gf_ref, cosf_ref, sinf_ref, of_ref),
            (qb_ref, kb_ref, vb_ref, gb_ref, cosb_ref, sinb_ref, ob_ref))
    inst = []
    for d, (q_ref, k_ref, v_ref, g_ref, cos_ref, sin_ref, o_ref) in enumerate(dirs):
        cos, sin = cos_ref[...], sin_ref[...]
        for bi in range(RET_ROWS):
            q = rot(q_ref[bi], cos, sin).astype(BF16)
            k = rot(k_ref[bi], cos, sin) * (RET_DK ** -0.5)
            kd_t = (k * kdec_ref[d]).T.astype(BF16)
            kb = k.astype(BF16)
            vb = v_ref[bi].astype(BF16)
            for h in range(RET_HEADS):
                ksl = slice(h * RET_DK, (h + 1) * RET_DK)
                vsl = slice(h * RET_DV, (h + 1) * RET_DV)
                inst.append(dict(d=d, bi=bi, h=h, vsl=vsl, q=q[:, ksl], k=kb[:, ksl], v=vb[:, vsl],
                                 kd=kd_t[ksl, :], g_ref=g_ref, o_ref=o_ref))
    s_old = [s_ref[s["bi"], s["d"], s["h"]] for s in inst]
    atts = [_dot_nt(s["q"], s["k"]) * dmat_ref[s["d"], s["h"]] for s in inst]
    inter = [_dot(s["q"], so.astype(BF16)) * qdec_ref[s["d"], s["h"]] for s, so in zip(inst, s_old)]
    kvs = [_dot(s["kd"], s["v"]) for s in inst]
    outs = [_dot(att.astype(BF16), s["v"]) + it for s, att, it in zip(inst, atts, inter)]
    for s, so, kv, o in zip(inst, s_old, kvs, outs):
        s_ref[s["bi"], s["d"], s["h"]] = so * cdec[s["h"]] + kv
        oc = o - jnp.mean(o, axis=-1, keepdims=True)
        y = oc * lax.rsqrt(jnp.mean(oc * oc, axis=-1, keepdims=True) + HEAD_NORM_EPS)
        s["o_ref"][s["bi"], :, s["vsl"]] = y * _silu(s["g_ref"][s["bi"], :, s["vsl"]])


def _retention(p3, cos_t, sin_t, consts, ctx_len):
    b, l, _ = p3.shape
    n_all = l // RET_CHUNK
    n_ctx = ctx_len // RET_CHUNK
    dmat, qdec, kdec, cdec = consts
    ch = functools.partial(_scan_chunk, n_ctx=n_ctx, n_all=n_all)
    qb, vb = RET_OFF // 256, RET_OFF // 512

    rows = RET_ROWS
    assert b % rows == 0

    def specs(d):
        return [
            pl.BlockSpec((rows, RET_CHUNK, 256), lambda bi, j: (bi, ch(d, j), qb)),
            pl.BlockSpec((rows, RET_CHUNK, 256), lambda bi, j: (bi, ch(d, j), qb + 1)),
            pl.BlockSpec((rows, RET_CHUNK, 512), lambda bi, j: (bi, ch(d, j), vb + 1)),
            pl.BlockSpec((rows, RET_CHUNK, 512), lambda bi, j: (bi, ch(d, j), vb + 2 + d)),
            pl.BlockSpec((RET_CHUNK, 256), lambda bi, j: (ch(d, j), 0)),
            pl.BlockSpec((RET_CHUNK, 256), lambda bi, j: (ch(d, j), 0)),
        ]

    const = lambda shape: pl.BlockSpec(shape, lambda bi, j: (0,) * len(shape))
    out = jax.ShapeDtypeStruct((b, l, RET_W), F32)
    out_spec = lambda d: pl.BlockSpec((rows, RET_CHUNK, RET_W), lambda bi, j: (bi, ch(d, j), 0))
    return pl.pallas_call(
        functools.partial(_ret_pair_kernel, cdec=cdec),
        grid=(b // rows, n_all),
        in_specs=specs(0) + specs(1) + [
            const((2, RET_HEADS, RET_CHUNK, RET_CHUNK)), const((2, RET_HEADS, RET_CHUNK, RET_DV)),
            const((2, RET_CHUNK, 256))],
        out_specs=[out_spec(0), out_spec(1)],
        out_shape=[out, out],
        scratch_shapes=[pltpu.VMEM((rows, 2, RET_HEADS, RET_DK, RET_DV), F32)],
        compiler_params=_params("arbitrary", "arbitrary"),
        name="retention",
    )(p3, p3, p3, p3, cos_t, sin_t, p3, p3, p3, p3, cos_t, sin_t, dmat, qdec, kdec)


def _retention_consts():
    gamma = 1.0 - jnp.exp2(-5.0 - jnp.arange(RET_HEADS, dtype=F32))
    lg = jnp.log(gamma)
    idx = jnp.arange(RET_CHUNK, dtype=F32)
    diff = idx[:, None] - idx[None, :]
    lower = jnp.where(diff >= 0, jnp.exp(lg[:, None, None] * jnp.maximum(diff, 0.0)), 0.0)
    dmat = jnp.stack([lower, jnp.swapaxes(lower, 1, 2)])
    qd_f = jnp.exp((idx + 1.0)[None, :] * lg[:, None])
    qd_b = jnp.exp((RET_CHUNK - idx)[None, :] * lg[:, None])
    qdec = jnp.broadcast_to(jnp.stack([qd_f, qd_b])[..., None], (2, RET_HEADS, RET_CHUNK, RET_DV))
    kd_f = jnp.exp((RET_CHUNK - 1.0 - idx)[None, :] * lg[:, None])
    kd_b = jnp.exp(idx[None, :] * lg[:, None])
    kdec = jnp.repeat(jnp.swapaxes(jnp.stack([kd_f, kd_b]), 1, 2), RET_DK, axis=2)
    gam64 = 1.0 - np.exp2(-5.0 - np.arange(RET_HEADS))
    cdec = tuple(float(np.float32(np.exp(RET_CHUNK * np.log(np.float32(gv))))) for gv in gam64)
    return dmat, qdec, kdec, cdec


def _mla_prep_kernel(p_ref, qng_ref, wuq_ref, kvg_ref, wuk_ref, wuv_ref, qg_ref, kg_ref,
                     cos_ref, sin_ref, q_out, k_out, v_out):
    blk = p_ref[...]
    cq = blk[:, 0:MLA_Q_RANK]
    ckv = blk[:, MLA_Q_RANK:MLA_Q_RANK + MLA_KV_RANK]
    kpe = blk[:, MLA_Q_RANK + MLA_KV_RANK:]

    def rms(x, g, n):
        return x * lax.rsqrt(jnp.sum(x * x, axis=-1, keepdims=True) * (1.0 / n) + NORM_EPS) * g

    qn = rms(cq, qng_ref[...], MLA_Q_RANK).astype(BF16)
    kvn = rms(ckv, kvg_ref[...], MLA_KV_RANK).astype(BF16)
    q_raw = _dot(qn, wuq_ref[...])
    k_raw = _dot(kvn, wuk_ref[...])
    ones_col = (lax.broadcasted_iota(jnp.int32, (1, MLA_PAD_W), 1) & 127) == MLA_DV
    v_out[...] = (_dot(kvn, wuv_ref[...]) + jnp.where(ones_col, 1.0, 0.0)).astype(BF16)
    cos = cos_ref[...]
    sin = sin_ref[...]
    tm = blk.shape[0]
    first = (lax.broadcasted_iota(jnp.int32, (tm, 128), 1) & 15) < 8

    def rope(x):
        partner = jnp.where(first, pltpu.roll(x, 120, 1), pltpu.roll(x, 8, 1))
        return x * cos + partner * sin

    heads = [slice(h * 128, (h + 1) * 128) for h in range(MLA_HEADS)]
    qs = [rms(q_raw[:, sl], qg_ref[...], MLA_QK) for sl in heads]
    ks = [rms(k_raw[:, sl] + kpe, kg_ref[...], MLA_QK) for sl in heads]
    for sl, qh, kh in zip(heads, qs, ks):
        q_out[:, sl] = (rope(qh) * ATTN_Q_SCALE).astype(BF16)
        k_out[:, sl] = rope(kh).astype(BF16)


def _mla_prep(p2, w, cos_t, sin_t, rows_per_b):
    t = p2.shape[0]
    tiles_per_b = rows_per_b // ROW_TILE
    const = lambda shape: pl.BlockSpec(shape, lambda i: (0,) * len(shape))
    wide = pl.BlockSpec((ROW_TILE, MLA_PAD_W), lambda i: (i, 0))
    out = jax.ShapeDtypeStruct((t, MLA_PAD_W), BF16)
    return pl.pallas_call(
        _mla_prep_kernel,
        grid=(t // ROW_TILE,),
        in_specs=[
            pl.BlockSpec((ROW_TILE, 512), lambda i: (i, MLA_OFF // 512)),
            const((1, MLA_Q_RANK)), const((MLA_Q_RANK, MLA_PAD_W)),
            const((1, MLA_KV_RANK)), const((MLA_KV_RANK, MLA_PAD_W)), const((MLA_KV_RANK, MLA_PAD_W)),
            const((1, 128)), const((1, 128)),
            pl.BlockSpec((ROW_TILE, 128), lambda i: (i % tiles_per_b, 0)),
            pl.BlockSpec((ROW_TILE, 128), lambda i: (i % tiles_per_b, 0)),
        ],
        out_specs=[wide, wide, wide],
        out_shape=[out, out, out],
        compiler_params=_params("arbitrary"),
        name="mla_prep",
    )(p2, w["qng"], w["wuq"], w["kvg"], w["wuk"], w["wuv"], w["qg"], w["kg"], cos_t, sin_t)


def _mla_weights(q_norm_g, w_uq, kv_norm_g, w_ukv, qk_q_g, qk_k_g):
    wq = w_uq.reshape(MLA_Q_RANK, MLA_HEADS, MLA_QK)
    wq = jnp.pad(wq, ((0, 0), (0, 0), (0, 128 - MLA_QK))).reshape(MLA_Q_RANK, MLA_PAD_W)
    wkv = w_ukv.reshape(MLA_KV_RANK, MLA_HEADS, MLA_NOPE + MLA_DV)
    wk = jnp.pad(wkv[:, :, :MLA_NOPE], ((0, 0), (0, 0), (0, 128 - MLA_NOPE))).reshape(MLA_KV_RANK, MLA_PAD_W)
    wv = jnp.pad(wkv[:, :, MLA_NOPE:], ((0, 0), (0, 0), (0, 128 - MLA_DV))).reshape(MLA_KV_RANK, MLA_PAD_W)
    padg = lambda g: jnp.pad(g, (0, 128 - MLA_QK)).reshape(1, 128)
    return dict(qng=q_norm_g.reshape(1, -1), wuq=wq.astype(BF16), kvg=kv_norm_g.reshape(1, -1),
                wuk=wk.astype(BF16), wuv=wv.astype(BF16), qg=padg(qk_q_g), kg=padg(qk_k_g))


def _attn_kernel(q_ref, k_ref, v_ref, o_ref, *, ctx_len, ctx_tiles):
    def attend(kv_len):
        hs = [slice(hh * 128, (hh + 1) * 128) for hh in range(ATTN_HEADS_PER_STEP)]
        ss = [_dot_nt(q_ref[:, sl], k_ref[0:kv_len, sl]) for sl in hs]
        ps = [jnp.exp2((s - jnp.max(s, axis=-1, keepdims=True)).astype(BF16)) for s in ss]
        o_aug = [_dot(p, v_ref[0:kv_len, sl]) for p, sl in zip(ps, hs)]
        outs = [o[:, 0:MLA_DV] / o[:, MLA_DV:MLA_DV + 1] for o in o_aug]
        o_ref[...] = jnp.concatenate(outs, axis=-1).astype(BF16)

    is_ctx = pl.program_id(2) < ctx_tiles

    @pl.when(is_ctx)
    def _():
        attend(ctx_len)

    @pl.when(jnp.logical_not(is_ctx))
    def _():
        attend(k_ref.shape[0])


def _attention(q3, k3, v3, ctx_len):
    b, l, _ = q3.shape
    tq = ROW_TILE
    hps = ATTN_HEADS_PER_STEP
    return pl.pallas_call(
        functools.partial(_attn_kernel, ctx_len=ctx_len, ctx_tiles=ctx_len // tq),
        grid=(b, MLA_HEADS // hps, l // tq),
        in_specs=[
            pl.BlockSpec((None, tq, hps * 128), lambda bi, h, i: (bi, i, h)),
            pl.BlockSpec((None, l, hps * 128), lambda bi, h, i: (bi, 0, h), pipeline_mode=pl.Buffered(1)),
            pl.BlockSpec((None, l, hps * 128), lambda bi, h, i: (bi, 0, h), pipeline_mode=pl.Buffered(1)),
        ],
        out_specs=pl.BlockSpec((None, tq, hps * MLA_DV), lambda bi, h, i: (bi, i, h)),
        out_shape=jax.ShapeDtypeStruct((b, l, MLA_W), BF16),
        compiler_params=_params("arbitrary", "arbitrary", "arbitrary"),
        name="mla_attention",
    )(q3, k3, v3)


def _rwkv_elem_kernel(t_ref, prev_ref, next_ref, mup_ref, mun_ref, w0_ref, w2_ref, a0_ref, a2_ref,
                      g2_ref, kk_ref, ka_ref, rk_ref, blk_ref,
                      r_out, v_out, na_out, kd_out, bb_out, lw_out, g_out, bonus_out,
                      *, tiles_per_b, ctx_tiles):
    i = pl.program_id(0) % tiles_per_b
    starts = jnp.logical_or(i == 0, i == ctx_tiles)
    ends = jnp.logical_or(i == ctx_tiles - 1, i == tiles_per_b - 1)
    t = t_ref[...]
    tm = t.shape[0]
    prev_row = jnp.where(starts, 0.0, prev_ref[7:8, :])
    next_row = jnp.where(ends, 0.0, next_ref[0:1, :])
    rid = lax.broadcasted_iota(jnp.int32, t.shape, 0)
    prev = jnp.where(rid == 0, prev_row, pltpu.roll(t, 1, 0))
    nxt = jnp.where(rid == tm - 1, next_row, pltpu.roll(t, tm - 1, 0))
    t = t + mup_ref[...] * (prev - t) + mun_ref[...] * (nxt - t)

    w = RWKV_W
    r = t[:, 0:w]
    k = t[:, w:2 * w]
    v = t[:, 2 * w:3 * w]
    o = 3 * w
    wh = (t[:, o:o + 64], t[:, o + 64:o + 128])
    ah = (t[:, o + 128:o + 192], t[:, o + 192:o + 256])
    gh = t[:, o + 256:o + 384]
    blk = blk_ref[...]

    kk = k * kk_ref[...]
    kk = kk * lax.rsqrt(jnp.maximum(_dot_exact_rhs01(kk * kk, blk), 1e-24))
    r_out[...] = r
    v_out[...] = v
    na_out[...] = -kk
    g_out[...] = _dot(_sigmoid(gh).astype(BF16), g2_ref[...])
    rk = r * rk_ref[...]
    bonus = jnp.zeros_like(v)
    for d in range(2):
        z = w0_ref[d] + _dot(jnp.tanh(wh[d]).astype(BF16), w2_ref[d])
        nz = -z
        softplus = jnp.maximum(nz, 0.0) + jnp.log(1.0 + jnp.exp(-jnp.abs(nz)))
        lw_out[d] = -jnp.exp(-softplus - 0.5)
        a = _sigmoid(a0_ref[d] + _dot(ah[d].astype(BF16), a2_ref[d]))
        kd = k * (1.0 + (a - 1.0) * ka_ref[...])
        kd_out[d] = kd
        bb_out[d] = kk * a
        bonus = bonus + _dot_exact_rhs01(rk * kd, blk) * v
    bonus_out[...] = bonus


def _rwkv_elem(p2, w, rows_per_b, ctx_len):
    t = p2.shape[0]
    tm = ROW_TILE
    tiles_per_b = rows_per_b // tm
    n8 = t // 8
    cb = RWKV_OFF // RWKV_IN
    const = lambda shape: pl.BlockSpec(shape, lambda i: (0,) * len(shape))
    one = pl.BlockSpec((tm, RWKV_W), lambda i: (i, 0))
    two = pl.BlockSpec((2, tm, RWKV_W), lambda i: (0, i, 0))
    s1 = jax.ShapeDtypeStruct((t, RWKV_W), F32)
    s2 = jax.ShapeDtypeStruct((2, t, RWKV_W), F32)
    return pl.pallas_call(
        functools.partial(_rwkv_elem_kernel, tiles_per_b=tiles_per_b, ctx_tiles=ctx_len // tm),
        grid=(t // tm,),
        in_specs=[
            pl.BlockSpec((tm, RWKV_IN), lambda i: (i, cb)),
            pl.BlockSpec((8, RWKV_IN), lambda i: (jnp.maximum(i * (tm // 8) - 1, 0), cb)),
            pl.BlockSpec((8, RWKV_IN), lambda i: (jnp.minimum((i + 1) * (tm // 8), n8 - 1), cb)),
            const((1, RWKV_IN)), const((1, RWKV_IN)),
            const((2, 1, RWKV_W)), const((2, RWKV_DECAY_LORA, RWKV_W)),
            const((2, 1, RWKV_W)), const((2, RWKV_A_LORA, RWKV_W)),
            const((RWKV_GATE_LORA, RWKV_W)),
            const((1, RWKV_W)), const((1, RWKV_W)), const((1, RWKV_W)),
            const((RWKV_W, RWKV_W)),
        ],
        out_specs=[one, one, one, two, two, two, one, one],
        out_shape=[s1, s1, s1, s2, s2, s2, s1, s1],
        compiler_params=_params("arbitrary"),
        name="rwkv_elem",
    )(p2, p2, p2, w["mup"], w["mun"], w["w0"], w["w2"], w["a0"], w["a2"], w["g2"],
      w["kk"], w["ka"], w["rk"], w["blk"])


def _block_rows(x, bm):
    return jnp.where(bm, jnp.concatenate([x, x, x, x], axis=0), 0.0)


def _rwkv_chunk_kernel(rf_ref, vf_ref, naf_ref, kdf_ref, bbf_ref, lwf_ref,
                       rb_ref, vb_ref, nab_ref, kdb_ref, bbb_ref, lwb_ref,
                       tri_ref, ms_ref, mi_ref, of_ref, ob_ref, h_ref):
    @pl.when(pl.program_id(1) == 0)
    def _():
        h_ref[...] = jnp.zeros_like(h_ref)

    dirs = ((rf_ref, vf_ref, naf_ref, kdf_ref, bbf_ref, lwf_ref, of_ref),
            (rb_ref, vb_ref, nab_ref, kdb_ref, bbb_ref, lwb_ref, ob_ref))
    c = RWKV_CHUNK
    assert c == 64 and RWKV_HD == 64
    bm = (lax.broadcasted_iota(jnp.int32, (HALF, HALF), 0) >> 6
          == lax.broadcasted_iota(jnp.int32, (HALF, HALF), 1) >> 6)
    eye_p = ((lax.broadcasted_iota(jnp.int32, (c, HALF), 1) & (c - 1))
             == lax.broadcasted_iota(jnp.int32, (c, HALF), 0))
    eye_s = (lax.broadcasted_iota(jnp.int32, (HALF, HALF), 0)
             == lax.broadcasted_iota(jnp.int32, (HALF, HALF), 1))
    bf = lambda x: x.astype(BF16)
    blk = lambda x: _block_rows(x, bm)
    cat0 = lambda xs: jnp.concatenate(xs, axis=0)
    cat1 = lambda xs: jnp.concatenate(xs, axis=1)

    chains = [(d, bi) + dirs[d] for d in range(2) for bi in range(RWKV_ROWS)]
    cums = [_dot_exact_lhs01(tri_ref[d], lw_ref[bi]) for d, bi, _, _, _, _, _, lw_ref, _ in chains]
    inst = []
    for (d, bi, r_ref, v_ref, na_ref, kd_ref, bb_ref, lw_ref, o_ref), cum in zip(chains, cums):
        lw = lw_ref[bi]
        tot = cum[c - 1:c, :] if d == 0 else cum[0:1, :]
        e_neg = jnp.exp(-cum)
        e_rel = jnp.exp(tot - cum)
        e_tot = jnp.exp(tot)
        a_t = na_ref[bi] * jnp.exp(cum - lw)
        r_t = r_ref[bi] * jnp.exp(cum)
        kd, bb, v = kd_ref[bi], bb_ref[bi], v_ref[bi]
        for g in range(2):
            sl = slice(g * HALF, (g + 1) * HALF)
            inst.append(dict(
                d=d, bi=bi, g=g, sl=sl, o_ref=o_ref, a=a_t[:, sl], r=bf(r_t[:, sl]), v=v[:, sl],
                bt=bb[:, sl] * e_neg[:, sl], kt=kd[:, sl] * e_neg[:, sl],
                bk_p=bf(cat0([bb[:, sl] * e_rel[:, sl], kd[:, sl] * e_rel[:, sl]])), etot=e_tot[:, sl]))
    strict = [ms_ref[0] > 0.5, ms_ref[1] > 0.5]
    incl = [mi_ref[0] > 0.5, mi_ref[1] > 0.5]

    bigs = [_dot_nt(bf(cat0([s["a"], s["r"].astype(F32)])), bf(cat0([blk(s["bt"]), blk(s["kt"])])))
            for s in inst]
    for s, big in zip(inst, bigs):
        s["a_ab"] = jnp.where(strict[s["d"]], big[0:c, 0:HALF], 0.0)
        s["a_ak"] = bf(jnp.where(strict[s["d"]], big[0:c, HALF:], 0.0))
        s["a_rb"] = bf(jnp.where(incl[s["d"]], big[c:, 0:HALF], 0.0))
        s["a_rk"] = bf(jnp.where(incl[s["d"]], big[c:, HALF:], 0.0))
        s["v_bd"] = bf(blk(s["v"]))
        s["t"] = jnp.where(eye_p, 1.0, 0.0) + s["a_ab"]
    pws = [_dot(bf(s["a_ab"]), bf(blk(s["a_ab"]))) for s in inst]
    akvs = [_dot(s["a_ak"], s["v_bd"]) for s in inst]
    arkv = [_dot(s["a_rk"], s["v_bd"]) for s in inst]
    for it in range(5):
        prods = [_dot(bf(cat0([s["t"], pw])), bf(blk(pw))) for s, pw in zip(inst, pws)]
        for s, prod in zip(inst, prods):
            s["t"] = s["t"] + prod[0:c]
        pws = [prod[c:] for prod in prods]
    wus = [_dot(bf(s["t"]), bf(cat1([blk(s["a"]), blk(akv)]))) for s, akv in zip(inst, akvs)]
    h_f32 = [h_ref[s["bi"], s["d"], s["g"]] for s in inst]
    h_old = [bf(h) for h in h_f32]
    us = [_dot(bf(wu[:, 0:HALF]), h) + wu[:, HALF:] for wu, h in zip(wus, h_old)]
    rhs = [_dot(s["r"], h) for s, h in zip(inst, h_old)]
    arbu = [_dot(s["a_rb"], bf(blk(u))) for s, u in zip(inst, us)]
    incs = [_dot_tn(s["bk_p"], bf(cat0([u, s["v"]]))) for s, u in zip(inst, us)]
    for s, rh, au, ak, inc, h in zip(inst, rhs, arbu, arkv, incs, h_f32):
        s["o_ref"][s["bi"], :, s["sl"]] = rh + au + ak
        decay_col = jnp.sum(jnp.where(eye_s, s["etot"], 0.0), axis=1, keepdims=True)
        h_ref[s["bi"], s["d"], s["g"]] = decay_col * h + jnp.where(bm, inc, 0.0)


def _rwkv_chunk(e, b, l, ctx_len, consts):
    c = RWKV_CHUNK
    rows = RWKV_ROWS
    assert b % rows == 0
    n_all, n_ctx = l // c, ctx_len // c
    ch = functools.partial(_scan_chunk, n_ctx=n_ctx, n_all=n_all)
    tri, ms, mi = consts
    r3, v3, na3 = (x.reshape(b, l, RWKV_W) for x in (e["r"], e["v"], e["na"]))
    kd4, bb4, lw4 = (x.reshape(2, b, l, RWKV_W) for x in (e["kd"], e["bb"], e["lw"]))

    def specs(d):
        one = pl.BlockSpec((rows, c, RWKV_W), lambda bi, j: (bi, ch(d, j), 0))
        two = pl.BlockSpec((None, rows, c, RWKV_W), lambda bi, j: (d, bi, ch(d, j), 0))
        return one, two

    one_f, two_f = specs(0)
    one_b, two_b = specs(1)
    const = lambda shape: pl.BlockSpec(shape, lambda bi, j: (0,) * len(shape))
    out = jax.ShapeDtypeStruct((b, l, RWKV_W), F32)
    return pl.pallas_call(
        _rwkv_chunk_kernel,
        grid=(b // rows, n_all),
        in_specs=[one_f, one_f, one_f, two_f, two_f, two_f,
                  one_b, one_b, one_b, two_b, two_b, two_b,
                  const((2, c, c)), const((2, c, HALF)), const((2, c, HALF))],
        out_specs=[one_f, one_b],
        out_shape=[out, out],
        scratch_shapes=[pltpu.VMEM((rows, 2, 2, HALF, HALF), F32)],
        compiler_params=_params("arbitrary", "arbitrary"),
        name="rwkv_chunk",
    )(r3, v3, na3, kd4, bb4, lw4, r3, v3, na3, kd4, bb4, lw4, tri, ms, mi)


def _rwkv_consts():
    c = RWKV_CHUNK
    idx = np.arange(c)
    lower = idx[None, :] <= idx[:, None]
    tri = np.stack([lower, lower.T]).astype(np.float32)
    s_in_head = np.tile(idx, HALF // c)[None, :]
    t_row = idx[:, None]
    ms = np.stack([s_in_head < t_row, s_in_head > t_row]).astype(np.float32)
    mi = np.stack([s_in_head <= t_row, s_in_head >= t_row]).astype(np.float32)
    return jnp.asarray(tri, BF16), jnp.asarray(ms), jnp.asarray(mi)


def _merge_kernel(x_ref, mod_ref, gate_ref, retf_ref, retb_ref, mla_ref, yf_ref, yb_ref, bonus_ref, g_ref,
                  lng_ref, lnb_ref, blk_ref, wr_ref, wm_ref, ww_ref, wo_ref, o_ref):
    blk = blk_ref[...]
    ret = retf_ref[...] + retb_ref[...]
    y = yf_ref[...] + yb_ref[...]
    mean = _dot_exact_rhs01(y, blk) * (1.0 / RWKV_HD)
    yc = y - mean
    var = _dot_exact_rhs01(yc * yc, blk) * (1.0 / RWKV_HD)
    y = yc * lax.rsqrt(var + RWKV_GN_EPS) * lng_ref[...] + lnb_ref[...] + bonus_ref[...]
    rwk = y * g_ref[...]
    gate = gate_ref[...]
    d = D_MODEL
    mix = (_sigmoid(gate[:, 0:d]) * _dot(ret.astype(BF16), wr_ref[...])
           + _sigmoid(gate[:, d:2 * d]) * _dot(mla_ref[...], wm_ref[...])
           + _sigmoid(gate[:, 2 * d:]) * _dot(rwk.astype(BF16), ww_ref[...]))
    o_ref[...] = x_ref[...] + mod_ref[2:3, :] * _dot(mix.astype(BF16), wo_ref[...])


def _merge(x2, mod_l, p2, retf, retb, mla, yf, yb, bonus, g, w, geom):
    t = x2.shape[0]
    tm = ROW_TILE
    tpb, ctx_tiles, batch = geom
    row = functools.partial(_mod_row, tiles_per_b=tpb, ctx_tiles=ctx_tiles, batch=batch)
    const = lambda shape: pl.BlockSpec(shape, lambda i: (0,) * len(shape))
    rows = lambda width: pl.BlockSpec((tm, width), lambda i: (i, 0))
    return pl.pallas_call(
        _merge_kernel,
        grid=(t // tm,),
        in_specs=[
            rows(D_MODEL),
            pl.BlockSpec((None, 6, D_MODEL), lambda i: (row(i), 0, 0)),
            rows(3 * D_MODEL),
            rows(RET_W), rows(RET_W), rows(MLA_W), rows(RWKV_W), rows(RWKV_W), rows(RWKV_W), rows(RWKV_W),
            const((1, RWKV_W)), const((1, RWKV_W)), const((RWKV_W, RWKV_W)),
            const((RET_W, D_MODEL)), const((MLA_W, D_MODEL)), const((RWKV_W, D_MODEL)),
            const((D_MODEL, D_MODEL)),
        ],
        out_specs=rows(D_MODEL),
        out_shape=jax.ShapeDtypeStruct((t, D_MODEL), F32),
        compiler_params=_params("arbitrary"),
        name="merge",
    )(x2, mod_l, p2, retf, retb, mla, yf, yb, bonus, g, w["lng"], w["lnb"], w["blk"],
      w["wr"], w["wm"], w["ww"], w["wo"])


def _route_kernel(x_ref, mod_ref, g_ref, wr_ref, br_ref, tri_ref,
                  h_out, idx_out, rank_out, gate_out, cnt_out, run_ref):
    @pl.when(pl.program_id(0) == 0)
    def _():
        run_ref[...] = jnp.zeros_like(run_ref)

    x = x_ref[...]
    y = x * lax.rsqrt(jnp.mean(x * x, axis=-1, keepdims=True) + NORM_EPS) * g_ref[...]
    h = y * (1.0 + mod_ref[4:5, :]) + mod_ref[3:4, :]
    h_out[...] = h
    logits = _dot(h.astype(BF16), wr_ref[...]) + br_ref[...]
    lane = lax.broadcasted_iota(jnp.int32, logits.shape, 1).astype(F32)
    vals, idxs = [], []
    for _ in range(TOP_K):
        m = jnp.max(logits, axis=-1, keepdims=True)
        sel = jnp.min(jnp.where(logits == m, lane, 128.0), axis=-1, keepdims=True)
        vals.append(m)
        idxs.append(sel)
        logits = jnp.where(lane == sel, -jnp.inf, logits)
    es = [jnp.exp(vv - vals[0]) for vv in vals]
    den = es[0] + es[1] + es[2] + es[3]
    chosen = jnp.zeros(lane.shape, F32)
    for kq in range(TOP_K):
        chosen = jnp.where(lane == idxs[kq], 1.0, chosen)
    before = _dot(tri_ref[...], chosen.astype(BF16)) + run_ref[...]
    idx_row = jnp.zeros(lane.shape, F32)
    rank_row = jnp.zeros(lane.shape, F32)
    gate_row = jnp.zeros(lane.shape, F32)
    for kq in range(TOP_K):
        rk = jnp.sum(jnp.where(lane == idxs[kq], before, 0.0), axis=-1, keepdims=True)
        idx_row = jnp.where(lane == float(kq), idxs[kq], idx_row)
        rank_row = jnp.where(lane == float(kq), rk, rank_row)
        gate_row = jnp.where(lane == float(kq), es[kq] / den, gate_row)
    idx_out[...] = idx_row.astype(jnp.int32)
    rank_out[...] = rank_row.astype(jnp.int32)
    gate_out[...] = gate_row
    run_ref[...] += jnp.sum(chosen, axis=0, keepdims=True)
    cnt_out[...] = run_ref[...]


def _route(x2, mod_l, g, wr_pad, br_pad, tri, geom):
    t = x2.shape[0]
    tm = ROW_TILE
    tpb, ctx_tiles, batch = geom
    row = functools.partial(_mod_row, tiles_per_b=tpb, ctx_tiles=ctx_tiles, batch=batch)
    lanes = lambda dt: jax.ShapeDtypeStruct((t, 128), dt)
    tile = pl.BlockSpec((tm, 128), lambda i: (i, 0))
    return pl.pallas_call(
        _route_kernel,
        grid=(t // tm,),
        in_specs=[
            pl.BlockSpec((tm, D_MODEL), lambda i: (i, 0)),
            pl.BlockSpec((None, 6, D_MODEL), lambda i: (row(i), 0, 0)),
            pl.BlockSpec((1, D_MODEL), lambda i: (0, 0)),
            pl.BlockSpec((D_MODEL, 128), lambda i: (0, 0)),
            pl.BlockSpec((1, 128), lambda i: (0, 0)),
            pl.BlockSpec((tm, tm), lambda i: (0, 0)),
        ],
        out_specs=[pl.BlockSpec((tm, D_MODEL), lambda i: (i, 0)), tile, tile, tile,
                   pl.BlockSpec((1, 128), lambda i: (0, 0))],
        out_shape=[jax.ShapeDtypeStruct((t, D_MODEL), F32), lanes(jnp.int32), lanes(jnp.int32),
                   lanes(F32), jax.ShapeDtypeStruct((1, 128), F32)],
        scratch_shapes=[pltpu.VMEM((1, 128), F32)],
        compiler_params=_params("arbitrary"),
        name="moe_router",
    )(x2, mod_l, g.reshape(1, D_MODEL), wr_pad, br_pad, tri)


def _moe_layout(idx, rank, counts, t):
    n_rows = -(-(t * TOP_K + N_EXPERTS * (MOE_TILE - 1)) // MOE_TILE) * MOE_TILE
    n_tiles = n_rows // MOE_TILE
    padded = (counts + MOE_TILE - 1) // MOE_TILE * MOE_TILE
    pad_end = jnp.cumsum(padded)
    pad_start = pad_end - padded
    onehot = idx[..., None] == jnp.arange(N_EXPERTS, dtype=jnp.int32)
    dest = jnp.sum(jnp.where(onehot, pad_start, 0), axis=-1) + rank
    tile_end = pad_end // MOE_TILE
    tile_expert = jnp.sum(jnp.arange(n_tiles, dtype=jnp.int32)[:, None] >= tile_end[None, :], axis=1)
    tile_expert = jnp.minimum(tile_expert, N_EXPERTS - 1).astype(jnp.int32)
    n_used = tile_end[-1:].astype(jnp.int32)
    return dest.astype(jnp.int32).reshape(t // ROW_TILE, 1, ROW_TILE * TOP_K), tile_expert, n_used, n_rows


def _dispatch_kernel(dest_ref, h_ref, xs_in_ref, xs_ref, sem):
    del xs_in_ref
    tm = h_ref.shape[0]

    def issue(tok, carry):
        src = h_ref.at[pl.ds(tok, 1), :]
        for kq in range(TOP_K):
            row = dest_ref[0, tok * TOP_K + kq]
            pltpu.make_async_copy(src, xs_ref.at[pl.ds(row, 1), :], sem.at[0]).start()
        return carry

    lax.fori_loop(0, tm, issue, 0, unroll=4)
    for _ in range(TOP_K):
        pltpu.make_async_copy(h_ref, xs_ref.at[pl.ds(0, tm), :], sem.at[0]).wait()


def _dispatch(dest, h2, xs_prev):
    t = h2.shape[0]
    tm = ROW_TILE
    n_rows = xs_prev.shape[0]
    return pl.pallas_call(
        _dispatch_kernel,
        grid=(t // tm,),
        in_specs=[
            pl.BlockSpec((None, 1, tm * TOP_K), lambda i: (i, 0, 0), memory_space=pltpu.SMEM),
            pl.BlockSpec((tm, D_MODEL), lambda i: (i, 0)),
            pl.BlockSpec(memory_space=pl.ANY),
        ],
        out_specs=pl.BlockSpec(memory_space=pl.ANY),
        out_shape=jax.ShapeDtypeStruct((n_rows, D_MODEL), F32),
        scratch_shapes=[pltpu.SemaphoreType.DMA((1,))],
        input_output_aliases={2: 0},
        compiler_params=_params("arbitrary"),
        name="moe_dispatch",
    )(dest, h2, xs_prev)


def _expert_kernel(te_ref, nu_ref, x_ref, wgu_ref, bgu_ref, wd_ref, bd_ref, o_ref, wgu_bf, wd_bf):
    g = pl.program_id(0)
    used = g < nu_ref[0]

    @pl.when(jnp.logical_not(used))
    def _():
        o_ref[...] = jnp.zeros_like(o_ref)

    @pl.when(jnp.logical_and(used, jnp.logical_or(g == 0, te_ref[g] != te_ref[jnp.maximum(g - 1, 0)])))
    def _():
        wgu_bf[...] = wgu_ref[...].astype(BF16)
        wd_bf[...] = wd_ref[...].astype(BF16)

    @pl.when(used)
    def _():
        gu = _dot(x_ref[...].astype(BF16), wgu_bf[...]) + bgu_ref[...]
        f = gu.shape[1] // 2
        g_lin = jnp.minimum(gu[:, 0:f], SWIGLU_LIMIT)
        up = jnp.clip(gu[:, f:], -SWIGLU_LIMIT, SWIGLU_LIMIT)
        act = g_lin * _sigmoid(SWIGLU_ALPHA * g_lin) * (up + 1.0)
        o_ref[...] = _dot(act.astype(BF16), wd_bf[...]) + bd_ref[...]


def _experts(xs, tile_expert, n_used, wgu, bgu, wd, bd, layer):
    n_rows = xs.shape[0]
    f2 = wgu.shape[3]
    used = lambda g, nu: jnp.minimum(g, nu[0] - 1)
    expert = lambda g, te, nu: (layer, te[used(g, nu)], 0, 0)
    grid_spec = pltpu.PrefetchScalarGridSpec(
        num_scalar_prefetch=2,
        grid=(n_rows // MOE_TILE,),
        in_specs=[
            pl.BlockSpec((MOE_TILE, D_MODEL), lambda g, te, nu: (used(g, nu), 0)),
            pl.BlockSpec((None, None, D_MODEL, f2), expert),
            pl.BlockSpec((None, None, 1, f2), expert),
            pl.BlockSpec((None, None, f2 // 2, D_MODEL), expert),
            pl.BlockSpec((None, None, 1, D_MODEL), expert),
        ],
        out_specs=pl.BlockSpec((MOE_TILE, D_MODEL), lambda g, te, nu: (g, 0)),
        scratch_shapes=[pltpu.VMEM((D_MODEL, f2), BF16), pltpu.VMEM((f2 // 2, D_MODEL), BF16)],
    )
    return pl.pallas_call(
        _expert_kernel,
        grid_spec=grid_spec,
        out_shape=jax.ShapeDtypeStruct((n_rows, D_MODEL), F32),
        compiler_params=_params("arbitrary"),
        name="moe_experts",
    )(tile_expert, n_used, xs, wgu, bgu, wd, bd)


def _collect_kernel(dest_ref, dest_next_ref, gate_ref, x_ref, mod_ref, y_ref, o_ref, buf_ref, sem):
    tm = x_ref.shape[0]
    step = pl.program_id(0)
    slot = step % 2

    def issue_tile(d_ref, to_slot):
        def issue(tok, carry):
            for kq in range(TOP_K):
                row = d_ref[0, tok * TOP_K + kq]
                pltpu.make_async_copy(y_ref.at[pl.ds(row, 1), :],
                                      buf_ref.at[to_slot, kq, pl.ds(tok, 1), :], sem.at[to_slot]).start()
            return carry

        lax.fori_loop(0, tm, issue, 0, unroll=4)

    @pl.when(step == 0)
    def _():
        issue_tile(dest_ref, slot)

    @pl.when(step + 1 < pl.num_programs(0))
    def _():
        issue_tile(dest_next_ref, 1 - slot)

    for kq in range(TOP_K):
        pltpu.make_async_copy(y_ref.at[pl.ds(0, tm), :], buf_ref.at[slot, kq], sem.at[slot]).wait()
    gate = gate_ref[...]
    acc = gate[:, 0:1] * buf_ref[slot, 0]
    for kq in range(1, TOP_K):
        acc = acc + gate[:, kq:kq + 1] * buf_ref[slot, kq]
    o_ref[...] = x_ref[...] + mod_ref[5:6, :] * acc


def _collect(dest, gate_rows, x2, mod_l, ys, geom, latent_only):
    t = x2.shape[0]
    tm = ROW_TILE
    tpb, ctx_tiles, batch = geom
    row = functools.partial(_mod_row, tiles_per_b=tpb, ctx_tiles=ctx_tiles, batch=batch)
    if latent_only:
        lat = tpb - ctx_tiles
        n_steps = batch * lat
        tile = lambda i: (i // lat) * tpb + ctx_tiles + i % lat
    else:
        n_steps = t // tm
        tile = lambda i: i
    nxt = lambda i: tile(jnp.minimum(i + 1, n_steps - 1))
    return pl.pallas_call(
        _collect_kernel,
        grid=(n_steps,),
        in_specs=[
            pl.BlockSpec((None, 1, tm * TOP_K), lambda i: (tile(i), 0, 0), memory_space=pltpu.SMEM),
            pl.BlockSpec((None, 1, tm * TOP_K), lambda i: (nxt(i), 0, 0), memory_space=pltpu.SMEM),
            pl.BlockSpec((tm, 128), lambda i: (tile(i), 0)),
            pl.BlockSpec((tm, D_MODEL), lambda i: (tile(i), 0)),
            pl.BlockSpec((None, 6, D_MODEL), lambda i: (row(tile(i)), 0, 0)),
            pl.BlockSpec(memory_space=pl.ANY),
        ],
        out_specs=pl.BlockSpec((tm, D_MODEL), lambda i: (i, 0)),
        out_shape=jax.ShapeDtypeStruct((n_steps * tm, D_MODEL), F32),
        scratch_shapes=[pltpu.VMEM((2, TOP_K, tm, D_MODEL), F32), pltpu.SemaphoreType.DMA((2,))],
        compiler_params=_params("arbitrary"),
        name="moe_collect",
    )(dest, dest, gate_rows, x2, mod_l, ys)


def _rope_angles(pos, dim):
    inv = ROPE_BASE ** (-jnp.arange(0, dim, 2, dtype=F32) / dim)
    return pos.astype(F32)[:, None] * inv[None, :]


def _rope_tables(seq, ctx_len):
    pos = jnp.arange(seq, dtype=jnp.int32)
    ang = _rope_angles(pos, RET_DK)
    r_cos = jnp.tile(jnp.concatenate([jnp.cos(ang), jnp.cos(ang)], -1), (1, RET_HEADS))
    r_sin = jnp.tile(jnp.concatenate([-jnp.sin(ang), jnp.sin(ang)], -1), (1, RET_HEADS))
    ra = _rope_angles(pos // GRID_W, MLA_ROPE // 2)
    ca = _rope_angles(pos % GRID_W, MLA_ROPE // 2)
    one = jnp.ones((seq, MLA_NOPE), F32)
    zero = jnp.zeros((seq, MLA_NOPE), F32)
    m_cos = jnp.concatenate([one, jnp.cos(ra), jnp.cos(ra), jnp.cos(ca), jnp.cos(ca), one[:, :32]], -1)
    m_sin = jnp.concatenate([zero, -jnp.sin(ra), jnp.sin(ra), -jnp.sin(ca), jnp.sin(ca), zero[:, :32]], -1)
    ctx1 = lambda n: jnp.ones((ctx_len, n), F32)
    ctx0 = lambda n: jnp.zeros((ctx_len, n), F32)
    return (jnp.concatenate([ctx1(256), r_cos], 0), jnp.concatenate([ctx0(256), r_sin], 0),
            jnp.concatenate([ctx1(128), m_cos], 0), jnp.concatenate([ctx0(128), m_sin], 0))


def kernel(x, c, ctx, c_ctx, ada_w, ada_b, norm1_g, norm2_g, w_in, mla_q_norm_g, mla_w_uq, mla_kv_norm_g, mla_w_ukv, mla_qk_q_g, mla_qk_k_g, rwkv_mu_prev, rwkv_mu_next, rwkv_w0, rwkv_w2, rwkv_a0, rwkv_a2, rwkv_g2, rwkv_k_k, rwkv_k_a, rwkv_r_k, rwkv_lnx_g, rwkv_lnx_b, w_br_ret, w_br_mla, w_br_rwkv, w_out, w_router, b_router, w_gu, b_gu, w_down, b_down):
    b, s, d = x.shape
    ctx_len = ctx.shape[1]
    depth = ada_w.shape[0]
    l = ctx_len + s
    t = b * l
    assert d == D_MODEL and b < 16 and ctx_len % ROW_TILE == 0 and s % ROW_TILE == 0
    geom = (l // ROW_TILE, ctx_len // ROW_TILE, b)

    xa = jnp.concatenate([ctx, x], axis=1).reshape(t, d)
    cc = jnp.zeros((16, d), F32).at[:b].set(c).at[b].set(c_ctx)
    mod = _ada_all(cc, ada_w.astype(BF16), ada_b).reshape(depth, 16, 6, d)

    ret_cos, ret_sin, mla_cos, mla_sin = _rope_tables(s, ctx_len)
    ret_consts = _retention_consts()
    rwkv_consts = _rwkv_consts()
    head_blk = jnp.asarray(np.kron(np.eye(RWKV_HEADS), np.ones((RWKV_HD, RWKV_HD))), BF16)
    route_tri = jnp.asarray(np.tril(np.ones((ROW_TILE, ROW_TILE)), -1), BF16)

    xs = None
    for li in range(depth):
        p2 = _inproj(xa, mod[li], norm1_g[li], _pad_w_in(w_in[li]), geom)
        p3 = p2.reshape(b, l, P_COLS)

        retf, retb = (r.reshape(t, RET_W) for r in _retention(p3, ret_cos, ret_sin, ret_consts, ctx_len))

        mw = _mla_weights(mla_q_norm_g[li], mla_w_uq[li], mla_kv_norm_g[li], mla_w_ukv[li],
                          mla_qk_q_g[li], mla_qk_k_g[li])
        q2, k2, v2 = (a.reshape(b, l, MLA_PAD_W) for a in _mla_prep(p2, mw, mla_cos, mla_sin, l))
        mla = _attention(q2, k2, v2, ctx_len).reshape(t, MLA_W)

        rw = dict(mup=rwkv_mu_prev[li].reshape(1, -1), mun=rwkv_mu_next[li].reshape(1, -1),
                  w0=rwkv_w0[li].reshape(2, 1, RWKV_W), w2=rwkv_w2[li].astype(BF16),
                  a0=rwkv_a0[li].reshape(2, 1, RWKV_W), a2=rwkv_a2[li].astype(BF16),
                  g2=rwkv_g2[li].astype(BF16), kk=rwkv_k_k[li].reshape(1, -1),
                  ka=rwkv_k_a[li].reshape(1, -1), rk=rwkv_r_k[li].reshape(1, -1), blk=head_blk)
        r_, v_, na_, kd_, bb_, lw_, g_, bonus_ = _rwkv_elem(p2, rw, l, ctx_len)
        yf, yb = _rwkv_chunk(dict(r=r_, v=v_, na=na_, kd=kd_, bb=bb_, lw=lw_), b, l, ctx_len, rwkv_consts)

        mg = dict(lng=rwkv_lnx_g[li].reshape(1, -1), lnb=rwkv_lnx_b[li].reshape(1, -1), blk=head_blk,
                  wr=w_br_ret[li].astype(BF16), wm=w_br_mla[li].astype(BF16),
                  ww=w_br_rwkv[li].astype(BF16), wo=w_out[li].astype(BF16))
        xa = _merge(xa, mod[li], p2, retf, retb, mla, yf.reshape(t, RWKV_W), yb.reshape(t, RWKV_W), bonus_, g_,
                    mg, geom)

        wr_pad = jnp.pad(w_router[li], ((0, 0), (0, 128 - N_EXPERTS))).astype(BF16)
        br_pad = jnp.concatenate([b_router[li], jnp.full((128 - N_EXPERTS,), -1e30, F32)]).reshape(1, 128)
        h2, idx_rows, rank_rows, gate_rows, cnt = _route(xa, mod[li], norm2_g[li], wr_pad, br_pad,
                                                         route_tri, geom)
        counts = cnt[0, :N_EXPERTS].astype(jnp.int32)
        dest, tile_expert, n_used, n_rows = _moe_layout(idx_rows[:, :TOP_K], rank_rows[:, :TOP_K], counts, t)
        if xs is None:
            xs = jnp.zeros((n_rows, D_MODEL), F32)
        xs = _dispatch(dest, h2, xs)
        ys = _experts(xs, tile_expert, n_used, w_gu, b_gu.reshape(depth, N_EXPERTS, 1, -1),
                      w_down, b_down.reshape(depth, N_EXPERTS, 1, -1), li)
        xa = _collect(dest, gate_rows, xa, mod[li], ys, geom, latent_only=li == depth - 1)

    return xa.reshape(b, s, d)
```

```python
import functools

import jax
import jax.numpy as jnp
import numpy as np
from jax import lax
from jax.experimental import pallas as pl
from jax.experimental.pallas import tpu as pltpu

F32 = jnp.float32
BF16 = jnp.bfloat16

D_MODEL = 1024
GRID_W = 64
RET_HEADS, RET_DK, RET_DV, RET_CHUNK = 4, 64, 128, 128
RET_W = RET_HEADS * RET_DV
MLA_HEADS, MLA_Q_RANK, MLA_KV_RANK, MLA_NOPE, MLA_ROPE, MLA_DV = 8, 256, 128, 64, 32, 64
MLA_QK = MLA_NOPE + MLA_ROPE
MLA_W = MLA_HEADS * MLA_DV
MLA_PAD_W = MLA_HEADS * 128
RWKV_HEADS, RWKV_HD = 8, 64
RWKV_W = RWKV_HEADS * RWKV_HD
RWKV_DECAY_LORA, RWKV_A_LORA, RWKV_GATE_LORA = 64, 64, 128
RWKV_GN_EPS = 64e-5
RWKV_IN = 3 * RWKV_W + 2 * RWKV_DECAY_LORA + 2 * RWKV_A_LORA + RWKV_GATE_LORA
N_EXPERTS, TOP_K = 32, 4
SWIGLU_LIMIT, SWIGLU_ALPHA = 7.0, 1.702
ROPE_BASE = 10000.0
NORM_EPS = 1e-6
HEAD_NORM_EPS = 1e-5

GATE_OFF = 0
RET_OFF = 3 * D_MODEL
MLA_OFF = RET_OFF + 2048
RWKV_OFF = MLA_OFF + 512 + 128
P_COLS = RWKV_OFF + RWKV_IN
IN_COL_TILE = 3840

ROW_TILE = 256
RET_ROWS = 1
RWKV_CHUNK = 64
RWKV_ROWS = 4
HALF = 256
ATTN_Q_SCALE = MLA_QK ** -0.5 * 1.4426950408889634
ATTN_HEADS_PER_STEP = 4
MOE_TILE = 256
VMEM_LIMIT = 48 * 1024 * 1024

NT_DIMS = (((1,), (1,)), ((), ()))
TN_DIMS = (((0,), (0,)), ((), ()))


def _params(*sem):
    return pltpu.CompilerParams(dimension_semantics=sem, vmem_limit_bytes=VMEM_LIMIT)


def _dot(a, b):
    return jnp.dot(a, b, preferred_element_type=F32)


def _dot_nt(a, b):
    return lax.dot_general(a, b, NT_DIMS, preferred_element_type=F32)


def _dot_tn(a, b):
    return lax.dot_general(a, b, TN_DIMS, preferred_element_type=F32)


def _split3(x):
    hi = x.astype(BF16)
    r1 = x - hi.astype(F32)
    mid = r1.astype(BF16)
    lo = (r1 - mid.astype(F32)).astype(BF16)
    return hi, mid, lo


def _dot_exact_rhs01(x, m01):
    hi = x.astype(BF16)
    lo = (x - hi.astype(F32)).astype(BF16)
    return _dot(hi, m01) + _dot(lo, m01)


def _dot_exact_lhs01(m01, x):
    hi, mid, lo = _split3(x)
    return _dot(m01, hi) + _dot(m01, mid) + _dot(m01, lo)


def _sigmoid(x):
    return 1.0 / (1.0 + jnp.exp(-x))


def _silu(x):
    return x * _sigmoid(x)


def _ada_kernel(c_ref, w_ref, b_ref, o_ref):
    s = _silu(c_ref[...])
    o_ref[...] = _dot(s.astype(BF16), w_ref[...]) + b_ref[...]


def _ada_all(cc, ada_w_bf, ada_b):
    depth = ada_w_bf.shape[0]
    tn = 1536
    return pl.pallas_call(
        _ada_kernel,
        grid=(depth, 6 * D_MODEL // tn),
        in_specs=[
            pl.BlockSpec((16, D_MODEL), lambda l, j: (0, 0)),
            pl.BlockSpec((None, D_MODEL, tn), lambda l, j: (l, 0, j)),
            pl.BlockSpec((None, 1, tn), lambda l, j: (l, 0, j)),
        ],
        out_specs=pl.BlockSpec((None, 16, tn), lambda l, j: (l, 0, j)),
        out_shape=jax.ShapeDtypeStruct((depth, 16, 6 * D_MODEL), F32),
        compiler_params=_params("arbitrary", "arbitrary"),
        name="ada_mod",
    )(cc, ada_w_bf, ada_b.reshape(depth, 1, 6 * D_MODEL))


def _mod_row(i, tiles_per_b, ctx_tiles, batch):
    return jnp.where(i % tiles_per_b < ctx_tiles, batch, i // tiles_per_b)


def _inproj_kernel(x_ref, mod_ref, g_ref, w_ref, o_ref):
    x = x_ref[...]
    y = x * lax.rsqrt(jnp.mean(x * x, axis=-1, keepdims=True) + NORM_EPS) * g_ref[...]
    h = y * (1.0 + mod_ref[1:2, :]) + mod_ref[0:1, :]
    o_ref[...] = _dot(h.astype(BF16), w_ref[...])


def _inproj(x2, mod_l, g, w_pad, geom):
    t = x2.shape[0]
    tpb, ctx_tiles, batch = geom
    row = functools.partial(_mod_row, tiles_per_b=tpb, ctx_tiles=ctx_tiles, batch=batch)
    return pl.pallas_call(
        _inproj_kernel,
        grid=(P_COLS // IN_COL_TILE, t // ROW_TILE),
        in_specs=[
            pl.BlockSpec((ROW_TILE, D_MODEL), lambda j, i: (i, 0)),
            pl.BlockSpec((None, 6, D_MODEL), lambda j, i: (row(i), 0, 0)),
            pl.BlockSpec((1, D_MODEL), lambda j, i: (0, 0)),
            pl.BlockSpec((D_MODEL, IN_COL_TILE), lambda j, i: (0, j)),
        ],
        out_specs=pl.BlockSpec((ROW_TILE, IN_COL_TILE), lambda j, i: (i, j)),
        out_shape=jax.ShapeDtypeStruct((t, P_COLS), F32),
        compiler_params=_params("arbitrary", "arbitrary"),
        name="in_proj",
    )(x2, mod_l, g.reshape(1, D_MODEL), w_pad)


def _pad_w_in(w_in):
    d = w_in.shape[0]
    ret = w_in[:, 0:2048]
    cq = w_in[:, 2048:2304]
    ckv = w_in[:, 2304:2432]
    kpe = w_in[:, 2432:2464]
    rwkv = w_in[:, 2464:2464 + RWKV_IN]
    gate = w_in[:, 2464 + RWKV_IN:]
    z = lambda n: jnp.zeros((d, n), w_in.dtype)
    return jnp.concatenate([gate, ret, cq, ckv, z(64), kpe, z(32), z(128), rwkv], axis=1).astype(BF16)


def _scan_chunk(d, j, n_ctx, n_all):
    bwd = jnp.where(j < n_ctx, n_ctx - 1 - j, n_all - 1 - j + n_ctx)
    return jnp.where(d == 0, j, bwd)


def _ret_pair_kernel(qf_ref, kf_ref, vf_ref, gf_ref, cosf_ref, sinf_ref,
                     qb_ref, kb_ref, vb_ref, gb_ref, cosb_ref, sinb_ref,
                     dmat_ref, qdec_ref, kdec_ref, of_ref, ob_ref, s_ref, *, cdec):
    @pl.when(pl.program_id(1) == 0)
    def _():
        s_ref[...] = jnp.zeros_like(s_ref)

    c = RET_CHUNK
    w = RET_HEADS * RET_DK
    first = (lax.broadcasted_iota(jnp.int32, (c, w), 1) & (RET_DK - 1)) < RET_DK // 2

    def rot(x, cos, sin):
        partner = jnp.where(first, pltpu.roll(x, w - RET_DK // 2, 1), pltpu.roll(x, RET_DK // 2, 1))
        return x * cos + partner * sin

    dirs = ((qf_ref, kf_ref, vf_ref, gf_ref, cosf_ref, sinf_ref, of_ref),
            (qb_ref, kb_ref, vb_ref, gb_ref, cosb_ref, sinb_ref, ob_ref))
    inst = []
    for d, (q_ref, k_ref, v_ref, g_ref, cos_ref, sin_ref, o_ref) in enumerate(dirs):
        cos, sin = cos_ref[...], sin_ref[...]
        for bi in range(RET_ROWS):
            q = rot(q_ref[bi], cos, sin).astype(BF16)
            k = rot(k_ref[bi], cos, sin) * (RET_DK ** -0.5)
            kd_t = (k * kdec_ref[d]).T.astype(BF16)
            kb = k.astype(BF16)
            vb = v_ref[bi].astype(BF16)
            for h in range(RET_HEADS):
                ksl = slice(h * RET_DK, (h + 1) * RET_DK)
                vsl = slice(h * RET_DV, (h + 1) * RET_DV)
                inst.append(dict(d=d, bi=bi, h=h, vsl=vsl, q=q[:, ksl], k=kb[:, ksl], v=vb[:, vsl],
                                 kd=kd_t[ksl, :], g_ref=g_ref, o_ref=o_ref))
    s_old = [s_ref[s["bi"], s["d"], s["h"]] for s in inst]
    atts = [_dot_nt(s["q"], s["k"]) * dmat_ref[s["d"], s["h"]] for s in inst]
    inter = [_dot(s["q"], so.astype(BF16)) * qdec_ref[s["d"], s["h"]] for s, so in zip(inst, s_old)]
    kvs = [_dot(s["kd"], s["v"]) for s in inst]
    outs = [_dot(att.astype(BF16), s["v"]) + it for s, att, it in zip(inst, atts, inter)]
    for s, so, kv, o in zip(inst, s_old, kvs, outs):
        s_ref[s["bi"], s["d"], s["h"]] = so * cdec[s["h"]] + kv
        oc = o - jnp.mean(o, axis=-1, keepdims=True)
        y = oc * lax.rsqrt(jnp.mean(oc * oc, axis=-1, keepdims=True) + HEAD_NORM_EPS)
        s["o_ref"][s["bi"], :, s["vsl"]] = y * _silu(s["g_ref"][s["bi"], :, s["vsl"]])


def _retention(p3, cos_t, sin_t, consts, ctx_len):
    b, l, _ = p3.shape
    n_all = l // RET_CHUNK
    n_ctx = ctx_len // RET_CHUNK
    dmat, qdec, kdec, cdec = consts
    ch = functools.partial(_scan_chunk, n_ctx=n_ctx, n_all=n_all)
    qb, vb = RET_OFF // 256, RET_OFF // 512

    rows = RET_ROWS
    assert b % rows == 0

    def specs(d):
        return [
            pl.BlockSpec((rows, RET_CHUNK, 256), lambda bi, j: (bi, ch(d, j), qb)),
            pl.BlockSpec((rows, RET_CHUNK, 256), lambda bi, j: (bi, ch(d, j), qb + 1)),
            pl.BlockSpec((rows, RET_CHUNK, 512), lambda bi, j: (bi, ch(d, j), vb + 1)),
            pl.BlockSpec((rows, RET_CHUNK, 512), lambda bi, j: (bi, ch(d, j), vb + 2 + d)),
            pl.BlockSpec((RET_CHUNK, 256), lambda bi, j: (ch(d, j), 0)),
            pl.BlockSpec((RET_CHUNK, 256), lambda bi, j: (ch(d, j), 0)),
        ]

    const = lambda shape: pl.BlockSpec(shape, lambda bi, j: (0,) * len(shape))
    out = jax.ShapeDtypeStruct((b, l, RET_W), F32)
    out_spec = lambda d: pl.BlockSpec((rows, RET_CHUNK, RET_W), lambda bi, j: (bi, ch(d, j), 0))
    return pl.pallas_call(
        functools.partial(_ret_pair_kernel, cdec=cdec),
        grid=(b // rows, n_all),
        in_specs=specs(0) + specs(1) + [
            const((2, RET_HEADS, RET_CHUNK, RET_CHUNK)), const((2, RET_HEADS, RET_CHUNK, RET_DV)),
            const((2, RET_CHUNK, 256))],
        out_specs=[out_spec(0), out_spec(1)],
        out_shape=[out, out],
        scratch_shapes=[pltpu.VMEM((rows, 2, RET_HEADS, RET_DK, RET_DV), F32)],
        compiler_params=_params("arbitrary", "arbitrary"),
        name="retention",
    )(p3, p3, p3, p3, cos_t, sin_t, p3, p3, p3, p3, cos_t, sin_t, dmat, qdec, kdec)


def _retention_consts():
    gamma = 1.0 - jnp.exp2(-5.0 - jnp.arange(RET_HEADS, dtype=F32))
    lg = jnp.log(gamma)
    idx = jnp.arange(RET_CHUNK, dtype=F32)
    diff = idx[:, None] - idx[None, :]
    lower = jnp.where(diff >= 0, jnp.exp(lg[:, None, None] * jnp.maximum(diff, 0.0)), 0.0)
    dmat = jnp.stack([lower, jnp.swapaxes(lower, 1, 2)])
    qd_f = jnp.exp((idx + 1.0)[None, :] * lg[:, None])
    qd_b = jnp.exp((RET_CHUNK - idx)[None, :] * lg[:, None])
    qdec = jnp.broadcast_to(jnp.stack([qd_f, qd_b])[..., None], (2, RET_HEADS, RET_CHUNK, RET_DV))
    kd_f = jnp.exp((RET_CHUNK - 1.0 - idx)[None, :] * lg[:, None])
    kd_b = jnp.exp(idx[None, :] * lg[:, None])
    kdec = jnp.repeat(jnp.swapaxes(jnp.stack([kd_f, kd_b]), 1, 2), RET_DK, axis=2)
    gam64 = 1.0 - np.exp2(-5.0 - np.arange(RET_HEADS))
    cdec = tuple(float(np.float32(np.exp(RET_CHUNK * np.log(np.float32(gv))))) for gv in gam64)
    return dmat, qdec, kdec, cdec


def _mla_prep_kernel(p_ref, qng_ref, wuq_ref, kvg_ref, wuk_ref, wuv_ref, qg_ref, kg_ref,
                     cos_ref, sin_ref, q_out, k_out, v_out):
    blk = p_ref[...]
    cq = blk[:, 0:MLA_Q_RANK]
    ckv = blk[:, MLA_Q_RANK:MLA_Q_RANK + MLA_KV_RANK]
    kpe = blk[:, MLA_Q_RANK + MLA_KV_RANK:]

    def rms(x, g, n):
        return x * lax.rsqrt(jnp.sum(x * x, axis=-1, keepdims=True) * (1.0 / n) + NORM_EPS) * g

    qn = rms(cq, qng_ref[...], MLA_Q_RANK).astype(BF16)
    kvn = rms(ckv, kvg_ref[...], MLA_KV_RANK).astype(BF16)
    q_raw = _dot(qn, wuq_ref[...])
    k_raw = _dot(kvn, wuk_ref[...])
    ones_col = (lax.broadcasted_iota(jnp.int32, (1, MLA_PAD_W), 1) & 127) == MLA_DV
    v_out[...] = (_dot(kvn, wuv_ref[...]) + jnp.where(ones_col, 1.0, 0.0)).astype(BF16)
    cos = cos_ref[...]
    sin = sin_ref[...]
    tm = blk.shape[0]
    first = (lax.broadcasted_iota(jnp.int32, (tm, 128), 1) & 15) < 8

    def rope(x):
        partner = jnp.where(first, pltpu.roll(x, 120, 1), pltpu.roll(x, 8, 1))
        return x * cos + partner * sin

    heads = [slice(h * 128, (h + 1) * 128) for h in range(MLA_HEADS)]
    qs = [rms(q_raw[:, sl], qg_ref[...], MLA_QK) for sl in heads]
    ks = [rms(k_raw[:, sl] + kpe, kg_ref[...], MLA_QK) for sl in heads]
    for sl, qh, kh in zip(heads, qs, ks):
        q_out[:, sl] = (rope(qh) * ATTN_Q_SCALE).astype(BF16)
        k_out[:, sl] = rope(kh).astype(BF16)


def _mla_prep(p2, w, cos_t, sin_t, rows_per_b):
    t = p2.shape[0]
    tiles_per_b = rows_per_b // ROW_TILE
    const = lambda shape: pl.BlockSpec(shape, lambda i: (0,) * len(shape))
    wide = pl.BlockSpec((ROW_TILE, MLA_PAD_W), lambda i: (i, 0))
    out = jax.ShapeDtypeStruct((t, MLA_PAD_W), BF16)
    return pl.pallas_call(
        _mla_prep_kernel,
        grid=(t // ROW_TILE,),
        in_specs=[
            pl.BlockSpec((ROW_TILE, 512), lambda i: (i, MLA_OFF // 512)),
            const((1, MLA_Q_RANK)), const((MLA_Q_RANK, MLA_PAD_W)),
            const((1, MLA_KV_RANK)), const((MLA_KV_RANK, MLA_PAD_W)), const((MLA_KV_RANK, MLA_PAD_W)),
            const((1, 128)), const((1, 128)),
            pl.BlockSpec((ROW_TILE, 128), lambda i: (i % tiles_per_b, 0)),
            pl.BlockSpec((ROW_TILE, 128), lambda i: (i % tiles_per_b, 0)),
        ],
        out_specs=[wide, wide, wide],
        out_shape=[out, out, out],
        compiler_params=_params("arbitrary"),
        name="mla_prep",
    )(p2, w["qng"], w["wuq"], w["kvg"], w["wuk"], w["wuv"], w["qg"], w["kg"], cos_t, sin_t)


def _mla_weights(q_norm_g, w_uq, kv_norm_g, w_ukv, qk_q_g, qk_k_g):
    wq = w_uq.reshape(MLA_Q_RANK, MLA_HEADS, MLA_QK)
    wq = jnp.pad(wq, ((0, 0), (0, 0), (0, 128 - MLA_QK))).reshape(MLA_Q_RANK, MLA_PAD_W)
    wkv = w_ukv.reshape(MLA_KV_RANK, MLA_HEADS, MLA_NOPE + MLA_DV)
    wk = jnp.pad(wkv[:, :, :MLA_NOPE], ((0, 0), (0, 0), (0, 128 - MLA_NOPE))).reshape(MLA_KV_RANK, MLA_PAD_W)
    wv = jnp.pad(wkv[:, :, MLA_NOPE:], ((0, 0), (0, 0), (0, 128 - MLA_DV))).reshape(MLA_KV_RANK, MLA_PAD_W)
    padg = lambda g: jnp.pad(g, (0, 128 - MLA_QK)).reshape(1, 128)
    return dict(qng=q_norm_g.reshape(1, -1), wuq=wq.astype(BF16), kvg=kv_norm_g.reshape(1, -1),
                wuk=wk.astype(BF16), wuv=wv.astype(BF16), qg=padg(qk_q_g), kg=padg(qk_k_g))


def _attn_kernel(q_ref, k_ref, v_ref, o_ref, *, ctx_len, ctx_tiles):
    def attend(kv_len):
        hs = [slice(hh * 128, (hh + 1) * 128) for hh in range(ATTN_HEADS_PER_STEP)]
        ss = [_dot_nt(q_ref[:, sl], k_ref[0:kv_len, sl]) for sl in hs]
        ps = [jnp.exp2((s - jnp.max(s, axis=-1, keepdims=True)).astype(BF16)) for s in ss]
        o_aug = [_dot(p, v_ref[0:kv_len, sl]) for p, sl in zip(ps, hs)]
        outs = [o[:, 0:MLA_DV] / o[:, MLA_DV:MLA_DV + 1] for o in o_aug]
        o_ref[...] = jnp.concatenate(outs, axis=-1).astype(BF16)

    is_ctx = pl.program_id(2) < ctx_tiles

    @pl.when(is_ctx)
    def _():
        attend(ctx_len)

    @pl.when(jnp.logical_not(is_ctx))
    def _():
        attend(k_ref.shape[0])


def _attention(q3, k3, v3, ctx_len):
    b, l, _ = q3.shape
    tq = ROW_TILE
    hps = ATTN_HEADS_PER_STEP
    return pl.pallas_call(
        functools.partial(_attn_kernel, ctx_len=ctx_len, ctx_tiles=ctx_len // tq),
        grid=(b, MLA_HEADS // hps, l // tq),
        in_specs=[
            pl.BlockSpec((None, tq, hps * 128), lambda bi, h, i: (bi, i, h)),
            pl.BlockSpec((None, l, hps * 128), lambda bi, h, i: (bi, 0, h)),
            pl.BlockSpec((None, l, hps * 128), lambda bi, h, i: (bi, 0, h)),
        ],
        out_specs=pl.BlockSpec((None, tq, hps * MLA_DV), lambda bi, h, i: (bi, i, h)),
        out_shape=jax.ShapeDtypeStruct((b, l, MLA_W), BF16),
        compiler_params=_params("arbitrary", "arbitrary", "arbitrary"),
        name="mla_attention",
    )(q3, k3, v3)


def _rwkv_elem_kernel(t_ref, prev_ref, next_ref, mup_ref, mun_ref, w0_ref, w2_ref, a0_ref, a2_ref,
                      g2_ref, kk_ref, ka_ref, rk_ref, blk_ref,
                      r_out, v_out, na_out, kd_out, bb_out, lw_out, g_out, bonus_out,
                      *, tiles_per_b, ctx_tiles):
    i = pl.program_id(0) % tiles_per_b
    starts = jnp.logical_or(i == 0, i == ctx_tiles)
    ends = jnp.logical_or(i == ctx_tiles - 1, i == tiles_per_b - 1)
    t = t_ref[...]
    tm = t.shape[0]
    prev_row = jnp.where(starts, 0.0, prev_ref[7:8, :])
    next_row = jnp.where(ends, 0.0, next_ref[0:1, :])
    rid = lax.broadcasted_iota(jnp.int32, t.shape, 0)
    prev = jnp.where(rid == 0, prev_row, pltpu.roll(t, 1, 0))
    nxt = jnp.where(rid == tm - 1, next_row, pltpu.roll(t, tm - 1, 0))
    t = t + mup_ref[...] * (prev - t) + mun_ref[...] * (nxt - t)

    w = RWKV_W
    r = t[:, 0:w]
    k = t[:, w:2 * w]
    v = t[:, 2 * w:3 * w]
    o = 3 * w
    wh = (t[:, o:o + 64], t[:, o + 64:o + 128])
    ah = (t[:, o + 128:o + 192], t[:, o + 192:o + 256])
    gh = t[:, o + 256:o + 384]
    blk = blk_ref[...]

    kk = k * kk_ref[...]
    kk = kk * lax.rsqrt(jnp.maximum(_dot_exact_rhs01(kk * kk, blk), 1e-24))
    r_out[...] = r
    v_out[...] = v
    na_out[...] = -kk
    g_out[...] = _dot(_sigmoid(gh).astype(BF16), g2_ref[...])
    rk = r * rk_ref[...]
    bonus = jnp.zeros_like(v)
    for d in range(2):
        z = w0_ref[d] + _dot(jnp.tanh(wh[d]).astype(BF16), w2_ref[d])
        nz = -z
        softplus = jnp.maximum(nz, 0.0) + jnp.log(1.0 + jnp.exp(-jnp.abs(nz)))
        lw_out[d] = -jnp.exp(-softplus - 0.5)
        a = _sigmoid(a0_ref[d] + _dot(ah[d].astype(BF16), a2_ref[d]))
        kd = k * (1.0 + (a - 1.0) * ka_ref[...])
        kd_out[d] = kd
        bb_out[d] = kk * a
        bonus = bonus + _dot_exact_rhs01(rk * kd, blk) * v
    bonus_out[...] = bonus


def _rwkv_elem(p2, w, rows_per_b, ctx_len):
    t = p2.shape[0]
    tm = ROW_TILE
    tiles_per_b = rows_per_b // tm
    n8 = t // 8
    cb = RWKV_OFF // RWKV_IN
    const = lambda shape: pl.BlockSpec(shape, lambda i: (0,) * len(shape))
    one = pl.BlockSpec((tm, RWKV_W), lambda i: (i, 0))
    two = pl.BlockSpec((2, tm, RWKV_W), lambda i: (0, i, 0))
    s1 = jax.ShapeDtypeStruct((t, RWKV_W), F32)
    s2 = jax.ShapeDtypeStruct((2, t, RWKV_W), F32)
    return pl.pallas_call(
        functools.partial(_rwkv_elem_kernel, tiles_per_b=tiles_per_b, ctx_tiles=ctx_len // tm),
        grid=(t // tm,),
        in_specs=[
            pl.BlockSpec((tm, RWKV_IN), lambda i: (i, cb)),
            pl.BlockSpec((8, RWKV_IN), lambda i: (jnp.maximum(i * (tm // 8) - 1, 0), cb)),
            pl.BlockSpec((8, RWKV_IN), lambda i: (jnp.minimum((i + 1) * (tm // 8), n8 - 1), cb)),
            const((1, RWKV_IN)), const((1, RWKV_IN)),
            const((2, 1, RWKV_W)), const((2, RWKV_DECAY_LORA, RWKV_W)),
            const((2, 1, RWKV_W)), const((2, RWKV_A_LORA, RWKV_W)),
            const((RWKV_GATE_LORA, RWKV_W)),
            const((1, RWKV_W)), const((1, RWKV_W)), const((1, RWKV_W)),
            const((RWKV_W, RWKV_W)),
        ],
        out_specs=[one, one, one, two, two, two, one, one],
        out_shape=[s1, s1, s1, s2, s2, s2, s1, s1],
        compiler_params=_params("arbitrary"),
        name="rwkv_elem",
    )(p2, p2, p2, w["mup"], w["mun"], w["w0"], w["w2"], w["a0"], w["a2"], w["g2"],
      w["kk"], w["ka"], w["rk"], w["blk"])


def _block_rows(x, bm):
    return jnp.where(bm, jnp.concatenate([x, x, x, x], axis=0), 0.0)


def _rwkv_chunk_kernel(rf_ref, vf_ref, naf_ref, kdf_ref, bbf_ref, lwf_ref,
                       rb_ref, vb_ref, nab_ref, kdb_ref, bbb_ref, lwb_ref,
                       tri_ref, ms_ref, mi_ref, of_ref, ob_ref, h_ref):
    @pl.when(pl.program_id(1) == 0)
    def _():
        h_ref[...] = jnp.zeros_like(h_ref)

    dirs = ((rf_ref, vf_ref, naf_ref, kdf_ref, bbf_ref, lwf_ref, of_ref),
            (rb_ref, vb_ref, nab_ref, kdb_ref, bbb_ref, lwb_ref, ob_ref))
    c = RWKV_CHUNK
    assert c == 64 and RWKV_HD == 64
    bm = (lax.broadcasted_iota(jnp.int32, (HALF, HALF), 0) >> 6
          == lax.broadcasted_iota(jnp.int32, (HALF, HALF), 1) >> 6)
    eye_p = ((lax.broadcasted_iota(jnp.int32, (c, HALF), 1) & (c - 1))
             == lax.broadcasted_iota(jnp.int32, (c, HALF), 0))
    eye_s = (lax.broadcasted_iota(jnp.int32, (HALF, HALF), 0)
             == lax.broadcasted_iota(jnp.int32, (HALF, HALF), 1))
    bf = lambda x: x.astype(BF16)
    blk = lambda x: _block_rows(x, bm)
    cat0 = lambda xs: jnp.concatenate(xs, axis=0)
    cat1 = lambda xs: jnp.concatenate(xs, axis=1)

    chains = [(d, bi) + dirs[d] for d in range(2) for bi in range(RWKV_ROWS)]
    cums = [_dot_exact_lhs01(tri_ref[d], lw_ref[bi]) for d, bi, _, _, _, _, _, lw_ref, _ in chains]
    inst = []
    for (d, bi, r_ref, v_ref, na_ref, kd_ref, bb_ref, lw_ref, o_ref), cum in zip(chains, cums):
        lw = lw_ref[bi]
        tot = cum[c - 1:c, :] if d == 0 else cum[0:1, :]
        e_neg = jnp.exp(-cum)
        e_rel = jnp.exp(tot - cum)
        e_tot = jnp.exp(tot)
        a_t = na_ref[bi] * jnp.exp(cum - lw)
        r_t = r_ref[bi] * jnp.exp(cum)
        kd, bb, v = kd_ref[bi], bb_ref[bi], v_ref[bi]
        for g in range(2):
            sl = slice(g * HALF, (g + 1) * HALF)
            inst.append(dict(
                d=d, bi=bi, g=g, sl=sl, o_ref=o_ref, a=a_t[:, sl], r=bf(r_t[:, sl]), v=v[:, sl],
                bt=bb[:, sl] * e_neg[:, sl], kt=kd[:, sl] * e_neg[:, sl],
                bk_p=bf(cat0([bb[:, sl] * e_rel[:, sl], kd[:, sl] * e_rel[:, sl]])), etot=e_tot[:, sl]))
    strict = [ms_ref[0] > 0.5, ms_ref[1] > 0.5]
    incl = [mi_ref[0] > 0.5, mi_ref[1] > 0.5]

    bigs = [_dot_nt(bf(cat0([s["a"], s["r"].astype(F32)])), bf(cat0([blk(s["bt"]), blk(s["kt"])])))
            for s in inst]
    for s, big in zip(inst, bigs):
        s["a_ab"] = jnp.where(strict[s["d"]], big[0:c, 0:HALF], 0.0)
        s["a_ak"] = bf(jnp.where(strict[s["d"]], big[0:c, HALF:], 0.0))
        s["a_rb"] = bf(jnp.where(incl[s["d"]], big[c:, 0:HALF], 0.0))
        s["a_rk"] = bf(jnp.where(incl[s["d"]], big[c:, HALF:], 0.0))
        s["v_bd"] = bf(blk(s["v"]))
        s["t"] = jnp.where(eye_p, 1.0, 0.0) + s["a_ab"]
    pws = [_dot(bf(s["a_ab"]), bf(blk(s["a_ab"]))) for s in inst]
    akvs = [_dot(s["a_ak"], s["v_bd"]) for s in inst]
    arkv = [_dot(s["a_rk"], s["v_bd"]) for s in inst]
    for it in range(5):
        prods = [_dot(bf(cat0([s["t"], pw])), bf(blk(pw))) for s, pw in zip(inst, pws)]
        for s, prod in zip(inst, prods):
            s["t"] = s["t"] + prod[0:c]
        pws = [prod[c:] for prod in prods]
    wus = [_dot(bf(s["t"]), bf(cat1([blk(s["a"]), blk(akv)]))) for s, akv in zip(inst, akvs)]
    h_f32 = [h_ref[s["bi"], s["d"], s["g"]] for s in inst]
    h_old = [bf(h) for h in h_f32]
    us = [_dot(bf(wu[:, 0:HALF]), h) + wu[:, HALF:] for wu, h in zip(wus, h_old)]
    rhs = [_dot(s["r"], h) for s, h in zip(inst, h_old)]
    arbu = [_dot(s["a_rb"], bf(blk(u))) for s, u in zip(inst, us)]
    incs = [_dot_tn(s["bk_p"], bf(cat0([u, s["v"]]))) for s, u in zip(inst, us)]
    for s, rh, au, ak, inc, h in zip(inst, rhs, arbu, arkv, incs, h_f32):
        s["o_ref"][s["bi"], :, s["sl"]] = rh + au + ak
        decay_col = jnp.sum(jnp.where(eye_s, s["etot"], 0.0), axis=1, keepdims=True)
        h_ref[s["bi"], s["d"], s["g"]] = decay_col * h + jnp.where(bm, inc, 0.0)


def _rwkv_chunk(e, b, l, ctx_len, consts):
    c = RWKV_CHUNK
    rows = RWKV_ROWS
    assert b % rows == 0
    n_all, n_ctx = l // c, ctx_len // c
    ch = functools.partial(_scan_chunk, n_ctx=n_ctx, n_all=n_all)
    tri, ms, mi = consts
    r3, v3, na3 = (x.reshape(b, l, RWKV_W) for x in (e["r"], e["v"], e["na"]))
    kd4, bb4, lw4 = (x.reshape(2, b, l, RWKV_W) for x in (e["kd"], e["bb"], e["lw"]))

    def specs(d):
        one = pl.BlockSpec((rows, c, RWKV_W), lambda bi, j: (bi, ch(d, j), 0))
        two = pl.BlockSpec((None, rows, c, RWKV_W), lambda bi, j: (d, bi, ch(d, j), 0))
        return one, two

    one_f, two_f = specs(0)
    one_b, two_b = specs(1)
    const = lambda shape: pl.BlockSpec(shape, lambda bi, j: (0,) * len(shape))
    out = jax.ShapeDtypeStruct((b, l, RWKV_W), F32)
    return pl.pallas_call(
        _rwkv_chunk_kernel,
        grid=(b // rows, n_all),
        in_specs=[one_f, one_f, one_f, two_f, two_f, two_f,
                  one_b, one_b, one_b, two_b, two_b, two_b,
                  const((2, c, c)), const((2, c, HALF)), const((2, c, HALF))],
        out_specs=[one_f, one_b],
        out_shape=[out, out],
        scratch_shapes=[pltpu.VMEM((rows, 2, 2, HALF, HALF), F32)],
        compiler_params=_params("arbitrary", "arbitrary"),
        name="rwkv_chunk",
    )(r3, v3, na3, kd4, bb4, lw4, r3, v3, na3, kd4, bb4, lw4, tri, ms, mi)


def _rwkv_consts():
    c = RWKV_CHUNK
    idx = np.arange(c)
    lower = idx[None, :] <= idx[:, None]
    tri = np.stack([lower, lower.T]).astype(np.float32)
    s_in_head = np.tile(idx, HALF // c)[None, :]
    t_row = idx[:, None]
    ms = np.stack([s_in_head < t_row, s_in_head > t_row]).astype(np.float32)
    mi = np.stack([s_in_head <= t_row, s_in_head >= t_row]).astype(np.float32)
    return jnp.asarray(tri, BF16), jnp.asarray(ms), jnp.asarray(mi)


def _merge_kernel(x_ref, mod_ref, gate_ref, retf_ref, retb_ref, mla_ref, yf_ref, yb_ref, bonus_ref, g_ref,
                  lng_ref, lnb_ref, blk_ref, wr_ref, wm_ref, ww_ref, wo_ref, o_ref):
    blk = blk_ref[...]
    ret = retf_ref[...] + retb_ref[...]
    y = yf_ref[...] + yb_ref[...]
    mean = _dot_exact_rhs01(y, blk) * (1.0 / RWKV_HD)
    yc = y - mean
    var = _dot_exact_rhs01(yc * yc, blk) * (1.0 / RWKV_HD)
    y = yc * lax.rsqrt(var + RWKV_GN_EPS) * lng_ref[...] + lnb_ref[...] + bonus_ref[...]
    rwk = y * g_ref[...]
    gate = gate_ref[...]
    d = D_MODEL
    mix = (_sigmoid(gate[:, 0:d]) * _dot(ret.astype(BF16), wr_ref[...])
           + _sigmoid(gate[:, d:2 * d]) * _dot(mla_ref[...], wm_ref[...])
           + _sigmoid(gate[:, 2 * d:]) * _dot(rwk.astype(BF16), ww_ref[...]))
    o_ref[...] = x_ref[...] + mod_ref[2:3, :] * _dot(mix.astype(BF16), wo_ref[...])


def _merge(x2, mod_l, p2, retf, retb, mla, yf, yb, bonus, g, w, geom):
    t = x2.shape[0]
    tm = ROW_TILE
    tpb, ctx_tiles, batch = geom
    row = functools.partial(_mod_row, tiles_per_b=tpb, ctx_tiles=ctx_tiles, batch=batch)
    const = lambda shape: pl.BlockSpec(shape, lambda i: (0,) * len(shape))
    rows = lambda width: pl.BlockSpec((tm, width), lambda i: (i, 0))
    return pl.pallas_call(
        _merge_kernel,
        grid=(t // tm,),
        in_specs=[
            rows(D_MODEL),
            pl.BlockSpec((None, 6, D_MODEL), lambda i: (row(i), 0, 0)),
            rows(3 * D_MODEL),
            rows(RET_W), rows(RET_W), rows(MLA_W), rows(RWKV_W), rows(RWKV_W), rows(RWKV_W), rows(RWKV_W),
            const((1, RWKV_W)), const((1, RWKV_W)), const((RWKV_W, RWKV_W)),
            const((RET_W, D_MODEL)), const((MLA_W, D_MODEL)), const((RWKV_W, D_MODEL)),
            const((D_MODEL, D_MODEL)),
        ],
        out_specs=rows(D_MODEL),
        out_shape=jax.ShapeDtypeStruct((t, D_MODEL), F32),
        compiler_params=_params("arbitrary"),
        name="merge",
    )(x2, mod_l, p2, retf, retb, mla, yf, yb, bonus, g, w["lng"], w["lnb"], w["blk"],
      w["wr"], w["wm"], w["ww"], w["wo"])


def _route_kernel(x_ref, mod_ref, g_ref, wr_ref, br_ref, tri_ref,
                  h_out, idx_out, rank_out, gate_out, cnt_out, run_ref):
    @pl.when(pl.program_id(0) == 0)
    def _():
        run_ref[...] = jnp.zeros_like(run_ref)

    x = x_ref[...]
    y = x * lax.rsqrt(jnp.mean(x * x, axis=-1, keepdims=True) + NORM_EPS) * g_ref[...]
    h = y * (1.0 + mod_ref[4:5, :]) + mod_ref[3:4, :]
    h_out[...] = h
    logits = _dot(h.astype(BF16), wr_ref[...]) + br_ref[...]
    lane = lax.broadcasted_iota(jnp.int32, logits.shape, 1).astype(F32)
    vals, idxs = [], []
    for _ in range(TOP_K):
        m = jnp.max(logits, axis=-1, keepdims=True)
        sel = jnp.min(jnp.where(logits == m, lane, 128.0), axis=-1, keepdims=True)
        vals.append(m)
        idxs.append(sel)
        logits = jnp.where(lane == sel, -jnp.inf, logits)
    es = [jnp.exp(vv - vals[0]) for vv in vals]
    den = es[0] + es[1] + es[2] + es[3]
    chosen = jnp.zeros(lane.shape, F32)
    for kq in range(TOP_K):
        chosen = jnp.where(lane == idxs[kq], 1.0, chosen)
    before = _dot(tri_ref[...], chosen.astype(BF16)) + run_ref[...]
    idx_row = jnp.zeros(lane.shape, F32)
    rank_row = jnp.zeros(lane.shape, F32)
    gate_row = jnp.zeros(lane.shape, F32)
    for kq in range(TOP_K):
        rk = jnp.sum(jnp.where(lane == idxs[kq], before, 0.0), axis=-1, keepdims=True)
        idx_row = jnp.where(lane == float(kq), idxs[kq], idx_row)
        rank_row = jnp.where(lane == float(kq), rk, rank_row)
        gate_row = jnp.where(lane == float(kq), es[kq] / den, gate_row)
    idx_out[...] = idx_row.astype(jnp.int32)
    rank_out[...] = rank_row.astype(jnp.int32)
    gate_out[...] = gate_row
    run_ref[...] += jnp.sum(chosen, axis=0, keepdims=True)
    cnt_out[...] = run_ref[...]


def _route(x2, mod_l, g, wr_pad, br_pad, tri, geom):
    t = x2.shape[0]
    tm = ROW_TILE
    tpb, ctx_tiles, batch = geom
    row = functools.partial(_mod_row, tiles_per_b=tpb, ctx_tiles=ctx_tiles, batch=batch)
    lanes = lambda dt: jax.ShapeDtypeStruct((t, 128), dt)
    tile = pl.BlockSpec((tm, 128), lambda i: (i, 0))
    return pl.pallas_call(
        _route_kernel,
        grid=(t // tm,),
        in_specs=[
            pl.BlockSpec((tm, D_MODEL), lambda i: (i, 0)),
            pl.BlockSpec((None, 6, D_MODEL), lambda i: (row(i), 0, 0)),
            pl.BlockSpec((1, D_MODEL), lambda i: (0, 0)),
            pl.BlockSpec((D_MODEL, 128), lambda i: (0, 0)),
            pl.BlockSpec((1, 128), lambda i: (0, 0)),
            pl.BlockSpec((tm, tm), lambda i: (0, 0)),
        ],
        out_specs=[pl.BlockSpec((tm, D_MODEL), lambda i: (i, 0)), tile, tile, tile,
                   pl.BlockSpec((1, 128), lambda i: (0, 0))],
        out_shape=[jax.ShapeDtypeStruct((t, D_MODEL), F32), lanes(jnp.int32), lanes(jnp.int32),
                   lanes(F32), jax.ShapeDtypeStruct((1, 128), F32)],
        scratch_shapes=[pltpu.VMEM((1, 128), F32)],
        compiler_params=_params("arbitrary"),
        name="moe_router",
    )(x2, mod_l, g.reshape(1, D_MODEL), wr_pad, br_pad, tri)


def _moe_layout(idx, rank, counts, t):
    n_rows = -(-(t * TOP_K + N_EXPERTS * (MOE_TILE - 1)) // MOE_TILE) * MOE_TILE
    n_tiles = n_rows // MOE_TILE
    padded = (counts + MOE_TILE - 1) // MOE_TILE * MOE_TILE
    pad_end = jnp.cumsum(padded)
    pad_start = pad_end - padded
    onehot = idx[..., None] == jnp.arange(N_EXPERTS, dtype=jnp.int32)
    dest = jnp.sum(jnp.where(onehot, pad_start, 0), axis=-1) + rank
    tile_end = pad_end // MOE_TILE
    tile_expert = jnp.sum(jnp.arange(n_tiles, dtype=jnp.int32)[:, None] >= tile_end[None, :], axis=1)
    tile_expert = jnp.minimum(tile_expert, N_EXPERTS - 1).astype(jnp.int32)
    n_used = tile_end[-1:].astype(jnp.int32)
    return dest.astype(jnp.int32).reshape(t // ROW_TILE, 1, ROW_TILE * TOP_K), tile_expert, n_used, n_rows


def _dispatch_kernel(dest_ref, h_ref, xs_in_ref, xs_ref, sem):
    del xs_in_ref
    tm = h_ref.shape[0]

    def issue(tok, carry):
        src = h_ref.at[pl.ds(tok, 1), :]
        for kq in range(TOP_K):
            row = dest_ref[0, tok * TOP_K + kq]
            pltpu.make_async_copy(src, xs_ref.at[pl.ds(row, 1), :], sem.at[0]).start()
        return carry

    lax.fori_loop(0, tm, issue, 0, unroll=4)
    for _ in range(TOP_K):
        pltpu.make_async_copy(h_ref, xs_ref.at[pl.ds(0, tm), :], sem.at[0]).wait()


def _dispatch(dest, h2, xs_prev):
    t = h2.shape[0]
    tm = ROW_TILE
    n_rows = xs_prev.shape[0]
    return pl.pallas_call(
        _dispatch_kernel,
        grid=(t // tm,),
        in_specs=[
            pl.BlockSpec((None, 1, tm * TOP_K), lambda i: (i, 0, 0), memory_space=pltpu.SMEM),
            pl.BlockSpec((tm, D_MODEL), lambda i: (i, 0)),
            pl.BlockSpec(memory_space=pl.ANY),
        ],
        out_specs=pl.BlockSpec(memory_space=pl.ANY),
        out_shape=jax.ShapeDtypeStruct((n_rows, D_MODEL), F32),
        scratch_shapes=[pltpu.SemaphoreType.DMA((1,))],
        input_output_aliases={2: 0},
        compiler_params=_params("arbitrary"),
        name="moe_dispatch",
    )(dest, h2, xs_prev)


def _expert_kernel(te_ref, nu_ref, x_ref, wgu_ref, bgu_ref, wd_ref, bd_ref, o_ref, wgu_bf, wd_bf):
    g = pl.program_id(0)
    used = g < nu_ref[0]

    @pl.when(jnp.logical_not(used))
    def _():
        o_ref[...] = jnp.zeros_like(o_ref)

    @pl.when(jnp.logical_and(used, jnp.logical_or(g == 0, te_ref[g] != te_ref[jnp.maximum(g - 1, 0)])))
    def _():
        wgu_bf[...] = wgu_ref[...].astype(BF16)
        wd_bf[...] = wd_ref[...].astype(BF16)

    @pl.when(used)
    def _():
        gu = _dot(x_ref[...].astype(BF16), wgu_bf[...]) + bgu_ref[...]
        f = gu.shape[1] // 2
        g_lin = jnp.minimum(gu[:, 0:f], SWIGLU_LIMIT)
        up = jnp.clip(gu[:, f:], -SWIGLU_LIMIT, SWIGLU_LIMIT)
        act = g_lin * _sigmoid(SWIGLU_ALPHA * g_lin) * (up + 1.0)
        o_ref[...] = _dot(act.astype(BF16), wd_bf[...]) + bd_ref[...]


def _experts(xs, tile_expert, n_used, wgu, bgu, wd, bd, layer):
    n_rows = xs.shape[0]
    f2 = wgu.shape[3]
    used = lambda g, nu: jnp.minimum(g, nu[0] - 1)
    expert = lambda g, te, nu: (layer, te[used(g, nu)], 0, 0)
    grid_spec = pltpu.PrefetchScalarGridSpec(
        num_scalar_prefetch=2,
        grid=(n_rows // MOE_TILE,),
        in_specs=[
            pl.BlockSpec((MOE_TILE, D_MODEL), lambda g, te, nu: (used(g, nu), 0)),
            pl.BlockSpec((None, None, D_MODEL, f2), expert),
            pl.BlockSpec((None, None, 1, f2), expert),
            pl.BlockSpec((None, None, f2 // 2, D_MODEL), expert),
            pl.BlockSpec((None, None, 1, D_MODEL), expert),
        ],
        out_specs=pl.BlockSpec((MOE_TILE, D_MODEL), lambda g, te, nu: (g, 0)),
        scratch_shapes=[pltpu.VMEM((D_MODEL, f2), BF16), pltpu.VMEM((f2 // 2, D_MODEL), BF16)],
    )
    return pl.pallas_call(
        _expert_kernel,
        grid_spec=grid_spec,
        out_shape=jax.ShapeDtypeStruct((n_rows, D_MODEL), F32),
        compiler_params=_params("arbitrary"),
        name="moe_experts",
    )(tile_expert, n_used, xs, wgu, bgu, wd, bd)


def _collect_kernel(dest_ref, dest_next_ref, gate_ref, x_ref, mod_ref, y_ref, o_ref, buf_ref, sem):
    tm = x_ref.shape[0]
    step = pl.program_id(0)
    slot = step % 2

    def issue_tile(d_ref, to_slot):
        def issue(tok, carry):
            for kq in range(TOP_K):
                row = d_ref[0, tok * TOP_K + kq]
                pltpu.make_async_copy(y_ref.at[pl.ds(row, 1), :],
                                      buf_ref.at[to_slot, kq, pl.ds(tok, 1), :], sem.at[to_slot]).start()
            return carry

        lax.fori_loop(0, tm, issue, 0, unroll=4)

    @pl.when(step == 0)
    def _():
        issue_tile(dest_ref, slot)

    @pl.when(step + 1 < pl.num_programs(0))
    def _():
        issue_tile(dest_next_ref, 1 - slot)

    for kq in range(TOP_K):
        pltpu.make_async_copy(y_ref.at[pl.ds(0, tm), :], buf_ref.at[slot, kq], sem.at[slot]).wait()
    gate = gate_ref[...]
    acc = gate[:, 0:1] * buf_ref[slot, 0]
    for kq in range(1, TOP_K):
        acc = acc + gate[:, kq:kq + 1] * buf_ref[slot, kq]
    o_ref[...] = x_ref[...] + mod_ref[5:6, :] * acc


def _collect(dest, gate_rows, x2, mod_l, ys, geom, latent_only):
    t = x2.shape[0]
    tm = ROW_TILE
    tpb, ctx_tiles, batch = geom
    row = functools.partial(_mod_row, tiles_per_b=tpb, ctx_tiles=ctx_tiles, batch=batch)
    if latent_only:
        lat = tpb - ctx_tiles
        n_steps = batch * lat
        tile = lambda i: (i // lat) * tpb + ctx_tiles + i % lat
    else:
        n_steps = t // tm
        tile = lambda i: i
    nxt = lambda i: tile(jnp.minimum(i + 1, n_steps - 1))
    return pl.pallas_call(
        _collect_kernel,
        grid=(n_steps,),
        in_specs=[
            pl.BlockSpec((None, 1, tm * TOP_K), lambda i: (tile(i), 0, 0), memory_space=pltpu.SMEM),
            pl.BlockSpec((None, 1, tm * TOP_K), lambda i: (nxt(i), 0, 0), memory_space=pltpu.SMEM),
            pl.BlockSpec((tm, 128), lambda i: (tile(i), 0)),
            pl.BlockSpec((tm, D_MODEL), lambda i: (tile(i), 0)),
            pl.BlockSpec((None, 6, D_MODEL), lambda i: (row(tile(i)), 0, 0)),
            pl.BlockSpec(memory_space=pl.ANY),
        ],
        out_specs=pl.BlockSpec((tm, D_MODEL), lambda i: (i, 0)),
        out_shape=jax.ShapeDtypeStruct((n_steps * tm, D_MODEL), F32),
        scratch_shapes=[pltpu.VMEM((2, TOP_K, tm, D_MODEL), F32), pltpu.SemaphoreType.DMA((2,))],
        compiler_params=_params("arbitrary"),
        name="moe_collect",
    )(dest, dest, gate_rows, x2, mod_l, ys)


def _rope_angles(pos, dim):
    inv = ROPE_BASE ** (-jnp.arange(0, dim, 2, dtype=F32) / dim)
    return pos.astype(F32)[:, None] * inv[None, :]


def _rope_tables(seq, ctx_len):
    pos = jnp.arange(seq, dtype=jnp.int32)
    ang = _rope_angles(pos, RET_DK)
    r_cos = jnp.tile(jnp.concatenate([jnp.cos(ang), jnp.cos(ang)], -1), (1, RET_HEADS))
    r_sin = jnp.tile(jnp.concatenate([-jnp.sin(ang), jnp.sin(ang)], -1), (1, RET_HEADS))
    ra = _rope_angles(pos // GRID_W, MLA_ROPE // 2)
    ca = _rope_angles(pos % GRID_W, MLA_ROPE // 2)
    one = jnp.ones((seq, MLA_NOPE), F32)
    zero = jnp.zeros((seq, MLA_NOPE), F32)
    m_cos = jnp.concatenate([one, jnp.cos(ra), jnp.cos(ra), jnp.cos(ca), jnp.cos(ca), one[:, :32]], -1)
    m_sin = jnp.concatenate([zero, -jnp.sin(ra), jnp.sin(ra), -jnp.sin(ca), jnp.sin(ca), zero[:, :32]], -1)
    ctx1 = lambda n: jnp.ones((ctx_len, n), F32)
    ctx0 = lambda n: jnp.zeros((ctx_len, n), F32)
    return (jnp.concatenate([ctx1(256), r_cos], 0), jnp.concatenate([ctx0(256), r_sin], 0),
            jnp.concatenate([ctx1(128), m_cos], 0), jnp.concatenate([ctx0(128), m_sin], 0))


def kernel(x, c, ctx, c_ctx, ada_w, ada_b, norm1_g, norm2_g, w_in, mla_q_norm_g, mla_w_uq, mla_kv_norm_g, mla_w_ukv, mla_qk_q_g, mla_qk_k_g, rwkv_mu_prev, rwkv_mu_next, rwkv_w0, rwkv_w2, rwkv_a0, rwkv_a2, rwkv_g2, rwkv_k_k, rwkv_k_a, rwkv_r_k, rwkv_lnx_g, rwkv_lnx_b, w_br_ret, w_br_mla, w_br_rwkv, w_out, w_router, b_router, w_gu, b_gu, w_down, b_down):
    b, s, d = x.shape
    ctx_len = ctx.shape[1]
    depth = ada_w.shape[0]
    l = ctx_len + s
    t = b * l
    assert d == D_MODEL and b < 16 and ctx_len % ROW_TILE == 0 and s % ROW_TILE == 0
    geom = (l // ROW_TILE, ctx_len // ROW_TILE, b)

    xa = jnp.concatenate([ctx, x], axis=1).reshape(t, d)
    cc = jnp.zeros((16, d), F32).at[:b].set(c).at[b].set(c_ctx)
    mod = _ada_all(cc, ada_w.astype(BF16), ada_b).reshape(depth, 16, 6, d)

    ret_cos, ret_sin, mla_cos, mla_sin = _rope_tables(s, ctx_len)
    ret_consts = _retention_consts()
    rwkv_consts = _rwkv_consts()
    head_blk = jnp.asarray(np.kron(np.eye(RWKV_HEADS), np.ones((RWKV_HD, RWKV_HD))), BF16)
    route_tri = jnp.asarray(np.tril(np.ones((ROW_TILE, ROW_TILE)), -1), BF16)

    xs = None
    for li in range(depth):
        p2 = _inproj(xa, mod[li], norm1_g[li], _pad_w_in(w_in[li]), geom)
        p3 = p2.reshape(b, l, P_COLS)

        retf, retb = (r.reshape(t, RET_W) for r in _retention(p3, ret_cos, ret_sin, ret_consts, ctx_len))

        mw = _mla_weights(mla_q_norm_g[li], mla_w_uq[li], mla_kv_norm_g[li], mla_w_ukv[li],
                          mla_qk_q_g[li], mla_qk_k_g[li])
        q2, k2, v2 = (a.reshape(b, l, MLA_PAD_W) for a in _mla_prep(p2, mw, mla_cos, mla_sin, l))
        mla = _attention(q2, k2, v2, ctx_len).reshape(t, MLA_W)

        rw = dict(mup=rwkv_mu_prev[li].reshape(1, -1), mun=rwkv_mu_next[li].reshape(1, -1),
                  w0=rwkv_w0[li].reshape(2, 1, RWKV_W), w2=rwkv_w2[li].astype(BF16),
                  a0=rwkv_a0[li].reshape(2, 1, RWKV_W), a2=rwkv_a2[li].astype(BF16),
                  g2=rwkv_g2[li].astype(BF16), kk=rwkv_k_k[li].reshape(1, -1),
                  ka=rwkv_k_a[li].reshape(1, -1), rk=rwkv_r_k[li].reshape(1, -1), blk=head_blk)
        r_, v_, na_, kd_, bb_, lw_, g_, bonus_ = _rwkv_elem(p2, rw, l, ctx_len)
        yf, yb = _rwkv_chunk(dict(r=r_, v=v_, na=na_, kd=kd_, bb=bb_, lw=lw_), b, l, ctx_len, rwkv_consts)

        mg = dict(lng=rwkv_lnx_g[li].reshape(1, -1), lnb=rwkv_lnx_b[li].reshape(1, -1), blk=head_blk,
                  wr=w_br_ret[li].astype(BF16), wm=w_br_mla[li].astype(BF16),
                  ww=w_br_rwkv[li].astype(BF16), wo=w_out[li].astype(BF16))
        xa = _merge(xa, mod[li], p2, retf, retb, mla, yf.reshape(t, RWKV_W), yb.reshape(t, RWKV_W), bonus_, g_,
                    mg, geom)

        wr_pad = jnp.pad(w_router[li], ((0, 0), (0, 128 - N_EXPERTS))).astype(BF16)
        br_pad = jnp.concatenate([b_router[li], jnp.full((128 - N_EXPERTS,), -1e30, F32)]).reshape(1, 128)
        h2, idx_rows, rank_rows, gate_rows, cnt = _route(xa, mod[li], norm2_g[li], wr_pad, br_pad,
                                                         route_tri, geom)
        counts = cnt[0, :N_EXPERTS].astype(jnp.int32)
        dest, tile_expert, n_used, n_rows = _moe_layout(idx_rows[:, :TOP_K], rank_rows[:, :TOP_K], counts, t)
        if xs is None:
            xs = jnp.zeros((n_rows, D_MODEL), F32)
        xs = _dispatch(dest, h2, xs)
        ys = _experts(xs, tile_expert, n_used, w_gu, b_gu.reshape(depth, N_EXPERTS, 1, -1),
                      w_down, b_down.reshape(depth, N_EXPERTS, 1, -1), li)
        xa = _collect(dest, gate_rows, xa, mod[li], ys, geom, latent_only=li == depth - 1)

    return xa.reshape(b, s, d)
```

```python
import functools

import jax
import jax.numpy as jnp
import numpy as np
from jax import lax
from jax.experimental import pallas as pl
from jax.experimental.pallas import tpu as pltpu

F32 = jnp.float32
BF16 = jnp.bfloat16

D_MODEL = 1024
GRID_W = 64
RET_HEADS, RET_DK, RET_DV, RET_CHUNK = 4, 64, 128, 128
RET_W = RET_HEADS * RET_DV
MLA_HEADS, MLA_Q_RANK, MLA_KV_RANK, MLA_NOPE, MLA_ROPE, MLA_DV = 8, 256, 128, 64, 32, 64
MLA_QK = MLA_NOPE + MLA_ROPE
MLA_W = MLA_HEADS * MLA_DV
MLA_PAD_W = MLA_HEADS * 128
RWKV_HEADS, RWKV_HD = 8, 64
RWKV_W = RWKV_HEADS * RWKV_HD
RWKV_DECAY_LORA, RWKV_A_LORA, RWKV_GATE_LORA = 64, 64, 128
RWKV_GN_EPS = 64e-5
RWKV_IN = 3 * RWKV_W + 2 * RWKV_DECAY_LORA + 2 * RWKV_A_LORA + RWKV_GATE_LORA
N_EXPERTS, TOP_K = 32, 4
SWIGLU_LIMIT, SWIGLU_ALPHA = 7.0, 1.702
ROPE_BASE = 10000.0
NORM_EPS = 1e-6
HEAD_NORM_EPS = 1e-5

GATE_OFF = 0
RET_OFF = 3 * D_MODEL
MLA_OFF = RET_OFF + 2048
RWKV_OFF = MLA_OFF + 512 + 128
P_COLS = RWKV_OFF + RWKV_IN
IN_COL_TILE = 7680

ROW_TILE = 256
RET_ROWS = 1
RWKV_CHUNK = 64
RWKV_ROWS = 4
HALF = 256
ATTN_Q_SCALE = MLA_QK ** -0.5 * 1.4426950408889634
ATTN_HEADS_PER_STEP = 4
MOE_TILE = 256
VMEM_LIMIT = 48 * 1024 * 1024

NT_DIMS = (((1,), (1,)), ((), ()))
TN_DIMS = (((0,), (0,)), ((), ()))


def _params(*sem):
    return pltpu.CompilerParams(dimension_semantics=sem, vmem_limit_bytes=VMEM_LIMIT)


def _dot(a, b):
    return jnp.dot(a, b, preferred_element_type=F32)


def _dot_nt(a, b):
    return lax.dot_general(a, b, NT_DIMS, preferred_element_type=F32)


def _dot_tn(a, b):
    return lax.dot_general(a, b, TN_DIMS, preferred_element_type=F32)


def _split3(x):
    hi = x.astype(BF16)
    r1 = x - hi.astype(F32)
    mid = r1.astype(BF16)
    lo = (r1 - mid.astype(F32)).astype(BF16)
    return hi, mid, lo


def _dot_exact_rhs01(x, m01):
    hi = x.astype(BF16)
    lo = (x - hi.astype(F32)).astype(BF16)
    return _dot(hi, m01) + _dot(lo, m01)


def _dot_exact_lhs01(m01, x):
    hi, mid, lo = _split3(x)
    return _dot(m01, hi) + _dot(m01, mid) + _dot(m01, lo)


def _sigmoid(x):
    return 1.0 / (1.0 + jnp.exp(-x))


def _silu(x):
    return x * _sigmoid(x)


def _ada_kernel(c_ref, w_ref, b_ref, o_ref):
    s = _silu(c_ref[...])
    o_ref[...] = _dot(s.astype(BF16), w_ref[...]) + b_ref[...]


def _ada_all(cc, ada_w_bf, ada_b):
    depth = ada_w_bf.shape[0]
    tn = 1536
    return pl.pallas_call(
        _ada_kernel,
        grid=(depth, 6 * D_MODEL // tn),
        in_specs=[
            pl.BlockSpec((16, D_MODEL), lambda l, j: (0, 0)),
            pl.BlockSpec((None, D_MODEL, tn), lambda l, j: (l, 0, j)),
            pl.BlockSpec((None, 1, tn), lambda l, j: (l, 0, j)),
        ],
        out_specs=pl.BlockSpec((None, 16, tn), lambda l, j: (l, 0, j)),
        out_shape=jax.ShapeDtypeStruct((depth, 16, 6 * D_MODEL), F32),
        compiler_params=_params("arbitrary", "arbitrary"),
        name="ada_mod",
    )(cc, ada_w_bf, ada_b.reshape(depth, 1, 6 * D_MODEL))


def _mod_row(i, tiles_per_b, ctx_tiles, batch):
    return jnp.where(i % tiles_per_b < ctx_tiles, batch, i // tiles_per_b)


def _inproj_kernel(x_ref, mod_ref, g_ref, w_ref, o_ref):
    x = x_ref[...]
    y = x * lax.rsqrt(jnp.mean(x * x, axis=-1, keepdims=True) + NORM_EPS) * g_ref[...]
    h = y * (1.0 + mod_ref[1:2, :]) + mod_ref[0:1, :]
    o_ref[...] = _dot(h.astype(BF16), w_ref[...])


def _inproj(x2, mod_l, g, w_pad, geom):
    t = x2.shape[0]
    tpb, ctx_tiles, batch = geom
    row = functools.partial(_mod_row, tiles_per_b=tpb, ctx_tiles=ctx_tiles, batch=batch)
    return pl.pallas_call(
        _inproj_kernel,
        grid=(P_COLS // IN_COL_TILE, t // ROW_TILE),
        in_specs=[
            pl.BlockSpec((ROW_TILE, D_MODEL), lambda j, i: (i, 0)),
            pl.BlockSpec((None, 6, D_MODEL), lambda j, i: (row(i), 0, 0)),
            pl.BlockSpec((1, D_MODEL), lambda j, i: (0, 0)),
            pl.BlockSpec((D_MODEL, IN_COL_TILE), lambda j, i: (0, j), pipeline_mode=pl.Buffered(1)),
        ],
        out_specs=pl.BlockSpec((ROW_TILE, IN_COL_TILE), lambda j, i: (i, j)),
        out_shape=jax.ShapeDtypeStruct((t, P_COLS), F32),
        compiler_params=_params("arbitrary", "arbitrary"),
        name="in_proj",
    )(x2, mod_l, g.reshape(1, D_MODEL), w_pad)


def _pad_w_in(w_in):
    d = w_in.shape[0]
    ret = w_in[:, 0:2048]
    cq = w_in[:, 2048:2304]
    ckv = w_in[:, 2304:2432]
    kpe = w_in[:, 2432:2464]
    rwkv = w_in[:, 2464:2464 + RWKV_IN]
    gate = w_in[:, 2464 + RWKV_IN:]
    z = lambda n: jnp.zeros((d, n), w_in.dtype)
    return jnp.concatenate([gate, ret, cq, ckv, z(64), kpe, z(32), z(128), rwkv], axis=1).astype(BF16)


def _scan_chunk(d, j, n_ctx, n_all):
    bwd = jnp.where(j < n_ctx, n_ctx - 1 - j, n_all - 1 - j + n_ctx)
    return jnp.where(d == 0, j, bwd)


def _ret_pair_kernel(qf_ref, kf_ref, vf_ref, gf_ref, cosf_ref, sinf_ref,
                     qb_ref, kb_ref, vb_ref, gb_ref, cosb_ref, sinb_ref,
                     dmat_ref, qdec_ref, kdec_ref, of_ref, ob_ref, s_ref, *, cdec):
    @pl.when(pl.program_id(1) == 0)
    def _():
        s_ref[...] = jnp.zeros_like(s_ref)

    c = RET_CHUNK
    w = RET_HEADS * RET_DK
    first = (lax.broadcasted_iota(jnp.int32, (c, w), 1) & (RET_DK - 1)) < RET_DK // 2

    def rot(x, cos, sin):
        partner = jnp.where(first, pltpu.roll(x, w - RET_DK // 2, 1), pltpu.roll(x, RET_DK // 2, 1))
        return x * cos + partner * sin

    dirs = ((qf_ref, kf_ref, vf_ref, gf_ref, cosf_ref, sinf_ref, of_ref),
            (qb_ref, kb_ref, vb_ref, gb_ref, cosb_ref, sinb_ref, ob_ref))
    inst = []
    for d, (q_ref, k_ref, v_ref, g_ref, cos_ref, sin_ref, o_ref) in enumerate(dirs):
        cos, sin = cos_ref[...], sin_ref[...]
        for bi in range(RET_ROWS):
            q = rot(q_ref[bi], cos, sin).astype(BF16)
            k = rot(k_ref[bi], cos, sin) * (RET_DK ** -0.5)
            kd_t = (k * kdec_ref[d]).T.astype(BF16)
            kb = k.astype(BF16)
            vb = v_ref[bi].astype(BF16)
            for h in range(RET_HEADS):
                ksl = slice(h * RET_DK, (h + 1) * RET_DK)
                vsl = slice(h * RET_DV, (h + 1) * RET_DV)
                inst.append(dict(d=d, bi=bi, h=h, vsl=vsl, q=q[:, ksl], k=kb[:, ksl], v=vb[:, vsl],
                                 kd=kd_t[ksl, :], g_ref=g_ref, o_ref=o_ref))
    s_old = [s_ref[s["bi"], s["d"], s["h"]] for s in inst]
    atts = [_dot_nt(s["q"], s["k"]) * dmat_ref[s["d"], s["h"]] for s in inst]
    inter = [_dot(s["q"], so.astype(BF16)) * qdec_ref[s["d"], s["h"]] for s, so in zip(inst, s_old)]
    kvs = [_dot(s["kd"], s["v"]) for s in inst]
    outs = [_dot(att.astype(BF16), s["v"]) + it for s, att, it in zip(inst, atts, inter)]
    for s, so, kv, o in zip(inst, s_old, kvs, outs):
        s_ref[s["bi"], s["d"], s["h"]] = so * cdec[s["h"]] + kv
        oc = o - jnp.mean(o, axis=-1, keepdims=True)
        y = oc * lax.rsqrt(jnp.mean(oc * oc, axis=-1, keepdims=True) + HEAD_NORM_EPS)
        s["o_ref"][s["bi"], :, s["vsl"]] = y * _silu(s["g_ref"][s["bi"], :, s["vsl"]])


def _retention(p3, cos_t, sin_t, consts, ctx_len):
    b, l, _ = p3.shape
    n_all = l // RET_CHUNK
    n_ctx = ctx_len // RET_CHUNK
    dmat, qdec, kdec, cdec = consts
    ch = functools.partial(_scan_chunk, n_ctx=n_ctx, n_all=n_all)
    qb, vb = RET_OFF // 256, RET_OFF // 512

    rows = RET_ROWS
    assert b % rows == 0

    def specs(d):
        return [
            pl.BlockSpec((rows, RET_CHUNK, 256), lambda bi, j: (bi, ch(d, j), qb)),
            pl.BlockSpec((rows, RET_CHUNK, 256), lambda bi, j: (bi, ch(d, j), qb + 1)),
            pl.BlockSpec((rows, RET_CHUNK, 512), lambda bi, j: (bi, ch(d, j), vb + 1)),
            pl.BlockSpec((rows, RET_CHUNK, 512), lambda bi, j: (bi, ch(d, j), vb + 2 + d)),
            pl.BlockSpec((RET_CHUNK, 256), lambda bi, j: (ch(d, j), 0)),
            pl.BlockSpec((RET_CHUNK, 256), lambda bi, j: (ch(d, j), 0)),
        ]

    const = lambda shape: pl.BlockSpec(shape, lambda bi, j: (0,) * len(shape))
    out = jax.ShapeDtypeStruct((b, l, RET_W), F32)
    out_spec = lambda d: pl.BlockSpec((rows, RET_CHUNK, RET_W), lambda bi, j: (bi, ch(d, j), 0))
    return pl.pallas_call(
        functools.partial(_ret_pair_kernel, cdec=cdec),
        grid=(b // rows, n_all),
        in_specs=specs(0) + specs(1) + [
            const((2, RET_HEADS, RET_CHUNK, RET_CHUNK)), const((2, RET_HEADS, RET_CHUNK, RET_DV)),
            const((2, RET_CHUNK, 256))],
        out_specs=[out_spec(0), out_spec(1)],
        out_shape=[out, out],
        scratch_shapes=[pltpu.VMEM((rows, 2, RET_HEADS, RET_DK, RET_DV), F32)],
        compiler_params=_params("arbitrary", "arbitrary"),
        name="retention",
    )(p3, p3, p3, p3, cos_t, sin_t, p3, p3, p3, p3, cos_t, sin_t, dmat, qdec, kdec)


def _retention_consts():
    gamma = 1.0 - jnp.exp2(-5.0 - jnp.arange(RET_HEADS, dtype=F32))
    lg = jnp.log(gamma)
    idx = jnp.arange(RET_CHUNK, dtype=F32)
    diff = idx[:, None] - idx[None, :]
    lower = jnp.where(diff >= 0, jnp.exp(lg[:, None, None] * jnp.maximum(diff, 0.0)), 0.0)
    dmat = jnp.stack([lower, jnp.swapaxes(lower, 1, 2)])
    qd_f = jnp.exp((idx + 1.0)[None, :] * lg[:, None])
    qd_b = jnp.exp((RET_CHUNK - idx)[None, :] * lg[:, None])
    qdec = jnp.broadcast_to(jnp.stack([qd_f, qd_b])[..., None], (2, RET_HEADS, RET_CHUNK, RET_DV))
    kd_f = jnp.exp((RET_CHUNK - 1.0 - idx)[None, :] * lg[:, None])
    kd_b = jnp.exp(idx[None, :] * lg[:, None])
    kdec = jnp.repeat(jnp.swapaxes(jnp.stack([kd_f, kd_b]), 1, 2), RET_DK, axis=2)
    gam64 = 1.0 - np.exp2(-5.0 - np.arange(RET_HEADS))
    cdec = tuple(float(np.float32(np.exp(RET_CHUNK * np.log(np.float32(gv))))) for gv in gam64)
    return dmat, qdec, kdec, cdec


def _mla_prep_kernel(p_ref, qng_ref, wuq_ref, kvg_ref, wuk_ref, wuv_ref, qg_ref, kg_ref,
                     cos_ref, sin_ref, q_out, k_out, v_out):
    blk = p_ref[...]
    cq = blk[:, 0:MLA_Q_RANK]
    ckv = blk[:, MLA_Q_RANK:MLA_Q_RANK + MLA_KV_RANK]
    kpe = blk[:, MLA_Q_RANK + MLA_KV_RANK:]

    def rms(x, g, n):
        return x * lax.rsqrt(jnp.sum(x * x, axis=-1, keepdims=True) * (1.0 / n) + NORM_EPS) * g

    qn = rms(cq, qng_ref[...], MLA_Q_RANK).astype(BF16)
    kvn = rms(ckv, kvg_ref[...], MLA_KV_RANK).astype(BF16)
    q_raw = _dot(qn, wuq_ref[...])
    k_raw = _dot(kvn, wuk_ref[...])
    ones_col = (lax.broadcasted_iota(jnp.int32, (1, MLA_PAD_W), 1) & 127) == MLA_DV
    v_out[...] = (_dot(kvn, wuv_ref[...]) + jnp.where(ones_col, 1.0, 0.0)).astype(BF16)
    cos = cos_ref[...]
    sin = sin_ref[...]
    tm = blk.shape[0]
    first = (lax.broadcasted_iota(jnp.int32, (tm, 128), 1) & 15) < 8

    def rope(x):
        partner = jnp.where(first, pltpu.roll(x, 120, 1), pltpu.roll(x, 8, 1))
        return x * cos + partner * sin

    heads = [slice(h * 128, (h + 1) * 128) for h in range(MLA_HEADS)]
    qs = [rms(q_raw[:, sl], qg_ref[...], MLA_QK) for sl in heads]
    ks = [rms(k_raw[:, sl] + kpe, kg_ref[...], MLA_QK) for sl in heads]
    for sl, qh, kh in zip(heads, qs, ks):
        q_out[:, sl] = (rope(qh) * ATTN_Q_SCALE).astype(BF16)
        k_out[:, sl] = rope(kh).astype(BF16)


def _mla_prep(p2, w, cos_t, sin_t, rows_per_b):
    t = p2.shape[0]
    tiles_per_b = rows_per_b // ROW_TILE
    const = lambda shape: pl.BlockSpec(shape, lambda i: (0,) * len(shape))
    wide = pl.BlockSpec((ROW_TILE, MLA_PAD_W), lambda i: (i, 0))
    out = jax.ShapeDtypeStruct((t, MLA_PAD_W), BF16)
    return pl.pallas_call(
        _mla_prep_kernel,
        grid=(t // ROW_TILE,),
        in_specs=[
            pl.BlockSpec((ROW_TILE, 512), lambda i: (i, MLA_OFF // 512)),
            const((1, MLA_Q_RANK)), const((MLA_Q_RANK, MLA_PAD_W)),
            const((1, MLA_KV_RANK)), const((MLA_KV_RANK, MLA_PAD_W)), const((MLA_KV_RANK, MLA_PAD_W)),
            const((1, 128)), const((1, 128)),
            pl.BlockSpec((ROW_TILE, 128), lambda i: (i % tiles_per_b, 0)),
            pl.BlockSpec((ROW_TILE, 128), lambda i: (i % tiles_per_b, 0)),
        ],
        out_specs=[wide, wide, wide],
        out_shape=[out, out, out],
        compiler_params=_params("arbitrary"),
        name="mla_prep",
    )(p2, w["qng"], w["wuq"], w["kvg"], w["wuk"], w["wuv"], w["qg"], w["kg"], cos_t, sin_t)


def _mla_weights(q_norm_g, w_uq, kv_norm_g, w_ukv, qk_q_g, qk_k_g):
    wq = w_uq.reshape(MLA_Q_RANK, MLA_HEADS, MLA_QK)
    wq = jnp.pad(wq, ((0, 0), (0, 0), (0, 128 - MLA_QK))).reshape(MLA_Q_RANK, MLA_PAD_W)
    wkv = w_ukv.reshape(MLA_KV_RANK, MLA_HEADS, MLA_NOPE + MLA_DV)
    wk = jnp.pad(wkv[:, :, :MLA_NOPE], ((0, 0), (0, 0), (0, 128 - MLA_NOPE))).reshape(MLA_KV_RANK, MLA_PAD_W)
    wv = jnp.pad(wkv[:, :, MLA_NOPE:], ((0, 0), (0, 0), (0, 128 - MLA_DV))).reshape(MLA_KV_RANK, MLA_PAD_W)
    padg = lambda g: jnp.pad(g, (0, 128 - MLA_QK)).reshape(1, 128)
    return dict(qng=q_norm_g.reshape(1, -1), wuq=wq.astype(BF16), kvg=kv_norm_g.reshape(1, -1),
                wuk=wk.astype(BF16), wuv=wv.astype(BF16), qg=padg(qk_q_g), kg=padg(qk_k_g))


def _attn_kernel(q_ref, k_ref, v_ref, o_ref, *, ctx_len, ctx_tiles):
    def attend(kv_len):
        hs = [slice(hh * 128, (hh + 1) * 128) for hh in range(ATTN_HEADS_PER_STEP)]
        ss = [_dot_nt(q_ref[:, sl], k_ref[0:kv_len, sl]) for sl in hs]
        ps = [jnp.exp2((s - jnp.max(s, axis=-1, keepdims=True)).astype(BF16)) for s in ss]
        o_aug = [_dot(p, v_ref[0:kv_len, sl]) for p, sl in zip(ps, hs)]
        outs = [o[:, 0:MLA_DV] / o[:, MLA_DV:MLA_DV + 1] for o in o_aug]
        o_ref[...] = jnp.concatenate(outs, axis=-1).astype(BF16)

    is_ctx = pl.program_id(2) < ctx_tiles

    @pl.when(is_ctx)
    def _():
        attend(ctx_len)

    @pl.when(jnp.logical_not(is_ctx))
    def _():
        attend(k_ref.shape[0])


def _attention(q3, k3, v3, ctx_len):
    b, l, _ = q3.shape
    tq = ROW_TILE
    hps = ATTN_HEADS_PER_STEP
    return pl.pallas_call(
        functools.partial(_attn_kernel, ctx_len=ctx_len, ctx_tiles=ctx_len // tq),
        grid=(b, MLA_HEADS // hps, l // tq),
        in_specs=[
            pl.BlockSpec((None, tq, hps * 128), lambda bi, h, i: (bi, i, h)),
            pl.BlockSpec((None, l, hps * 128), lambda bi, h, i: (bi, 0, h)),
            pl.BlockSpec((None, l, hps * 128), lambda bi, h, i: (bi, 0, h)),
        ],
        out_specs=pl.BlockSpec((None, tq, hps * MLA_DV), lambda bi, h, i: (bi, i, h)),
        out_shape=jax.ShapeDtypeStruct((b, l, MLA_W), BF16),
        compiler_params=_params("arbitrary", "arbitrary", "arbitrary"),
        name="mla_attention",
    )(q3, k3, v3)


def _rwkv_elem_kernel(t_ref, prev_ref, next_ref, mup_ref, mun_ref, w0_ref, w2_ref, a0_ref, a2_ref,
                      g2_ref, kk_ref, ka_ref, rk_ref, blk_ref,
                      r_out, v_out, na_out, kd_out, bb_out, lw_out, g_out, bonus_out,
                      *, tiles_per_b, ctx_tiles):
    i = pl.program_id(0) % tiles_per_b
    starts = jnp.logical_or(i == 0, i == ctx_tiles)
    ends = jnp.logical_or(i == ctx_tiles - 1, i == tiles_per_b - 1)
    t = t_ref[...]
    tm = t.shape[0]
    prev_row = jnp.where(starts, 0.0, prev_ref[7:8, :])
    next_row = jnp.where(ends, 0.0, next_ref[0:1, :])
    rid = lax.broadcasted_iota(jnp.int32, t.shape, 0)
    prev = jnp.where(rid == 0, prev_row, pltpu.roll(t, 1, 0))
    nxt = jnp.where(rid == tm - 1, next_row, pltpu.roll(t, tm - 1, 0))
    t = t + mup_ref[...] * (prev - t) + mun_ref[...] * (nxt - t)

    w = RWKV_W
    r = t[:, 0:w]
    k = t[:, w:2 * w]
    v = t[:, 2 * w:3 * w]
    o = 3 * w
    wh = (t[:, o:o + 64], t[:, o + 64:o + 128])
    ah = (t[:, o + 128:o + 192], t[:, o + 192:o + 256])
    gh = t[:, o + 256:o + 384]
    blk = blk_ref[...]

    kk = k * kk_ref[...]
    kk = kk * lax.rsqrt(jnp.maximum(_dot_exact_rhs01(kk * kk, blk), 1e-24))
    r_out[...] = r
    v_out[...] = v
    na_out[...] = -kk
    g_out[...] = _dot(_sigmoid(gh).astype(BF16), g2_ref[...])
    rk = r * rk_ref[...]
    bonus = jnp.zeros_like(v)
    for d in range(2):
        z = w0_ref[d] + _dot(jnp.tanh(wh[d]).astype(BF16), w2_ref[d])
        nz = -z
        softplus = jnp.maximum(nz, 0.0) + jnp.log(1.0 + jnp.exp(-jnp.abs(nz)))
        lw_out[d] = -jnp.exp(-softplus - 0.5)
        a = _sigmoid(a0_ref[d] + _dot(ah[d].astype(BF16), a2_ref[d]))
        kd = k * (1.0 + (a - 1.0) * ka_ref[...])
        kd_out[d] = kd
        bb_out[d] = kk * a
        bonus = bonus + _dot_exact_rhs01(rk * kd, blk) * v
    bonus_out[...] = bonus


def _rwkv_elem(p2, w, rows_per_b, ctx_len):
    t = p2.shape[0]
    tm = ROW_TILE
    tiles_per_b = rows_per_b // tm
    n8 = t // 8
    cb = RWKV_OFF // RWKV_IN
    const = lambda shape: pl.BlockSpec(shape, lambda i: (0,) * len(shape))
    one = pl.BlockSpec((tm, RWKV_W), lambda i: (i, 0))
    two = pl.BlockSpec((2, tm, RWKV_W), lambda i: (0, i, 0))
    s1 = jax.ShapeDtypeStruct((t, RWKV_W), F32)
    s2 = jax.ShapeDtypeStruct((2, t, RWKV_W), F32)
    return pl.pallas_call(
        functools.partial(_rwkv_elem_kernel, tiles_per_b=tiles_per_b, ctx_tiles=ctx_len // tm),
        grid=(t // tm,),
        in_specs=[
            pl.BlockSpec((tm, RWKV_IN), lambda i: (i, cb)),
            pl.BlockSpec((8, RWKV_IN), lambda i: (jnp.maximum(i * (tm // 8) - 1, 0), cb)),
            pl.BlockSpec((8, RWKV_IN), lambda i: (jnp.minimum((i + 1) * (tm // 8), n8 - 1), cb)),
            const((1, RWKV_IN)), const((1, RWKV_IN)),
            const((2, 1, RWKV_W)), const((2, RWKV_DECAY_LORA, RWKV_W)),
            const((2, 1, RWKV_W)), const((2, RWKV_A_LORA, RWKV_W)),
            const((RWKV_GATE_LORA, RWKV_W)),
            const((1, RWKV_W)), const((1, RWKV_W)), const((1, RWKV_W)),
            const((RWKV_W, RWKV_W)),
        ],
        out_specs=[one, one, one, two, two, two, one, one],
        out_shape=[s1, s1, s1, s2, s2, s2, s1, s1],
        compiler_params=_params("arbitrary"),
        name="rwkv_elem",
    )(p2, p2, p2, w["mup"], w["mun"], w["w0"], w["w2"], w["a0"], w["a2"], w["g2"],
      w["kk"], w["ka"], w["rk"], w["blk"])


def _block_rows(x, bm):
    return jnp.where(bm, jnp.concatenate([x, x, x, x], axis=0), 0.0)


def _rwkv_chunk_kernel(rf_ref, vf_ref, naf_ref, kdf_ref, bbf_ref, lwf_ref,
                       rb_ref, vb_ref, nab_ref, kdb_ref, bbb_ref, lwb_ref,
                       tri_ref, ms_ref, mi_ref, of_ref, ob_ref, h_ref):
    @pl.when(pl.program_id(1) == 0)
    def _():
        h_ref[...] = jnp.zeros_like(h_ref)

    dirs = ((rf_ref, vf_ref, naf_ref, kdf_ref, bbf_ref, lwf_ref, of_ref),
            (rb_ref, vb_ref, nab_ref, kdb_ref, bbb_ref, lwb_ref, ob_ref))
    c = RWKV_CHUNK
    assert c == 64 and RWKV_HD == 64
    bm = (lax.broadcasted_iota(jnp.int32, (HALF, HALF), 0) >> 6
          == lax.broadcasted_iota(jnp.int32, (HALF, HALF), 1) >> 6)
    eye_p = ((lax.broadcasted_iota(jnp.int32, (c, HALF), 1) & (c - 1))
             == lax.broadcasted_iota(jnp.int32, (c, HALF), 0))
    eye_s = (lax.broadcasted_iota(jnp.int32, (HALF, HALF), 0)
             == lax.broadcasted_iota(jnp.int32, (HALF, HALF), 1))
    bf = lambda x: x.astype(BF16)
    blk = lambda x: _block_rows(x, bm)
    cat0 = lambda xs: jnp.concatenate(xs, axis=0)
    cat1 = lambda xs: jnp.concatenate(xs, axis=1)

    chains = [(d, bi) + dirs[d] for d in range(2) for bi in range(RWKV_ROWS)]
    cums = [_dot_exact_lhs01(tri_ref[d], lw_ref[bi]) for d, bi, _, _, _, _, _, lw_ref, _ in chains]
    inst = []
    for (d, bi, r_ref, v_ref, na_ref, kd_ref, bb_ref, lw_ref, o_ref), cum in zip(chains, cums):
        lw = lw_ref[bi]
        tot = cum[c - 1:c, :] if d == 0 else cum[0:1, :]
        e_neg = jnp.exp(-cum)
        e_rel = jnp.exp(tot - cum)
        e_tot = jnp.exp(tot)
        a_t = na_ref[bi] * jnp.exp(cum - lw)
        r_t = r_ref[bi] * jnp.exp(cum)
        kd, bb, v = kd_ref[bi], bb_ref[bi], v_ref[bi]
        for g in range(2):
            sl = slice(g * HALF, (g + 1) * HALF)
            inst.append(dict(
                d=d, bi=bi, g=g, sl=sl, o_ref=o_ref, a=a_t[:, sl], r=bf(r_t[:, sl]), v=v[:, sl],
                bt=bb[:, sl] * e_neg[:, sl], kt=kd[:, sl] * e_neg[:, sl],
                bk_p=bf(cat0([bb[:, sl] * e_rel[:, sl], kd[:, sl] * e_rel[:, sl]])), etot=e_tot[:, sl]))
    strict = [ms_ref[0] > 0.5, ms_ref[1] > 0.5]
    incl = [mi_ref[0] > 0.5, mi_ref[1] > 0.5]

    bigs = [_dot_nt(bf(cat0([s["a"], s["r"].astype(F32)])), bf(cat0([blk(s["bt"]), blk(s["kt"])])))
            for s in inst]
    for s, big in zip(inst, bigs):
        s["a_ab"] = jnp.where(strict[s["d"]], big[0:c, 0:HALF], 0.0)
        s["a_ak"] = bf(jnp.where(strict[s["d"]], big[0:c, HALF:], 0.0))
        s["a_rb"] = bf(jnp.where(incl[s["d"]], big[c:, 0:HALF], 0.0))
        s["a_rk"] = bf(jnp.where(incl[s["d"]], big[c:, HALF:], 0.0))
        s["v_bd"] = bf(blk(s["v"]))
        s["t"] = jnp.where(eye_p, 1.0, 0.0) + s["a_ab"]
    pws = [_dot(bf(s["a_ab"]), bf(blk(s["a_ab"]))) for s in inst]
    akvs = [_dot(s["a_ak"], s["v_bd"]) for s in inst]
    arkv = [_dot(s["a_rk"], s["v_bd"]) for s in inst]
    for it in range(5):
        prods = [_dot(bf(cat0([s["t"], pw])), bf(blk(pw))) for s, pw in zip(inst, pws)]
        for s, prod in zip(inst, prods):
            s["t"] = s["t"] + prod[0:c]
        pws = [prod[c:] for prod in prods]
    wus = [_dot(bf(s["t"]), bf(cat1([blk(s["a"]), blk(akv)]))) for s, akv in zip(inst, akvs)]
    h_f32 = [h_ref[s["bi"], s["d"], s["g"]] for s in inst]
    h_old = [bf(h) for h in h_f32]
    us = [_dot(bf(wu[:, 0:HALF]), h) + wu[:, HALF:] for wu, h in zip(wus, h_old)]
    rhs = [_dot(s["r"], h) for s, h in zip(inst, h_old)]
    arbu = [_dot(s["a_rb"], bf(blk(u))) for s, u in zip(inst, us)]
    incs = [_dot_tn(s["bk_p"], bf(cat0([u, s["v"]]))) for s, u in zip(inst, us)]
    for s, rh, au, ak, inc, h in zip(inst, rhs, arbu, arkv, incs, h_f32):
        s["o_ref"][s["bi"], :, s["sl"]] = rh + au + ak
        decay_col = jnp.sum(jnp.where(eye_s, s["etot"], 0.0), axis=1, keepdims=True)
        h_ref[s["bi"], s["d"], s["g"]] = decay_col * h + jnp.where(bm, inc, 0.0)


def _rwkv_chunk(e, b, l, ctx_len, consts):
    c = RWKV_CHUNK
    rows = RWKV_ROWS
    assert b % rows == 0
    n_all, n_ctx = l // c, ctx_len // c
    ch = functools.partial(_scan_chunk, n_ctx=n_ctx, n_all=n_all)
    tri, ms, mi = consts
    r3, v3, na3 = (x.reshape(b, l, RWKV_W) for x in (e["r"], e["v"], e["na"]))
    kd4, bb4, lw4 = (x.reshape(2, b, l, RWKV_W) for x in (e["kd"], e["bb"], e["lw"]))

    def specs(d):
        one = pl.BlockSpec((rows, c, RWKV_W), lambda bi, j: (bi, ch(d, j), 0))
        two = pl.BlockSpec((None, rows, c, RWKV_W), lambda bi, j: (d, bi, ch(d, j), 0))
        return one, two

    one_f, two_f = specs(0)
    one_b, two_b = specs(1)
    const = lambda shape: pl.BlockSpec(shape, lambda bi, j: (0,) * len(shape))
    out = jax.ShapeDtypeStruct((b, l, RWKV_W), F32)
    return pl.pallas_call(
        _rwkv_chunk_kernel,
        grid=(b // rows, n_all),
        in_specs=[one_f, one_f, one_f, two_f, two_f, two_f,
                  one_b, one_b, one_b, two_b, two_b, two_b,
                  const((2, c, c)), const((2, c, HALF)), const((2, c, HALF))],
        out_specs=[one_f, one_b],
        out_shape=[out, out],
        scratch_shapes=[pltpu.VMEM((rows, 2, 2, HALF, HALF), F32)],
        compiler_params=_params("arbitrary", "arbitrary"),
        name="rwkv_chunk",
    )(r3, v3, na3, kd4, bb4, lw4, r3, v3, na3, kd4, bb4, lw4, tri, ms, mi)


def _rwkv_consts():
    c = RWKV_CHUNK
    idx = np.arange(c)
    lower = idx[None, :] <= idx[:, None]
    tri = np.stack([lower, lower.T]).astype(np.float32)
    s_in_head = np.tile(idx, HALF // c)[None, :]
    t_row = idx[:, None]
    ms = np.stack([s_in_head < t_row, s_in_head > t_row]).astype(np.float32)
    mi = np.stack([s_in_head <= t_row, s_in_head >= t_row]).astype(np.float32)
    return jnp.asarray(tri, BF16), jnp.asarray(ms), jnp.asarray(mi)


def _merge_kernel(x_ref, mod_ref, gate_ref, retf_ref, retb_ref, mla_ref, yf_ref, yb_ref, bonus_ref, g_ref,
                  lng_ref, lnb_ref, blk_ref, wr_ref, wm_ref, ww_ref, wo_ref, o_ref):
    blk = blk_ref[...]
    ret = retf_ref[...] + retb_ref[...]
    y = yf_ref[...] + yb_ref[...]
    mean = _dot_exact_rhs01(y, blk) * (1.0 / RWKV_HD)
    yc = y - mean
    var = _dot_exact_rhs01(yc * yc, blk) * (1.0 / RWKV_HD)
    y = yc * lax.rsqrt(var + RWKV_GN_EPS) * lng_ref[...] + lnb_ref[...] + bonus_ref[...]
    rwk = y * g_ref[...]
    gate = gate_ref[...]
    d = D_MODEL
    mix = (_sigmoid(gate[:, 0:d]) * _dot(ret.astype(BF16), wr_ref[...])
           + _sigmoid(gate[:, d:2 * d]) * _dot(mla_ref[...], wm_ref[...])
           + _sigmoid(gate[:, 2 * d:]) * _dot(rwk.astype(BF16), ww_ref[...]))
    o_ref[...] = x_ref[...] + mod_ref[2:3, :] * _dot(mix.astype(BF16), wo_ref[...])


def _merge(x2, mod_l, p2, retf, retb, mla, yf, yb, bonus, g, w, geom):
    t = x2.shape[0]
    tm = ROW_TILE
    tpb, ctx_tiles, batch = geom
    row = functools.partial(_mod_row, tiles_per_b=tpb, ctx_tiles=ctx_tiles, batch=batch)
    const = lambda shape: pl.BlockSpec(shape, lambda i: (0,) * len(shape))
    rows = lambda width: pl.BlockSpec((tm, width), lambda i: (i, 0))
    return pl.pallas_call(
        _merge_kernel,
        grid=(t // tm,),
        in_specs=[
            rows(D_MODEL),
            pl.BlockSpec((None, 6, D_MODEL), lambda i: (row(i), 0, 0)),
            rows(3 * D_MODEL),
            rows(RET_W), rows(RET_W), rows(MLA_W), rows(RWKV_W), rows(RWKV_W), rows(RWKV_W), rows(RWKV_W),
            const((1, RWKV_W)), const((1, RWKV_W)), const((RWKV_W, RWKV_W)),
            const((RET_W, D_MODEL)), const((MLA_W, D_MODEL)), const((RWKV_W, D_MODEL)),
            const((D_MODEL, D_MODEL)),
        ],
        out_specs=rows(D_MODEL),
        out_shape=jax.ShapeDtypeStruct((t, D_MODEL), F32),
        compiler_params=_params("arbitrary"),
        name="merge",
    )(x2, mod_l, p2, retf, retb, mla, yf, yb, bonus, g, w["lng"], w["lnb"], w["blk"],
      w["wr"], w["wm"], w["ww"], w["wo"])


def _route_kernel(x_ref, mod_ref, g_ref, wr_ref, br_ref, tri_ref,
                  h_out, idx_out, rank_out, gate_out, cnt_out, run_ref):
    @pl.when(pl.program_id(0) == 0)
    def _():
        run_ref[...] = jnp.zeros_like(run_ref)

    x = x_ref[...]
    y = x * lax.rsqrt(jnp.mean(x * x, axis=-1, keepdims=True) + NORM_EPS) * g_ref[...]
    h = y * (1.0 + mod_ref[4:5, :]) + mod_ref[3:4, :]
    h_out[...] = h
    logits = _dot(h.astype(BF16), wr_ref[...]) + br_ref[...]
    lane = lax.broadcasted_iota(jnp.int32, logits.shape, 1).astype(F32)
    vals, idxs = [], []
    for _ in range(TOP_K):
        m = jnp.max(logits, axis=-1, keepdims=True)
        sel = jnp.min(jnp.where(logits == m, lane, 128.0), axis=-1, keepdims=True)
        vals.append(m)
        idxs.append(sel)
        logits = jnp.where(lane == sel, -jnp.inf, logits)
    es = [jnp.exp(vv - vals[0]) for vv in vals]
    den = es[0] + es[1] + es[2] + es[3]
    chosen = jnp.zeros(lane.shape, F32)
    for kq in range(TOP_K):
        chosen = jnp.where(lane == idxs[kq], 1.0, chosen)
    before = _dot(tri_ref[...], chosen.astype(BF16)) + run_ref[...]
    idx_row = jnp.zeros(lane.shape, F32)
    rank_row = jnp.zeros(lane.shape, F32)
    gate_row = jnp.zeros(lane.shape, F32)
    for kq in range(TOP_K):
        rk = jnp.sum(jnp.where(lane == idxs[kq], before, 0.0), axis=-1, keepdims=True)
        idx_row = jnp.where(lane == float(kq), idxs[kq], idx_row)
        rank_row = jnp.where(lane == float(kq), rk, rank_row)
        gate_row = jnp.where(lane == float(kq), es[kq] / den, gate_row)
    idx_out[...] = idx_row.astype(jnp.int32)
    rank_out[...] = rank_row.astype(jnp.int32)
    gate_out[...] = gate_row
    run_ref[...] += jnp.sum(chosen, axis=0, keepdims=True)
    cnt_out[...] = run_ref[...]


def _route(x2, mod_l, g, wr_pad, br_pad, tri, geom):
    t = x2.shape[0]
    tm = ROW_TILE
    tpb, ctx_tiles, batch = geom
    row = functools.partial(_mod_row, tiles_per_b=tpb, ctx_tiles=ctx_tiles, batch=batch)
    lanes = lambda dt: jax.ShapeDtypeStruct((t, 128), dt)
    tile = pl.BlockSpec((tm, 128), lambda i: (i, 0))
    return pl.pallas_call(
        _route_kernel,
        grid=(t // tm,),
        in_specs=[
            pl.BlockSpec((tm, D_MODEL), lambda i: (i, 0)),
            pl.BlockSpec((None, 6, D_MODEL), lambda i: (row(i), 0, 0)),
            pl.BlockSpec((1, D_MODEL), lambda i: (0, 0)),
            pl.BlockSpec((D_MODEL, 128), lambda i: (0, 0)),
            pl.BlockSpec((1, 128), lambda i: (0, 0)),
            pl.BlockSpec((tm, tm), lambda i: (0, 0)),
        ],
        out_specs=[pl.BlockSpec((tm, D_MODEL), lambda i: (i, 0)), tile, tile, tile,
                   pl.BlockSpec((1, 128), lambda i: (0, 0))],
        out_shape=[jax.ShapeDtypeStruct((t, D_MODEL), F32), lanes(jnp.int32), lanes(jnp.int32),
                   lanes(F32), jax.ShapeDtypeStruct((1, 128), F32)],
        scratch_shapes=[pltpu.VMEM((1, 128), F32)],
        compiler_params=_params("arbitrary"),
        name="moe_router",
    )(x2, mod_l, g.reshape(1, D_MODEL), wr_pad, br_pad, tri)


def _moe_layout(idx, rank, counts, t):
    n_rows = -(-(t * TOP_K + N_EXPERTS * (MOE_TILE - 1)) // MOE_TILE) * MOE_TILE
    n_tiles = n_rows // MOE_TILE
    padded = (counts + MOE_TILE - 1) // MOE_TILE * MOE_TILE
    pad_end = jnp.cumsum(padded)
    pad_start = pad_end - padded
    onehot = idx[..., None] == jnp.arange(N_EXPERTS, dtype=jnp.int32)
    dest = jnp.sum(jnp.where(onehot, pad_start, 0), axis=-1) + rank
    tile_end = pad_end // MOE_TILE
    tile_expert = jnp.sum(jnp.arange(n_tiles, dtype=jnp.int32)[:, None] >= tile_end[None, :], axis=1)
    tile_expert = jnp.minimum(tile_expert, N_EXPERTS - 1).astype(jnp.int32)
    n_used = tile_end[-1:].astype(jnp.int32)
    return dest.astype(jnp.int32).reshape(t // ROW_TILE, 1, ROW_TILE * TOP_K), tile_expert, n_used, n_rows


def _dispatch_kernel(dest_ref, h_ref, xs_in_ref, xs_ref, sem):
    del xs_in_ref
    tm = h_ref.shape[0]

    def issue(tok, carry):
        src = h_ref.at[pl.ds(tok, 1), :]
        for kq in range(TOP_K):
            row = dest_ref[0, tok * TOP_K + kq]
            pltpu.make_async_copy(src, xs_ref.at[pl.ds(row, 1), :], sem.at[0]).start()
        return carry

    lax.fori_loop(0, tm, issue, 0, unroll=4)
    for _ in range(TOP_K):
        pltpu.make_async_copy(h_ref, xs_ref.at[pl.ds(0, tm), :], sem.at[0]).wait()


def _dispatch(dest, h2, xs_prev):
    t = h2.shape[0]
    tm = ROW_TILE
    n_rows = xs_prev.shape[0]
    return pl.pallas_call(
        _dispatch_kernel,
        grid=(t // tm,),
        in_specs=[
            pl.BlockSpec((None, 1, tm * TOP_K), lambda i: (i, 0, 0), memory_space=pltpu.SMEM),
            pl.BlockSpec((tm, D_MODEL), lambda i: (i, 0)),
            pl.BlockSpec(memory_space=pl.ANY),
        ],
        out_specs=pl.BlockSpec(memory_space=pl.ANY),
        out_shape=jax.ShapeDtypeStruct((n_rows, D_MODEL), F32),
        scratch_shapes=[pltpu.SemaphoreType.DMA((1,))],
        input_output_aliases={2: 0},
        compiler_params=_params("arbitrary"),
        name="moe_dispatch",
    )(dest, h2, xs_prev)


def _expert_kernel(te_ref, nu_ref, x_ref, wgu_ref, bgu_ref, wd_ref, bd_ref, o_ref, wgu_bf, wd_bf):
    g = pl.program_id(0)
    used = g < nu_ref[0]

    @pl.when(jnp.logical_not(used))
    def _():
        o_ref[...] = jnp.zeros_like(o_ref)

    @pl.when(jnp.logical_and(used, jnp.logical_or(g == 0, te_ref[g] != te_ref[jnp.maximum(g - 1, 0)])))
    def _():
        wgu_bf[...] = wgu_ref[...].astype(BF16)
        wd_bf[...] = wd_ref[...].astype(BF16)

    @pl.when(used)
    def _():
        gu = _dot(x_ref[...].astype(BF16), wgu_bf[...]) + bgu_ref[...]
        f = gu.shape[1] // 2
        g_lin = jnp.minimum(gu[:, 0:f], SWIGLU_LIMIT)
        up = jnp.clip(gu[:, f:], -SWIGLU_LIMIT, SWIGLU_LIMIT)
        act = g_lin * _sigmoid(SWIGLU_ALPHA * g_lin) * (up + 1.0)
        o_ref[...] = _dot(act.astype(BF16), wd_bf[...]) + bd_ref[...]


def _experts(xs, tile_expert, n_used, wgu, bgu, wd, bd, layer):
    n_rows = xs.shape[0]
    f2 = wgu.shape[3]
    used = lambda g, nu: jnp.minimum(g, nu[0] - 1)
    expert = lambda g, te, nu: (layer, te[used(g, nu)], 0, 0)
    grid_spec = pltpu.PrefetchScalarGridSpec(
        num_scalar_prefetch=2,
        grid=(n_rows // MOE_TILE,),
        in_specs=[
            pl.BlockSpec((MOE_TILE, D_MODEL), lambda g, te, nu: (used(g, nu), 0)),
            pl.BlockSpec((None, None, D_MODEL, f2), expert),
            pl.BlockSpec((None, None, 1, f2), expert),
            pl.BlockSpec((None, None, f2 // 2, D_MODEL), expert),
            pl.BlockSpec((None, None, 1, D_MODEL), expert),
        ],
        out_specs=pl.BlockSpec((MOE_TILE, D_MODEL), lambda g, te, nu: (g, 0)),
        scratch_shapes=[pltpu.VMEM((D_MODEL, f2), BF16), pltpu.VMEM((f2 // 2, D_MODEL), BF16)],
    )
    return pl.pallas_call(
        _expert_kernel,
        grid_spec=grid_spec,
        out_shape=jax.ShapeDtypeStruct((n_rows, D_MODEL), F32),
        compiler_params=_params("arbitrary"),
        name="moe_experts",
    )(tile_expert, n_used, xs, wgu, bgu, wd, bd)


def _collect_kernel(dest_ref, dest_next_ref, gate_ref, x_ref, mod_ref, y_ref, o_ref, buf_ref, sem):
    tm = x_ref.shape[0]
    step = pl.program_id(0)
    slot = step % 2

    def issue_tile(d_ref, to_slot):
        def issue(tok, carry):
            for kq in range(TOP_K):
                row = d_ref[0, tok * TOP_K + kq]
                pltpu.make_async_copy(y_ref.at[pl.ds(row, 1), :],
                                      buf_ref.at[to_slot, kq, pl.ds(tok, 1), :], sem.at[to_slot]).start()
            return carry

        lax.fori_loop(0, tm, issue, 0, unroll=4)

    @pl.when(step == 0)
    def _():
        issue_tile(dest_ref, slot)

    @pl.when(step + 1 < pl.num_programs(0))
    def _():
        issue_tile(dest_next_ref, 1 - slot)

    for kq in range(TOP_K):
        pltpu.make_async_copy(y_ref.at[pl.ds(0, tm), :], buf_ref.at[slot, kq], sem.at[slot]).wait()
    gate = gate_ref[...]
    acc = gate[:, 0:1] * buf_ref[slot, 0]
    for kq in range(1, TOP_K):
        acc = acc + gate[:, kq:kq + 1] * buf_ref[slot, kq]
    o_ref[...] = x_ref[...] + mod_ref[5:6, :] * acc


def _collect(dest, gate_rows, x2, mod_l, ys, geom, latent_only):
    t = x2.shape[0]
    tm = ROW_TILE
    tpb, ctx_tiles, batch = geom
    row = functools.partial(_mod_row, tiles_per_b=tpb, ctx_tiles=ctx_tiles, batch=batch)
    if latent_only:
        lat = tpb - ctx_tiles
        n_steps = batch * lat
        tile = lambda i: (i // lat) * tpb + ctx_tiles + i % lat
    else:
        n_steps = t // tm
        tile = lambda i: i
    nxt = lambda i: tile(jnp.minimum(i + 1, n_steps - 1))
    return pl.pallas_call(
        _collect_kernel,
        grid=(n_steps,),
        in_specs=[
            pl.BlockSpec((None, 1, tm * TOP_K), lambda i: (tile(i), 0, 0), memory_space=pltpu.SMEM),
            pl.BlockSpec((None, 1, tm * TOP_K), lambda i: (nxt(i), 0, 0), memory_space=pltpu.SMEM),
            pl.BlockSpec((tm, 128), lambda i: (tile(i), 0)),
            pl.BlockSpec((tm, D_MODEL), lambda i: (tile(i), 0)),
            pl.BlockSpec((None, 6, D_MODEL), lambda i: (row(tile(i)), 0, 0)),
            pl.BlockSpec(memory_space=pl.ANY),
        ],
        out_specs=pl.BlockSpec((tm, D_MODEL), lambda i: (i, 0)),
        out_shape=jax.ShapeDtypeStruct((n_steps * tm, D_MODEL), F32),
        scratch_shapes=[pltpu.VMEM((2, TOP_K, tm, D_MODEL), F32), pltpu.SemaphoreType.DMA((2,))],
        compiler_params=_params("arbitrary"),
        name="moe_collect",
    )(dest, dest, gate_rows, x2, mod_l, ys)


def _rope_angles(pos, dim):
    inv = ROPE_BASE ** (-jnp.arange(0, dim, 2, dtype=F32) / dim)
    return pos.astype(F32)[:, None] * inv[None, :]


def _rope_tables(seq, ctx_len):
    pos = jnp.arange(seq, dtype=jnp.int32)
    ang = _rope_angles(pos, RET_DK)
    r_cos = jnp.tile(jnp.concatenate([jnp.cos(ang), jnp.cos(ang)], -1), (1, RET_HEADS))
    r_sin = jnp.tile(jnp.concatenate([-jnp.sin(ang), jnp.sin(ang)], -1), (1, RET_HEADS))
    ra = _rope_angles(pos // GRID_W, MLA_ROPE // 2)
    ca = _rope_angles(pos % GRID_W, MLA_ROPE // 2)
    one = jnp.ones((seq, MLA_NOPE), F32)
    zero = jnp.zeros((seq, MLA_NOPE), F32)
    m_cos = jnp.concatenate([one, jnp.cos(ra), jnp.cos(ra), jnp.cos(ca), jnp.cos(ca), one[:, :32]], -1)
    m_sin = jnp.concatenate([zero, -jnp.sin(ra), jnp.sin(ra), -jnp.sin(ca), jnp.sin(ca), zero[:, :32]], -1)
    ctx1 = lambda n: jnp.ones((ctx_len, n), F32)
    ctx0 = lambda n: jnp.zeros((ctx_len, n), F32)
    return (jnp.concatenate([ctx1(256), r_cos], 0), jnp.concatenate([ctx0(256), r_sin], 0),
            jnp.concatenate([ctx1(128), m_cos], 0), jnp.concatenate([ctx0(128), m_sin], 0))


def kernel(x, c, ctx, c_ctx, ada_w, ada_b, norm1_g, norm2_g, w_in, mla_q_norm_g, mla_w_uq, mla_kv_norm_g, mla_w_ukv, mla_qk_q_g, mla_qk_k_g, rwkv_mu_prev, rwkv_mu_next, rwkv_w0, rwkv_w2, rwkv_a0, rwkv_a2, rwkv_g2, rwkv_k_k, rwkv_k_a, rwkv_r_k, rwkv_lnx_g, rwkv_lnx_b, w_br_ret, w_br_mla, w_br_rwkv, w_out, w_router, b_router, w_gu, b_gu, w_down, b_down):
    b, s, d = x.shape
    ctx_len = ctx.shape[1]
    depth = ada_w.shape[0]
    l = ctx_len + s
    t = b * l
    assert d == D_MODEL and b < 16 and ctx_len % ROW_TILE == 0 and s % ROW_TILE == 0
    geom = (l // ROW_TILE, ctx_len // ROW_TILE, b)

    xa = jnp.concatenate([ctx, x], axis=1).reshape(t, d)
    cc = jnp.zeros((16, d), F32).at[:b].set(c).at[b].set(c_ctx)
    mod = _ada_all(cc, ada_w.astype(BF16), ada_b).reshape(depth, 16, 6, d)

    ret_cos, ret_sin, mla_cos, mla_sin = _rope_tables(s, ctx_len)
    ret_consts = _retention_consts()
    rwkv_consts = _rwkv_consts()
    head_blk = jnp.asarray(np.kron(np.eye(RWKV_HEADS), np.ones((RWKV_HD, RWKV_HD))), BF16)
    route_tri = jnp.asarray(np.tril(np.ones((ROW_TILE, ROW_TILE)), -1), BF16)

    xs = None
    for li in range(depth):
        p2 = _inproj(xa, mod[li], norm1_g[li], _pad_w_in(w_in[li]), geom)
        p3 = p2.reshape(b, l, P_COLS)

        retf, retb = (r.reshape(t, RET_W) for r in _retention(p3, ret_cos, ret_sin, ret_consts, ctx_len))

        mw = _mla_weights(mla_q_norm_g[li], mla_w_uq[li], mla_kv_norm_g[li], mla_w_ukv[li],
                          mla_qk_q_g[li], mla_qk_k_g[li])
        q2, k2, v2 = (a.reshape(b, l, MLA_PAD_W) for a in _mla_prep(p2, mw, mla_cos, mla_sin, l))
        mla = _attention(q2, k2, v2, ctx_len).reshape(t, MLA_W)

        rw = dict(mup=rwkv_mu_prev[li].reshape(1, -1), mun=rwkv_mu_next[li].reshape(1, -1),
                  w0=rwkv_w0[li].reshape(2, 1, RWKV_W), w2=rwkv_w2[li].astype(BF16),
                  a0=rwkv_a0[li].reshape(2, 1, RWKV_W), a2=rwkv_a2[li].astype(BF16),
                  g2=rwkv_g2[li].astype(BF16), kk=rwkv_k_k[li].reshape(1, -1),
                  ka=rwkv_k_a[li].reshape(1, -1), rk=rwkv_r_k[li].reshape(1, -1), blk=head_blk)
        r_, v_, na_, kd_, bb_, lw_, g_, bonus_ = _rwkv_elem(p2, rw, l, ctx_len)
        yf, yb = _rwkv_chunk(dict(r=r_, v=v_, na=na_, kd=kd_, bb=bb_, lw=lw_), b, l, ctx_len, rwkv_consts)

        mg = dict(lng=rwkv_lnx_g[li].reshape(1, -1), lnb=rwkv_lnx_b[li].reshape(1, -1), blk=head_blk,
                  wr=w_br_ret[li].astype(BF16), wm=w_br_mla[li].astype(BF16),
                  ww=w_br_rwkv[li].astype(BF16), wo=w_out[li].astype(BF16))
        xa = _merge(xa, mod[li], p2, retf, retb, mla, yf.reshape(t, RWKV_W), yb.reshape(t, RWKV_W), bonus_, g_,
                    mg, geom)

        wr_pad = jnp.pad(w_router[li], ((0, 0), (0, 128 - N_EXPERTS))).astype(BF16)
        br_pad = jnp.concatenate([b_router[li], jnp.full((128 - N_EXPERTS,), -1e30, F32)]).reshape(1, 128)
        h2, idx_rows, rank_rows, gate_rows, cnt = _route(xa, mod[li], norm2_g[li], wr_pad, br_pad,
                                                         route_tri, geom)
        counts = cnt[0, :N_EXPERTS].astype(jnp.int32)
        dest, tile_expert, n_used, n_rows = _moe_layout(idx_rows[:, :TOP_K], rank_rows[:, :TOP_K], counts, t)
        if xs is None:
            xs = jnp.zeros((n_rows, D_MODEL), F32)
        xs = _dispatch(dest, h2, xs)
        ys = _experts(xs, tile_expert, n_used, w_gu, b_gu.reshape(depth, N_EXPERTS, 1, -1),
                      w_down, b_down.reshape(depth, N_EXPERTS, 1, -1), li)
        xa = _collect(dest, gate_rows, xa, mod[li], ys, geom, latent_only=li == depth - 1)

    return xa.reshape(b, s, d)
```
